```python
import jax, jax.numpy as jnp
from jax import lax
import numpy as np

D_MODEL = 2048
BATCH = 1
SEQ = 8192
DEPTH = 2

N_MEM = 256
EPS = 1e-6
POOL_WINDOWS = (2, 4, 8, 16)
N_POOL_GROUPS = len(POOL_WINDOWS)
POOL_WIDTH = D_MODEL // 2
POOL_GROUP_DIM = POOL_WIDTH // N_POOL_GROUPS
MLSTM_HEADS = 4
MLSTM_WIDTH = D_MODEL - POOL_WIDTH
MLSTM_HEAD_DIM = MLSTM_WIDTH // MLSTM_HEADS
MLSTM_CHUNK = 128
EVEN_IN_WIDTH = POOL_WIDTH + 4 * MLSTM_WIDTH + 2 * MLSTM_HEADS
FORGET_BIAS = 3.0
CONV_WIDTH = 3
XATTN_HEADS = 4
XATTN_HEAD_DIM = D_MODEL // XATTN_HEADS
N_GROUPS = 4
EXPERTS_PER_GROUP = 8
N_EXPERTS = N_GROUPS * EXPERTS_PER_GROUP
TOP_K = 2
D_EXPERT = D_MODEL // 2
MOE_BLOCK = 128
N_EVEN = (DEPTH + 1) // 2
N_ODD = DEPTH // 2

kernel_name = 'hybrid_pool_mlstm_shortconv_xattn_hmoe'


def rmsnorm(x, g):
    xf = x.astype(jnp.float32)
    y = xf * lax.rsqrt(jnp.mean(xf * xf, axis=-1, keepdims=True) + EPS)
    return (y * g.astype(jnp.float32)).astype(x.dtype)


def pool_mixer(u, w_grp, scale):
    b, s, _ = u.shape
    ug = u.astype(jnp.float32).reshape(b, s, N_POOL_GROUPS, POOL_GROUP_DIM)
    cs = jnp.cumsum(ug, axis=1)
    pos = jnp.arange(s)
    diffs = []
    for j, w in enumerate(POOL_WINDOWS):
        c = cs[:, :, j]
        c_prev = jnp.pad(c[:, :s - w], ((0, 0), (w, 0), (0, 0)))
        count = jnp.minimum(pos + 1, w).astype(jnp.float32)[None, :, None]
        diffs.append((c - c_prev) / count - ug[:, :, j])
    d = jnp.stack(diffs, axis=2).astype(u.dtype)
    y = jnp.einsum('bsgc,gcd->bsgd', d, w_grp)
    return y.reshape(b, s, POOL_WIDTH) * scale


def to_chunks(a, n_chunks):
    b, s, h = a.shape[:3]
    a = a.reshape((b, n_chunks, s // n_chunks, h) + a.shape[3:])
    return jnp.moveaxis(a, (1, 3), (0, 2))


def mlstm_chunkwise(q, k, v, i_pre, f_pre):
    b, s, h, dh = q.shape
    nc = s // MLSTM_CHUNK
    f32 = jnp.float32
    qc = to_chunks(q.astype(f32) * dh ** -0.5, nc)
    kc = to_chunks(k.astype(f32), nc)
    vc = to_chunks(v.astype(f32), nc)
    ic = to_chunks(i_pre.astype(f32), nc)
    lfc = to_chunks(jax.nn.log_sigmoid(f_pre.astype(f32)), nc)
    causal = jnp.tril(jnp.ones((MLSTM_CHUNK, MLSTM_CHUNK), dtype=bool))

    def step(carry, inp):
        c_st, n_st, m_st = carry
        qb, kb, vb, ib, lfb = inp
        bcum = jnp.cumsum(lfb, axis=-1)
        dmat = jnp.where(causal, bcum[..., :, None] - bcum[..., None, :] + ib[..., None, :], -jnp.inf)
        inter = bcum + m_st[..., None]
        m_t = jnp.maximum(jnp.max(dmat, axis=-1), inter)
        w_inter = jnp.exp(inter - m_t)
        scores = jnp.einsum('bhtd,bhsd->bhts', qb, kb) * jnp.exp(dmat - m_t[..., None])
        num = (w_inter[..., None] * jnp.einsum('bhvk,bhtk->bhtv', c_st, qb)
               + jnp.einsum('bhts,bhsv->bhtv', scores, vb))
        den = w_inter * jnp.einsum('bhk,bhtk->bht', n_st, qb) + jnp.sum(scores, axis=-1)
        h_out = num / jnp.maximum(jnp.abs(den), jnp.exp(-m_t))[..., None]
        b_last = bcum[..., -1]
        d_end = b_last[..., None] - bcum + ib
        m_new = jnp.maximum(b_last + m_st, jnp.max(d_end, axis=-1))
        a_prev = jnp.exp(b_last + m_st - m_new)
        a_s = jnp.exp(d_end - m_new[..., None])
        c_new = a_prev[..., None, None] * c_st + jnp.einsum('bhsv,bhsk->bhvk', vb * a_s[..., None], kb)
        n_new = a_prev[..., None] * n_st + jnp.einsum('bhs,bhsk->bhk', a_s, kb)
        return (c_new, n_new, m_new), h_out

    init = (jnp.zeros((b, h, dh, dh), f32), jnp.zeros((b, h, dh), f32), jnp.zeros((b, h), f32))
    _, hs = lax.scan(step, init, (qc, kc, vc, ic, lfc))
    return jnp.moveaxis(hs, (0, 2), (1, 3)).reshape(b, s, h, dh)


def head_rmsnorm(hm, gain):
    b, s, h, dh = hm.shape
    y = hm * lax.rsqrt(jnp.mean(hm * hm, axis=-1, keepdims=True) + EPS)
    return y.reshape(b, s, h * dh) * gain.astype(jnp.float32)


def even_mixer(hn, w_in, b_gates, pool_w, pool_scale, head_gain, w_out):
    b, s, _ = hn.shape
    z = hn @ w_in
    cuts = [POOL_WIDTH + j * MLSTM_WIDTH for j in range(5)]
    u, q, k, v, o, g = jnp.split(z, cuts, axis=-1)
    g = g.astype(jnp.float32) + b_gates.astype(jnp.float32)
    heads = lambda a: a.reshape(b, s, MLSTM_HEADS, MLSTM_HEAD_DIM)
    hm = mlstm_chunkwise(heads(q), heads(k), heads(v), g[..., :MLSTM_HEADS], g[..., MLSTM_HEADS:])
    y_m = (jax.nn.sigmoid(o.astype(jnp.float32)) * head_rmsnorm(hm, head_gain)).astype(hn.dtype)
    y_p = pool_mixer(u, pool_w, pool_scale)
    return jnp.concatenate([y_p, y_m], axis=-1) @ w_out


def odd_mixer(hn, w_in, conv_w, w_out):
    s = hn.shape[1]
    gate_b, gate_c, u = jnp.split(hn @ w_in, 3, axis=-1)
    z = gate_c * u
    zp = jnp.pad(z, ((0, 0), (CONV_WIDTH - 1, 0), (0, 0)))
    conv = conv_w[0] * zp[:, 0:s]
    for j in range(1, CONV_WIDTH):
        conv = conv + conv_w[j] * zp[:, j:j + s]
    return (gate_b * conv) @ w_out


def cross_attn(hn, mem_n, wq, wk, wv, wo):
    b, s, d = hn.shape
    m = mem_n.shape[1]
    q = (hn @ wq).reshape(b, s, XATTN_HEADS, XATTN_HEAD_DIM)
    k = (mem_n @ wk).reshape(b, m, XATTN_HEADS, XATTN_HEAD_DIM)
    v = (mem_n @ wv).reshape(b, m, XATTN_HEADS, XATTN_HEAD_DIM)
    sc = jnp.einsum('bshd,bmhd->bhsm', q, k).astype(jnp.float32) * XATTN_HEAD_DIM ** -0.5
    p = jax.nn.softmax(sc, axis=-1).astype(v.dtype)
    o = jnp.einsum('bhsm,bmhd->bshd', p, v).reshape(b, s, d)
    return o @ wo


def hier_moe(hn, wg_r, bg_r, we_r, be_r, w_gate, w_up, w_down):
    b, s, d = hn.shape
    n = b * s
    xt = hn.reshape(n, d)
    tok = jnp.arange(n, dtype=jnp.int32)
    g_logits = (xt @ wg_r).astype(jnp.float32) + bg_r.astype(jnp.float32)
    g_prob = jax.nn.softmax(g_logits, axis=-1)
    grp = jnp.argmax(g_logits, axis=-1).astype(jnp.int32)
    e_logits = ((xt @ we_r).astype(jnp.float32) + be_r.astype(jnp.float32)).reshape(n, N_GROUPS, EXPERTS_PER_GROUP)
    e_in = e_logits[tok, grp]
    top_v, top_i = lax.top_k(e_in, TOP_K)
    w_tok = jax.nn.softmax(top_v, axis=-1) * g_prob[tok, grp][:, None]
    expert = grp[:, None] * EXPERTS_PER_GROUP + top_i.astype(jnp.int32)
    a = n * TOP_K
    e_flat = expert.reshape(a)
    t_flat = jnp.repeat(tok, TOP_K)
    w_flat = w_tok.reshape(a)
    order = jnp.argsort(e_flat)
    e_s, t_s, w_s = e_flat[order], t_flat[order], w_flat[order]
    counts = jnp.bincount(e_flat, length=N_EXPERTS)
    start = jnp.cumsum(counts) - counts
    padded = (counts + MOE_BLOCK - 1) // MOE_BLOCK * MOE_BLOCK
    pend = jnp.cumsum(padded)
    pstart = pend - padded
    dest = pstart[e_s] + jnp.arange(a, dtype=jnp.int32) - start[e_s]
    n_blocks = -(-a // MOE_BLOCK) + N_EXPERTS
    p_rows = n_blocks * MOE_BLOCK
    row_tok = jnp.zeros((p_rows,), jnp.int32).at[dest].set(t_s)
    row_w = jnp.zeros((p_rows,), jnp.float32).at[dest].set(w_s)
    blk_e = jnp.minimum(jnp.searchsorted(pend, jnp.arange(n_blocks) * MOE_BLOCK, side='right'), N_EXPERTS - 1)
    xb = xt[row_tok].reshape(n_blocks, MOE_BLOCK, d)

    def expert_block(args):
        xblk, e = args
        hid = jax.nn.silu(xblk @ w_gate[e]) * (xblk @ w_up[e])
        return hid @ w_down[e]

    yb = lax.map(expert_block, (xb, blk_e)).reshape(p_rows, d)
    y = jnp.zeros((n, d), hn.dtype).at[row_tok].add(yb * row_w[:, None].astype(yb.dtype))
    return y.reshape(b, s, d)


def setup_inputs(seed: int = 0) -> dict:
    key = jax.random.key(seed)
    ks = iter(jax.random.split(key, 40))
    f32 = jnp.float32

    def normal(shape):
        return jax.random.normal(next(ks), shape, f32)

    def dense(shape, fan_in):
        return normal(shape) * fan_in ** -0.5

    def gain(shape):
        return 1.0 + 0.02 * normal(shape)

    def bias(shape):
        return 0.01 * normal(shape)

    gate_b = jnp.concatenate([bias((N_EVEN, MLSTM_HEADS)),
                              FORGET_BIAS + 0.1 * normal((N_EVEN, MLSTM_HEADS))], axis=-1)
    return {
        'x': normal((BATCH, SEQ, D_MODEL)),
        'mem': normal((BATCH, N_MEM, D_MODEL)),
        'mix_norm': gain((DEPTH, D_MODEL)),
        'xattn_norm': gain((DEPTH, D_MODEL)),
        'mem_norm': gain((DEPTH, D_MODEL)),
        'ffn_norm': gain((DEPTH, D_MODEL)),
        'final_norm': gain((D_MODEL,)),
        'ev_w_in': dense((N_EVEN, D_MODEL, EVEN_IN_WIDTH), D_MODEL),
        'ev_b_gates': gate_b,
        'ev_pool_w': dense((N_EVEN, N_POOL_GROUPS, POOL_GROUP_DIM, POOL_GROUP_DIM), POOL_GROUP_DIM),
        'ev_pool_scale': gain((N_EVEN, POOL_WIDTH)),
        'ev_head_norm': gain((N_EVEN, MLSTM_WIDTH)),
        'ev_w_out': dense((N_EVEN, D_MODEL, D_MODEL), D_MODEL),
        'od_w_in': dense((N_ODD, D_MODEL, 3 * D_MODEL), D_MODEL),
        'od_conv_w': dense((N_ODD, CONV_WIDTH, D_MODEL), CONV_WIDTH),
        'od_w_out': dense((N_ODD, D_MODEL, D_MODEL), D_MODEL),
        'xa_wq': dense((DEPTH, D_MODEL, D_MODEL), D_MODEL),
        'xa_wk': dense((DEPTH, D_MODEL, D_MODEL), D_MODEL),
        'xa_wv': dense((DEPTH, D_MODEL, D_MODEL), D_MODEL),
        'xa_wo': dense((DEPTH, D_MODEL, D_MODEL), D_MODEL),
        'rt_group_w': dense((DEPTH, D_MODEL, N_GROUPS), D_MODEL),
        'rt_group_b': bias((DEPTH, N_GROUPS)),
        'rt_expert_w': dense((DEPTH, D_MODEL, N_EXPERTS), D_MODEL),
        'rt_expert_b': bias((DEPTH, N_EXPERTS)),
        'ex_w_gate': dense((DEPTH, N_EXPERTS, D_MODEL, D_EXPERT), D_MODEL),
        'ex_w_up': dense((DEPTH, N_EXPERTS, D_MODEL, D_EXPERT), D_MODEL),
        'ex_w_down': dense((DEPTH, N_EXPERTS, D_EXPERT, D_MODEL), D_EXPERT),
    }


def reference(x, mem, mix_norm, xattn_norm, mem_norm, ffn_norm, final_norm,
              ev_w_in, ev_b_gates, ev_pool_w, ev_pool_scale, ev_head_norm, ev_w_out,
              od_w_in, od_conv_w, od_w_out,
              xa_wq, xa_wk, xa_wv, xa_wo,
              rt_group_w, rt_group_b, rt_expert_w, rt_expert_b,
              ex_w_gate, ex_w_up, ex_w_down):
    h = x
    for layer in range(DEPTH):
        j = layer // 2
        hn = rmsnorm(h, mix_norm[layer])
        if layer % 2 == 0:
            h = h + even_mixer(hn, ev_w_in[j], ev_b_gates[j], ev_pool_w[j], ev_pool_scale[j],
                               ev_head_norm[j], ev_w_out[j])
        else:
            h = h + odd_mixer(hn, od_w_in[j], od_conv_w[j], od_w_out[j])
        h = h + cross_attn(rmsnorm(h, xattn_norm[layer]), rmsnorm(mem, mem_norm[layer]),
                           xa_wq[layer], xa_wk[layer], xa_wv[layer], xa_wo[layer])
        h = h + hier_moe(rmsnorm(h, ffn_norm[layer]), rt_group_w[layer], rt_group_b[layer],
                         rt_expert_w[layer], rt_expert_b[layer],
                         ex_w_gate[layer], ex_w_up[layer], ex_w_down[layer])
    return rmsnorm(h, final_norm)
```

```python
import functools

import jax
import jax.numpy as jnp
from jax import lax
from jax.experimental import pallas as pl
from jax.experimental.pallas import tpu as pltpu

F32 = jnp.float32
BF = jnp.bfloat16
EPS = 1e-6

POOL_WINDOWS = (2, 4, 8, 16)
N_POOL_GROUPS = 4
MLSTM_HEADS = 4
FORGET_LANE0 = MLSTM_HEADS
XATTN_HEADS = 4
N_GROUPS = 4
EXPERTS_PER_GROUP = 8
N_EXPERTS = N_GROUPS * EXPERTS_PER_GROUP
TOP_K = 2
CONV_WIDTH = 3

LANES = 128
SUBLANES = 8
MXU_DIM = 256
VMEM_BYTES_V7X = 64 * 1024 * 1024

ROW_TILE = 1024
COL_TILE = 512
MLSTM_CHUNK = 256
POOL_ROWS = 512
POOL_HALO = 128
CONV_ROWS = 512
CONV_COLS = 512
CONV_HALO = 16
ATTN_ROWS = 512
ROUTER_ROWS = 512
MOE_BLOCK = 256
COMBINE_ROWS = 256
ROW_CHUNKS = 16


def _cparams(semantics, vmem_mib):
    return pltpu.CompilerParams(dimension_semantics=semantics, vmem_limit_bytes=vmem_mib * 1024 * 1024)


def _sigmoid(x):
    return 1.0 / (1.0 + jnp.exp(-x))


def _log_sigmoid(x):
    return jnp.minimum(x, 0.0) - jnp.log(1.0 + jnp.exp(-jnp.abs(x)))


def _rms(x, g):
    ms = jnp.mean(x * x, axis=-1, keepdims=True)
    return x * lax.rsqrt(ms + EPS) * g


def _norm_mm_kernel(x_ref, g_ref, w_ref, o_ref, xn_ref):
    @pl.when(pl.program_id(1) == 0)
    def _():
        xn_ref[...] = _rms(x_ref[...], g_ref[...]).astype(BF)

    o_ref[...] = jnp.dot(xn_ref[...], w_ref[...].astype(BF), preferred_element_type=F32).astype(o_ref.dtype)


def _norm_mm_gates_kernel(x_ref, g_ref, w_ref, wg_ref, bg_ref, o_ref, gates_ref, xn_ref):
    @pl.when(pl.program_id(1) == 0)
    def _():
        xn = _rms(x_ref[...], g_ref[...]).astype(BF)
        xn_ref[...] = xn
        gates_ref[...] = jnp.dot(xn, wg_ref[...].astype(BF), preferred_element_type=F32) + bg_ref[...]

    o_ref[...] = jnp.dot(xn_ref[...], w_ref[...].astype(BF), preferred_element_type=F32).astype(o_ref.dtype)


def _stacked(w, layer):
    return (w[None], 0) if layer is None else (w, layer)


def norm_matmul(x, gain, w, n_cols, gate_w=None, gate_b=None, layer=None):
    n, k = x.shape
    w, li = _stacked(w, layer)
    tm = min(ROW_TILE, n)
    tn = COL_TILE
    grid = (n // tm, n_cols // tn)
    x_spec = pl.BlockSpec((tm, k), lambda i, j: (i, 0))
    g_spec = pl.BlockSpec((1, k), lambda i, j: (0, 0))
    w_spec = pl.BlockSpec((None, k, tn), lambda i, j: (li, 0, j))
    o_spec = pl.BlockSpec((tm, tn), lambda i, j: (i, j))
    scratch = [pltpu.VMEM((tm, k), BF)]
    gain = gain.reshape(1, k)
    if gate_w is None:
        return pl.pallas_call(
            _norm_mm_kernel, grid=grid, in_specs=[x_spec, g_spec, w_spec], out_specs=o_spec,
            out_shape=jax.ShapeDtypeStruct((n, n_cols), BF), scratch_shapes=scratch,
            compiler_params=_cparams(("parallel", "arbitrary"), 48), name="norm_matmul",
        )(x, gain, w)
    small = pl.BlockSpec((k, LANES), lambda i, j: (0, 0))
    bias = pl.BlockSpec((1, LANES), lambda i, j: (0, 0))
    return pl.pallas_call(
        _norm_mm_gates_kernel, grid=grid, in_specs=[x_spec, g_spec, w_spec, small, bias],
        out_specs=[o_spec, pl.BlockSpec((tm, LANES), lambda i, j: (i, 0))],
        out_shape=[jax.ShapeDtypeStruct((n, n_cols), BF), jax.ShapeDtypeStruct((n, LANES), F32)],
        scratch_shapes=scratch, compiler_params=_cparams(("parallel", "arbitrary"), 48), name="norm_matmul_gates",
    )(x, gain, w, gate_w, gate_b)


def _mm_res_kernel(*refs, nparts):
    xs, ws = refs[:nparts], refs[nparts:2 * nparts]
    res_ref, o_ref = refs[2 * nparts:]
    acc = res_ref[...]
    for x_ref, w_ref in zip(xs, ws):
        acc = acc + jnp.dot(x_ref[...], w_ref[...].astype(BF), preferred_element_type=F32)
    o_ref[...] = acc


def matmul_residual(xs, w, res, layer=None):
    n, d = res.shape
    w, li = _stacked(w, layer)
    tm = min(ROW_TILE, n)
    tn = COL_TILE
    nparts = len(xs)
    kp = xs[0].shape[1]
    in_specs = [pl.BlockSpec((tm, kp), lambda i, j: (i, 0)) for _ in xs]
    in_specs += [pl.BlockSpec((None, kp, tn), lambda i, j, p=p: (li, p, j)) for p in range(nparts)]
    in_specs += [pl.BlockSpec((tm, tn), lambda i, j: (i, j))]
    return pl.pallas_call(
        functools.partial(_mm_res_kernel, nparts=nparts), grid=(n // tm, d // tn), in_specs=in_specs,
        out_specs=pl.BlockSpec((tm, tn), lambda i, j: (i, j)), out_shape=jax.ShapeDtypeStruct((n, d), F32),
        compiler_params=_cparams(("parallel", "arbitrary"), 48), name="matmul_residual",
    )(*xs, *([w] * nparts), res)


def _pool_kernel(cur_ref, prev_ref, w_ref, sc_ref, o_ref, *, tp):
    i, j = pl.program_id(0), pl.program_id(1)
    win = lax.shift_left(jnp.int32(POOL_WINDOWS[0]), j)
    cur = cur_ref[...]
    dist = lax.broadcasted_iota(jnp.int32, (tp, tp), 0) - lax.broadcasted_iota(jnp.int32, (tp, tp), 1)
    band = jnp.where(dist >= 0, jnp.where(dist < win, 1.0, 0.0), 0.0).astype(BF)
    s = jnp.dot(band, cur, preferred_element_type=F32)
    distp = (lax.broadcasted_iota(jnp.int32, (tp, POOL_HALO), 0) + POOL_HALO
             - lax.broadcasted_iota(jnp.int32, (tp, POOL_HALO), 1))
    limit = jnp.where(i > 0, win, 0)
    bandp = jnp.where(distp < limit, 1.0, 0.0).astype(BF)
    s = s + jnp.dot(bandp, prev_ref[...], preferred_element_type=F32)
    pos = i * tp + lax.broadcasted_iota(jnp.int32, (tp, 1), 0)
    cnt = jnp.minimum(pos + 1, win).astype(F32)
    d = s / cnt - cur.astype(F32)
    y = jnp.dot(d.astype(BF), w_ref[...].astype(BF), preferred_element_type=F32) * sc_ref[...]
    o_ref[...] = y.astype(o_ref.dtype)


def pool_mixer(z, pool_w, pool_scale):
    n = z.shape[0]
    gdim = pool_w.shape[-1]
    width = N_POOL_GROUPS * gdim
    tp = min(POOL_ROWS, n)
    halo_blocks = tp // POOL_HALO
    return pl.pallas_call(
        functools.partial(_pool_kernel, tp=tp), grid=(n // tp, N_POOL_GROUPS),
        in_specs=[
            pl.BlockSpec((tp, gdim), lambda i, j: (i, j)),
            pl.BlockSpec((POOL_HALO, gdim), lambda i, j: (jnp.maximum(i * halo_blocks - 1, 0), j)),
            pl.BlockSpec((None, gdim, gdim), lambda i, j: (j, 0, 0)),
            pl.BlockSpec((1, gdim), lambda i, j: (0, j)),
        ],
        out_specs=pl.BlockSpec((tp, gdim), lambda i, j: (i, j)),
        out_shape=jax.ShapeDtypeStruct((n, width), BF),
        compiler_params=_cparams(("parallel", "arbitrary"), 32), name="pool_mixer",
    )(z, z, pool_w, pool_scale.reshape(1, width))


def _mlstm_kernel(q_ref, k_ref, v_ref, o_ref, g_ref, gain_ref, y_ref, ct_ref, m_ref, *, chunk, dh):
    c = pl.program_id(0)
    aug = dh + LANES

    @pl.when(c == 0)
    def _():
        ct_ref[...] = jnp.zeros_like(ct_ref)
        m_ref[...] = jnp.zeros_like(m_ref)

    g = g_ref[...]
    lf = _log_sigmoid(g)
    row = lax.broadcasted_iota(jnp.int32, (chunk, chunk), 0)
    col = lax.broadcasted_iota(jnp.int32, (chunk, chunk), 1)
    causal = col <= row
    ltri = jnp.where(causal, 1.0, 0.0).astype(BF)
    hi = lf.astype(BF)
    r1 = lf - hi.astype(F32)
    mid = r1.astype(BF)
    lo = (r1 - mid.astype(F32)).astype(BF)
    bcum = (jnp.dot(ltri, hi, preferred_element_type=F32) + jnp.dot(ltri, mid, preferred_element_type=F32)
            + jnp.dot(ltri, lo, preferred_element_type=F32))
    g_t = g.T
    b_t = bcum.T
    ones_col = jnp.where(lax.broadcasted_iota(jnp.int32, (chunk, LANES), 1) == 0, 1.0, 0.0).astype(BF)

    for h in range(MLSTM_HEADS):
        hs = slice(h * dh, (h + 1) * dh)
        fl = FORGET_LANE0 + h
        bc = bcum[:, fl:fl + 1]
        br = b_t[fl:fl + 1, :]
        ir = g_t[h:h + 1, :]
        b_last = bcum[chunk - 1:chunk, fl:fl + 1]
        m_prev = m_ref[h][:, 0:1]

        dmat = jnp.where(causal, bc + (ir - br), -jnp.inf)
        inter = bc + m_prev
        m_t = jnp.maximum(jnp.max(dmat, axis=1, keepdims=True), inter)
        w_inter = jnp.exp(inter - m_t)
        p = jnp.exp(dmat - m_t)

        qh = q_ref[:, hs] * (dh ** -0.5)
        kh = k_ref[:, hs]
        v_aug = jnp.concatenate([v_ref[:, hs], ones_col], axis=1)
        s = lax.dot_general(qh, kh, (((1,), (1,)), ((), ())), preferred_element_type=F32)
        sc = (s * p).astype(BF)
        ct = ct_ref[h]
        num_aug = (w_inter * jnp.dot(qh, ct.astype(BF), preferred_element_type=F32)
                   + jnp.dot(sc, v_aug, preferred_element_type=F32))
        num = num_aug[:, :dh]
        den = num_aug[:, dh:dh + 1]
        hout = num / jnp.maximum(jnp.abs(den), jnp.exp(-m_t))

        yn = _rms(hout, gain_ref[:, hs])
        y_ref[:, hs] = (_sigmoid(o_ref[:, hs].astype(F32)) * yn).astype(y_ref.dtype)

        d_end = b_last - br + ir
        m_new = jnp.maximum(b_last + m_prev, jnp.max(d_end, axis=1, keepdims=True))
        a_prev = jnp.exp(b_last + m_prev - m_new)
        a_s = jnp.exp(d_end - m_new)
        k_t = (kh.astype(F32).T * a_s).astype(BF)
        ct_ref[h] = a_prev * ct + jnp.dot(k_t, v_aug, preferred_element_type=F32)
        m_ref[h] = jnp.broadcast_to(m_new, (1, LANES))


def mlstm_mixer(z, gates, head_gain, col0):
    n = z.shape[0]
    width = head_gain.shape[0]
    dh = width // MLSTM_HEADS
    chunk = min(MLSTM_CHUNK, n)
    base = col0 // width
    qkvo = [pl.BlockSpec((chunk, width), lambda c, p=p: (c, base + p)) for p in range(4)]
    return pl.pallas_call(
        functools.partial(_mlstm_kernel, chunk=chunk, dh=dh), grid=(n // chunk,),
        in_specs=qkvo + [pl.BlockSpec((chunk, LANES), lambda c: (c, 0)), pl.BlockSpec((1, width), lambda c: (0, 0))],
        out_specs=pl.BlockSpec((chunk, width), lambda c: (c, 0)),
        out_shape=jax.ShapeDtypeStruct((n, width), BF),
        scratch_shapes=[pltpu.VMEM((MLSTM_HEADS, dh, dh + LANES), F32), pltpu.VMEM((MLSTM_HEADS, 1, LANES), F32)],
        compiler_params=_cparams(("arbitrary",), 32), name="mlstm_mixer",
    )(z, z, z, z, gates, head_gain.reshape(1, width))


def _conv_kernel(b_ref, c_ref, u_ref, cp_ref, up_ref, w_ref, o_ref, *, tr):
    i = pl.program_id(0)
    zc = c_ref[...].astype(F32) * u_ref[...].astype(F32)
    zp = cp_ref[...].astype(F32) * up_ref[...].astype(F32)
    zp = jnp.where(i > 0, zp, 0.0)
    row = lax.broadcasted_iota(jnp.int32, zc.shape, 0)
    acc = w_ref[CONV_WIDTH - 1:CONV_WIDTH, :] * zc
    for back in range(1, CONV_WIDTH):
        shifted = pltpu.roll(zc, back, 0)
        for r in range(back):
            shifted = jnp.where(row == r, zp[CONV_HALO - back + r:CONV_HALO - back + r + 1, :], shifted)
        acc = acc + w_ref[CONV_WIDTH - 1 - back:CONV_WIDTH - back, :] * shifted
    o_ref[...] = (b_ref[...].astype(F32) * acc).astype(o_ref.dtype)


def conv_mixer(z, conv_w):
    n = z.shape[0]
    d = conv_w.shape[1]
    tr = min(CONV_ROWS, n)
    tc = CONV_COLS
    nb = d // tc
    halo_blocks = tr // CONV_HALO
    prev = lambda part: pl.BlockSpec(
        (CONV_HALO, tc), lambda i, j: (jnp.maximum(i * halo_blocks - 1, 0), part * nb + j))
    cur = lambda part: pl.BlockSpec((tr, tc), lambda i, j: (i, part * nb + j))
    return pl.pallas_call(
        functools.partial(_conv_kernel, tr=tr), grid=(n // tr, nb),
        in_specs=[cur(0), cur(1), cur(2), prev(1), prev(2), pl.BlockSpec((CONV_WIDTH, tc), lambda i, j: (0, j))],
        out_specs=pl.BlockSpec((tr, tc), lambda i, j: (i, j)),
        out_shape=jax.ShapeDtypeStruct((n, d), BF),
        compiler_params=_cparams(("parallel", "arbitrary"), 32), name="conv_mixer",
    )(z, z, z, z, z, conv_w)


def _attn_kernel(q_ref, k_ref, v_ref, o_ref, *, hd):
    scale = hd ** -0.5
    for h in range(XATTN_HEADS):
        hs = slice(h * hd, (h + 1) * hd)
        s = lax.dot_general(q_ref[:, hs], k_ref[:, hs], (((1,), (1,)), ((), ())),
                            preferred_element_type=F32) * scale
        e = jnp.exp(s - jnp.max(s, axis=-1, keepdims=True))
        p = (e / jnp.sum(e, axis=-1, keepdims=True)).astype(BF)
        o_ref[:, hs] = jnp.dot(p, v_ref[:, hs], preferred_element_type=F32).astype(o_ref.dtype)


def attention(q, k, v):
    n, d = q.shape
    m = k.shape[0]
    ta = min(ATTN_ROWS, n)
    return pl.pallas_call(
        functools.partial(_attn_kernel, hd=d // XATTN_HEADS), grid=(n // ta,),
        in_specs=[pl.BlockSpec((ta, d), lambda i: (i, 0)), pl.BlockSpec((m, d), lambda i: (0, 0)),
                  pl.BlockSpec((m, d), lambda i: (0, 0))],
        out_specs=pl.BlockSpec((ta, d), lambda i: (i, 0)),
        out_shape=jax.ShapeDtypeStruct((n, d), BF),
        compiler_params=_cparams(("parallel",), 32), name="cross_attention",
    )(q, k, v)


def _router_kernel(x_ref, g_ref, wr_ref, br_ref, xn3_ref, ids_ref, wts_ref, cnt_ref, carry_ref, *, tr):
    @pl.when(pl.program_id(0) == 0)
    def _():
        carry_ref[...] = jnp.zeros_like(carry_ref)

    xn = _rms(x_ref[...], g_ref[...])
    for s in range(ROW_CHUNKS):
        xn3_ref[:, s, :] = xn[:, s * LANES:(s + 1) * LANES]
    logits = jnp.dot(xn, wr_ref[...], preferred_element_type=F32, precision=lax.Precision.HIGHEST) + br_ref[...]
    lane = lax.broadcasted_iota(jnp.int32, (tr, LANES), 1).astype(F32)
    neg = -jnp.inf

    def first_argmax(vals):
        top = jnp.max(vals, axis=-1, keepdims=True)
        return top, jnp.min(jnp.where(vals == top, lane, float(LANES)), axis=-1, keepdims=True)

    gl = jnp.where(lane < N_GROUPS, logits, neg)
    gmax, grp = first_argmax(gl)
    g_prob = 1.0 / jnp.sum(jnp.exp(gl - gmax), axis=-1, keepdims=True)
    lo = N_GROUPS + EXPERTS_PER_GROUP * grp
    el = jnp.where(lane >= lo, jnp.where(lane < lo + EXPERTS_PER_GROUP, logits, neg), neg)
    v1, l1 = first_argmax(el)
    v2, l2 = first_argmax(jnp.where(lane == l1, neg, el))
    e2 = jnp.exp(v2 - v1)
    w1 = g_prob / (1.0 + e2)
    w2 = g_prob * e2 / (1.0 + e2)
    hot1 = lane == l1
    hot2 = lane == l2
    hot = jnp.where(hot1, 1.0, jnp.where(hot2, 1.0, 0.0))
    earlier = (lax.broadcasted_iota(jnp.int32, (tr, tr), 1) < lax.broadcasted_iota(jnp.int32, (tr, tr), 0))
    before = jnp.dot(jnp.where(earlier, 1.0, 0.0).astype(BF), hot.astype(BF), preferred_element_type=F32)
    before = before + carry_ref[0:1, :]
    r1 = jnp.sum(jnp.where(hot1, before, 0.0), axis=-1, keepdims=True)
    r2 = jnp.sum(jnp.where(hot2, before, 0.0), axis=-1, keepdims=True)
    carry_ref[0:1, :] = carry_ref[0:1, :] + jnp.sum(hot, axis=0, keepdims=True)
    ids = jnp.where(lane == 0, l1 - N_GROUPS, jnp.where(lane == 1, l2 - N_GROUPS,
                    jnp.where(lane == 2, r1, jnp.where(lane == 3, r2, 0.0))))
    ids_ref[...] = ids.astype(jnp.int32)
    wts_ref[...] = jnp.where(lane == 0, w1, jnp.where(lane == 1, w2, 0.0))
    cnt_ref[...] = carry_ref[...]


def route(h, gain, w_router, b_router):
    n, d = h.shape
    tr = min(ROUTER_ROWS, n)
    return pl.pallas_call(
        functools.partial(_router_kernel, tr=tr), grid=(n // tr,),
        in_specs=[pl.BlockSpec((tr, d), lambda i: (i, 0)), pl.BlockSpec((1, d), lambda i: (0, 0)),
                  pl.BlockSpec((d, LANES), lambda i: (0, 0)), pl.BlockSpec((1, LANES), lambda i: (0, 0))],
        out_specs=[pl.BlockSpec((tr, ROW_CHUNKS, LANES), lambda i: (i, 0, 0)),
                   pl.BlockSpec((tr, LANES), lambda i: (i, 0)), pl.BlockSpec((tr, LANES), lambda i: (i, 0)),
                   pl.BlockSpec((SUBLANES, LANES), lambda i: (0, 0))],
        out_shape=[jax.ShapeDtypeStruct((n, ROW_CHUNKS, LANES), F32), jax.ShapeDtypeStruct((n, LANES), jnp.int32),
                   jax.ShapeDtypeStruct((n, LANES), F32), jax.ShapeDtypeStruct((SUBLANES, LANES), F32)],
        scratch_shapes=[pltpu.VMEM((SUBLANES, LANES), F32)],
        compiler_params=_cparams(("arbitrary",), 40), name="moe_router",
    )(h, gain.reshape(1, d), w_router, b_router)


def _row_gather_copy(src_hbm, row, dst, sem):
    return pltpu.make_async_copy(src_hbm.at[row], dst, sem)


def _moe_up_kernel(blk_ref, nact_ref, rtok_ref, xn3_hbm, wg_ref, wu_ref, hid_ref, buf, x2d, sem, *, tb, fchunk):
    b = pl.program_id(0)
    nact = nact_ref[0]

    def issue(blk, slot):
        base = blk * tb

        def body(r, carry):
            _row_gather_copy(xn3_hbm, rtok_ref[base + r], buf.at[slot, r], sem.at[slot]).start()
            return carry

        lax.fori_loop(0, tb, body, 0)

    @pl.when(b == 0)
    def _():
        issue(0, 0)

    @pl.when(b + 1 < nact)
    def _():
        issue(b + 1, (b + 1) % 2)

    @pl.when(b < nact)
    def _():
        slot = b % 2
        pltpu.make_async_copy(xn3_hbm.at[pl.ds(0, tb)], buf.at[slot], sem.at[slot]).wait()
        for s in range(ROW_CHUNKS):
            x2d[:, s * LANES:(s + 1) * LANES] = buf[slot, :, s, :].astype(BF)
        x = x2d[...]
        for f in range(0, hid_ref.shape[1], fchunk):
            gate = jnp.dot(x, wg_ref[:, f:f + fchunk].astype(BF), preferred_element_type=F32)
            up = jnp.dot(x, wu_ref[:, f:f + fchunk].astype(BF), preferred_element_type=F32)
            hid_ref[:, f:f + fchunk] = (gate * _sigmoid(gate) * up).astype(hid_ref.dtype)


def _moe_down_kernel(blk_ref, nact_ref, hid_ref, wd_ref, ys3_ref, *, nchunk):
    @pl.when(pl.program_id(0) < nact_ref[0])
    def _():
        hid = hid_ref[...]
        for c in range(0, ROW_CHUNKS, nchunk):
            y = jnp.dot(hid, wd_ref[:, c * LANES:(c + nchunk) * LANES].astype(BF), preferred_element_type=F32)
            for s in range(nchunk):
                ys3_ref[:, c + s, :] = y[:, s * LANES:(s + 1) * LANES]


def _combine_kernel(dest_ref, ys3_hbm, h_ref, w_ref, gain_ref, o_ref, buf, sem, *, tc, final_norm):
    i = pl.program_id(0)

    def issue(blk, slot):
        base = blk * tc

        def body(r, carry):
            for k in range(TOP_K):
                _row_gather_copy(ys3_hbm, dest_ref[TOP_K * (base + r) + k], buf.at[slot, k * tc + r],
                                 sem.at[slot]).start()
            return carry

        lax.fori_loop(0, tc, body, 0)

    @pl.when(i == 0)
    def _():
        issue(0, 0)

    @pl.when(i + 1 < pl.num_programs(0))
    def _():
        issue(i + 1, (i + 1) % 2)

    slot = i % 2
    pltpu.make_async_copy(ys3_hbm.at[pl.ds(0, TOP_K * tc)], buf.at[slot], sem.at[slot]).wait()
    w0 = w_ref[:, 0:1]
    w1 = w_ref[:, 1:2]
    for s in range(ROW_CHUNKS):
        cs = slice(s * LANES, (s + 1) * LANES)
        o_ref[:, cs] = h_ref[:, cs] + (w0 * buf[slot, 0:tc, s, :] + w1 * buf[slot, tc:TOP_K * tc, s, :])
    if final_norm:
        o_ref[...] = _rms(o_ref[...], gain_ref[...])


def hier_moe(h, ffn_gain, wg_r, bg_r, we_r, be_r, w_gate, w_up, w_down, layer, final_gain=None):
    n, d = h.shape
    f = w_gate.shape[-1]
    assert d == ROW_CHUNKS * LANES
    a = n * TOP_K
    tb = MOE_BLOCK
    nb = a // tb + N_EXPERTS

    pad = LANES - N_GROUPS - N_EXPERTS
    w_router = jnp.concatenate([wg_r, we_r, jnp.zeros((d, pad), F32)], axis=1)
    b_router = jnp.concatenate([bg_r, be_r, jnp.zeros((pad,), F32)]).reshape(1, LANES)
    xn3, ids, wts, cnt = route(h, ffn_gain, w_router, b_router)

    counts = cnt[0, N_GROUPS:N_GROUPS + N_EXPERTS].astype(jnp.int32)
    nblk = (counts + tb - 1) // tb
    bend = jnp.cumsum(nblk)
    nact = bend[-1]
    expert = ids[:, :TOP_K]
    dest = ((bend - nblk)[expert] * tb + ids[:, TOP_K:2 * TOP_K]).reshape(a)
    blk = jnp.minimum(jnp.arange(nb, dtype=jnp.int32), nact - 1)
    blk_e = jnp.minimum(jnp.searchsorted(bend, blk, side='right'), N_EXPERTS - 1).astype(jnp.int32)
    tok = jnp.repeat(jnp.arange(n, dtype=jnp.int32), TOP_K)
    row_tok = jnp.zeros((nb * tb,), jnp.int32).at[dest].set(tok)
    nact1 = nact.reshape(1).astype(jnp.int32)

    last = lambda b, blk_ref, nact_ref, *_: jnp.minimum(b, nact_ref[0] - 1)
    expert_w = lambda rows, cols: pl.BlockSpec((None, None, rows, cols),
                                               lambda b, blk_ref, *_: (layer, blk_ref[b], 0, 0))
    hid = pl.pallas_call(
        functools.partial(_moe_up_kernel, tb=tb, fchunk=COL_TILE),
        grid_spec=pltpu.PrefetchScalarGridSpec(
            num_scalar_prefetch=3, grid=(nb,),
            in_specs=[pl.BlockSpec(memory_space=pl.ANY), expert_w(d, f), expert_w(d, f)],
            out_specs=pl.BlockSpec((tb, f), lambda b, *s: (last(b, *s), 0)),
            scratch_shapes=[pltpu.VMEM((2, tb, ROW_CHUNKS, LANES), F32), pltpu.VMEM((tb, d), BF),
                            pltpu.SemaphoreType.DMA((2,))]),
        out_shape=jax.ShapeDtypeStruct((nb * tb, f), BF),
        compiler_params=_cparams(("arbitrary",), 52), name="moe_up",
    )(blk_e, nact1, row_tok, xn3, w_gate, w_up)

    ys3 = pl.pallas_call(
        functools.partial(_moe_down_kernel, nchunk=COL_TILE // LANES),
        grid_spec=pltpu.PrefetchScalarGridSpec(
            num_scalar_prefetch=2, grid=(nb,),
            in_specs=[pl.BlockSpec((tb, f), lambda b, *s: (last(b, *s), 0)), expert_w(f, d)],
            out_specs=pl.BlockSpec((tb, ROW_CHUNKS, LANES), lambda b, *s: (last(b, *s), 0, 0))),
        out_shape=jax.ShapeDtypeStruct((nb * tb, ROW_CHUNKS, LANES), F32),
        compiler_params=_cparams(("arbitrary",), 40), name="moe_down",
    )(blk_e, nact1, hid, w_down)

    tc = min(COMBINE_ROWS, n)
    gain = (final_gain if final_gain is not None else ffn_gain).reshape(1, d)
    return pl.pallas_call(
        functools.partial(_combine_kernel, tc=tc, final_norm=final_gain is not None),
        grid_spec=pltpu.PrefetchScalarGridSpec(
            num_scalar_prefetch=1, grid=(n // tc,),
            in_specs=[pl.BlockSpec(memory_space=pl.ANY), pl.BlockSpec((tc, d), lambda i, *_: (i, 0)),
                      pl.BlockSpec((tc, LANES), lambda i, *_: (i, 0)), pl.BlockSpec((1, d), lambda i, *_: (0, 0))],
            out_specs=pl.BlockSpec((tc, d), lambda i, *_: (i, 0)),
            scratch_shapes=[pltpu.VMEM((2, TOP_K * tc, ROW_CHUNKS, LANES), F32), pltpu.SemaphoreType.DMA((2,))]),
        out_shape=jax.ShapeDtypeStruct((n, d), F32),
        compiler_params=_cparams(("arbitrary",), 40), name="moe_combine",
    )(dest, ys3, h, wts, gain)


def even_mixer(h, gain, w_in, b_gates, pool_w, pool_scale, head_gain, w_out):
    d = h.shape[1]
    pool_width = N_POOL_GROUPS * pool_w.shape[-1]
    mlstm_width = head_gain.shape[0]
    main_cols = pool_width + 4 * mlstm_width
    n_gates = 2 * MLSTM_HEADS
    gate_w = jnp.pad(w_in[:, main_cols:], ((0, 0), (0, LANES - n_gates)))
    gate_b = jnp.pad(b_gates, (0, LANES - n_gates)).reshape(1, LANES)
    z, gates = norm_matmul(h, gain, w_in, main_cols, gate_w, gate_b)
    y_p = pool_mixer(z, pool_w, pool_scale)
    y_m = mlstm_mixer(z, gates, head_gain, pool_width)
    assert pool_width == mlstm_width and pool_width + mlstm_width == d
    return matmul_residual([y_p, y_m], w_out, h)


def odd_mixer(h, gain, w_in, conv_w, w_out):
    z = norm_matmul(h, gain, w_in, w_in.shape[1])
    return matmul_residual([conv_mixer(z, conv_w)], w_out, h)


def cross_attn(h, mem, gain, mem_gain, wq, wk, wv, wo, layer):
    d = h.shape[1]
    k = norm_matmul(mem, mem_gain, wk, d, layer=layer)
    v = norm_matmul(mem, mem_gain, wv, d, layer=layer)
    q = norm_matmul(h, gain, wq, d, layer=layer)
    return matmul_residual([attention(q, k, v)], wo, h, layer=layer)


def kernel(x, mem, mix_norm, xattn_norm, mem_norm, ffn_norm, final_norm, ev_w_in, ev_b_gates, ev_pool_w, ev_pool_scale, ev_head_norm, ev_w_out, od_w_in, od_conv_w, od_w_out, xa_wq, xa_wk, xa_wv, xa_wo, rt_group_w, rt_group_b, rt_expert_w, rt_expert_b, ex_w_gate, ex_w_up, ex_w_down):
    depth = mix_norm.shape[0]
    h = x[0]
    m = mem[0]
    for layer in range(depth):
        j = layer // 2
        if layer % 2 == 0:
            h = even_mixer(h, mix_norm[layer], ev_w_in[j], ev_b_gates[j], ev_pool_w[j], ev_pool_scale[j],
                           ev_head_norm[j], ev_w_out[j])
        else:
            h = odd_mixer(h, mix_norm[layer], od_w_in[j], od_conv_w[j], od_w_out[j])
        h = cross_attn(h, m, xattn_norm[layer], mem_norm[layer], xa_wq, xa_wk, xa_wv, xa_wo, layer)
        h = hier_moe(h, ffn_norm[layer], rt_group_w[layer], rt_group_b[layer], rt_expert_w[layer],
                     rt_expert_b[layer], ex_w_gate, ex_w_up, ex_w_down, layer,
                     final_gain=final_norm if layer == depth - 1 else None)
    return h[None]
```

```python
import functools

import jax
import jax.numpy as jnp
from jax import lax
from jax.experimental import pallas as pl
from jax.experimental.pallas import tpu as pltpu

F32 = jnp.float32
BF = jnp.bfloat16
EPS = 1e-6

POOL_WINDOWS = (2, 4, 8, 16)
N_POOL_GROUPS = 4
MLSTM_HEADS = 4
FORGET_LANE0 = MLSTM_HEADS
XATTN_HEADS = 4
N_GROUPS = 4
EXPERTS_PER_GROUP = 8
N_EXPERTS = N_GROUPS * EXPERTS_PER_GROUP
TOP_K = 2
CONV_WIDTH = 3

LANES = 128
SUBLANES = 8

ROW_TILE = 1024
COL_TILE = 1024
OUT_ROW_TILE = 512
CAST_ROWS = 256
MOE_COL_CHUNK = 256
MLSTM_CHUNK = 256
POOL_ROWS = 512
POOL_HALO = 128
CONV_ROWS = 512
CONV_COLS = 512
CONV_HALO = 16
ATTN_ROWS = 512
ROUTER_ROWS = 512
MOE_BLOCK = 256
COMBINE_ROWS = 256


def _cparams(semantics, vmem_mib):
    return pltpu.CompilerParams(dimension_semantics=semantics, vmem_limit_bytes=vmem_mib * 1024 * 1024)


def _sigmoid(x):
    return 1.0 / (1.0 + jnp.exp(-x))


def _log_sigmoid(x):
    return jnp.minimum(x, 0.0) - jnp.log(1.0 + jnp.exp(-jnp.abs(x)))


def _rms(x, g):
    ms = jnp.mean(x * x, axis=-1, keepdims=True)
    return x * lax.rsqrt(ms + EPS) * g


def _norm_mm_kernel(x_ref, g_ref, w_ref, o_ref, xn_ref):
    @pl.when(pl.program_id(1) == 0)
    def _():
        xn_ref[...] = _rms(x_ref[...], g_ref[...]).astype(BF)

    o_ref[...] = jnp.dot(xn_ref[...], w_ref[...].astype(BF), preferred_element_type=F32).astype(o_ref.dtype)


def _norm_mm_gates_kernel(x_ref, g_ref, w_ref, wg_ref, bg_ref, o_ref, gates_ref, xn_ref):
    @pl.when(pl.program_id(1) == 0)
    def _():
        xn = _rms(x_ref[...], g_ref[...]).astype(BF)
        xn_ref[...] = xn
        gates_ref[...] = jnp.dot(xn, wg_ref[...].astype(BF), preferred_element_type=F32) + bg_ref[...]

    o_ref[...] = jnp.dot(xn_ref[...], w_ref[...].astype(BF), preferred_element_type=F32).astype(o_ref.dtype)


def _stacked(w, layer):
    return (w[None], 0) if layer is None else (w, layer)


def norm_matmul(x, gain, w, n_cols, gate_w=None, gate_b=None, layer=None):
    n, k = x.shape
    w, li = _stacked(w, layer)
    tm = min(ROW_TILE, n)
    tn = COL_TILE
    grid = (n // tm, n_cols // tn)
    x_spec = pl.BlockSpec((tm, k), lambda i, j: (i, 0))
    g_spec = pl.BlockSpec((1, k), lambda i, j: (0, 0))
    w_spec = pl.BlockSpec((None, k, tn), lambda i, j: (li, 0, j))
    o_spec = pl.BlockSpec((tm, tn), lambda i, j: (i, j))
    scratch = [pltpu.VMEM((tm, k), BF)]
    gain = gain.reshape(1, k)
    if gate_w is None:
        return pl.pallas_call(
            _norm_mm_kernel, grid=grid, in_specs=[x_spec, g_spec, w_spec], out_specs=o_spec,
            out_shape=jax.ShapeDtypeStruct((n, n_cols), BF), scratch_shapes=scratch,
            compiler_params=_cparams(("parallel", "arbitrary"), 56), name="norm_matmul",
        )(x, gain, w)
    small = pl.BlockSpec((k, LANES), lambda i, j: (0, 0))
    bias = pl.BlockSpec((1, LANES), lambda i, j: (0, 0))
    return pl.pallas_call(
        _norm_mm_gates_kernel, grid=grid, in_specs=[x_spec, g_spec, w_spec, small, bias],
        out_specs=[o_spec, pl.BlockSpec((tm, LANES), lambda i, j: (i, 0))],
        out_shape=[jax.ShapeDtypeStruct((n, n_cols), BF), jax.ShapeDtypeStruct((n, LANES), F32)],
        scratch_shapes=scratch, compiler_params=_cparams(("parallel", "arbitrary"), 56), name="norm_matmul_gates",
    )(x, gain, w, gate_w, gate_b)


def _fill_bf16(dst_ref, src_ref):
    rows = src_ref.shape[0]
    for r in range(0, rows, CAST_ROWS):
        dst_ref[r:r + CAST_ROWS, :] = src_ref[r:r + CAST_ROWS, :].astype(BF)


def _mm_res_kernel(*refs, nparts):
    xs = refs[:nparts]
    w_ref, res_ref, o_ref, wb_ref = refs[nparts:]

    @pl.when(pl.program_id(0) == 0)
    def _():
        _fill_bf16(wb_ref, w_ref)

    acc = res_ref[...]
    k0 = 0
    for x_ref in xs:
        kp = x_ref.shape[1]
        acc = acc + jnp.dot(x_ref[...], wb_ref[k0:k0 + kp, :], preferred_element_type=F32)
        k0 += kp
    o_ref[...] = acc


def matmul_residual(xs, w, res, layer=None):
    n, d = res.shape
    w, li = _stacked(w, layer)
    k = w.shape[1]
    tm = min(OUT_ROW_TILE, n)
    in_specs = [pl.BlockSpec((tm, x.shape[1]), lambda i: (i, 0)) for x in xs]
    in_specs += [pl.BlockSpec((None, k, d), lambda i: (li, 0, 0), pipeline_mode=pl.Buffered(1)),
                 pl.BlockSpec((tm, d), lambda i: (i, 0))]
    return pl.pallas_call(
        functools.partial(_mm_res_kernel, nparts=len(xs)), grid=(n // tm,), in_specs=in_specs,
        out_specs=pl.BlockSpec((tm, d), lambda i: (i, 0)), out_shape=jax.ShapeDtypeStruct((n, d), F32),
        scratch_shapes=[pltpu.VMEM((k, d), BF)],
        compiler_params=_cparams(("arbitrary",), 52), name="matmul_residual",
    )(*xs, w, res)


def _pool_kernel(cur_ref, prev_ref, w_ref, sc_ref, o_ref, *, tp):
    i, j = pl.program_id(0), pl.program_id(1)
    win = lax.shift_left(jnp.int32(POOL_WINDOWS[0]), j)
    cur = cur_ref[...]
    dist = lax.broadcasted_iota(jnp.int32, (tp, tp), 0) - lax.broadcasted_iota(jnp.int32, (tp, tp), 1)
    band = jnp.where(dist >= 0, jnp.where(dist < win, 1.0, 0.0), 0.0).astype(BF)
    s = jnp.dot(band, cur, preferred_element_type=F32)
    distp = (lax.broadcasted_iota(jnp.int32, (tp, POOL_HALO), 0) + POOL_HALO
             - lax.broadcasted_iota(jnp.int32, (tp, POOL_HALO), 1))
    limit = jnp.where(i > 0, win, 0)
    bandp = jnp.where(distp < limit, 1.0, 0.0).astype(BF)
    s = s + jnp.dot(bandp, prev_ref[...], preferred_element_type=F32)
    pos = i * tp + lax.broadcasted_iota(jnp.int32, (tp, 1), 0)
    cnt = jnp.minimum(pos + 1, win).astype(F32)
    d = s / cnt - cur.astype(F32)
    y = jnp.dot(d.astype(BF), w_ref[...].astype(BF), preferred_element_type=F32) * sc_ref[...]
    o_ref[...] = y.astype(o_ref.dtype)


def pool_mixer(z, pool_w, pool_scale):
    n = z.shape[0]
    gdim = pool_w.shape[-1]
    width = N_POOL_GROUPS * gdim
    tp = min(POOL_ROWS, n)
    halo_blocks = tp // POOL_HALO
    return pl.pallas_call(
        functools.partial(_pool_kernel, tp=tp), grid=(n // tp, N_POOL_GROUPS),
        in_specs=[
            pl.BlockSpec((tp, gdim), lambda i, j: (i, j)),
            pl.BlockSpec((POOL_HALO, gdim), lambda i, j: (jnp.maximum(i * halo_blocks - 1, 0), j)),
            pl.BlockSpec((None, gdim, gdim), lambda i, j: (j, 0, 0)),
            pl.BlockSpec((1, gdim), lambda i, j: (0, j)),
        ],
        out_specs=pl.BlockSpec((tp, gdim), lambda i, j: (i, j)),
        out_shape=jax.ShapeDtypeStruct((n, width), BF),
        compiler_params=_cparams(("parallel", "arbitrary"), 32), name="pool_mixer",
    )(z, z, pool_w, pool_scale.reshape(1, width))


def _mlstm_kernel(q_ref, k_ref, v_ref, o_ref, g_ref, gain_ref, y_ref, ct_ref, m_ref, *, chunk, dh):
    c = pl.program_id(0)
    aug = dh + LANES

    @pl.when(c == 0)
    def _():
        ct_ref[...] = jnp.zeros_like(ct_ref)
        m_ref[...] = jnp.zeros_like(m_ref)

    g = g_ref[...]
    lf = _log_sigmoid(g)
    row = lax.broadcasted_iota(jnp.int32, (chunk, chunk), 0)
    col = lax.broadcasted_iota(jnp.int32, (chunk, chunk), 1)
    causal = col <= row
    ltri = jnp.where(causal, 1.0, 0.0).astype(BF)
    hi = lf.astype(BF)
    r1 = lf - hi.astype(F32)
    mid = r1.astype(BF)
    lo = (r1 - mid.astype(F32)).astype(BF)
    bcum = (jnp.dot(ltri, hi, preferred_element_type=F32) + jnp.dot(ltri, mid, preferred_element_type=F32)
            + jnp.dot(ltri, lo, preferred_element_type=F32))
    g_t = g.T
    b_t = bcum.T
    ones_col = jnp.where(lax.broadcasted_iota(jnp.int32, (chunk, LANES), 1) == 0, 1.0, 0.0).astype(BF)

    for h in range(MLSTM_HEADS):
        hs = slice(h * dh, (h + 1) * dh)
        fl = FORGET_LANE0 + h
        bc = bcum[:, fl:fl + 1]
        br = b_t[fl:fl + 1, :]
        ir = g_t[h:h + 1, :]
        b_last = bcum[chunk - 1:chunk, fl:fl + 1]
        m_prev = m_ref[h][:, 0:1]

        dmat = jnp.where(causal, bc + (ir - br), -jnp.inf)
        inter = bc + m_prev
        m_t = jnp.maximum(jnp.max(dmat, axis=1, keepdims=True), inter)
        w_inter = jnp.exp(inter - m_t)
        p = jnp.exp(dmat - m_t)

        qh = q_ref[:, hs] * (dh ** -0.5)
        kh = k_ref[:, hs]
        v_aug = jnp.concatenate([v_ref[:, hs], ones_col], axis=1)
        s = lax.dot_general(qh, kh, (((1,), (1,)), ((), ())), preferred_element_type=F32)
        sc = (s * p).astype(BF)
        ct = ct_ref[h]
        num_aug = (w_inter * jnp.dot(qh, ct.astype(BF), preferred_element_type=F32)
                   + jnp.dot(sc, v_aug, preferred_element_type=F32))
        num = num_aug[:, :dh]
        den = num_aug[:, dh:dh + 1]
        hout = num / jnp.maximum(jnp.abs(den), jnp.exp(-m_t))

        yn = _rms(hout, gain_ref[:, hs])
        y_ref[:, hs] = (_sigmoid(o_ref[:, hs].astype(F32)) * yn).astype(y_ref.dtype)

        d_end = b_last - br + ir
        m_new = jnp.maximum(b_last + m_prev, jnp.max(d_end, axis=1, keepdims=True))
        a_prev = jnp.exp(b_last + m_prev - m_new)
        a_s = jnp.exp(d_end - m_new)
        k_t = (kh.astype(F32).T * a_s).astype(BF)
        ct_ref[h] = a_prev * ct + jnp.dot(k_t, v_aug, preferred_element_type=F32)
        m_ref[h] = jnp.broadcast_to(m_new, (1, LANES))


def mlstm_mixer(z, gates, head_gain, col0):
    n = z.shape[0]
    width = head_gain.shape[0]
    dh = width // MLSTM_HEADS
    chunk = min(MLSTM_CHUNK, n)
    base = col0 // width
    qkvo = [pl.BlockSpec((chunk, width), lambda c, p=p: (c, base + p)) for p in range(4)]
    return pl.pallas_call(
        functools.partial(_mlstm_kernel, chunk=chunk, dh=dh), grid=(n // chunk,),
        in_specs=qkvo + [pl.BlockSpec((chunk, LANES), lambda c: (c, 0)), pl.BlockSpec((1, width), lambda c: (0, 0))],
        out_specs=pl.BlockSpec((chunk, width), lambda c: (c, 0)),
        out_shape=jax.ShapeDtypeStruct((n, width), BF),
        scratch_shapes=[pltpu.VMEM((MLSTM_HEADS, dh, dh + LANES), F32), pltpu.VMEM((MLSTM_HEADS, 1, LANES), F32)],
        compiler_params=_cparams(("arbitrary",), 32), name="mlstm_mixer",
    )(z, z, z, z, gates, head_gain.reshape(1, width))


def _conv_kernel(b_ref, c_ref, u_ref, cp_ref, up_ref, w_ref, o_ref, *, tr):
    i = pl.program_id(0)
    zc = c_ref[...].astype(F32) * u_ref[...].astype(F32)
    zp = cp_ref[...].astype(F32) * up_ref[...].astype(F32)
    zp = jnp.where(i > 0, zp, 0.0)
    row = lax.broadcasted_iota(jnp.int32, zc.shape, 0)
    acc = w_ref[CONV_WIDTH - 1:CONV_WIDTH, :] * zc
    for back in range(1, CONV_WIDTH):
        shifted = pltpu.roll(zc, back, 0)
        for r in range(back):
            shifted = jnp.where(row == r, zp[CONV_HALO - back + r:CONV_HALO - back + r + 1, :], shifted)
        acc = acc + w_ref[CONV_WIDTH - 1 - back:CONV_WIDTH - back, :] * shifted
    o_ref[...] = (b_ref[...].astype(F32) * acc).astype(o_ref.dtype)


def conv_mixer(z, conv_w):
    n = z.shape[0]
    d = conv_w.shape[1]
    tr = min(CONV_ROWS, n)
    tc = CONV_COLS
    nb = d // tc
    halo_blocks = tr // CONV_HALO
    prev = lambda part: pl.BlockSpec(
        (CONV_HALO, tc), lambda i, j: (jnp.maximum(i * halo_blocks - 1, 0), part * nb + j))
    cur = lambda part: pl.BlockSpec((tr, tc), lambda i, j: (i, part * nb + j))
    return pl.pallas_call(
        functools.partial(_conv_kernel, tr=tr), grid=(n // tr, nb),
        in_specs=[cur(0), cur(1), cur(2), prev(1), prev(2), pl.BlockSpec((CONV_WIDTH, tc), lambda i, j: (0, j))],
        out_specs=pl.BlockSpec((tr, tc), lambda i, j: (i, j)),
        out_shape=jax.ShapeDtypeStruct((n, d), BF),
        compiler_params=_cparams(("parallel", "arbitrary"), 32), name="conv_mixer",
    )(z, z, z, z, z, conv_w)


def _attn_kernel(q_ref, k_ref, v_ref, o_ref, *, hd):
    scale = hd ** -0.5
    for h in range(XATTN_HEADS):
        hs = slice(h * hd, (h + 1) * hd)
        s = lax.dot_general(q_ref[:, hs], k_ref[:, hs], (((1,), (1,)), ((), ())),
                            preferred_element_type=F32) * scale
        e = jnp.exp(s - jnp.max(s, axis=-1, keepdims=True))
        p = (e / jnp.sum(e, axis=-1, keepdims=True)).astype(BF)
        o_ref[:, hs] = jnp.dot(p, v_ref[:, hs], preferred_element_type=F32).astype(o_ref.dtype)


def attention(q, k, v):
    n, d = q.shape
    m = k.shape[0]
    ta = min(ATTN_ROWS, n)
    return pl.pallas_call(
        functools.partial(_attn_kernel, hd=d // XATTN_HEADS), grid=(n // ta,),
        in_specs=[pl.BlockSpec((ta, d), lambda i: (i, 0)), pl.BlockSpec((m, d), lambda i: (0, 0)),
                  pl.BlockSpec((m, d), lambda i: (0, 0))],
        out_specs=pl.BlockSpec((ta, d), lambda i: (i, 0)),
        out_shape=jax.ShapeDtypeStruct((n, d), BF),
        compiler_params=_cparams(("parallel",), 32), name="cross_attention",
    )(q, k, v)


def _router_kernel(x_ref, g_ref, wr_ref, br_ref, xn_ref, ids_ref, wts_ref, cnt_ref, carry_ref, *, tr):
    @pl.when(pl.program_id(0) == 0)
    def _():
        carry_ref[...] = jnp.zeros_like(carry_ref)

    xn = _rms(x_ref[...], g_ref[...])
    xn_ref[...] = xn
    logits = jnp.dot(xn, wr_ref[...], preferred_element_type=F32, precision=lax.Precision.HIGHEST) + br_ref[...]
    lane = lax.broadcasted_iota(jnp.int32, (tr, LANES), 1).astype(F32)
    neg = -jnp.inf

    def first_argmax(vals):
        top = jnp.max(vals, axis=-1, keepdims=True)
        return top, jnp.min(jnp.where(vals == top, lane, float(LANES)), axis=-1, keepdims=True)

    gl = jnp.where(lane < N_GROUPS, logits, neg)
    gmax, grp = first_argmax(gl)
    g_prob = 1.0 / jnp.sum(jnp.exp(gl - gmax), axis=-1, keepdims=True)
    lo = N_GROUPS + EXPERTS_PER_GROUP * grp
    el = jnp.where(lane >= lo, jnp.where(lane < lo + EXPERTS_PER_GROUP, logits, neg), neg)
    v1, l1 = first_argmax(el)
    v2, l2 = first_argmax(jnp.where(lane == l1, neg, el))
    e2 = jnp.exp(v2 - v1)
    w1 = g_prob / (1.0 + e2)
    w2 = g_prob * e2 / (1.0 + e2)
    hot1 = lane == l1
    hot2 = lane == l2
    hot = jnp.where(hot1, 1.0, jnp.where(hot2, 1.0, 0.0))
    earlier = (lax.broadcasted_iota(jnp.int32, (tr, tr), 1) < lax.broadcasted_iota(jnp.int32, (tr, tr), 0))
    before = jnp.dot(jnp.where(earlier, 1.0, 0.0).astype(BF), hot.astype(BF), preferred_element_type=F32)
    before = before + carry_ref[0:1, :]
    r1 = jnp.sum(jnp.where(hot1, before, 0.0), axis=-1, keepdims=True)
    r2 = jnp.sum(jnp.where(hot2, before, 0.0), axis=-1, keepdims=True)
    carry_ref[0:1, :] = carry_ref[0:1, :] + jnp.sum(hot, axis=0, keepdims=True)
    ids = jnp.where(lane == 0, l1 - N_GROUPS, jnp.where(lane == 1, l2 - N_GROUPS,
                    jnp.where(lane == 2, r1, jnp.where(lane == 3, r2, 0.0))))
    ids_ref[...] = ids.astype(jnp.int32)
    wts_ref[...] = jnp.where(lane == 0, w1, jnp.where(lane == 1, w2, 0.0))
    cnt_ref[...] = carry_ref[...]


def route(h, gain, w_router, b_router):
    n, d = h.shape
    tr = min(ROUTER_ROWS, n)
    return pl.pallas_call(
        functools.partial(_router_kernel, tr=tr), grid=(n // tr,),
        in_specs=[pl.BlockSpec((tr, d), lambda i: (i, 0)), pl.BlockSpec((1, d), lambda i: (0, 0)),
                  pl.BlockSpec((d, LANES), lambda i: (0, 0)), pl.BlockSpec((1, LANES), lambda i: (0, 0))],
        out_specs=[pl.BlockSpec((tr, d), lambda i: (i, 0)),
                   pl.BlockSpec((tr, LANES), lambda i: (i, 0)), pl.BlockSpec((tr, LANES), lambda i: (i, 0)),
                   pl.BlockSpec((SUBLANES, LANES), lambda i: (0, 0))],
        out_shape=[jax.ShapeDtypeStruct((n, d), F32), jax.ShapeDtypeStruct((n, LANES), jnp.int32),
                   jax.ShapeDtypeStruct((n, LANES), F32), jax.ShapeDtypeStruct((SUBLANES, LANES), F32)],
        scratch_shapes=[pltpu.VMEM((SUBLANES, LANES), F32)],
        compiler_params=_cparams(("arbitrary",), 40), name="moe_router",
    )(h, gain.reshape(1, d), w_router, b_router)


def _row_copy(src_hbm, row, dst, dst_row, sem):
    return pltpu.make_async_copy(src_hbm.at[pl.ds(row, 1), :], dst.at[pl.ds(dst_row, 1), :], sem)


def _rows_wait(src_hbm, dst, sem):
    pltpu.make_async_copy(src_hbm.at[pl.ds(0, dst.shape[0]), :], dst, sem).wait()


def _is_new_expert(blk_ref, b):
    return jnp.logical_or(b == 0, blk_ref[b] != blk_ref[jnp.maximum(b - 1, 0)])


def _moe_up_kernel(blk_ref, nact_ref, rtok_ref, xn_hbm, wg_ref, wu_ref, hid_ref, buf, wgb, wub, sem, *, tb, nb):
    b = pl.program_id(0)
    nact = nact_ref[0]
    slot = b % 2
    f = hid_ref.shape[1]
    nchunks = f // MOE_COL_CHUNK
    rows_per_chunk = tb // nchunks

    @pl.when(b == 0)
    def _():
        def body(r, carry):
            _row_copy(xn_hbm, rtok_ref[r], buf.at[0], r, sem.at[0]).start()
            return carry

        lax.fori_loop(0, tb, body, 0)

    @pl.when(jnp.logical_and(b < nact, _is_new_expert(blk_ref, b)))
    def _():
        _fill_bf16(wgb, wg_ref)
        _fill_bf16(wub, wu_ref)

    def compute(prefetch_next):
        _rows_wait(xn_hbm, buf.at[slot], sem.at[slot])
        x = buf[slot].astype(BF)
        for c in range(nchunks):
            if prefetch_next:
                for r in range(c * rows_per_chunk, (c + 1) * rows_per_chunk):
                    _row_copy(xn_hbm, rtok_ref[(b + 1) * tb + r], buf.at[1 - slot], r, sem.at[1 - slot]).start()
            cs = slice(c * MOE_COL_CHUNK, (c + 1) * MOE_COL_CHUNK)
            gate = jnp.dot(x, wgb[:, cs], preferred_element_type=F32)
            up = jnp.dot(x, wub[:, cs], preferred_element_type=F32)
            hid_ref[:, cs] = (gate * _sigmoid(gate) * up).astype(hid_ref.dtype)

    @pl.when(jnp.logical_and(b < nact, b + 1 < nb))
    def _():
        compute(True)

    @pl.when(jnp.logical_and(b < nact, b + 1 == nb))
    def _():
        compute(False)

    @pl.when(b >= nact)
    def _():
        hid_ref[...] = jnp.zeros_like(hid_ref)

    @pl.when(b == nact)
    def _():
        _rows_wait(xn_hbm, buf.at[slot], sem.at[slot])


def _moe_down_kernel(blk_ref, nact_ref, hid_ref, wd_ref, ys_ref, wdb):
    b = pl.program_id(0)
    nact = nact_ref[0]

    @pl.when(jnp.logical_and(b < nact, _is_new_expert(blk_ref, b)))
    def _():
        _fill_bf16(wdb, wd_ref)

    @pl.when(b < nact)
    def _():
        hid = hid_ref[...]
        for c in range(0, ys_ref.shape[1], MOE_COL_CHUNK):
            cs = slice(c, c + MOE_COL_CHUNK)
            ys_ref[:, cs] = jnp.dot(hid, wdb[:, cs], preferred_element_type=F32)

    @pl.when(b >= nact)
    def _():
        ys_ref[...] = jnp.zeros_like(ys_ref)


def _combine_kernel(dest_ref, ys_hbm, h_ref, w_ref, gain_ref, o_ref, buf, sem, *, tc, final_norm):
    i = pl.program_id(0)

    def issue(blk, slot):
        base = blk * tc

        def body(r, carry):
            for k in range(TOP_K):
                _row_copy(ys_hbm, dest_ref[TOP_K * (base + r) + k], buf.at[slot], k * tc + r, sem.at[slot]).start()
            return carry

        lax.fori_loop(0, tc, body, 0, unroll=8)

    @pl.when(i == 0)
    def _():
        issue(0, 0)

    @pl.when(i + 1 < pl.num_programs(0))
    def _():
        issue(i + 1, (i + 1) % 2)

    slot = i % 2
    _rows_wait(ys_hbm, buf.at[slot], sem.at[slot])
    out = h_ref[...] + (w_ref[:, 0:1] * buf[slot, 0:tc, :] + w_ref[:, 1:2] * buf[slot, tc:TOP_K * tc, :])
    if final_norm:
        out = _rms(out, gain_ref[...])
    o_ref[...] = out


def hier_moe(h, ffn_gain, wg_r, bg_r, we_r, be_r, w_gate, w_up, w_down, layer, final_gain=None):
    n, d = h.shape
    f = w_gate.shape[-1]
    a = n * TOP_K
    tb = MOE_BLOCK
    nb = a // tb + N_EXPERTS

    pad = LANES - N_GROUPS - N_EXPERTS
    w_router = jnp.concatenate([wg_r, we_r, jnp.zeros((d, pad), F32)], axis=1)
    b_router = jnp.concatenate([bg_r, be_r, jnp.zeros((pad,), F32)]).reshape(1, LANES)
    xn, ids, wts, cnt = route(h, ffn_gain, w_router, b_router)

    counts = cnt[0, N_GROUPS:N_GROUPS + N_EXPERTS].astype(jnp.int32)
    nblk = (counts + tb - 1) // tb
    bend = jnp.cumsum(nblk)
    nact = bend[-1]
    expert = ids[:, :TOP_K]
    dest = ((bend - nblk)[expert] * tb + ids[:, TOP_K:2 * TOP_K]).reshape(a)
    blk = jnp.minimum(jnp.arange(nb, dtype=jnp.int32), nact - 1)
    blk_e = jnp.minimum(jnp.searchsorted(bend, blk, side='right'), N_EXPERTS - 1).astype(jnp.int32)
    tok = jnp.repeat(jnp.arange(n, dtype=jnp.int32), TOP_K)
    row_tok = jnp.zeros((nb * tb,), jnp.int32).at[dest].set(tok)
    nact1 = nact.reshape(1).astype(jnp.int32)

    last = lambda b, blk_ref, nact_ref, *_: jnp.minimum(b, nact_ref[0] - 1)
    expert_w = lambda rows, cols: pl.BlockSpec((None, None, rows, cols),
                                               lambda b, blk_ref, *_: (layer, blk_ref[b], 0, 0))
    hid = pl.pallas_call(
        functools.partial(_moe_up_kernel, tb=tb, nb=nb),
        grid_spec=pltpu.PrefetchScalarGridSpec(
            num_scalar_prefetch=3, grid=(nb,),
            in_specs=[pl.BlockSpec(memory_space=pl.ANY), expert_w(d, f), expert_w(d, f)],
            out_specs=pl.BlockSpec((tb, f), lambda b, *_: (b, 0)),
            scratch_shapes=[pltpu.VMEM((2, tb, d), F32), pltpu.VMEM((d, f), BF), pltpu.VMEM((d, f), BF),
                            pltpu.SemaphoreType.DMA((2,))]),
        out_shape=jax.ShapeDtypeStruct((nb * tb, f), BF),
        compiler_params=_cparams(("arbitrary",), 56), name="moe_up",
    )(blk_e, nact1, row_tok, xn, w_gate, w_up)

    ys = pl.pallas_call(
        _moe_down_kernel,
        grid_spec=pltpu.PrefetchScalarGridSpec(
            num_scalar_prefetch=2, grid=(nb,),
            in_specs=[pl.BlockSpec((tb, f), lambda b, *s: (last(b, *s), 0)), expert_w(f, d)],
            out_specs=pl.BlockSpec((tb, d), lambda b, *_: (b, 0)),
            scratch_shapes=[pltpu.VMEM((f, d), BF)]),
        out_shape=jax.ShapeDtypeStruct((nb * tb, d), F32),
        compiler_params=_cparams(("arbitrary",), 40), name="moe_down",
    )(blk_e, nact1, hid, w_down)

    tc = min(COMBINE_ROWS, n)
    gain = (final_gain if final_gain is not None else ffn_gain).reshape(1, d)
    return pl.pallas_call(
        functools.partial(_combine_kernel, tc=tc, final_norm=final_gain is not None),
        grid_spec=pltpu.PrefetchScalarGridSpec(
            num_scalar_prefetch=1, grid=(n // tc,),
            in_specs=[pl.BlockSpec(memory_space=pl.ANY), pl.BlockSpec((tc, d), lambda i, *_: (i, 0)),
                      pl.BlockSpec((tc, LANES), lambda i, *_: (i, 0)), pl.BlockSpec((1, d), lambda i, *_: (0, 0))],
            out_specs=pl.BlockSpec((tc, d), lambda i, *_: (i, 0)),
            scratch_shapes=[pltpu.VMEM((2, TOP_K * tc, d), F32), pltpu.SemaphoreType.DMA((2,))]),
        out_shape=jax.ShapeDtypeStruct((n, d), F32),
        compiler_params=_cparams(("arbitrary",), 40), name="moe_combine",
    )(dest, ys, h, wts, gain)


def even_mixer(h, gain, w_in, b_gates, pool_w, pool_scale, head_gain, w_out):
    d = h.shape[1]
    pool_width = N_POOL_GROUPS * pool_w.shape[-1]
    mlstm_width = head_gain.shape[0]
    main_cols = pool_width + 4 * mlstm_width
    n_gates = 2 * MLSTM_HEADS
    gate_w = jnp.pad(w_in[:, main_cols:], ((0, 0), (0, LANES - n_gates)))
    gate_b = jnp.pad(b_gates, (0, LANES - n_gates)).reshape(1, LANES)
    z, gates = norm_matmul(h, gain, w_in, main_cols, gate_w, gate_b)
    y_p = pool_mixer(z, pool_w, pool_scale)
    y_m = mlstm_mixer(z, gates, head_gain, pool_width)
    assert pool_width == mlstm_width and pool_width + mlstm_width == d
    return matmul_residual([y_p, y_m], w_out, h)


def odd_mixer(h, gain, w_in, conv_w, w_out):
    z = norm_matmul(h, gain, w_in, w_in.shape[1])
    return matmul_residual([conv_mixer(z, conv_w)], w_out, h)


def cross_attn(h, mem, gain, mem_gain, wq, wk, wv, wo, layer):
    d = h.shape[1]
    k = norm_matmul(mem, mem_gain, wk, d, layer=layer)
    v = norm_matmul(mem, mem_gain, wv, d, layer=layer)
    q = norm_matmul(h, gain, wq, d, layer=layer)
    return matmul_residual([attention(q, k, v)], wo, h, layer=layer)


def kernel(x, mem, mix_norm, xattn_norm, mem_norm, ffn_norm, final_norm, ev_w_in, ev_b_gates, ev_pool_w, ev_pool_scale, ev_head_norm, ev_w_out, od_w_in, od_conv_w, od_w_out, xa_wq, xa_wk, xa_wv, xa_wo, rt_group_w, rt_group_b, rt_expert_w, rt_expert_b, ex_w_gate, ex_w_up, ex_w_down):
    depth = mix_norm.shape[0]
    h = x[0]
    m = mem[0]
    for layer in range(depth):
        j = layer // 2
        if layer % 2 == 0:
            h = even_mixer(h, mix_norm[layer], ev_w_in[j], ev_b_gates[j], ev_pool_w[j], ev_pool_scale[j],
                           ev_head_norm[j], ev_w_out[j])
        else:
            h = odd_mixer(h, mix_norm[layer], od_w_in[j], od_conv_w[j], od_w_out[j])
        h = cross_attn(h, m, xattn_norm[layer], mem_norm[layer], xa_wq, xa_wk, xa_wv, xa_wo, layer)
        h = hier_moe(h, ffn_norm[layer], rt_group_w[layer], rt_group_b[layer], rt_expert_w[layer],
                     rt_expert_b[layer], ex_w_gate, ex_w_up, ex_w_down, layer,
                     final_gain=final_norm if layer == depth - 1 else None)
    return h[None]
```

```python
import functools

import jax
import jax.numpy as jnp
from jax import lax
from jax.experimental import pallas as pl
from jax.experimental.pallas import tpu as pltpu

F32 = jnp.float32
BF = jnp.bfloat16
EPS = 1e-6

POOL_WINDOWS = (2, 4, 8, 16)
N_POOL_GROUPS = 4
MLSTM_HEADS = 4
FORGET_LANE0 = MLSTM_HEADS
XATTN_HEADS = 4
N_GROUPS = 4
EXPERTS_PER_GROUP = 8
N_EXPERTS = N_GROUPS * EXPERTS_PER_GROUP
TOP_K = 2
CONV_WIDTH = 3

LANES = 128
SUBLANES = 8

ROW_TILE = 1024
COL_TILE = 1024
OUT_ROW_TILE = 512
CAST_ROWS = 64
MOE_COL_CHUNK = 256
WEIGHT_DMA_PARTS = 4
MLSTM_CHUNK = 256
POOL_ROWS = 512
POOL_HALO = 128
CONV_ROWS = 512
CONV_COLS = 512
CONV_HALO = 16
ATTN_ROWS = 512
ROUTER_ROWS = 512
MOE_BLOCK = 256
COMBINE_ROWS = 256


def _cparams(semantics, vmem_mib):
    return pltpu.CompilerParams(dimension_semantics=semantics, vmem_limit_bytes=vmem_mib * 1024 * 1024)


def _sigmoid(x):
    return 1.0 / (1.0 + jnp.exp(-x))


def _log_sigmoid(x):
    return jnp.minimum(x, 0.0) - jnp.log(1.0 + jnp.exp(-jnp.abs(x)))


def _rms(x, g):
    ms = jnp.mean(x * x, axis=-1, keepdims=True)
    return x * lax.rsqrt(ms + EPS) * g


def _norm_mm_kernel(x_ref, g_ref, w_ref, o_ref, xn_ref):
    @pl.when(pl.program_id(1) == 0)
    def _():
        xn_ref[...] = _rms(x_ref[...], g_ref[...]).astype(BF)

    o_ref[...] = jnp.dot(xn_ref[...], w_ref[...].astype(BF), preferred_element_type=F32).astype(o_ref.dtype)


def _norm_mm_gates_kernel(x_ref, g_ref, w_ref, wg_ref, bg_ref, o_ref, gates_ref, xn_ref):
    @pl.when(pl.program_id(1) == 0)
    def _():
        xn = _rms(x_ref[...], g_ref[...]).astype(BF)
        xn_ref[...] = xn
        gates_ref[...] = jnp.dot(xn, wg_ref[...].astype(BF), preferred_element_type=F32) + bg_ref[...]

    o_ref[...] = jnp.dot(xn_ref[...], w_ref[...].astype(BF), preferred_element_type=F32).astype(o_ref.dtype)


def _stacked(w, layer):
    return (w[None], 0) if layer is None else (w, layer)


def norm_matmul(x, gain, w, n_cols, gate_w=None, gate_b=None, layer=None):
    n, k = x.shape
    w, li = _stacked(w, layer)
    tm = min(ROW_TILE, n)
    tn = COL_TILE
    grid = (n // tm, n_cols // tn)
    x_spec = pl.BlockSpec((tm, k), lambda i, j: (i, 0))
    g_spec = pl.BlockSpec((1, k), lambda i, j: (0, 0))
    w_spec = pl.BlockSpec((None, k, tn), lambda i, j: (li, 0, j))
    o_spec = pl.BlockSpec((tm, tn), lambda i, j: (i, j))
    scratch = [pltpu.VMEM((tm, k), BF)]
    gain = gain.reshape(1, k)
    if gate_w is None:
        return pl.pallas_call(
            _norm_mm_kernel, grid=grid, in_specs=[x_spec, g_spec, w_spec], out_specs=o_spec,
            out_shape=jax.ShapeDtypeStruct((n, n_cols), BF), scratch_shapes=scratch,
            compiler_params=_cparams(("parallel", "arbitrary"), 56), name="norm_matmul",
        )(x, gain, w)
    small = pl.BlockSpec((k, LANES), lambda i, j: (0, 0))
    bias = pl.BlockSpec((1, LANES), lambda i, j: (0, 0))
    return pl.pallas_call(
        _norm_mm_gates_kernel, grid=grid, in_specs=[x_spec, g_spec, w_spec, small, bias],
        out_specs=[o_spec, pl.BlockSpec((tm, LANES), lambda i, j: (i, 0))],
        out_shape=[jax.ShapeDtypeStruct((n, n_cols), BF), jax.ShapeDtypeStruct((n, LANES), F32)],
        scratch_shapes=scratch, compiler_params=_cparams(("parallel", "arbitrary"), 56), name="norm_matmul_gates",
    )(x, gain, w, gate_w, gate_b)


def _fill_bf16(dst_ref, src_ref):
    def body(i, carry):
        r = pl.multiple_of(i * CAST_ROWS, CAST_ROWS)
        dst_ref[pl.ds(r, CAST_ROWS), :] = src_ref[pl.ds(r, CAST_ROWS), :].astype(BF)
        return carry

    lax.fori_loop(0, src_ref.shape[0] // CAST_ROWS, body, 0)


def _mm_res_kernel(*refs, nparts):
    xs = refs[:nparts]
    w_ref, res_ref, o_ref, wb_ref = refs[nparts:]

    @pl.when(pl.program_id(0) == 0)
    def _():
        _fill_bf16(wb_ref, w_ref)

    acc = res_ref[...]
    k0 = 0
    for x_ref in xs:
        kp = x_ref.shape[1]
        acc = acc + jnp.dot(x_ref[...], wb_ref[k0:k0 + kp, :], preferred_element_type=F32)
        k0 += kp
    o_ref[...] = acc


def matmul_residual(xs, w, res, layer=None):
    n, d = res.shape
    w, li = _stacked(w, layer)
    k = w.shape[1]
    tm = min(OUT_ROW_TILE, n)
    in_specs = [pl.BlockSpec((tm, x.shape[1]), lambda i: (i, 0)) for x in xs]
    in_specs += [pl.BlockSpec((None, k, d), lambda i: (li, 0, 0), pipeline_mode=pl.Buffered(1)),
                 pl.BlockSpec((tm, d), lambda i: (i, 0))]
    return pl.pallas_call(
        functools.partial(_mm_res_kernel, nparts=len(xs)), grid=(n // tm,), in_specs=in_specs,
        out_specs=pl.BlockSpec((tm, d), lambda i: (i, 0)), out_shape=jax.ShapeDtypeStruct((n, d), F32),
        scratch_shapes=[pltpu.VMEM((k, d), BF)],
        compiler_params=_cparams(("arbitrary",), 52), name="matmul_residual",
    )(*xs, w, res)


def _pool_kernel(cur_ref, prev_ref, w_ref, sc_ref, o_ref, *, tp):
    i, j = pl.program_id(0), pl.program_id(1)
    win = lax.shift_left(jnp.int32(POOL_WINDOWS[0]), j)
    cur = cur_ref[...]
    dist = lax.broadcasted_iota(jnp.int32, (tp, tp), 0) - lax.broadcasted_iota(jnp.int32, (tp, tp), 1)
    band = jnp.where(dist >= 0, jnp.where(dist < win, 1.0, 0.0), 0.0).astype(BF)
    s = jnp.dot(band, cur, preferred_element_type=F32)
    distp = (lax.broadcasted_iota(jnp.int32, (tp, POOL_HALO), 0) + POOL_HALO
             - lax.broadcasted_iota(jnp.int32, (tp, POOL_HALO), 1))
    limit = jnp.where(i > 0, win, 0)
    bandp = jnp.where(distp < limit, 1.0, 0.0).astype(BF)
    s = s + jnp.dot(bandp, prev_ref[...], preferred_element_type=F32)
    pos = i * tp + lax.broadcasted_iota(jnp.int32, (tp, 1), 0)
    cnt = jnp.minimum(pos + 1, win).astype(F32)
    d = s / cnt - cur.astype(F32)
    y = jnp.dot(d.astype(BF), w_ref[...].astype(BF), preferred_element_type=F32) * sc_ref[...]
    o_ref[...] = y.astype(o_ref.dtype)


def pool_mixer(z, pool_w, pool_scale):
    n = z.shape[0]
    gdim = pool_w.shape[-1]
    width = N_POOL_GROUPS * gdim
    tp = min(POOL_ROWS, n)
    halo_blocks = tp // POOL_HALO
    return pl.pallas_call(
        functools.partial(_pool_kernel, tp=tp), grid=(n // tp, N_POOL_GROUPS),
        in_specs=[
            pl.BlockSpec((tp, gdim), lambda i, j: (i, j)),
            pl.BlockSpec((POOL_HALO, gdim), lambda i, j: (jnp.maximum(i * halo_blocks - 1, 0), j)),
            pl.BlockSpec((None, gdim, gdim), lambda i, j: (j, 0, 0)),
            pl.BlockSpec((1, gdim), lambda i, j: (0, j)),
        ],
        out_specs=pl.BlockSpec((tp, gdim), lambda i, j: (i, j)),
        out_shape=jax.ShapeDtypeStruct((n, width), BF),
        compiler_params=_cparams(("parallel", "arbitrary"), 32), name="pool_mixer",
    )(z, z, pool_w, pool_scale.reshape(1, width))


def _mlstm_kernel(q_ref, k_ref, v_ref, o_ref, g_ref, gain_ref, y_ref, ct_ref, m_ref, *, chunk, dh):
    c = pl.program_id(0)
    aug = dh + LANES

    @pl.when(c == 0)
    def _():
        ct_ref[...] = jnp.zeros_like(ct_ref)
        m_ref[...] = jnp.zeros_like(m_ref)

    g = g_ref[...]
    lf = _log_sigmoid(g)
    row = lax.broadcasted_iota(jnp.int32, (chunk, chunk), 0)
    col = lax.broadcasted_iota(jnp.int32, (chunk, chunk), 1)
    causal = col <= row
    ltri = jnp.where(causal, 1.0, 0.0).astype(BF)
    hi = lf.astype(BF)
    r1 = lf - hi.astype(F32)
    mid = r1.astype(BF)
    lo = (r1 - mid.astype(F32)).astype(BF)
    bcum = (jnp.dot(ltri, hi, preferred_element_type=F32) + jnp.dot(ltri, mid, preferred_element_type=F32)
            + jnp.dot(ltri, lo, preferred_element_type=F32))
    g_t = g.T
    b_t = bcum.T
    ones_col = jnp.where(lax.broadcasted_iota(jnp.int32, (chunk, LANES), 1) == 0, 1.0, 0.0).astype(BF)

    for h in range(MLSTM_HEADS):
        hs = slice(h * dh, (h + 1) * dh)
        fl = FORGET_LANE0 + h
        bc = bcum[:, fl:fl + 1]
        br = b_t[fl:fl + 1, :]
        ir = g_t[h:h + 1, :]
        b_last = bcum[chunk - 1:chunk, fl:fl + 1]
        m_prev = m_ref[h][:, 0:1]

        dmat = jnp.where(causal, bc + (ir - br), -jnp.inf)
        inter = bc + m_prev
        m_t = jnp.maximum(jnp.max(dmat, axis=1, keepdims=True), inter)
        w_inter = jnp.exp(inter - m_t)
        p = jnp.exp(dmat - m_t)

        qh = q_ref[:, hs] * (dh ** -0.5)
        kh = k_ref[:, hs]
        v_aug = jnp.concatenate([v_ref[:, hs], ones_col], axis=1)
        s = lax.dot_general(qh, kh, (((1,), (1,)), ((), ())), preferred_element_type=F32)
        sc = (s * p).astype(BF)
        ct = ct_ref[h]
        num_aug = (w_inter * jnp.dot(qh, ct.astype(BF), preferred_element_type=F32)
                   + jnp.dot(sc, v_aug, preferred_element_type=F32))
        num = num_aug[:, :dh]
        den = num_aug[:, dh:dh + 1]
        hout = num / jnp.maximum(jnp.abs(den), jnp.exp(-m_t))

        yn = _rms(hout, gain_ref[:, hs])
        y_ref[:, hs] = (_sigmoid(o_ref[:, hs].astype(F32)) * yn).astype(y_ref.dtype)

        d_end = b_last - br + ir
        m_new = jnp.maximum(b_last + m_prev, jnp.max(d_end, axis=1, keepdims=True))
        a_prev = jnp.exp(b_last + m_prev - m_new)
        a_s = jnp.exp(d_end - m_new)
        k_t = (kh.astype(F32).T * a_s).astype(BF)
        ct_ref[h] = a_prev * ct + jnp.dot(k_t, v_aug, preferred_element_type=F32)
        m_ref[h] = jnp.broadcast_to(m_new, (1, LANES))


def mlstm_mixer(z, gates, head_gain, col0):
    n = z.shape[0]
    width = head_gain.shape[0]
    dh = width // MLSTM_HEADS
    chunk = min(MLSTM_CHUNK, n)
    base = col0 // width
    qkvo = [pl.BlockSpec((chunk, width), lambda c, p=p: (c, base + p)) for p in range(4)]
    return pl.pallas_call(
        functools.partial(_mlstm_kernel, chunk=chunk, dh=dh), grid=(n // chunk,),
        in_specs=qkvo + [pl.BlockSpec((chunk, LANES), lambda c: (c, 0)), pl.BlockSpec((1, width), lambda c: (0, 0))],
        out_specs=pl.BlockSpec((chunk, width), lambda c: (c, 0)),
        out_shape=jax.ShapeDtypeStruct((n, width), BF),
        scratch_shapes=[pltpu.VMEM((MLSTM_HEADS, dh, dh + LANES), F32), pltpu.VMEM((MLSTM_HEADS, 1, LANES), F32)],
        compiler_params=_cparams(("arbitrary",), 32), name="mlstm_mixer",
    )(z, z, z, z, gates, head_gain.reshape(1, width))


def _conv_kernel(b_ref, c_ref, u_ref, cp_ref, up_ref, w_ref, o_ref, *, tr):
    i = pl.program_id(0)
    zc = c_ref[...].astype(F32) * u_ref[...].astype(F32)
    zp = cp_ref[...].astype(F32) * up_ref[...].astype(F32)
    zp = jnp.where(i > 0, zp, 0.0)
    row = lax.broadcasted_iota(jnp.int32, zc.shape, 0)
    acc = w_ref[CONV_WIDTH - 1:CONV_WIDTH, :] * zc
    for back in range(1, CONV_WIDTH):
        shifted = pltpu.roll(zc, back, 0)
        for r in range(back):
            shifted = jnp.where(row == r, zp[CONV_HALO - back + r:CONV_HALO - back + r + 1, :], shifted)
        acc = acc + w_ref[CONV_WIDTH - 1 - back:CONV_WIDTH - back, :] * shifted
    o_ref[...] = (b_ref[...].astype(F32) * acc).astype(o_ref.dtype)


def conv_mixer(z, conv_w):
    n = z.shape[0]
    d = conv_w.shape[1]
    tr = min(CONV_ROWS, n)
    tc = CONV_COLS
    nb = d // tc
    halo_blocks = tr // CONV_HALO
    prev = lambda part: pl.BlockSpec(
        (CONV_HALO, tc), lambda i, j: (jnp.maximum(i * halo_blocks - 1, 0), part * nb + j))
    cur = lambda part: pl.BlockSpec((tr, tc), lambda i, j: (i, part * nb + j))
    return pl.pallas_call(
        functools.partial(_conv_kernel, tr=tr), grid=(n // tr, nb),
        in_specs=[cur(0), cur(1), cur(2), prev(1), prev(2), pl.BlockSpec((CONV_WIDTH, tc), lambda i, j: (0, j))],
        out_specs=pl.BlockSpec((tr, tc), lambda i, j: (i, j)),
        out_shape=jax.ShapeDtypeStruct((n, d), BF),
        compiler_params=_cparams(("parallel", "arbitrary"), 32), name="conv_mixer",
    )(z, z, z, z, z, conv_w)


def _attn_kernel(q_ref, k_ref, v_ref, o_ref, *, hd):
    scale = hd ** -0.5
    for h in range(XATTN_HEADS):
        hs = slice(h * hd, (h + 1) * hd)
        s = lax.dot_general(q_ref[:, hs], k_ref[:, hs], (((1,), (1,)), ((), ())),
                            preferred_element_type=F32) * scale
        e = jnp.exp(s - jnp.max(s, axis=-1, keepdims=True))
        p = (e / jnp.sum(e, axis=-1, keepdims=True)).astype(BF)
        o_ref[:, hs] = jnp.dot(p, v_ref[:, hs], preferred_element_type=F32).astype(o_ref.dtype)


def attention(q, k, v):
    n, d = q.shape
    m = k.shape[0]
    ta = min(ATTN_ROWS, n)
    return pl.pallas_call(
        functools.partial(_attn_kernel, hd=d // XATTN_HEADS), grid=(n // ta,),
        in_specs=[pl.BlockSpec((ta, d), lambda i: (i, 0)), pl.BlockSpec((m, d), lambda i: (0, 0)),
                  pl.BlockSpec((m, d), lambda i: (0, 0))],
        out_specs=pl.BlockSpec((ta, d), lambda i: (i, 0)),
        out_shape=jax.ShapeDtypeStruct((n, d), BF),
        compiler_params=_cparams(("parallel",), 32), name="cross_attention",
    )(q, k, v)


def _router_kernel(x_ref, g_ref, wr_ref, br_ref, xn_ref, ids_ref, wts_ref, cnt_ref, carry_ref, *, tr):
    @pl.when(pl.program_id(0) == 0)
    def _():
        carry_ref[...] = jnp.zeros_like(carry_ref)

    xn = _rms(x_ref[...], g_ref[...])
    xn_ref[...] = xn
    logits = jnp.dot(xn, wr_ref[...], preferred_element_type=F32, precision=lax.Precision.HIGHEST) + br_ref[...]
    lane = lax.broadcasted_iota(jnp.int32, (tr, LANES), 1).astype(F32)
    neg = -jnp.inf

    def first_argmax(vals):
        top = jnp.max(vals, axis=-1, keepdims=True)
        return top, jnp.min(jnp.where(vals == top, lane, float(LANES)), axis=-1, keepdims=True)

    gl = jnp.where(lane < N_GROUPS, logits, neg)
    gmax, grp = first_argmax(gl)
    g_prob = 1.0 / jnp.sum(jnp.exp(gl - gmax), axis=-1, keepdims=True)
    lo = N_GROUPS + EXPERTS_PER_GROUP * grp
    el = jnp.where(lane >= lo, jnp.where(lane < lo + EXPERTS_PER_GROUP, logits, neg), neg)
    v1, l1 = first_argmax(el)
    v2, l2 = first_argmax(jnp.where(lane == l1, neg, el))
    e2 = jnp.exp(v2 - v1)
    w1 = g_prob / (1.0 + e2)
    w2 = g_prob * e2 / (1.0 + e2)
    hot1 = lane == l1
    hot2 = lane == l2
    hot = jnp.where(hot1, 1.0, jnp.where(hot2, 1.0, 0.0))
    earlier = (lax.broadcasted_iota(jnp.int32, (tr, tr), 1) < lax.broadcasted_iota(jnp.int32, (tr, tr), 0))
    before = jnp.dot(jnp.where(earlier, 1.0, 0.0).astype(BF), hot.astype(BF), preferred_element_type=F32)
    before = before + carry_ref[0:1, :]
    r1 = jnp.sum(jnp.where(hot1, before, 0.0), axis=-1, keepdims=True)
    r2 = jnp.sum(jnp.where(hot2, before, 0.0), axis=-1, keepdims=True)
    carry_ref[0:1, :] = carry_ref[0:1, :] + jnp.sum(hot, axis=0, keepdims=True)
    ids = jnp.where(lane == 0, l1 - N_GROUPS, jnp.where(lane == 1, l2 - N_GROUPS,
                    jnp.where(lane == 2, r1, jnp.where(lane == 3, r2, 0.0))))
    ids_ref[...] = ids.astype(jnp.int32)
    wts_ref[...] = jnp.where(lane == 0, w1, jnp.where(lane == 1, w2, 0.0))
    cnt_ref[...] = carry_ref[...]


def route(h, gain, w_router, b_router):
    n, d = h.shape
    tr = min(ROUTER_ROWS, n)
    return pl.pallas_call(
        functools.partial(_router_kernel, tr=tr), grid=(n // tr,),
        in_specs=[pl.BlockSpec((tr, d), lambda i: (i, 0)), pl.BlockSpec((1, d), lambda i: (0, 0)),
                  pl.BlockSpec((d, LANES), lambda i: (0, 0)), pl.BlockSpec((1, LANES), lambda i: (0, 0))],
        out_specs=[pl.BlockSpec((tr, d), lambda i: (i, 0)),
                   pl.BlockSpec((tr, LANES), lambda i: (i, 0)), pl.BlockSpec((tr, LANES), lambda i: (i, 0)),
                   pl.BlockSpec((SUBLANES, LANES), lambda i: (0, 0))],
        out_shape=[jax.ShapeDtypeStruct((n, d), F32), jax.ShapeDtypeStruct((n, LANES), jnp.int32),
                   jax.ShapeDtypeStruct((n, LANES), F32), jax.ShapeDtypeStruct((SUBLANES, LANES), F32)],
        scratch_shapes=[pltpu.VMEM((SUBLANES, LANES), F32)],
        compiler_params=_cparams(("arbitrary",), 40), name="moe_router",
    )(h, gain.reshape(1, d), w_router, b_router)


def _row_copy(src_hbm, row, dst, dst_row, sem):
    return pltpu.make_async_copy(src_hbm.at[pl.ds(row, 1), :], dst.at[pl.ds(dst_row, 1), :], sem)


def _rows_wait(src_hbm, dst, sem):
    pltpu.make_async_copy(src_hbm.at[pl.ds(0, dst.shape[0]), :], dst, sem).wait()


def _is_new_expert(blk_ref, b):
    return jnp.logical_or(b == 0, blk_ref[b] != blk_ref[jnp.maximum(b - 1, 0)])


def _weight_copies(w_hbm, layer, expert, stage, sem):
    rows = stage.shape[0] // WEIGHT_DMA_PARTS
    return [pltpu.make_async_copy(w_hbm.at[layer, expert, pl.ds(p * rows, rows), :],
                                  stage.at[pl.ds(p * rows, rows), :], sem) for p in range(WEIGHT_DMA_PARTS)]


def _moe_expert_kernel(blk_ref, nxt_ref, nact_ref, rtok_ref, xn_hbm, wg_hbm, wu_hbm, wd_hbm, ys_ref,
                       buf, stage_g, stage_u, stage_d, wgb, wub, wdb, xb, hid, sem, wsem, *, tb, nb, layer):
    b = pl.program_id(0)
    nact = nact_ref[0]
    slot = b % 2
    f = hid.shape[1]
    d = ys_ref.shape[1]
    up_chunks = f // MOE_COL_CHUNK
    rows_per_chunk = tb // up_chunks

    def expert_copies(e):
        return (_weight_copies(wg_hbm, layer, e, stage_g, wsem.at[0])
                + _weight_copies(wu_hbm, layer, e, stage_u, wsem.at[1])
                + _weight_copies(wd_hbm, layer, e, stage_d, wsem.at[2]))

    @pl.when(b == 0)
    def _():
        for cp in expert_copies(blk_ref[0]):
            cp.start()

        def body(r, carry):
            _row_copy(xn_hbm, rtok_ref[r], buf.at[0], r, sem.at[0]).start()
            return carry

        lax.fori_loop(0, tb, body, 0, unroll=8)

    @pl.when(jnp.logical_and(b < nact, _is_new_expert(blk_ref, b)))
    def _():
        for cp in expert_copies(blk_ref[b]):
            cp.wait()
        _fill_bf16(wgb, stage_g)
        _fill_bf16(wub, stage_u)
        _fill_bf16(wdb, stage_d)

        @pl.when(nxt_ref[b] >= 0)
        def _():
            for cp in expert_copies(nxt_ref[b]):
                cp.start()

    def compute(prefetch_next):
        _rows_wait(xn_hbm, buf.at[slot], sem.at[slot])
        xb[...] = buf[slot].astype(BF)
        for c in range(up_chunks):
            if prefetch_next:
                for r in range(c * rows_per_chunk, (c + 1) * rows_per_chunk):
                    _row_copy(xn_hbm, rtok_ref[(b + 1) * tb + r], buf.at[1 - slot], r, sem.at[1 - slot]).start()
            cs = slice(c * MOE_COL_CHUNK, (c + 1) * MOE_COL_CHUNK)
            gate = jnp.dot(xb[...], wgb[:, cs], preferred_element_type=F32)
            up = jnp.dot(xb[...], wub[:, cs], preferred_element_type=F32)
            hid[:, cs] = (gate * _sigmoid(gate) * up).astype(hid.dtype)
        for c in range(0, d, MOE_COL_CHUNK):
            cs = slice(c, c + MOE_COL_CHUNK)
            ys_ref[:, cs] = jnp.dot(hid[...], wdb[:, cs], preferred_element_type=F32)

    @pl.when(jnp.logical_and(b < nact, b + 1 < nb))
    def _():
        compute(True)

    @pl.when(jnp.logical_and(b < nact, b + 1 == nb))
    def _():
        compute(False)

    @pl.when(b >= nact)
    def _():
        ys_ref[...] = jnp.zeros_like(ys_ref)

    @pl.when(b == nact)
    def _():
        _rows_wait(xn_hbm, buf.at[slot], sem.at[slot])


def _combine_kernel(dest_ref, ys_hbm, h_ref, w_ref, gain_ref, o_ref, buf, sem, *, tc, final_norm):
    i = pl.program_id(0)

    def issue(blk, slot):
        base = blk * tc

        def body(r, carry):
            for k in range(TOP_K):
                _row_copy(ys_hbm, dest_ref[TOP_K * (base + r) + k], buf.at[slot], k * tc + r, sem.at[slot]).start()
            return carry

        lax.fori_loop(0, tc, body, 0, unroll=8)

    @pl.when(i == 0)
    def _():
        issue(0, 0)

    @pl.when(i + 1 < pl.num_programs(0))
    def _():
        issue(i + 1, (i + 1) % 2)

    slot = i % 2
    _rows_wait(ys_hbm, buf.at[slot], sem.at[slot])
    out = h_ref[...] + (w_ref[:, 0:1] * buf[slot, 0:tc, :] + w_ref[:, 1:2] * buf[slot, tc:TOP_K * tc, :])
    if final_norm:
        out = _rms(out, gain_ref[...])
    o_ref[...] = out


def hier_moe(h, ffn_gain, wg_r, bg_r, we_r, be_r, w_gate, w_up, w_down, layer, final_gain=None):
    n, d = h.shape
    f = w_gate.shape[-1]
    a = n * TOP_K
    tb = MOE_BLOCK
    nb = a // tb + N_EXPERTS

    pad = LANES - N_GROUPS - N_EXPERTS
    w_router = jnp.concatenate([wg_r, we_r, jnp.zeros((d, pad), F32)], axis=1)
    b_router = jnp.concatenate([bg_r, be_r, jnp.zeros((pad,), F32)]).reshape(1, LANES)
    xn, ids, wts, cnt = route(h, ffn_gain, w_router, b_router)

    counts = cnt[0, N_GROUPS:N_GROUPS + N_EXPERTS].astype(jnp.int32)
    nblk = (counts + tb - 1) // tb
    bend = jnp.cumsum(nblk)
    nact = bend[-1]
    expert = ids[:, :TOP_K]
    dest = ((bend - nblk)[expert] * tb + ids[:, TOP_K:2 * TOP_K]).reshape(a)
    blk = jnp.minimum(jnp.arange(nb, dtype=jnp.int32), jnp.maximum(nact - 1, 0))
    owner = lambda bi: jnp.minimum(jnp.searchsorted(bend, bi, side='right'), N_EXPERTS - 1).astype(jnp.int32)
    blk_e = owner(blk)
    after = bend[blk_e]
    nxt_e = jnp.where(after < nact, owner(after), -1)
    tok = jnp.repeat(jnp.arange(n, dtype=jnp.int32), TOP_K)
    row_tok = jnp.zeros((nb * tb,), jnp.int32).at[dest].set(tok)
    nact1 = nact.reshape(1).astype(jnp.int32)

    any_space = pl.BlockSpec(memory_space=pl.ANY)
    ys = pl.pallas_call(
        functools.partial(_moe_expert_kernel, tb=tb, nb=nb, layer=layer),
        grid_spec=pltpu.PrefetchScalarGridSpec(
            num_scalar_prefetch=4, grid=(nb,),
            in_specs=[any_space, any_space, any_space, any_space],
            out_specs=pl.BlockSpec((tb, d), lambda b, *_: (b, 0)),
            scratch_shapes=[pltpu.VMEM((2, tb, d), F32),
                            pltpu.VMEM((d, f), F32), pltpu.VMEM((d, f), F32), pltpu.VMEM((f, d), F32),
                            pltpu.VMEM((d, f), BF), pltpu.VMEM((d, f), BF), pltpu.VMEM((f, d), BF),
                            pltpu.VMEM((tb, d), BF), pltpu.VMEM((tb, f), BF),
                            pltpu.SemaphoreType.DMA((2,)), pltpu.SemaphoreType.DMA((3,))]),
        out_shape=jax.ShapeDtypeStruct((nb * tb, d), F32),
        compiler_params=_cparams(("arbitrary",), 58), name="moe_experts",
    )(blk_e, nxt_e, nact1, row_tok, xn, w_gate, w_up, w_down)

    tc = min(COMBINE_ROWS, n)
    gain = (final_gain if final_gain is not None else ffn_gain).reshape(1, d)
    return pl.pallas_call(
        functools.partial(_combine_kernel, tc=tc, final_norm=final_gain is not None),
        grid_spec=pltpu.PrefetchScalarGridSpec(
            num_scalar_prefetch=1, grid=(n // tc,),
            in_specs=[pl.BlockSpec(memory_space=pl.ANY), pl.BlockSpec((tc, d), lambda i, *_: (i, 0)),
                      pl.BlockSpec((tc, LANES), lambda i, *_: (i, 0)), pl.BlockSpec((1, d), lambda i, *_: (0, 0))],
            out_specs=pl.BlockSpec((tc, d), lambda i, *_: (i, 0)),
            scratch_shapes=[pltpu.VMEM((2, TOP_K * tc, d), F32), pltpu.SemaphoreType.DMA((2,))]),
        out_shape=jax.ShapeDtypeStruct((n, d), F32),
        compiler_params=_cparams(("arbitrary",), 40), name="moe_combine",
    )(dest, ys, h, wts, gain)


def even_mixer(h, gain, w_in, b_gates, pool_w, pool_scale, head_gain, w_out):
    d = h.shape[1]
    pool_width = N_POOL_GROUPS * pool_w.shape[-1]
    mlstm_width = head_gain.shape[0]
    main_cols = pool_width + 4 * mlstm_width
    n_gates = 2 * MLSTM_HEADS
    gate_w = jnp.pad(w_in[:, main_cols:], ((0, 0), (0, LANES - n_gates)))
    gate_b = jnp.pad(b_gates, (0, LANES - n_gates)).reshape(1, LANES)
    z, gates = norm_matmul(h, gain, w_in, main_cols, gate_w, gate_b)
    y_p = pool_mixer(z, pool_w, pool_scale)
    y_m = mlstm_mixer(z, gates, head_gain, pool_width)
    assert pool_width == mlstm_width and pool_width + mlstm_width == d
    return matmul_residual([y_p, y_m], w_out, h)


def odd_mixer(h, gain, w_in, conv_w, w_out):
    z = norm_matmul(h, gain, w_in, w_in.shape[1])
    return matmul_residual([conv_mixer(z, conv_w)], w_out, h)


def cross_attn(h, mem, gain, mem_gain, wq, wk, wv, wo, layer):
    d = h.shape[1]
    k = norm_matmul(mem, mem_gain, wk, d, layer=layer)
    v = norm_matmul(mem, mem_gain, wv, d, layer=layer)
    q = norm_matmul(h, gain, wq, d, layer=layer)
    return matmul_residual([attention(q, k, v)], wo, h, layer=layer)


def kernel(x, mem, mix_norm, xattn_norm, mem_norm, ffn_norm, final_norm, ev_w_in, ev_b_gates, ev_pool_w, ev_pool_scale, ev_head_norm, ev_w_out, od_w_in, od_conv_w, od_w_out, xa_wq, xa_wk, xa_wv, xa_wo, rt_group_w, rt_group_b, rt_expert_w, rt_expert_b, ex_w_gate, ex_w_up, ex_w_down):
    depth = mix_norm.shape[0]
    h = x[0]
    m = mem[0]
    for layer in range(depth):
        j = layer // 2
        if layer % 2 == 0:
            h = even_mixer(h, mix_norm[layer], ev_w_in[j], ev_b_gates[j], ev_pool_w[j], ev_pool_scale[j],
                           ev_head_norm[j], ev_w_out[j])
        else:
            h = odd_mixer(h, mix_norm[layer], od_w_in[j], od_conv_w[j], od_w_out[j])
        h = cross_attn(h, m, xattn_norm[layer], mem_norm[layer], xa_wq, xa_wk, xa_wv, xa_wo, layer)
        h = hier_moe(h, ffn_norm[layer], rt_group_w[layer], rt_group_b[layer], rt_expert_w[layer],
                     rt_expert_b[layer], ex_w_gate, ex_w_up, ex_w_down, layer,
                     final_gain=final_norm if layer == depth - 1 else None)
    return h[None]
```

```python
import functools

import jax
import jax.numpy as jnp
from jax import lax
from jax.experimental import pallas as pl
from jax.experimental.pallas import tpu as pltpu

F32 = jnp.float32
BF = jnp.bfloat16
EPS = 1e-6

POOL_WINDOWS = (2, 4, 8, 16)
N_POOL_GROUPS = 4
MLSTM_HEADS = 4
FORGET_LANE0 = MLSTM_HEADS
XATTN_HEADS = 4
N_GROUPS = 4
EXPERTS_PER_GROUP = 8
N_EXPERTS = N_GROUPS * EXPERTS_PER_GROUP
TOP_K = 2
CONV_WIDTH = 3

LANES = 128
SUBLANES = 8
DMA_QUEUE_ROWS = 0
DMA_QUEUE_BULK = 1

ROW_TILE = 1024
COL_TILE = 1024
OUT_ROW_TILE = 512
CAST_ROWS = 64
MOE_COL_CHUNK = 256
WEIGHT_DMA_PARTS = 4
MLSTM_CHUNK = 256
POOL_ROWS = 512
POOL_HALO = 128
CONV_ROWS = 512
CONV_COLS = 512
CONV_HALO = 16
ATTN_ROWS = 512
ROUTER_ROWS = 512
MOE_BLOCK = 256
COMBINE_ROWS = 256


def _cparams(semantics, vmem_mib):
    return pltpu.CompilerParams(dimension_semantics=semantics, vmem_limit_bytes=vmem_mib * 1024 * 1024)


def _sigmoid(x):
    return 1.0 / (1.0 + jnp.exp(-x))


def _log_sigmoid(x):
    return jnp.minimum(x, 0.0) - jnp.log(1.0 + jnp.exp(-jnp.abs(x)))


def _rms(x, g):
    ms = jnp.mean(x * x, axis=-1, keepdims=True)
    return x * lax.rsqrt(ms + EPS) * g


def _norm_mm_kernel(x_ref, g_ref, w_ref, o_ref, xn_ref):
    @pl.when(pl.program_id(1) == 0)
    def _():
        xn_ref[...] = _rms(x_ref[...], g_ref[...]).astype(BF)

    o_ref[...] = jnp.dot(xn_ref[...], w_ref[...].astype(BF), preferred_element_type=F32).astype(o_ref.dtype)


def _norm_mm_gates_kernel(x_ref, g_ref, w_ref, wg_ref, bg_ref, o_ref, gates_ref, xn_ref):
    @pl.when(pl.program_id(1) == 0)
    def _():
        xn = _rms(x_ref[...], g_ref[...]).astype(BF)
        xn_ref[...] = xn
        gates_ref[...] = jnp.dot(xn, wg_ref[...].astype(BF), preferred_element_type=F32) + bg_ref[...]

    o_ref[...] = jnp.dot(xn_ref[...], w_ref[...].astype(BF), preferred_element_type=F32).astype(o_ref.dtype)


def _stacked(w, layer):
    return (w[None], 0) if layer is None else (w, layer)


def norm_matmul(x, gain, w, n_cols, gate_w=None, gate_b=None, layer=None):
    n, k = x.shape
    w, li = _stacked(w, layer)
    tm = min(ROW_TILE, n)
    tn = COL_TILE
    grid = (n // tm, n_cols // tn)
    x_spec = pl.BlockSpec((tm, k), lambda i, j: (i, 0))
    g_spec = pl.BlockSpec((1, k), lambda i, j: (0, 0))
    w_spec = pl.BlockSpec((None, k, tn), lambda i, j: (li, 0, j))
    o_spec = pl.BlockSpec((tm, tn), lambda i, j: (i, j))
    scratch = [pltpu.VMEM((tm, k), BF)]
    gain = gain.reshape(1, k)
    if gate_w is None:
        return pl.pallas_call(
            _norm_mm_kernel, grid=grid, in_specs=[x_spec, g_spec, w_spec], out_specs=o_spec,
            out_shape=jax.ShapeDtypeStruct((n, n_cols), BF), scratch_shapes=scratch,
            compiler_params=_cparams(("parallel", "arbitrary"), 56), name="norm_matmul",
        )(x, gain, w)
    small = pl.BlockSpec((k, LANES), lambda i, j: (0, 0))
    bias = pl.BlockSpec((1, LANES), lambda i, j: (0, 0))
    return pl.pallas_call(
        _norm_mm_gates_kernel, grid=grid, in_specs=[x_spec, g_spec, w_spec, small, bias],
        out_specs=[o_spec, pl.BlockSpec((tm, LANES), lambda i, j: (i, 0))],
        out_shape=[jax.ShapeDtypeStruct((n, n_cols), BF), jax.ShapeDtypeStruct((n, LANES), F32)],
        scratch_shapes=scratch, compiler_params=_cparams(("parallel", "arbitrary"), 56), name="norm_matmul_gates",
    )(x, gain, w, gate_w, gate_b)


def _fill_bf16(dst_ref, src_ref):
    def body(i, carry):
        r = pl.multiple_of(i * CAST_ROWS, CAST_ROWS)
        dst_ref[pl.ds(r, CAST_ROWS), :] = src_ref[pl.ds(r, CAST_ROWS), :].astype(BF)
        return carry

    lax.fori_loop(0, src_ref.shape[0] // CAST_ROWS, body, 0)


def _mm_res_kernel(*refs, nparts):
    xs = refs[:nparts]
    w_ref, res_ref, o_ref, wb_ref = refs[nparts:]

    @pl.when(pl.program_id(0) == 0)
    def _():
        _fill_bf16(wb_ref, w_ref)

    acc = res_ref[...]
    k0 = 0
    for x_ref in xs:
        kp = x_ref.shape[1]
        acc = acc + jnp.dot(x_ref[...], wb_ref[k0:k0 + kp, :], preferred_element_type=F32)
        k0 += kp
    o_ref[...] = acc


def matmul_residual(xs, w, res, layer=None):
    n, d = res.shape
    w, li = _stacked(w, layer)
    k = w.shape[1]
    tm = min(OUT_ROW_TILE, n)
    in_specs = [pl.BlockSpec((tm, x.shape[1]), lambda i: (i, 0)) for x in xs]
    in_specs += [pl.BlockSpec((None, k, d), lambda i: (li, 0, 0), pipeline_mode=pl.Buffered(1)),
                 pl.BlockSpec((tm, d), lambda i: (i, 0))]
    return pl.pallas_call(
        functools.partial(_mm_res_kernel, nparts=len(xs)), grid=(n // tm,), in_specs=in_specs,
        out_specs=pl.BlockSpec((tm, d), lambda i: (i, 0)), out_shape=jax.ShapeDtypeStruct((n, d), F32),
        scratch_shapes=[pltpu.VMEM((k, d), BF)],
        compiler_params=_cparams(("arbitrary",), 52), name="matmul_residual",
    )(*xs, w, res)


def _pool_kernel(cur_ref, prev_ref, w_ref, sc_ref, o_ref, *, tp):
    i, j = pl.program_id(0), pl.program_id(1)
    win = lax.shift_left(jnp.int32(POOL_WINDOWS[0]), j)
    cur = cur_ref[...]
    dist = lax.broadcasted_iota(jnp.int32, (tp, tp), 0) - lax.broadcasted_iota(jnp.int32, (tp, tp), 1)
    band = jnp.where(dist >= 0, jnp.where(dist < win, 1.0, 0.0), 0.0).astype(BF)
    s = jnp.dot(band, cur, preferred_element_type=F32)
    distp = (lax.broadcasted_iota(jnp.int32, (tp, POOL_HALO), 0) + POOL_HALO
             - lax.broadcasted_iota(jnp.int32, (tp, POOL_HALO), 1))
    limit = jnp.where(i > 0, win, 0)
    bandp = jnp.where(distp < limit, 1.0, 0.0).astype(BF)
    s = s + jnp.dot(bandp, prev_ref[...], preferred_element_type=F32)
    pos = i * tp + lax.broadcasted_iota(jnp.int32, (tp, 1), 0)
    cnt = jnp.minimum(pos + 1, win).astype(F32)
    d = s / cnt - cur.astype(F32)
    y = jnp.dot(d.astype(BF), w_ref[...].astype(BF), preferred_element_type=F32) * sc_ref[...]
    o_ref[...] = y.astype(o_ref.dtype)


def pool_mixer(z, pool_w, pool_scale):
    n = z.shape[0]
    gdim = pool_w.shape[-1]
    width = N_POOL_GROUPS * gdim
    tp = min(POOL_ROWS, n)
    halo_blocks = tp // POOL_HALO
    return pl.pallas_call(
        functools.partial(_pool_kernel, tp=tp), grid=(n // tp, N_POOL_GROUPS),
        in_specs=[
            pl.BlockSpec((tp, gdim), lambda i, j: (i, j)),
            pl.BlockSpec((POOL_HALO, gdim), lambda i, j: (jnp.maximum(i * halo_blocks - 1, 0), j)),
            pl.BlockSpec((None, gdim, gdim), lambda i, j: (j, 0, 0)),
            pl.BlockSpec((1, gdim), lambda i, j: (0, j)),
        ],
        out_specs=pl.BlockSpec((tp, gdim), lambda i, j: (i, j)),
        out_shape=jax.ShapeDtypeStruct((n, width), BF),
        compiler_params=_cparams(("parallel", "arbitrary"), 32), name="pool_mixer",
    )(z, z, pool_w, pool_scale.reshape(1, width))


def _mlstm_kernel(q_ref, k_ref, v_ref, o_ref, g_ref, gain_ref, y_ref, ct_ref, m_ref, *, chunk, dh):
    c = pl.program_id(0)

    @pl.when(c == 0)
    def _():
        ct_ref[...] = jnp.zeros_like(ct_ref)
        m_ref[...] = jnp.zeros_like(m_ref)

    g = g_ref[...]
    lf = _log_sigmoid(g)
    row = lax.broadcasted_iota(jnp.int32, (chunk, chunk), 0)
    col = lax.broadcasted_iota(jnp.int32, (chunk, chunk), 1)
    causal = col <= row
    ltri = jnp.where(causal, 1.0, 0.0).astype(BF)
    hi = lf.astype(BF)
    r1 = lf - hi.astype(F32)
    mid = r1.astype(BF)
    lo = (r1 - mid.astype(F32)).astype(BF)
    bcum = (jnp.dot(ltri, hi, preferred_element_type=F32) + jnp.dot(ltri, mid, preferred_element_type=F32)
            + jnp.dot(ltri, lo, preferred_element_type=F32))
    g_t = g.T
    b_t = bcum.T
    ones_col = jnp.where(lax.broadcasted_iota(jnp.int32, (chunk, LANES), 1) == 0, 1.0, 0.0).astype(BF)

    for h in range(MLSTM_HEADS):
        hs = slice(h * dh, (h + 1) * dh)
        fl = FORGET_LANE0 + h
        bc = bcum[:, fl:fl + 1]
        br = b_t[fl:fl + 1, :]
        ir = g_t[h:h + 1, :]
        b_last = bcum[chunk - 1:chunk, fl:fl + 1]
        m_prev = m_ref[h][:, 0:1]

        dmat = jnp.where(causal, bc + (ir - br), -jnp.inf)
        inter = bc + m_prev
        m_t = jnp.maximum(jnp.max(dmat, axis=1, keepdims=True), inter)
        w_inter = jnp.exp(inter - m_t)
        p = jnp.exp(dmat - m_t)

        qh = q_ref[:, hs] * (dh ** -0.5)
        kh = k_ref[:, hs]
        v_aug = jnp.concatenate([v_ref[:, hs], ones_col], axis=1)
        s = lax.dot_general(qh, kh, (((1,), (1,)), ((), ())), preferred_element_type=F32)
        sc = (s * p).astype(BF)
        ct = ct_ref[h]
        num_aug = (w_inter * jnp.dot(qh, ct.astype(BF), preferred_element_type=F32)
                   + jnp.dot(sc, v_aug, preferred_element_type=F32))
        num = num_aug[:, :dh]
        den = num_aug[:, dh:dh + 1]
        hout = num / jnp.maximum(jnp.abs(den), jnp.exp(-m_t))

        yn = _rms(hout, gain_ref[:, hs])
        y_ref[:, hs] = (_sigmoid(o_ref[:, hs].astype(F32)) * yn).astype(y_ref.dtype)

        d_end = b_last - br + ir
        m_new = jnp.maximum(b_last + m_prev, jnp.max(d_end, axis=1, keepdims=True))
        a_prev = jnp.exp(b_last + m_prev - m_new)
        a_s = jnp.exp(d_end - m_new)
        k_t = (kh.astype(F32).T * a_s).astype(BF)
        ct_ref[h] = a_prev * ct + jnp.dot(k_t, v_aug, preferred_element_type=F32)
        m_ref[h] = jnp.broadcast_to(m_new, (1, LANES))


def mlstm_mixer(z, gates, head_gain, col0):
    n = z.shape[0]
    width = head_gain.shape[0]
    dh = width // MLSTM_HEADS
    chunk = min(MLSTM_CHUNK, n)
    base = col0 // width
    qkvo = [pl.BlockSpec((chunk, width), lambda c, p=p: (c, base + p)) for p in range(4)]
    return pl.pallas_call(
        functools.partial(_mlstm_kernel, chunk=chunk, dh=dh), grid=(n // chunk,),
        in_specs=qkvo + [pl.BlockSpec((chunk, LANES), lambda c: (c, 0)), pl.BlockSpec((1, width), lambda c: (0, 0))],
        out_specs=pl.BlockSpec((chunk, width), lambda c: (c, 0)),
        out_shape=jax.ShapeDtypeStruct((n, width), BF),
        scratch_shapes=[pltpu.VMEM((MLSTM_HEADS, dh, dh + LANES), F32), pltpu.VMEM((MLSTM_HEADS, 1, LANES), F32)],
        compiler_params=_cparams(("arbitrary",), 32), name="mlstm_mixer",
    )(z, z, z, z, gates, head_gain.reshape(1, width))


def _conv_kernel(b_ref, c_ref, u_ref, cp_ref, up_ref, w_ref, o_ref, *, tr):
    i = pl.program_id(0)
    zc = c_ref[...].astype(F32) * u_ref[...].astype(F32)
    zp = cp_ref[...].astype(F32) * up_ref[...].astype(F32)
    zp = jnp.where(i > 0, zp, 0.0)
    row = lax.broadcasted_iota(jnp.int32, zc.shape, 0)
    acc = w_ref[CONV_WIDTH - 1:CONV_WIDTH, :] * zc
    for back in range(1, CONV_WIDTH):
        shifted = pltpu.roll(zc, back, 0)
        for r in range(back):
            shifted = jnp.where(row == r, zp[CONV_HALO - back + r:CONV_HALO - back + r + 1, :], shifted)
        acc = acc + w_ref[CONV_WIDTH - 1 - back:CONV_WIDTH - back, :] * shifted
    o_ref[...] = (b_ref[...].astype(F32) * acc).astype(o_ref.dtype)


def conv_mixer(z, conv_w):
    n = z.shape[0]
    d = conv_w.shape[1]
    tr = min(CONV_ROWS, n)
    tc = CONV_COLS
    nb = d // tc
    halo_blocks = tr // CONV_HALO
    prev = lambda part: pl.BlockSpec(
        (CONV_HALO, tc), lambda i, j: (jnp.maximum(i * halo_blocks - 1, 0), part * nb + j))
    cur = lambda part: pl.BlockSpec((tr, tc), lambda i, j: (i, part * nb + j))
    return pl.pallas_call(
        functools.partial(_conv_kernel, tr=tr), grid=(n // tr, nb),
        in_specs=[cur(0), cur(1), cur(2), prev(1), prev(2), pl.BlockSpec((CONV_WIDTH, tc), lambda i, j: (0, j))],
        out_specs=pl.BlockSpec((tr, tc), lambda i, j: (i, j)),
        out_shape=jax.ShapeDtypeStruct((n, d), BF),
        compiler_params=_cparams(("parallel", "arbitrary"), 32), name="conv_mixer",
    )(z, z, z, z, z, conv_w)


def _attn_kernel(q_ref, k_ref, v_ref, o_ref, *, hd):
    scale = hd ** -0.5
    for h in range(XATTN_HEADS):
        hs = slice(h * hd, (h + 1) * hd)
        s = lax.dot_general(q_ref[:, hs], k_ref[:, hs], (((1,), (1,)), ((), ())),
                            preferred_element_type=F32) * scale
        e = jnp.exp(s - jnp.max(s, axis=-1, keepdims=True))
        p = (e / jnp.sum(e, axis=-1, keepdims=True)).astype(BF)
        o_ref[:, hs] = jnp.dot(p, v_ref[:, hs], preferred_element_type=F32).astype(o_ref.dtype)


def attention(q, k, v):
    n, d = q.shape
    m = k.shape[0]
    ta = min(ATTN_ROWS, n)
    return pl.pallas_call(
        functools.partial(_attn_kernel, hd=d // XATTN_HEADS), grid=(n // ta,),
        in_specs=[pl.BlockSpec((ta, d), lambda i: (i, 0)), pl.BlockSpec((m, d), lambda i: (0, 0)),
                  pl.BlockSpec((m, d), lambda i: (0, 0))],
        out_specs=pl.BlockSpec((ta, d), lambda i: (i, 0)),
        out_shape=jax.ShapeDtypeStruct((n, d), BF),
        compiler_params=_cparams(("parallel",), 32), name="cross_attention",
    )(q, k, v)


def _router_kernel(x_ref, g_ref, wr_ref, br_ref, xn_ref, ids_ref, wts_ref, cnt_ref, carry_ref, *, tr):
    @pl.when(pl.program_id(0) == 0)
    def _():
        carry_ref[...] = jnp.zeros_like(carry_ref)

    xn = _rms(x_ref[...], g_ref[...])
    xn_ref[...] = xn
    logits = jnp.dot(xn, wr_ref[...], preferred_element_type=F32, precision=lax.Precision.HIGHEST) + br_ref[...]
    lane = lax.broadcasted_iota(jnp.int32, (tr, LANES), 1).astype(F32)
    neg = -jnp.inf

    def first_argmax(vals):
        top = jnp.max(vals, axis=-1, keepdims=True)
        return top, jnp.min(jnp.where(vals == top, lane, float(LANES)), axis=-1, keepdims=True)

    gl = jnp.where(lane < N_GROUPS, logits, neg)
    gmax, grp = first_argmax(gl)
    g_prob = 1.0 / jnp.sum(jnp.exp(gl - gmax), axis=-1, keepdims=True)
    lo = N_GROUPS + EXPERTS_PER_GROUP * grp
    el = jnp.where(lane >= lo, jnp.where(lane < lo + EXPERTS_PER_GROUP, logits, neg), neg)
    v1, l1 = first_argmax(el)
    v2, l2 = first_argmax(jnp.where(lane == l1, neg, el))
    e2 = jnp.exp(v2 - v1)
    w1 = g_prob / (1.0 + e2)
    w2 = g_prob * e2 / (1.0 + e2)
    hot1 = lane == l1
    hot2 = lane == l2
    hot = jnp.where(hot1, 1.0, jnp.where(hot2, 1.0, 0.0))
    earlier = (lax.broadcasted_iota(jnp.int32, (tr, tr), 1) < lax.broadcasted_iota(jnp.int32, (tr, tr), 0))
    before = jnp.dot(jnp.where(earlier, 1.0, 0.0).astype(BF), hot.astype(BF), preferred_element_type=F32)
    before = before + carry_ref[0:1, :]
    r1 = jnp.sum(jnp.where(hot1, before, 0.0), axis=-1, keepdims=True)
    r2 = jnp.sum(jnp.where(hot2, before, 0.0), axis=-1, keepdims=True)
    carry_ref[0:1, :] = carry_ref[0:1, :] + jnp.sum(hot, axis=0, keepdims=True)
    ids = jnp.where(lane == 0, l1 - N_GROUPS, jnp.where(lane == 1, l2 - N_GROUPS,
                    jnp.where(lane == 2, r1, jnp.where(lane == 3, r2, 0.0))))
    ids_ref[...] = ids.astype(jnp.int32)
    wts_ref[...] = jnp.where(lane == 0, w1, jnp.where(lane == 1, w2, 0.0))
    cnt_ref[...] = carry_ref[...]


def route(h, gain, w_router, b_router):
    n, d = h.shape
    tr = min(ROUTER_ROWS, n)
    return pl.pallas_call(
        functools.partial(_router_kernel, tr=tr), grid=(n // tr,),
        in_specs=[pl.BlockSpec((tr, d), lambda i: (i, 0)), pl.BlockSpec((1, d), lambda i: (0, 0)),
                  pl.BlockSpec((d, LANES), lambda i: (0, 0)), pl.BlockSpec((1, LANES), lambda i: (0, 0))],
        out_specs=[pl.BlockSpec((tr, d), lambda i: (i, 0)),
                   pl.BlockSpec((tr, LANES), lambda i: (i, 0)), pl.BlockSpec((tr, LANES), lambda i: (i, 0)),
                   pl.BlockSpec((SUBLANES, LANES), lambda i: (0, 0))],
        out_shape=[jax.ShapeDtypeStruct((n, d), F32), jax.ShapeDtypeStruct((n, LANES), jnp.int32),
                   jax.ShapeDtypeStruct((n, LANES), F32), jax.ShapeDtypeStruct((SUBLANES, LANES), F32)],
        scratch_shapes=[pltpu.VMEM((SUBLANES, LANES), F32)],
        compiler_params=_cparams(("arbitrary",), 40), name="moe_router",
    )(h, gain.reshape(1, d), w_router, b_router)


def _row_copy(src_hbm, row, dst, dst_row, sem):
    return pltpu.make_async_copy(src_hbm.at[pl.ds(row, 1), :], dst.at[pl.ds(dst_row, 1), :], sem)


def _rows_wait(src_hbm, dst, sem):
    pltpu.make_async_copy(src_hbm.at[pl.ds(0, dst.shape[0]), :], dst, sem).wait()


def _is_new_expert(blk_ref, b):
    return jnp.logical_or(b == 0, blk_ref[b] != blk_ref[jnp.maximum(b - 1, 0)])


def _weight_copies(w_hbm, layer, expert, stage, sem):
    rows = stage.shape[0] // WEIGHT_DMA_PARTS
    return [pltpu.make_async_copy(w_hbm.at[layer, expert, pl.ds(p * rows, rows), :],
                                  stage.at[pl.ds(p * rows, rows), :], sem) for p in range(WEIGHT_DMA_PARTS)]


def _moe_expert_kernel(blk_ref, nxt_ref, nact_ref, rtok_ref, xn_hbm, wg_hbm, wu_hbm, wd_hbm, ys_ref,
                       buf, stage_g, stage_u, stage_d, wgb, wub, wdb, xb, hid, sem, wsem, *, tb, nb, layer):
    b = pl.program_id(0)
    nact = nact_ref[0]
    slot = b % 2
    f = hid.shape[1]
    d = ys_ref.shape[1]
    up_chunks = f // MOE_COL_CHUNK
    rows_per_chunk = tb // up_chunks

    def expert_copies(e):
        return (_weight_copies(wg_hbm, layer, e, stage_g, wsem.at[0])
                + _weight_copies(wu_hbm, layer, e, stage_u, wsem.at[1])
                + _weight_copies(wd_hbm, layer, e, stage_d, wsem.at[2]))

    @pl.when(b == 0)
    def _():
        for cp in expert_copies(blk_ref[0]):
            cp.start(priority=DMA_QUEUE_BULK)

        def body(r, carry):
            _row_copy(xn_hbm, rtok_ref[r], buf.at[0], r, sem.at[0]).start(priority=DMA_QUEUE_ROWS)
            return carry

        lax.fori_loop(0, tb, body, 0, unroll=8)

    @pl.when(jnp.logical_and(b < nact, _is_new_expert(blk_ref, b)))
    def _():
        for cp in expert_copies(blk_ref[b]):
            cp.wait()
        _fill_bf16(wgb, stage_g)
        _fill_bf16(wub, stage_u)
        _fill_bf16(wdb, stage_d)

        @pl.when(nxt_ref[b] >= 0)
        def _():
            for cp in expert_copies(nxt_ref[b]):
                cp.start(priority=DMA_QUEUE_BULK)

    def compute(prefetch_next):
        _rows_wait(xn_hbm, buf.at[slot], sem.at[slot])
        xb[...] = buf[slot].astype(BF)
        for c in range(up_chunks):
            if prefetch_next:
                for r in range(c * rows_per_chunk, (c + 1) * rows_per_chunk):
                    _row_copy(xn_hbm, rtok_ref[(b + 1) * tb + r], buf.at[1 - slot], r,
                              sem.at[1 - slot]).start(priority=DMA_QUEUE_ROWS)
            cs = slice(c * MOE_COL_CHUNK, (c + 1) * MOE_COL_CHUNK)
            gate = jnp.dot(xb[...], wgb[:, cs], preferred_element_type=F32)
            up = jnp.dot(xb[...], wub[:, cs], preferred_element_type=F32)
            hid[:, cs] = (gate * _sigmoid(gate) * up).astype(hid.dtype)
        for c in range(0, d, MOE_COL_CHUNK):
            cs = slice(c, c + MOE_COL_CHUNK)
            ys_ref[:, cs] = jnp.dot(hid[...], wdb[:, cs], preferred_element_type=F32)

    @pl.when(jnp.logical_and(b < nact, b + 1 < nb))
    def _():
        compute(True)

    @pl.when(jnp.logical_and(b < nact, b + 1 == nb))
    def _():
        compute(False)

    @pl.when(b >= nact)
    def _():
        ys_ref[...] = jnp.zeros_like(ys_ref)

    @pl.when(b == nact)
    def _():
        _rows_wait(xn_hbm, buf.at[slot], sem.at[slot])


def _combine_kernel(dest_ref, ys_hbm, h_ref, w_ref, gain_ref, o_ref, buf, sem, *, tc, final_norm):
    i = pl.program_id(0)

    def issue(blk, slot):
        base = blk * tc

        def body(r, carry):
            for k in range(TOP_K):
                _row_copy(ys_hbm, dest_ref[TOP_K * (base + r) + k], buf.at[slot], k * tc + r,
                          sem.at[slot]).start(priority=k)
            return carry

        lax.fori_loop(0, tc, body, 0, unroll=8)

    @pl.when(i == 0)
    def _():
        issue(0, 0)

    @pl.when(i + 1 < pl.num_programs(0))
    def _():
        issue(i + 1, (i + 1) % 2)

    slot = i % 2
    _rows_wait(ys_hbm, buf.at[slot], sem.at[slot])
    out = h_ref[...] + (w_ref[:, 0:1] * buf[slot, 0:tc, :] + w_ref[:, 1:2] * buf[slot, tc:TOP_K * tc, :])
    if final_norm:
        out = _rms(out, gain_ref[...])
    o_ref[...] = out


def hier_moe(h, ffn_gain, wg_r, bg_r, we_r, be_r, w_gate, w_up, w_down, layer, final_gain=None):
    n, d = h.shape
    f = w_gate.shape[-1]
    a = n * TOP_K
    tb = MOE_BLOCK
    nb = a // tb + N_EXPERTS

    pad = LANES - N_GROUPS - N_EXPERTS
    w_router = jnp.concatenate([wg_r, we_r, jnp.zeros((d, pad), F32)], axis=1)
    b_router = jnp.concatenate([bg_r, be_r, jnp.zeros((pad,), F32)]).reshape(1, LANES)
    xn, ids, wts, cnt = route(h, ffn_gain, w_router, b_router)

    counts = cnt[0, N_GROUPS:N_GROUPS + N_EXPERTS].astype(jnp.int32)
    nblk = (counts + tb - 1) // tb
    bend = jnp.cumsum(nblk)
    nact = bend[-1]
    expert = ids[:, :TOP_K]
    dest = ((bend - nblk)[expert] * tb + ids[:, TOP_K:2 * TOP_K]).reshape(a)
    blk = jnp.minimum(jnp.arange(nb, dtype=jnp.int32), jnp.maximum(nact - 1, 0))
    owner = lambda bi: jnp.minimum(jnp.searchsorted(bend, bi, side='right'), N_EXPERTS - 1).astype(jnp.int32)
    blk_e = owner(blk)
    after = bend[blk_e]
    nxt_e = jnp.where(after < nact, owner(after), -1)
    tok = jnp.repeat(jnp.arange(n, dtype=jnp.int32), TOP_K)
    row_tok = jnp.zeros((nb * tb,), jnp.int32).at[dest].set(tok)
    nact1 = nact.reshape(1).astype(jnp.int32)

    any_space = pl.BlockSpec(memory_space=pl.ANY)
    ys = pl.pallas_call(
        functools.partial(_moe_expert_kernel, tb=tb, nb=nb, layer=layer),
        grid_spec=pltpu.PrefetchScalarGridSpec(
            num_scalar_prefetch=4, grid=(nb,),
            in_specs=[any_space, any_space, any_space, any_space],
            out_specs=pl.BlockSpec((tb, d), lambda b, *_: (b, 0)),
            scratch_shapes=[pltpu.VMEM((2, tb, d), F32),
                            pltpu.VMEM((d, f), F32), pltpu.VMEM((d, f), F32), pltpu.VMEM((f, d), F32),
                            pltpu.VMEM((d, f), BF), pltpu.VMEM((d, f), BF), pltpu.VMEM((f, d), BF),
                            pltpu.VMEM((tb, d), BF), pltpu.VMEM((tb, f), BF),
                            pltpu.SemaphoreType.DMA((2,)), pltpu.SemaphoreType.DMA((3,))]),
        out_shape=jax.ShapeDtypeStruct((nb * tb, d), F32),
        compiler_params=_cparams(("arbitrary",), 58), name="moe_experts",
    )(blk_e, nxt_e, nact1, row_tok, xn, w_gate, w_up, w_down)

    tc = min(COMBINE_ROWS, n)
    gain = (final_gain if final_gain is not None else ffn_gain).reshape(1, d)
    return pl.pallas_call(
        functools.partial(_combine_kernel, tc=tc, final_norm=final_gain is not None),
        grid_spec=pltpu.PrefetchScalarGridSpec(
            num_scalar_prefetch=1, grid=(n // tc,),
            in_specs=[pl.BlockSpec(memory_space=pl.ANY), pl.BlockSpec((tc, d), lambda i, *_: (i, 0)),
                      pl.BlockSpec((tc, LANES), lambda i, *_: (i, 0)), pl.BlockSpec((1, d), lambda i, *_: (0, 0))],
            out_specs=pl.BlockSpec((tc, d), lambda i, *_: (i, 0)),
            scratch_shapes=[pltpu.VMEM((2, TOP_K * tc, d), F32), pltpu.SemaphoreType.DMA((2,))]),
        out_shape=jax.ShapeDtypeStruct((n, d), F32),
        compiler_params=_cparams(("arbitrary",), 40), name="moe_combine",
    )(dest, ys, h, wts, gain)


def even_mixer(h, gain, w_in, b_gates, pool_w, pool_scale, head_gain, w_out):
    d = h.shape[1]
    pool_width = N_POOL_GROUPS * pool_w.shape[-1]
    mlstm_width = head_gain.shape[0]
    main_cols = pool_width + 4 * mlstm_width
    n_gates = 2 * MLSTM_HEADS
    gate_w = jnp.pad(w_in[:, main_cols:], ((0, 0), (0, LANES - n_gates)))
    gate_b = jnp.pad(b_gates, (0, LANES - n_gates)).reshape(1, LANES)
    z, gates = norm_matmul(h, gain, w_in, main_cols, gate_w, gate_b)
    y_p = pool_mixer(z, pool_w, pool_scale)
    y_m = mlstm_mixer(z, gates, head_gain, pool_width)
    assert pool_width == mlstm_width and pool_width + mlstm_width == d
    return matmul_residual([y_p, y_m], w_out, h)


def odd_mixer(h, gain, w_in, conv_w, w_out):
    z = norm_matmul(h, gain, w_in, w_in.shape[1])
    return matmul_residual([conv_mixer(z, conv_w)], w_out, h)


def cross_attn(h, mem, gain, mem_gain, wq, wk, wv, wo, layer):
    d = h.shape[1]
    k = norm_matmul(mem, mem_gain, wk, d, layer=layer)
    v = norm_matmul(mem, mem_gain, wv, d, layer=layer)
    q = norm_matmul(h, gain, wq, d, layer=layer)
    return matmul_residual([attention(q, k, v)], wo, h, layer=layer)


def kernel(x, mem, mix_norm, xattn_norm, mem_norm, ffn_norm, final_norm, ev_w_in, ev_b_gates, ev_pool_w, ev_pool_scale, ev_head_norm, ev_w_out, od_w_in, od_conv_w, od_w_out, xa_wq, xa_wk, xa_wv, xa_wo, rt_group_w, rt_group_b, rt_expert_w, rt_expert_b, ex_w_gate, ex_w_up, ex_w_down):
    depth = mix_norm.shape[0]
    h = x[0]
    m = mem[0]
    for layer in range(depth):
        j = layer // 2
        if layer % 2 == 0:
            h = even_mixer(h, mix_norm[layer], ev_w_in[j], ev_b_gates[j], ev_pool_w[j], ev_pool_scale[j],
                           ev_head_norm[j], ev_w_out[j])
        else:
            h = odd_mixer(h, mix_norm[layer], od_w_in[j], od_conv_w[j], od_w_out[j])
        h = cross_attn(h, m, xattn_norm[layer], mem_norm[layer], xa_wq, xa_wk, xa_wv, xa_wo, layer)
        h = hier_moe(h, ffn_norm[layer], rt_group_w[layer], rt_group_b[layer], rt_expert_w[layer],
                     rt_expert_b[layer], ex_w_gate, ex_w_up, ex_w_down, layer,
                     final_gain=final_norm if layer == depth - 1 else None)
    return h[None]
```

```python
import functools

import jax
import jax.numpy as jnp
from jax import lax
from jax.experimental import pallas as pl
from jax.experimental.pallas import tpu as pltpu

F32 = jnp.float32
BF = jnp.bfloat16
EPS = 1e-6

POOL_WINDOWS = (2, 4, 8, 16)
N_POOL_GROUPS = 4
MLSTM_HEADS = 4
FORGET_LANE0 = MLSTM_HEADS
XATTN_HEADS = 4
N_GROUPS = 4
EXPERTS_PER_GROUP = 8
N_EXPERTS = N_GROUPS * EXPERTS_PER_GROUP
TOP_K = 2
CONV_WIDTH = 3

LANES = 128
SUBLANES = 8
DMA_QUEUE_ROWS = 0
DMA_QUEUE_BULK = 1

ROW_TILE = 1024
COL_TILE = 1024
OUT_ROW_TILE = 512
CAST_ROWS = 64
CAST_UNROLL = 4
MOE_COL_CHUNK = 256
WEIGHT_DMA_PARTS = 4
MLSTM_CHUNK = 256
POOL_ROWS = 512
POOL_HALO = 128
CONV_ROWS = 512
CONV_COLS = 512
CONV_HALO = 16
ATTN_ROWS = 512
ROUTER_ROWS = 512
MOE_BLOCK = 256
COMBINE_ROWS = 256


def _cparams(semantics, vmem_mib):
    return pltpu.CompilerParams(dimension_semantics=semantics, vmem_limit_bytes=vmem_mib * 1024 * 1024)


def _sigmoid(x):
    return 1.0 / (1.0 + jnp.exp(-x))


def _log_sigmoid(x):
    return jnp.minimum(x, 0.0) - jnp.log(1.0 + jnp.exp(-jnp.abs(x)))


def _rms(x, g):
    ms = jnp.mean(x * x, axis=-1, keepdims=True)
    return x * lax.rsqrt(ms + EPS) * g


def _norm_mm_kernel(x_ref, g_ref, w_ref, o_ref, xn_ref):
    @pl.when(pl.program_id(1) == 0)
    def _():
        xn_ref[...] = _rms(x_ref[...], g_ref[...]).astype(BF)

    o_ref[...] = jnp.dot(xn_ref[...], w_ref[...].astype(BF), preferred_element_type=F32).astype(o_ref.dtype)


def _norm_mm_gates_kernel(x_ref, g_ref, w_ref, wg_ref, bg_ref, o_ref, gates_ref, xn_ref):
    @pl.when(pl.program_id(1) == 0)
    def _():
        xn = _rms(x_ref[...], g_ref[...]).astype(BF)
        xn_ref[...] = xn
        gates_ref[...] = jnp.dot(xn, wg_ref[...].astype(BF), preferred_element_type=F32) + bg_ref[...]

    o_ref[...] = jnp.dot(xn_ref[...], w_ref[...].astype(BF), preferred_element_type=F32).astype(o_ref.dtype)


def _stacked(w, layer):
    return (w[None], 0) if layer is None else (w, layer)


def norm_matmul(x, gain, w, n_cols, gate_w=None, gate_b=None, layer=None):
    n, k = x.shape
    w, li = _stacked(w, layer)
    tm = min(ROW_TILE, n)
    tn = COL_TILE
    grid = (n // tm, n_cols // tn)
    x_spec = pl.BlockSpec((tm, k), lambda i, j: (i, 0))
    g_spec = pl.BlockSpec((1, k), lambda i, j: (0, 0))
    w_spec = pl.BlockSpec((None, k, tn), lambda i, j: (li, 0, j))
    o_spec = pl.BlockSpec((tm, tn), lambda i, j: (i, j))
    scratch = [pltpu.VMEM((tm, k), BF)]
    gain = gain.reshape(1, k)
    if gate_w is None:
        return pl.pallas_call(
            _norm_mm_kernel, grid=grid, in_specs=[x_spec, g_spec, w_spec], out_specs=o_spec,
            out_shape=jax.ShapeDtypeStruct((n, n_cols), BF), scratch_shapes=scratch,
            compiler_params=_cparams(("parallel", "arbitrary"), 56), name="norm_matmul",
        )(x, gain, w)
    small = pl.BlockSpec((k, LANES), lambda i, j: (0, 0))
    bias = pl.BlockSpec((1, LANES), lambda i, j: (0, 0))
    return pl.pallas_call(
        _norm_mm_gates_kernel, grid=grid, in_specs=[x_spec, g_spec, w_spec, small, bias],
        out_specs=[o_spec, pl.BlockSpec((tm, LANES), lambda i, j: (i, 0))],
        out_shape=[jax.ShapeDtypeStruct((n, n_cols), BF), jax.ShapeDtypeStruct((n, LANES), F32)],
        scratch_shapes=scratch, compiler_params=_cparams(("parallel", "arbitrary"), 56), name="norm_matmul_gates",
    )(x, gain, w, gate_w, gate_b)


def _fill_bf16(dst_ref, src_ref):
    def body(i, carry):
        r = pl.multiple_of(i * CAST_ROWS, CAST_ROWS)
        dst_ref[pl.ds(r, CAST_ROWS), :] = src_ref[pl.ds(r, CAST_ROWS), :].astype(BF)
        return carry

    lax.fori_loop(0, src_ref.shape[0] // CAST_ROWS, body, 0, unroll=CAST_UNROLL)


def _norm_mm_resident_kernel(x_ref, g_ref, w_ref, o_ref, wb_ref):
    @pl.when(pl.program_id(0) == 0)
    def _():
        _fill_bf16(wb_ref, w_ref)

    xn = _rms(x_ref[...], g_ref[...]).astype(BF)
    o_ref[...] = jnp.dot(xn, wb_ref[...], preferred_element_type=F32).astype(o_ref.dtype)


def norm_matmul_resident(x, gain, w, layer):
    n, k = x.shape
    d = w.shape[2]
    tm = min(OUT_ROW_TILE, n)
    return pl.pallas_call(
        _norm_mm_resident_kernel, grid=(n // tm,),
        in_specs=[pl.BlockSpec((tm, k), lambda i: (i, 0)), pl.BlockSpec((1, k), lambda i: (0, 0)),
                  pl.BlockSpec((None, k, d), lambda i: (layer, 0, 0), pipeline_mode=pl.Buffered(1))],
        out_specs=pl.BlockSpec((tm, d), lambda i: (i, 0)), out_shape=jax.ShapeDtypeStruct((n, d), BF),
        scratch_shapes=[pltpu.VMEM((k, d), BF)],
        compiler_params=_cparams(("arbitrary",), 52), name="norm_matmul_resident",
    )(x, gain.reshape(1, k), w)


def _mm_res_kernel(*refs, nparts):
    xs = refs[:nparts]
    w_ref, res_ref, o_ref, wb_ref = refs[nparts:]

    @pl.when(pl.program_id(0) == 0)
    def _():
        _fill_bf16(wb_ref, w_ref)

    acc = res_ref[...]
    k0 = 0
    for x_ref in xs:
        kp = x_ref.shape[1]
        acc = acc + jnp.dot(x_ref[...], wb_ref[k0:k0 + kp, :], preferred_element_type=F32)
        k0 += kp
    o_ref[...] = acc


def matmul_residual(xs, w, res, layer=None):
    n, d = res.shape
    w, li = _stacked(w, layer)
    k = w.shape[1]
    tm = min(OUT_ROW_TILE, n)
    in_specs = [pl.BlockSpec((tm, x.shape[1]), lambda i: (i, 0)) for x in xs]
    in_specs += [pl.BlockSpec((None, k, d), lambda i: (li, 0, 0), pipeline_mode=pl.Buffered(1)),
                 pl.BlockSpec((tm, d), lambda i: (i, 0))]
    return pl.pallas_call(
        functools.partial(_mm_res_kernel, nparts=len(xs)), grid=(n // tm,), in_specs=in_specs,
        out_specs=pl.BlockSpec((tm, d), lambda i: (i, 0)), out_shape=jax.ShapeDtypeStruct((n, d), F32),
        scratch_shapes=[pltpu.VMEM((k, d), BF)],
        compiler_params=_cparams(("arbitrary",), 52), name="matmul_residual",
    )(*xs, w, res)


def _pool_kernel(cur_ref, prev_ref, w_ref, sc_ref, o_ref, *, tp):
    i, j = pl.program_id(0), pl.program_id(1)
    win = lax.shift_left(jnp.int32(POOL_WINDOWS[0]), j)
    cur = cur_ref[...]
    dist = lax.broadcasted_iota(jnp.int32, (tp, tp), 0) - lax.broadcasted_iota(jnp.int32, (tp, tp), 1)
    band = jnp.where(dist >= 0, jnp.where(dist < win, 1.0, 0.0), 0.0).astype(BF)
    s = jnp.dot(band, cur, preferred_element_type=F32)
    distp = (lax.broadcasted_iota(jnp.int32, (tp, POOL_HALO), 0) + POOL_HALO
             - lax.broadcasted_iota(jnp.int32, (tp, POOL_HALO), 1))
    limit = jnp.where(i > 0, win, 0)
    bandp = jnp.where(distp < limit, 1.0, 0.0).astype(BF)
    s = s + jnp.dot(bandp, prev_ref[...], preferred_element_type=F32)
    pos = i * tp + lax.broadcasted_iota(jnp.int32, (tp, 1), 0)
    cnt = jnp.minimum(pos + 1, win).astype(F32)
    d = s / cnt - cur.astype(F32)
    y = jnp.dot(d.astype(BF), w_ref[...].astype(BF), preferred_element_type=F32) * sc_ref[...]
    o_ref[...] = y.astype(o_ref.dtype)


def pool_mixer(z, pool_w, pool_scale):
    n = z.shape[0]
    gdim = pool_w.shape[-1]
    width = N_POOL_GROUPS * gdim
    tp = min(POOL_ROWS, n)
    halo_blocks = tp // POOL_HALO
    return pl.pallas_call(
        functools.partial(_pool_kernel, tp=tp), grid=(n // tp, N_POOL_GROUPS),
        in_specs=[
            pl.BlockSpec((tp, gdim), lambda i, j: (i, j)),
            pl.BlockSpec((POOL_HALO, gdim), lambda i, j: (jnp.maximum(i * halo_blocks - 1, 0), j)),
            pl.BlockSpec((None, gdim, gdim), lambda i, j: (j, 0, 0)),
            pl.BlockSpec((1, gdim), lambda i, j: (0, j)),
        ],
        out_specs=pl.BlockSpec((tp, gdim), lambda i, j: (i, j)),
        out_shape=jax.ShapeDtypeStruct((n, width), BF),
        compiler_params=_cparams(("parallel", "arbitrary"), 32), name="pool_mixer",
    )(z, z, pool_w, pool_scale.reshape(1, width))


def _mlstm_kernel(q_ref, k_ref, v_ref, o_ref, g_ref, gain_ref, y_ref, ct_ref, m_ref, *, chunk, dh):
    c = pl.program_id(0)

    @pl.when(c == 0)
    def _():
        ct_ref[...] = jnp.zeros_like(ct_ref)
        m_ref[...] = jnp.zeros_like(m_ref)

    g = g_ref[...]
    lf = _log_sigmoid(g)
    row = lax.broadcasted_iota(jnp.int32, (chunk, chunk), 0)
    col = lax.broadcasted_iota(jnp.int32, (chunk, chunk), 1)
    causal = col <= row
    ltri = jnp.where(causal, 1.0, 0.0).astype(BF)
    hi = lf.astype(BF)
    r1 = lf - hi.astype(F32)
    mid = r1.astype(BF)
    lo = (r1 - mid.astype(F32)).astype(BF)
    bcum = (jnp.dot(ltri, hi, preferred_element_type=F32) + jnp.dot(ltri, mid, preferred_element_type=F32)
            + jnp.dot(ltri, lo, preferred_element_type=F32))
    g_t = g.T
    b_t = bcum.T
    ones_col = jnp.where(lax.broadcasted_iota(jnp.int32, (chunk, LANES), 1) == 0, 1.0, 0.0).astype(BF)

    for h in range(MLSTM_HEADS):
        hs = slice(h * dh, (h + 1) * dh)
        fl = FORGET_LANE0 + h
        bc = bcum[:, fl:fl + 1]
        br = b_t[fl:fl + 1, :]
        ir = g_t[h:h + 1, :]
        b_last = bcum[chunk - 1:chunk, fl:fl + 1]
        m_prev = m_ref[h][:, 0:1]

        dmat = jnp.where(causal, bc + (ir - br), -jnp.inf)
        inter = bc + m_prev
        m_t = jnp.maximum(jnp.max(dmat, axis=1, keepdims=True), inter)
        w_inter = jnp.exp(inter - m_t)
        p = jnp.exp(dmat - m_t)

        qh = q_ref[:, hs] * (dh ** -0.5)
        kh = k_ref[:, hs]
        v_aug = jnp.concatenate([v_ref[:, hs], ones_col], axis=1)
        s = lax.dot_general(qh, kh, (((1,), (1,)), ((), ())), preferred_element_type=F32)
        sc = (s * p).astype(BF)
        ct = ct_ref[h]
        num_aug = (w_inter * jnp.dot(qh, ct.astype(BF), preferred_element_type=F32)
                   + jnp.dot(sc, v_aug, preferred_element_type=F32))
        num = num_aug[:, :dh]
        den = num_aug[:, dh:dh + 1]
        hout = num / jnp.maximum(jnp.abs(den), jnp.exp(-m_t))

        yn = _rms(hout, gain_ref[:, hs])
        y_ref[:, hs] = (_sigmoid(o_ref[:, hs].astype(F32)) * yn).astype(y_ref.dtype)

        d_end = b_last - br + ir
        m_new = jnp.maximum(b_last + m_prev, jnp.max(d_end, axis=1, keepdims=True))
        a_prev = jnp.exp(b_last + m_prev - m_new)
        a_s = jnp.exp(d_end - m_new)
        k_t = (kh.astype(F32).T * a_s).astype(BF)
        ct_ref[h] = a_prev * ct + jnp.dot(k_t, v_aug, preferred_element_type=F32)
        m_ref[h] = jnp.broadcast_to(m_new, (1, LANES))


def mlstm_mixer(z, gates, head_gain, col0):
    n = z.shape[0]
    width = head_gain.shape[0]
    dh = width // MLSTM_HEADS
    chunk = min(MLSTM_CHUNK, n)
    base = col0 // width
    qkvo = [pl.BlockSpec((chunk, width), lambda c, p=p: (c, base + p)) for p in range(4)]
    return pl.pallas_call(
        functools.partial(_mlstm_kernel, chunk=chunk, dh=dh), grid=(n // chunk,),
        in_specs=qkvo + [pl.BlockSpec((chunk, LANES), lambda c: (c, 0)), pl.BlockSpec((1, width), lambda c: (0, 0))],
        out_specs=pl.BlockSpec((chunk, width), lambda c: (c, 0)),
        out_shape=jax.ShapeDtypeStruct((n, width), BF),
        scratch_shapes=[pltpu.VMEM((MLSTM_HEADS, dh, dh + LANES), F32), pltpu.VMEM((MLSTM_HEADS, 1, LANES), F32)],
        compiler_params=_cparams(("arbitrary",), 32), name="mlstm_mixer",
    )(z, z, z, z, gates, head_gain.reshape(1, width))


def _conv_kernel(b_ref, c_ref, u_ref, cp_ref, up_ref, w_ref, o_ref, *, tr):
    i = pl.program_id(0)
    zc = c_ref[...].astype(F32) * u_ref[...].astype(F32)
    zp = cp_ref[...].astype(F32) * up_ref[...].astype(F32)
    zp = jnp.where(i > 0, zp, 0.0)
    row = lax.broadcasted_iota(jnp.int32, zc.shape, 0)
    acc = w_ref[CONV_WIDTH - 1:CONV_WIDTH, :] * zc
    for back in range(1, CONV_WIDTH):
        shifted = pltpu.roll(zc, back, 0)
        for r in range(back):
            shifted = jnp.where(row == r, zp[CONV_HALO - back + r:CONV_HALO - back + r + 1, :], shifted)
        acc = acc + w_ref[CONV_WIDTH - 1 - back:CONV_WIDTH - back, :] * shifted
    o_ref[...] = (b_ref[...].astype(F32) * acc).astype(o_ref.dtype)


def conv_mixer(z, conv_w):
    n = z.shape[0]
    d = conv_w.shape[1]
    tr = min(CONV_ROWS, n)
    tc = CONV_COLS
    nb = d // tc
    halo_blocks = tr // CONV_HALO
    prev = lambda part: pl.BlockSpec(
        (CONV_HALO, tc), lambda i, j: (jnp.maximum(i * halo_blocks - 1, 0), part * nb + j))
    cur = lambda part: pl.BlockSpec((tr, tc), lambda i, j: (i, part * nb + j))
    return pl.pallas_call(
        functools.partial(_conv_kernel, tr=tr), grid=(n // tr, nb),
        in_specs=[cur(0), cur(1), cur(2), prev(1), prev(2), pl.BlockSpec((CONV_WIDTH, tc), lambda i, j: (0, j))],
        out_specs=pl.BlockSpec((tr, tc), lambda i, j: (i, j)),
        out_shape=jax.ShapeDtypeStruct((n, d), BF),
        compiler_params=_cparams(("parallel", "arbitrary"), 32), name="conv_mixer",
    )(z, z, z, z, z, conv_w)


def _attn_kernel(q_ref, k_ref, v_ref, o_ref, *, hd):
    scale = hd ** -0.5
    for h in range(XATTN_HEADS):
        hs = slice(h * hd, (h + 1) * hd)
        s = lax.dot_general(q_ref[:, hs], k_ref[:, hs], (((1,), (1,)), ((), ())),
                            preferred_element_type=F32) * scale
        e = jnp.exp(s - jnp.max(s, axis=-1, keepdims=True))
        p = (e / jnp.sum(e, axis=-1, keepdims=True)).astype(BF)
        o_ref[:, hs] = jnp.dot(p, v_ref[:, hs], preferred_element_type=F32).astype(o_ref.dtype)


def attention(q, k, v):
    n, d = q.shape
    m = k.shape[0]
    ta = min(ATTN_ROWS, n)
    return pl.pallas_call(
        functools.partial(_attn_kernel, hd=d // XATTN_HEADS), grid=(n // ta,),
        in_specs=[pl.BlockSpec((ta, d), lambda i: (i, 0)), pl.BlockSpec((m, d), lambda i: (0, 0)),
                  pl.BlockSpec((m, d), lambda i: (0, 0))],
        out_specs=pl.BlockSpec((ta, d), lambda i: (i, 0)),
        out_shape=jax.ShapeDtypeStruct((n, d), BF),
        compiler_params=_cparams(("parallel",), 32), name="cross_attention",
    )(q, k, v)


def _router_kernel(x_ref, g_ref, wr_ref, br_ref, xn_ref, ids_ref, wts_ref, cnt_ref, carry_ref, *, tr):
    @pl.when(pl.program_id(0) == 0)
    def _():
        carry_ref[...] = jnp.zeros_like(carry_ref)

    xn = _rms(x_ref[...], g_ref[...])
    xn_ref[...] = xn
    logits = jnp.dot(xn, wr_ref[...], preferred_element_type=F32, precision=lax.Precision.HIGHEST) + br_ref[...]
    lane = lax.broadcasted_iota(jnp.int32, (tr, LANES), 1).astype(F32)
    neg = -jnp.inf

    def first_argmax(vals):
        top = jnp.max(vals, axis=-1, keepdims=True)
        return top, jnp.min(jnp.where(vals == top, lane, float(LANES)), axis=-1, keepdims=True)

    gl = jnp.where(lane < N_GROUPS, logits, neg)
    gmax, grp = first_argmax(gl)
    g_prob = 1.0 / jnp.sum(jnp.exp(gl - gmax), axis=-1, keepdims=True)
    lo = N_GROUPS + EXPERTS_PER_GROUP * grp
    el = jnp.where(lane >= lo, jnp.where(lane < lo + EXPERTS_PER_GROUP, logits, neg), neg)
    v1, l1 = first_argmax(el)
    v2, l2 = first_argmax(jnp.where(lane == l1, neg, el))
    e2 = jnp.exp(v2 - v1)
    w1 = g_prob / (1.0 + e2)
    w2 = g_prob * e2 / (1.0 + e2)
    hot1 = lane == l1
    hot2 = lane == l2
    hot = jnp.where(hot1, 1.0, jnp.where(hot2, 1.0, 0.0))
    earlier = (lax.broadcasted_iota(jnp.int32, (tr, tr), 1) < lax.broadcasted_iota(jnp.int32, (tr, tr), 0))
    before = jnp.dot(jnp.where(earlier, 1.0, 0.0).astype(BF), hot.astype(BF), preferred_element_type=F32)
    before = before + carry_ref[0:1, :]
    r1 = jnp.sum(jnp.where(hot1, before, 0.0), axis=-1, keepdims=True)
    r2 = jnp.sum(jnp.where(hot2, before, 0.0), axis=-1, keepdims=True)
    carry_ref[0:1, :] = carry_ref[0:1, :] + jnp.sum(hot, axis=0, keepdims=True)
    ids = jnp.where(lane == 0, l1 - N_GROUPS, jnp.where(lane == 1, l2 - N_GROUPS,
                    jnp.where(lane == 2, r1, jnp.where(lane == 3, r2, 0.0))))
    ids_ref[...] = ids.astype(jnp.int32)
    wts_ref[...] = jnp.where(lane == 0, w1, jnp.where(lane == 1, w2, 0.0))
    cnt_ref[...] = carry_ref[...]


def route(h, gain, w_router, b_router):
    n, d = h.shape
    tr = min(ROUTER_ROWS, n)
    return pl.pallas_call(
        functools.partial(_router_kernel, tr=tr), grid=(n // tr,),
        in_specs=[pl.BlockSpec((tr, d), lambda i: (i, 0)), pl.BlockSpec((1, d), lambda i: (0, 0)),
                  pl.BlockSpec((d, LANES), lambda i: (0, 0)), pl.BlockSpec((1, LANES), lambda i: (0, 0))],
        out_specs=[pl.BlockSpec((tr, d), lambda i: (i, 0)),
                   pl.BlockSpec((tr, LANES), lambda i: (i, 0)), pl.BlockSpec((tr, LANES), lambda i: (i, 0)),
                   pl.BlockSpec((SUBLANES, LANES), lambda i: (0, 0))],
        out_shape=[jax.ShapeDtypeStruct((n, d), F32), jax.ShapeDtypeStruct((n, LANES), jnp.int32),
                   jax.ShapeDtypeStruct((n, LANES), F32), jax.ShapeDtypeStruct((SUBLANES, LANES), F32)],
        scratch_shapes=[pltpu.VMEM((SUBLANES, LANES), F32)],
        compiler_params=_cparams(("arbitrary",), 40), name="moe_router",
    )(h, gain.reshape(1, d), w_router, b_router)


def _row_copy(src_hbm, row, dst, dst_row, sem):
    return pltpu.make_async_copy(src_hbm.at[pl.ds(row, 1), :], dst.at[pl.ds(dst_row, 1), :], sem)


def _rows_wait(src_hbm, dst, sem):
    pltpu.make_async_copy(src_hbm.at[pl.ds(0, dst.shape[0]), :], dst, sem).wait()


def _is_new_expert(blk_ref, b):
    return jnp.logical_or(b == 0, blk_ref[b] != blk_ref[jnp.maximum(b - 1, 0)])


def _weight_copies(w_hbm, layer, expert, stage, sem):
    rows = stage.shape[0] // WEIGHT_DMA_PARTS
    return [pltpu.make_async_copy(w_hbm.at[layer, expert, pl.ds(p * rows, rows), :],
                                  stage.at[pl.ds(p * rows, rows), :], sem) for p in range(WEIGHT_DMA_PARTS)]


def _moe_expert_kernel(blk_ref, nxt_ref, nact_ref, rtok_ref, xn_hbm, wg_hbm, wu_hbm, wd_hbm, ys_ref,
                       buf, stage_g, stage_u, stage_d, wgb, wub, wdb, xb, hid, sem, wsem, *, tb, nb, layer):
    b = pl.program_id(0)
    nact = nact_ref[0]
    slot = b % 2
    f = hid.shape[1]
    d = ys_ref.shape[1]
    up_chunks = f // MOE_COL_CHUNK
    rows_per_chunk = tb // up_chunks

    def expert_copies(e):
        return (_weight_copies(wg_hbm, layer, e, stage_g, wsem.at[0])
                + _weight_copies(wu_hbm, layer, e, stage_u, wsem.at[1])
                + _weight_copies(wd_hbm, layer, e, stage_d, wsem.at[2]))

    @pl.when(b == 0)
    def _():
        for cp in expert_copies(blk_ref[0]):
            cp.start(priority=DMA_QUEUE_BULK)

        def body(r, carry):
            _row_copy(xn_hbm, rtok_ref[r], buf.at[0], r, sem.at[0]).start(priority=DMA_QUEUE_ROWS)
            return carry

        lax.fori_loop(0, tb, body, 0, unroll=8)

    @pl.when(jnp.logical_and(b < nact, _is_new_expert(blk_ref, b)))
    def _():
        for cp in expert_copies(blk_ref[b]):
            cp.wait()
        _fill_bf16(wgb, stage_g)
        _fill_bf16(wub, stage_u)
        _fill_bf16(wdb, stage_d)

        @pl.when(nxt_ref[b] >= 0)
        def _():
            for cp in expert_copies(nxt_ref[b]):
                cp.start(priority=DMA_QUEUE_BULK)

    def compute(prefetch_next):
        _rows_wait(xn_hbm, buf.at[slot], sem.at[slot])
        xb[...] = buf[slot].astype(BF)
        for c in range(up_chunks):
            if prefetch_next:
                for r in range(c * rows_per_chunk, (c + 1) * rows_per_chunk):
                    _row_copy(xn_hbm, rtok_ref[(b + 1) * tb + r], buf.at[1 - slot], r,
                              sem.at[1 - slot]).start(priority=DMA_QUEUE_ROWS)
            cs = slice(c * MOE_COL_CHUNK, (c + 1) * MOE_COL_CHUNK)
            gate = jnp.dot(xb[...], wgb[:, cs], preferred_element_type=F32)
            up = jnp.dot(xb[...], wub[:, cs], preferred_element_type=F32)
            hid[:, cs] = (gate * _sigmoid(gate) * up).astype(hid.dtype)
        for c in range(0, d, MOE_COL_CHUNK):
            cs = slice(c, c + MOE_COL_CHUNK)
            ys_ref[:, cs] = jnp.dot(hid[...], wdb[:, cs], preferred_element_type=F32)

    @pl.when(jnp.logical_and(b < nact, b + 1 < nb))
    def _():
        compute(True)

    @pl.when(jnp.logical_and(b < nact, b + 1 == nb))
    def _():
        compute(False)

    @pl.when(b >= nact)
    def _():
        ys_ref[...] = jnp.zeros_like(ys_ref)

    @pl.when(b == nact)
    def _():
        _rows_wait(xn_hbm, buf.at[slot], sem.at[slot])


def _combine_kernel(dest_ref, ys_hbm, h_ref, w_ref, gain_ref, o_ref, buf, sem, *, tc, final_norm):
    i = pl.program_id(0)

    def issue(blk, slot):
        base = blk * tc

        def body(r, carry):
            for k in range(TOP_K):
                _row_copy(ys_hbm, dest_ref[TOP_K * (base + r) + k], buf.at[slot], k * tc + r,
                          sem.at[slot]).start(priority=k)
            return carry

        lax.fori_loop(0, tc, body, 0, unroll=8)

    @pl.when(i == 0)
    def _():
        issue(0, 0)

    @pl.when(i + 1 < pl.num_programs(0))
    def _():
        issue(i + 1, (i + 1) % 2)

    slot = i % 2
    _rows_wait(ys_hbm, buf.at[slot], sem.at[slot])
    out = h_ref[...] + (w_ref[:, 0:1] * buf[slot, 0:tc, :] + w_ref[:, 1:2] * buf[slot, tc:TOP_K * tc, :])
    if final_norm:
        out = _rms(out, gain_ref[...])
    o_ref[...] = out


def hier_moe(h, ffn_gain, wg_r, bg_r, we_r, be_r, w_gate, w_up, w_down, layer, final_gain=None):
    n, d = h.shape
    f = w_gate.shape[-1]
    a = n * TOP_K
    tb = MOE_BLOCK
    nb = a // tb + N_EXPERTS

    pad = LANES - N_GROUPS - N_EXPERTS
    w_router = jnp.concatenate([wg_r, we_r, jnp.zeros((d, pad), F32)], axis=1)
    b_router = jnp.concatenate([bg_r, be_r, jnp.zeros((pad,), F32)]).reshape(1, LANES)
    xn, ids, wts, cnt = route(h, ffn_gain, w_router, b_router)

    counts = cnt[0, N_GROUPS:N_GROUPS + N_EXPERTS].astype(jnp.int32)
    nblk = (counts + tb - 1) // tb
    bend = jnp.cumsum(nblk)
    nact = bend[-1]
    expert = ids[:, :TOP_K]
    dest = ((bend - nblk)[expert] * tb + ids[:, TOP_K:2 * TOP_K]).reshape(a)
    blk = jnp.minimum(jnp.arange(nb, dtype=jnp.int32), jnp.maximum(nact - 1, 0))
    owner = lambda bi: jnp.minimum(jnp.searchsorted(bend, bi, side='right'), N_EXPERTS - 1).astype(jnp.int32)
    blk_e = owner(blk)
    after = bend[blk_e]
    nxt_e = jnp.where(after < nact, owner(after), -1)
    tok = jnp.repeat(jnp.arange(n, dtype=jnp.int32), TOP_K)
    row_tok = (jnp.arange(nb * tb, dtype=jnp.int32) % n).at[dest].set(tok)
    nact1 = nact.reshape(1).astype(jnp.int32)

    any_space = pl.BlockSpec(memory_space=pl.ANY)
    ys = pl.pallas_call(
        functools.partial(_moe_expert_kernel, tb=tb, nb=nb, layer=layer),
        grid_spec=pltpu.PrefetchScalarGridSpec(
            num_scalar_prefetch=4, grid=(nb,),
            in_specs=[any_space, any_space, any_space, any_space],
            out_specs=pl.BlockSpec((tb, d), lambda b, *_: (b, 0)),
            scratch_shapes=[pltpu.VMEM((2, tb, d), F32),
                            pltpu.VMEM((d, f), F32), pltpu.VMEM((d, f), F32), pltpu.VMEM((f, d), F32),
                            pltpu.VMEM((d, f), BF), pltpu.VMEM((d, f), BF), pltpu.VMEM((f, d), BF),
                            pltpu.VMEM((tb, d), BF), pltpu.VMEM((tb, f), BF),
                            pltpu.SemaphoreType.DMA((2,)), pltpu.SemaphoreType.DMA((3,))]),
        out_shape=jax.ShapeDtypeStruct((nb * tb, d), F32),
        compiler_params=_cparams(("arbitrary",), 58), name="moe_experts",
    )(blk_e, nxt_e, nact1, row_tok, xn, w_gate, w_up, w_down)

    tc = min(COMBINE_ROWS, n)
    gain = (final_gain if final_gain is not None else ffn_gain).reshape(1, d)
    return pl.pallas_call(
        functools.partial(_combine_kernel, tc=tc, final_norm=final_gain is not None),
        grid_spec=pltpu.PrefetchScalarGridSpec(
            num_scalar_prefetch=1, grid=(n // tc,),
            in_specs=[pl.BlockSpec(memory_space=pl.ANY), pl.BlockSpec((tc, d), lambda i, *_: (i, 0)),
                      pl.BlockSpec((tc, LANES), lambda i, *_: (i, 0)), pl.BlockSpec((1, d), lambda i, *_: (0, 0))],
            out_specs=pl.BlockSpec((tc, d), lambda i, *_: (i, 0)),
            scratch_shapes=[pltpu.VMEM((2, TOP_K * tc, d), F32), pltpu.SemaphoreType.DMA((2,))]),
        out_shape=jax.ShapeDtypeStruct((n, d), F32),
        compiler_params=_cparams(("arbitrary",), 40), name="moe_combine",
    )(dest, ys, h, wts, gain)


def even_mixer(h, gain, w_in, b_gates, pool_w, pool_scale, head_gain, w_out):
    d = h.shape[1]
    pool_width = N_POOL_GROUPS * pool_w.shape[-1]
    mlstm_width = head_gain.shape[0]
    main_cols = pool_width + 4 * mlstm_width
    n_gates = 2 * MLSTM_HEADS
    gate_w = jnp.pad(w_in[:, main_cols:], ((0, 0), (0, LANES - n_gates)))
    gate_b = jnp.pad(b_gates, (0, LANES - n_gates)).reshape(1, LANES)
    z, gates = norm_matmul(h, gain, w_in, main_cols, gate_w, gate_b)
    y_p = pool_mixer(z, pool_w, pool_scale)
    y_m = mlstm_mixer(z, gates, head_gain, pool_width)
    assert pool_width == mlstm_width and pool_width + mlstm_width == d
    return matmul_residual([y_p, y_m], w_out, h)


def odd_mixer(h, gain, w_in, conv_w, w_out):
    z = norm_matmul(h, gain, w_in, w_in.shape[1])
    return matmul_residual([conv_mixer(z, conv_w)], w_out, h)


def cross_attn(h, mem, gain, mem_gain, wq, wk, wv, wo, layer):
    d = h.shape[1]
    k = norm_matmul(mem, mem_gain, wk, d, layer=layer)
    v = norm_matmul(mem, mem_gain, wv, d, layer=layer)
    q = norm_matmul_resident(h, gain, wq, layer)
    return matmul_residual([attention(q, k, v)], wo, h, layer=layer)


def kernel(x, mem, mix_norm, xattn_norm, mem_norm, ffn_norm, final_norm, ev_w_in, ev_b_gates, ev_pool_w, ev_pool_scale, ev_head_norm, ev_w_out, od_w_in, od_conv_w, od_w_out, xa_wq, xa_wk, xa_wv, xa_wo, rt_group_w, rt_group_b, rt_expert_w, rt_expert_b, ex_w_gate, ex_w_up, ex_w_down):
    depth = mix_norm.shape[0]
    h = x[0]
    m = mem[0]
    for layer in range(depth):
        j = layer // 2
        if layer % 2 == 0:
            h = even_mixer(h, mix_norm[layer], ev_w_in[j], ev_b_gates[j], ev_pool_w[j], ev_pool_scale[j],
                           ev_head_norm[j], ev_w_out[j])
        else:
            h = odd_mixer(h, mix_norm[layer], od_w_in[j], od_conv_w[j], od_w_out[j])
        h = cross_attn(h, m, xattn_norm[layer], mem_norm[layer], xa_wq, xa_wk, xa_wv, xa_wo, layer)
        h = hier_moe(h, ffn_norm[layer], rt_group_w[layer], rt_group_b[layer], rt_expert_w[layer],
                     rt_expert_b[layer], ex_w_gate, ex_w_up, ex_w_down, layer,
                     final_gain=final_norm if layer == depth - 1 else None)
    return h[None]
```

```python
import functools

import jax
import jax.numpy as jnp
from jax import lax
from jax.experimental import pallas as pl
from jax.experimental.pallas import tpu as pltpu

F32 = jnp.float32
BF = jnp.bfloat16
EPS = 1e-6

POOL_WINDOWS = (2, 4, 8, 16)
N_POOL_GROUPS = 4
MLSTM_HEADS = 4
FORGET_LANE0 = MLSTM_HEADS
XATTN_HEADS = 4
N_GROUPS = 4
EXPERTS_PER_GROUP = 8
N_EXPERTS = N_GROUPS * EXPERTS_PER_GROUP
TOP_K = 2
CONV_WIDTH = 3

LANES = 128
SUBLANES = 8
DMA_QUEUE_ROWS = 0
DMA_QUEUE_BULK = 1

ROW_TILE = 1024
COL_TILE = 1024
OUT_ROW_TILE = 512
CAST_ROWS = 64
CAST_UNROLL = 4
MOE_COL_CHUNK = 256
WEIGHT_DMA_PARTS = 4
MLSTM_CHUNK = 256
POOL_ROWS = 512
POOL_HALO = 128
CONV_ROWS = 512
CONV_COLS = 512
CONV_HALO = 16
ATTN_ROWS = 512
ROUTER_ROWS = 512
MOE_BLOCK = 256
COMBINE_ROWS = 256


def _cparams(semantics, vmem_mib):
    return pltpu.CompilerParams(dimension_semantics=semantics, vmem_limit_bytes=vmem_mib * 1024 * 1024)


def _sigmoid(x):
    return 1.0 / (1.0 + jnp.exp(-x))


def _log_sigmoid(x):
    return jnp.minimum(x, 0.0) - jnp.log(1.0 + jnp.exp(-jnp.abs(x)))


def _rms(x, g):
    ms = jnp.mean(x * x, axis=-1, keepdims=True)
    return x * lax.rsqrt(ms + EPS) * g


def _norm_mm_kernel(x_ref, g_ref, w_ref, o_ref, xn_ref):
    @pl.when(pl.program_id(1) == 0)
    def _():
        xn_ref[...] = _rms(x_ref[...], g_ref[...]).astype(BF)

    o_ref[...] = jnp.dot(xn_ref[...], w_ref[...].astype(BF), preferred_element_type=F32).astype(o_ref.dtype)


_NT = (((1,), (1,)), ((), ()))


def _norm_mm_gates_t_kernel(x_ref, g_ref, wt_ref, wgt_ref, bg_ref, o_ref, gates_ref, xn_ref):
    @pl.when(pl.program_id(1) == 0)
    def _():
        xn = _rms(x_ref[...], g_ref[...]).astype(BF)
        xn_ref[...] = xn
        wg = wgt_ref[...]
        wg = jnp.concatenate([wg, jnp.zeros((LANES - wg.shape[0], wg.shape[1]), F32)], axis=0).astype(BF)
        gates_ref[...] = lax.dot_general(xn, wg, _NT, preferred_element_type=F32) + bg_ref[...]

    o_ref[...] = lax.dot_general(xn_ref[...], wt_ref[...].astype(BF), _NT,
                                 preferred_element_type=F32).astype(o_ref.dtype)


def norm_matmul_gates_t(x, gain, wt, n_cols, n_gates, gate_b, layer):
    n, k = x.shape
    tm = min(ROW_TILE, n)
    tn = COL_TILE
    return pl.pallas_call(
        _norm_mm_gates_t_kernel, grid=(n // tm, n_cols // tn),
        in_specs=[pl.BlockSpec((tm, k), lambda i, j: (i, 0)), pl.BlockSpec((1, k), lambda i, j: (0, 0)),
                  pl.BlockSpec((None, tn, k), lambda i, j: (layer, j, 0)),
                  pl.BlockSpec((None, n_gates, k), lambda i, j: (layer, n_cols // n_gates, 0)),
                  pl.BlockSpec((1, LANES), lambda i, j: (0, 0))],
        out_specs=[pl.BlockSpec((tm, tn), lambda i, j: (i, j)), pl.BlockSpec((tm, LANES), lambda i, j: (i, 0))],
        out_shape=[jax.ShapeDtypeStruct((n, n_cols), BF), jax.ShapeDtypeStruct((n, LANES), F32)],
        scratch_shapes=[pltpu.VMEM((tm, k), BF)],
        compiler_params=_cparams(("parallel", "arbitrary"), 56), name="norm_matmul_gates",
    )(x, gain.reshape(1, k), wt, wt, gate_b)


def _stacked(w, layer):
    return (w[None], 0) if layer is None else (w, layer)


def norm_matmul(x, gain, w, n_cols, layer=None):
    n, k = x.shape
    w, li = _stacked(w, layer)
    tm = min(ROW_TILE, n)
    tn = COL_TILE
    return pl.pallas_call(
        _norm_mm_kernel, grid=(n // tm, n_cols // tn),
        in_specs=[pl.BlockSpec((tm, k), lambda i, j: (i, 0)), pl.BlockSpec((1, k), lambda i, j: (0, 0)),
                  pl.BlockSpec((None, k, tn), lambda i, j: (li, 0, j))],
        out_specs=pl.BlockSpec((tm, tn), lambda i, j: (i, j)),
        out_shape=jax.ShapeDtypeStruct((n, n_cols), BF), scratch_shapes=[pltpu.VMEM((tm, k), BF)],
        compiler_params=_cparams(("parallel", "arbitrary"), 56), name="norm_matmul",
    )(x, gain.reshape(1, k), w)


def _fill_bf16(dst_ref, src_ref):
    def body(i, carry):
        r = pl.multiple_of(i * CAST_ROWS, CAST_ROWS)
        dst_ref[pl.ds(r, CAST_ROWS), :] = src_ref[pl.ds(r, CAST_ROWS), :].astype(BF)
        return carry

    lax.fori_loop(0, src_ref.shape[0] // CAST_ROWS, body, 0, unroll=CAST_UNROLL)


def _norm_mm_resident_kernel(x_ref, g_ref, w_ref, o_ref, wb_ref):
    @pl.when(pl.program_id(0) == 0)
    def _():
        _fill_bf16(wb_ref, w_ref)

    xn = _rms(x_ref[...], g_ref[...]).astype(BF)
    o_ref[...] = jnp.dot(xn, wb_ref[...], preferred_element_type=F32).astype(o_ref.dtype)


def norm_matmul_resident(x, gain, w, layer):
    n, k = x.shape
    d = w.shape[2]
    tm = min(OUT_ROW_TILE, n)
    return pl.pallas_call(
        _norm_mm_resident_kernel, grid=(n // tm,),
        in_specs=[pl.BlockSpec((tm, k), lambda i: (i, 0)), pl.BlockSpec((1, k), lambda i: (0, 0)),
                  pl.BlockSpec((None, k, d), lambda i: (layer, 0, 0), pipeline_mode=pl.Buffered(1))],
        out_specs=pl.BlockSpec((tm, d), lambda i: (i, 0)), out_shape=jax.ShapeDtypeStruct((n, d), BF),
        scratch_shapes=[pltpu.VMEM((k, d), BF)],
        compiler_params=_cparams(("arbitrary",), 52), name="norm_matmul_resident",
    )(x, gain.reshape(1, k), w)


def _mm_res_kernel(*refs, nparts):
    xs = refs[:nparts]
    w_ref, res_ref, o_ref, wb_ref = refs[nparts:]

    @pl.when(pl.program_id(0) == 0)
    def _():
        _fill_bf16(wb_ref, w_ref)

    acc = res_ref[...]
    k0 = 0
    for x_ref in xs:
        kp = x_ref.shape[1]
        acc = acc + jnp.dot(x_ref[...], wb_ref[k0:k0 + kp, :], preferred_element_type=F32)
        k0 += kp
    o_ref[...] = acc


def matmul_residual(xs, w, res, layer=None):
    n, d = res.shape
    w, li = _stacked(w, layer)
    k = w.shape[1]
    tm = min(OUT_ROW_TILE, n)
    in_specs = [pl.BlockSpec((tm, x.shape[1]), lambda i: (i, 0)) for x in xs]
    in_specs += [pl.BlockSpec((None, k, d), lambda i: (li, 0, 0), pipeline_mode=pl.Buffered(1)),
                 pl.BlockSpec((tm, d), lambda i: (i, 0))]
    return pl.pallas_call(
        functools.partial(_mm_res_kernel, nparts=len(xs)), grid=(n // tm,), in_specs=in_specs,
        out_specs=pl.BlockSpec((tm, d), lambda i: (i, 0)), out_shape=jax.ShapeDtypeStruct((n, d), F32),
        scratch_shapes=[pltpu.VMEM((k, d), BF)],
        compiler_params=_cparams(("arbitrary",), 52), name="matmul_residual",
    )(*xs, w, res)


def _pool_kernel(cur_ref, prev_ref, w_ref, sc_ref, o_ref, *, tp):
    i, j = pl.program_id(0), pl.program_id(1)
    win = lax.shift_left(jnp.int32(POOL_WINDOWS[0]), j)
    cur = cur_ref[...]
    dist = lax.broadcasted_iota(jnp.int32, (tp, tp), 0) - lax.broadcasted_iota(jnp.int32, (tp, tp), 1)
    band = jnp.where(dist >= 0, jnp.where(dist < win, 1.0, 0.0), 0.0).astype(BF)
    s = jnp.dot(band, cur, preferred_element_type=F32)
    distp = (lax.broadcasted_iota(jnp.int32, (tp, POOL_HALO), 0) + POOL_HALO
             - lax.broadcasted_iota(jnp.int32, (tp, POOL_HALO), 1))
    limit = jnp.where(i > 0, win, 0)
    bandp = jnp.where(distp < limit, 1.0, 0.0).astype(BF)
    s = s + jnp.dot(bandp, prev_ref[...], preferred_element_type=F32)
    pos = i * tp + lax.broadcasted_iota(jnp.int32, (tp, 1), 0)
    cnt = jnp.minimum(pos + 1, win).astype(F32)
    d = s / cnt - cur.astype(F32)
    y = jnp.dot(d.astype(BF), w_ref[...].astype(BF), preferred_element_type=F32) * sc_ref[...]
    o_ref[...] = y.astype(o_ref.dtype)


def pool_mixer(z, pool_w, pool_scale):
    n = z.shape[0]
    gdim = pool_w.shape[-1]
    width = N_POOL_GROUPS * gdim
    tp = min(POOL_ROWS, n)
    halo_blocks = tp // POOL_HALO
    return pl.pallas_call(
        functools.partial(_pool_kernel, tp=tp), grid=(n // tp, N_POOL_GROUPS),
        in_specs=[
            pl.BlockSpec((tp, gdim), lambda i, j: (i, j)),
            pl.BlockSpec((POOL_HALO, gdim), lambda i, j: (jnp.maximum(i * halo_blocks - 1, 0), j)),
            pl.BlockSpec((None, gdim, gdim), lambda i, j: (j, 0, 0)),
            pl.BlockSpec((1, gdim), lambda i, j: (0, j)),
        ],
        out_specs=pl.BlockSpec((tp, gdim), lambda i, j: (i, j)),
        out_shape=jax.ShapeDtypeStruct((n, width), BF),
        compiler_params=_cparams(("parallel", "arbitrary"), 32), name="pool_mixer",
    )(z, z, pool_w, pool_scale.reshape(1, width))


def _mlstm_kernel(q_ref, k_ref, v_ref, o_ref, g_ref, gain_ref, y_ref, ct_ref, m_ref, *, chunk, dh):
    c = pl.program_id(0)

    @pl.when(c == 0)
    def _():
        ct_ref[...] = jnp.zeros_like(ct_ref)
        m_ref[...] = jnp.zeros_like(m_ref)

    g = g_ref[...]
    lf = _log_sigmoid(g)
    row = lax.broadcasted_iota(jnp.int32, (chunk, chunk), 0)
    col = lax.broadcasted_iota(jnp.int32, (chunk, chunk), 1)
    causal = col <= row
    ltri = jnp.where(causal, 1.0, 0.0).astype(BF)
    hi = lf.astype(BF)
    r1 = lf - hi.astype(F32)
    mid = r1.astype(BF)
    lo = (r1 - mid.astype(F32)).astype(BF)
    bcum = (jnp.dot(ltri, hi, preferred_element_type=F32) + jnp.dot(ltri, mid, preferred_element_type=F32)
            + jnp.dot(ltri, lo, preferred_element_type=F32))
    g_t = g.T
    b_t = bcum.T
    ones_col = jnp.where(lax.broadcasted_iota(jnp.int32, (chunk, LANES), 1) == 0, 1.0, 0.0).astype(BF)

    for h in range(MLSTM_HEADS):
        hs = slice(h * dh, (h + 1) * dh)
        fl = FORGET_LANE0 + h
        bc = bcum[:, fl:fl + 1]
        br = b_t[fl:fl + 1, :]
        ir = g_t[h:h + 1, :]
        b_last = bcum[chunk - 1:chunk, fl:fl + 1]
        m_prev = m_ref[h][:, 0:1]

        dmat = jnp.where(causal, bc + (ir - br), -jnp.inf)
        inter = bc + m_prev
        m_t = jnp.maximum(jnp.max(dmat, axis=1, keepdims=True), inter)
        w_inter = jnp.exp(inter - m_t)
        p = jnp.exp(dmat - m_t)

        qh = q_ref[:, hs] * (dh ** -0.5)
        kh = k_ref[:, hs]
        v_aug = jnp.concatenate([v_ref[:, hs], ones_col], axis=1)
        s = lax.dot_general(qh, kh, (((1,), (1,)), ((), ())), preferred_element_type=F32)
        sc = (s * p).astype(BF)
        ct = ct_ref[h]
        num_aug = (w_inter * jnp.dot(qh, ct.astype(BF), preferred_element_type=F32)
                   + jnp.dot(sc, v_aug, preferred_element_type=F32))
        num = num_aug[:, :dh]
        den = num_aug[:, dh:dh + 1]
        hout = num / jnp.maximum(jnp.abs(den), jnp.exp(-m_t))

        yn = _rms(hout, gain_ref[:, hs])
        y_ref[:, hs] = (_sigmoid(o_ref[:, hs].astype(F32)) * yn).astype(y_ref.dtype)

        d_end = b_last - br + ir
        m_new = jnp.maximum(b_last + m_prev, jnp.max(d_end, axis=1, keepdims=True))
        a_prev = jnp.exp(b_last + m_prev - m_new)
        a_s = jnp.exp(d_end - m_new)
        k_t = (kh.astype(F32).T * a_s).astype(BF)
        ct_ref[h] = a_prev * ct + jnp.dot(k_t, v_aug, preferred_element_type=F32)
        m_ref[h] = jnp.broadcast_to(m_new, (1, LANES))


def mlstm_mixer(z, gates, head_gain, col0):
    n = z.shape[0]
    width = head_gain.shape[0]
    dh = width // MLSTM_HEADS
    chunk = min(MLSTM_CHUNK, n)
    base = col0 // width
    qkvo = [pl.BlockSpec((chunk, width), lambda c, p=p: (c, base + p)) for p in range(4)]
    return pl.pallas_call(
        functools.partial(_mlstm_kernel, chunk=chunk, dh=dh), grid=(n // chunk,),
        in_specs=qkvo + [pl.BlockSpec((chunk, LANES), lambda c: (c, 0)), pl.BlockSpec((1, width), lambda c: (0, 0))],
        out_specs=pl.BlockSpec((chunk, width), lambda c: (c, 0)),
        out_shape=jax.ShapeDtypeStruct((n, width), BF),
        scratch_shapes=[pltpu.VMEM((MLSTM_HEADS, dh, dh + LANES), F32), pltpu.VMEM((MLSTM_HEADS, 1, LANES), F32)],
        compiler_params=_cparams(("arbitrary",), 32), name="mlstm_mixer",
    )(z, z, z, z, gates, head_gain.reshape(1, width))


def _conv_kernel(b_ref, c_ref, u_ref, cp_ref, up_ref, w_ref, o_ref, *, tr):
    i = pl.program_id(0)
    zc = c_ref[...].astype(F32) * u_ref[...].astype(F32)
    zp = cp_ref[...].astype(F32) * up_ref[...].astype(F32)
    zp = jnp.where(i > 0, zp, 0.0)
    row = lax.broadcasted_iota(jnp.int32, zc.shape, 0)
    acc = w_ref[CONV_WIDTH - 1:CONV_WIDTH, :] * zc
    for back in range(1, CONV_WIDTH):
        shifted = pltpu.roll(zc, back, 0)
        for r in range(back):
            shifted = jnp.where(row == r, zp[CONV_HALO - back + r:CONV_HALO - back + r + 1, :], shifted)
        acc = acc + w_ref[CONV_WIDTH - 1 - back:CONV_WIDTH - back, :] * shifted
    o_ref[...] = (b_ref[...].astype(F32) * acc).astype(o_ref.dtype)


def conv_mixer(z, conv_w):
    n = z.shape[0]
    d = conv_w.shape[1]
    tr = min(CONV_ROWS, n)
    tc = CONV_COLS
    nb = d // tc
    halo_blocks = tr // CONV_HALO
    prev = lambda part: pl.BlockSpec(
        (CONV_HALO, tc), lambda i, j: (jnp.maximum(i * halo_blocks - 1, 0), part * nb + j))
    cur = lambda part: pl.BlockSpec((tr, tc), lambda i, j: (i, part * nb + j))
    return pl.pallas_call(
        functools.partial(_conv_kernel, tr=tr), grid=(n // tr, nb),
        in_specs=[cur(0), cur(1), cur(2), prev(1), prev(2), pl.BlockSpec((CONV_WIDTH, tc), lambda i, j: (0, j))],
        out_specs=pl.BlockSpec((tr, tc), lambda i, j: (i, j)),
        out_shape=jax.ShapeDtypeStruct((n, d), BF),
        compiler_params=_cparams(("parallel", "arbitrary"), 32), name="conv_mixer",
    )(z, z, z, z, z, conv_w)


def _attn_kernel(q_ref, k_ref, v_ref, o_ref, *, hd):
    scale = hd ** -0.5
    for h in range(XATTN_HEADS):
        hs = slice(h * hd, (h + 1) * hd)
        s = lax.dot_general(q_ref[:, hs], k_ref[:, hs], (((1,), (1,)), ((), ())),
                            preferred_element_type=F32) * scale
        e = jnp.exp(s - jnp.max(s, axis=-1, keepdims=True))
        p = (e / jnp.sum(e, axis=-1, keepdims=True)).astype(BF)
        o_ref[:, hs] = jnp.dot(p, v_ref[:, hs], preferred_element_type=F32).astype(o_ref.dtype)


def attention(q, k, v):
    n, d = q.shape
    m = k.shape[0]
    ta = min(ATTN_ROWS, n)
    return pl.pallas_call(
        functools.partial(_attn_kernel, hd=d // XATTN_HEADS), grid=(n // ta,),
        in_specs=[pl.BlockSpec((ta, d), lambda i: (i, 0)), pl.BlockSpec((m, d), lambda i: (0, 0)),
                  pl.BlockSpec((m, d), lambda i: (0, 0))],
        out_specs=pl.BlockSpec((ta, d), lambda i: (i, 0)),
        out_shape=jax.ShapeDtypeStruct((n, d), BF),
        compiler_params=_cparams(("parallel",), 32), name="cross_attention",
    )(q, k, v)


def _router_kernel(x_ref, g_ref, wr_ref, br_ref, xn_ref, ids_ref, wts_ref, cnt_ref, carry_ref, whi_ref, wlo_ref,
                   *, tr):
    @pl.when(pl.program_id(0) == 0)
    def _():
        carry_ref[...] = jnp.zeros_like(carry_ref)

    @pl.when(pl.program_id(0) == 0)
    def _():
        w = wr_ref[...]
        w_hi = w.astype(BF)
        whi_ref[...] = w_hi
        wlo_ref[...] = (w - w_hi.astype(F32)).astype(BF)

    xn = _rms(x_ref[...], g_ref[...])
    xn_ref[...] = xn
    x_hi = xn.astype(BF)
    x_lo = (xn - x_hi.astype(F32)).astype(BF)
    logits = (jnp.dot(x_hi, whi_ref[...], preferred_element_type=F32)
              + jnp.dot(x_lo, whi_ref[...], preferred_element_type=F32)
              + jnp.dot(x_hi, wlo_ref[...], preferred_element_type=F32)) + br_ref[...]
    lane = lax.broadcasted_iota(jnp.int32, (tr, LANES), 1).astype(F32)
    neg = -jnp.inf

    def first_argmax(vals):
        top = jnp.max(vals, axis=-1, keepdims=True)
        return top, jnp.min(jnp.where(vals == top, lane, float(LANES)), axis=-1, keepdims=True)

    gl = jnp.where(lane < N_GROUPS, logits, neg)
    gmax, grp = first_argmax(gl)
    g_prob = 1.0 / jnp.sum(jnp.exp(gl - gmax), axis=-1, keepdims=True)
    lo = N_GROUPS + EXPERTS_PER_GROUP * grp
    el = jnp.where(lane >= lo, jnp.where(lane < lo + EXPERTS_PER_GROUP, logits, neg), neg)
    v1, l1 = first_argmax(el)
    v2, l2 = first_argmax(jnp.where(lane == l1, neg, el))
    e2 = jnp.exp(v2 - v1)
    w1 = g_prob / (1.0 + e2)
    w2 = g_prob * e2 / (1.0 + e2)
    hot1 = lane == l1
    hot2 = lane == l2
    hot = jnp.where(hot1, 1.0, jnp.where(hot2, 1.0, 0.0))
    earlier = (lax.broadcasted_iota(jnp.int32, (tr, tr), 1) < lax.broadcasted_iota(jnp.int32, (tr, tr), 0))
    before = jnp.dot(jnp.where(earlier, 1.0, 0.0).astype(BF), hot.astype(BF), preferred_element_type=F32)
    before = before + carry_ref[0:1, :]
    r1 = jnp.sum(jnp.where(hot1, before, 0.0), axis=-1, keepdims=True)
    r2 = jnp.sum(jnp.where(hot2, before, 0.0), axis=-1, keepdims=True)
    carry_ref[0:1, :] = carry_ref[0:1, :] + jnp.sum(hot, axis=0, keepdims=True)
    ids = jnp.where(lane == 0, l1 - N_GROUPS, jnp.where(lane == 1, l2 - N_GROUPS,
                    jnp.where(lane == 2, r1, jnp.where(lane == 3, r2, 0.0))))
    ids_ref[...] = ids.astype(jnp.int32)
    wts_ref[...] = jnp.where(lane == 0, w1, jnp.where(lane == 1, w2, 0.0))
    cnt_ref[...] = carry_ref[...]


def route(h, gain, w_router, b_router):
    n, d = h.shape
    tr = min(ROUTER_ROWS, n)
    return pl.pallas_call(
        functools.partial(_router_kernel, tr=tr), grid=(n // tr,),
        in_specs=[pl.BlockSpec((tr, d), lambda i: (i, 0)), pl.BlockSpec((1, d), lambda i: (0, 0)),
                  pl.BlockSpec((d, LANES), lambda i: (0, 0)), pl.BlockSpec((1, LANES), lambda i: (0, 0))],
        out_specs=[pl.BlockSpec((tr, d), lambda i: (i, 0)),
                   pl.BlockSpec((tr, LANES), lambda i: (i, 0)), pl.BlockSpec((tr, LANES), lambda i: (i, 0)),
                   pl.BlockSpec((SUBLANES, LANES), lambda i: (0, 0))],
        out_shape=[jax.ShapeDtypeStruct((n, d), F32), jax.ShapeDtypeStruct((n, LANES), jnp.int32),
                   jax.ShapeDtypeStruct((n, LANES), F32), jax.ShapeDtypeStruct((SUBLANES, LANES), F32)],
        scratch_shapes=[pltpu.VMEM((SUBLANES, LANES), F32), pltpu.VMEM((d, LANES), BF), pltpu.VMEM((d, LANES), BF)],
        compiler_params=_cparams(("arbitrary",), 40), name="moe_router",
    )(h, gain.reshape(1, d), w_router, b_router)


def _dest_kernel(ids_ref, start_ref, o_ref, *, tr):
    ids = ids_ref[...].astype(F32)
    lane = lax.broadcasted_iota(jnp.int32, (tr, LANES), 1).astype(F32)
    start = start_ref[...]
    rows = []
    for k in range(TOP_K):
        first = jnp.sum(jnp.where(lane == ids[:, k:k + 1], start, 0.0), axis=-1, keepdims=True)
        rows.append(first + ids[:, TOP_K + k:TOP_K + k + 1])
    packed = jnp.where(lane == 0, rows[0], jnp.where(lane == 1, rows[1], 0.0))
    o_ref[...] = packed.T[0:SUBLANES, :].astype(jnp.int32)


def assignment_rows(ids, start_rows):
    n = ids.shape[0]
    tr = min(ROUTER_ROWS, n)
    out = pl.pallas_call(
        functools.partial(_dest_kernel, tr=tr), grid=(n // tr,),
        in_specs=[pl.BlockSpec((tr, LANES), lambda i: (i, 0)), pl.BlockSpec((1, LANES), lambda i: (0, 0))],
        out_specs=pl.BlockSpec((SUBLANES, tr), lambda i: (0, i)),
        out_shape=jax.ShapeDtypeStruct((SUBLANES, n), jnp.int32),
        compiler_params=_cparams(("parallel",), 32), name="moe_assignment_rows",
    )(ids, start_rows)
    return out[:TOP_K]


def _row_copy(src_hbm, row, dst, dst_row, sem):
    return pltpu.make_async_copy(src_hbm.at[pl.ds(row, 1), :], dst.at[pl.ds(dst_row, 1), :], sem)


def _rows_wait(src_hbm, dst, sem):
    pltpu.make_async_copy(src_hbm.at[pl.ds(0, dst.shape[0]), :], dst, sem).wait()


def _is_new_expert(blk_ref, b):
    return jnp.logical_or(b == 0, blk_ref[b] != blk_ref[jnp.maximum(b - 1, 0)])


def _weight_copies(w_hbm, layer, expert, stage, sem):
    rows = stage.shape[0] // WEIGHT_DMA_PARTS
    return [pltpu.make_async_copy(w_hbm.at[layer, expert, pl.ds(p * rows, rows), :],
                                  stage.at[pl.ds(p * rows, rows), :], sem) for p in range(WEIGHT_DMA_PARTS)]


def _moe_expert_kernel(blk_ref, nxt_ref, nact_ref, rtok_ref, xn_hbm, wg_hbm, wu_hbm, wd_hbm, ys_ref,
                       buf, stage_g, stage_u, stage_d, wgb, wub, wdb, xb, hid, sem, wsem, *, tb, nb, layer):
    b = pl.program_id(0)
    nact = nact_ref[0]
    slot = b % 2
    f = hid.shape[1]
    d = ys_ref.shape[1]
    up_chunks = f // MOE_COL_CHUNK
    rows_per_chunk = tb // up_chunks

    def expert_copies(e):
        return (_weight_copies(wg_hbm, layer, e, stage_g, wsem.at[0])
                + _weight_copies(wu_hbm, layer, e, stage_u, wsem.at[1])
                + _weight_copies(wd_hbm, layer, e, stage_d, wsem.at[2]))

    @pl.when(b == 0)
    def _():
        for cp in expert_copies(blk_ref[0]):
            cp.start(priority=DMA_QUEUE_BULK)

        def body(r, carry):
            _row_copy(xn_hbm, rtok_ref[r], buf.at[0], r, sem.at[0]).start(priority=DMA_QUEUE_ROWS)
            return carry

        lax.fori_loop(0, tb, body, 0, unroll=8)

    @pl.when(jnp.logical_and(b < nact, _is_new_expert(blk_ref, b)))
    def _():
        for cp in expert_copies(blk_ref[b]):
            cp.wait()
        _fill_bf16(wgb, stage_g)
        _fill_bf16(wub, stage_u)
        _fill_bf16(wdb, stage_d)

        @pl.when(nxt_ref[b] >= 0)
        def _():
            for cp in expert_copies(nxt_ref[b]):
                cp.start(priority=DMA_QUEUE_BULK)

    def compute(prefetch_next):
        _rows_wait(xn_hbm, buf.at[slot], sem.at[slot])
        xb[...] = buf[slot].astype(BF)
        for c in range(up_chunks):
            if prefetch_next:
                for r in range(c * rows_per_chunk, (c + 1) * rows_per_chunk):
                    _row_copy(xn_hbm, rtok_ref[(b + 1) * tb + r], buf.at[1 - slot], r,
                              sem.at[1 - slot]).start(priority=DMA_QUEUE_ROWS)
            cs = slice(c * MOE_COL_CHUNK, (c + 1) * MOE_COL_CHUNK)
            gate = jnp.dot(xb[...], wgb[:, cs], preferred_element_type=F32)
            up = jnp.dot(xb[...], wub[:, cs], preferred_element_type=F32)
            hid[:, cs] = (gate * _sigmoid(gate) * up).astype(hid.dtype)
        for c in range(0, d, MOE_COL_CHUNK):
            cs = slice(c, c + MOE_COL_CHUNK)
            ys_ref[:, cs] = jnp.dot(hid[...], wdb[:, cs], preferred_element_type=F32)

    @pl.when(jnp.logical_and(b < nact, b + 1 < nb))
    def _():
        compute(True)

    @pl.when(jnp.logical_and(b < nact, b + 1 == nb))
    def _():
        compute(False)

    @pl.when(b >= nact)
    def _():
        ys_ref[...] = jnp.zeros_like(ys_ref)

    @pl.when(b == nact)
    def _():
        _rows_wait(xn_hbm, buf.at[slot], sem.at[slot])


def _combine_kernel(dest_ref, ys_hbm, h_ref, w_ref, gain_ref, o_ref, buf, sem, *, tc, n, final_norm):
    i = pl.program_id(0)

    def issue(blk, slot):
        base = blk * tc

        def body(r, carry):
            for k in range(TOP_K):
                _row_copy(ys_hbm, dest_ref[k * n + base + r], buf.at[slot], k * tc + r,
                          sem.at[slot]).start(priority=k)
            return carry

        lax.fori_loop(0, tc, body, 0, unroll=8)

    @pl.when(i == 0)
    def _():
        issue(0, 0)

    @pl.when(i + 1 < pl.num_programs(0))
    def _():
        issue(i + 1, (i + 1) % 2)

    slot = i % 2
    _rows_wait(ys_hbm, buf.at[slot], sem.at[slot])
    out = h_ref[...] + (w_ref[:, 0:1] * buf[slot, 0:tc, :] + w_ref[:, 1:2] * buf[slot, tc:TOP_K * tc, :])
    if final_norm:
        out = _rms(out, gain_ref[...])
    o_ref[...] = out


def hier_moe(h, ffn_gain, wg_r, bg_r, we_r, be_r, w_gate, w_up, w_down, layer, final_gain=None):
    n, d = h.shape
    f = w_gate.shape[-1]
    a = n * TOP_K
    tb = MOE_BLOCK
    nb = a // tb + N_EXPERTS

    pad = LANES - N_GROUPS - N_EXPERTS
    w_router = jnp.concatenate([wg_r, we_r, jnp.zeros((d, pad), F32)], axis=1)
    b_router = jnp.concatenate([bg_r, be_r, jnp.zeros((pad,), F32)]).reshape(1, LANES)
    xn, ids, wts, cnt = route(h, ffn_gain, w_router, b_router)

    counts = cnt[0, N_GROUPS:N_GROUPS + N_EXPERTS].astype(jnp.int32)
    nblk = (counts + tb - 1) // tb
    bend = jnp.cumsum(nblk)
    nact = bend[-1]
    start_rows = jnp.pad(((bend - nblk) * tb).astype(F32), (0, LANES - N_EXPERTS)).reshape(1, LANES)
    dest = assignment_rows(ids, start_rows).reshape(a)
    blk = jnp.minimum(jnp.arange(nb, dtype=jnp.int32), jnp.maximum(nact - 1, 0))
    owner = lambda bi: jnp.minimum(jnp.searchsorted(bend, bi, side='right'), N_EXPERTS - 1).astype(jnp.int32)
    blk_e = owner(blk)
    after = bend[blk_e]
    nxt_e = jnp.where(after < nact, owner(after), -1)
    tok = jnp.tile(jnp.arange(n, dtype=jnp.int32), TOP_K)
    row_tok = (jnp.arange(nb * tb, dtype=jnp.int32) % n).at[dest].set(tok)
    nact1 = nact.reshape(1).astype(jnp.int32)

    any_space = pl.BlockSpec(memory_space=pl.ANY)
    ys = pl.pallas_call(
        functools.partial(_moe_expert_kernel, tb=tb, nb=nb, layer=layer),
        grid_spec=pltpu.PrefetchScalarGridSpec(
            num_scalar_prefetch=4, grid=(nb,),
            in_specs=[any_space, any_space, any_space, any_space],
            out_specs=pl.BlockSpec((tb, d), lambda b, *_: (b, 0)),
            scratch_shapes=[pltpu.VMEM((2, tb, d), F32),
                            pltpu.VMEM((d, f), F32), pltpu.VMEM((d, f), F32), pltpu.VMEM((f, d), F32),
                            pltpu.VMEM((d, f), BF), pltpu.VMEM((d, f), BF), pltpu.VMEM((f, d), BF),
                            pltpu.VMEM((tb, d), BF), pltpu.VMEM((tb, f), BF),
                            pltpu.SemaphoreType.DMA((2,)), pltpu.SemaphoreType.DMA((3,))]),
        out_shape=jax.ShapeDtypeStruct((nb * tb, d), F32),
        compiler_params=_cparams(("arbitrary",), 58), name="moe_experts",
    )(blk_e, nxt_e, nact1, row_tok, xn, w_gate, w_up, w_down)

    tc = min(COMBINE_ROWS, n)
    gain = (final_gain if final_gain is not None else ffn_gain).reshape(1, d)
    return pl.pallas_call(
        functools.partial(_combine_kernel, tc=tc, n=n, final_norm=final_gain is not None),
        grid_spec=pltpu.PrefetchScalarGridSpec(
            num_scalar_prefetch=1, grid=(n // tc,),
            in_specs=[pl.BlockSpec(memory_space=pl.ANY), pl.BlockSpec((tc, d), lambda i, *_: (i, 0)),
                      pl.BlockSpec((tc, LANES), lambda i, *_: (i, 0)), pl.BlockSpec((1, d), lambda i, *_: (0, 0))],
            out_specs=pl.BlockSpec((tc, d), lambda i, *_: (i, 0)),
            scratch_shapes=[pltpu.VMEM((2, TOP_K * tc, d), F32), pltpu.SemaphoreType.DMA((2,))]),
        out_shape=jax.ShapeDtypeStruct((n, d), F32),
        compiler_params=_cparams(("arbitrary",), 40), name="moe_combine",
    )(dest, ys, h, wts, gain)


def even_mixer(h, gain, w_in, b_gates, pool_w, pool_scale, head_gain, w_out, j):
    d = h.shape[1]
    pool_width = N_POOL_GROUPS * pool_w.shape[-1]
    mlstm_width = head_gain.shape[0]
    main_cols = pool_width + 4 * mlstm_width
    n_gates = 2 * MLSTM_HEADS
    gate_b = jnp.pad(b_gates, (0, LANES - n_gates)).reshape(1, LANES)
    z, gates = norm_matmul_gates_t(h, gain, jnp.swapaxes(w_in, 1, 2), main_cols, n_gates, gate_b, j)
    y_p = pool_mixer(z, pool_w, pool_scale)
    y_m = mlstm_mixer(z, gates, head_gain, pool_width)
    assert pool_width == mlstm_width and pool_width + mlstm_width == d
    return matmul_residual([y_p, y_m], w_out, h, layer=j)


def odd_mixer(h, gain, w_in, conv_w, w_out, j):
    z = norm_matmul(h, gain, w_in, w_in.shape[2], layer=j)
    return matmul_residual([conv_mixer(z, conv_w)], w_out, h, layer=j)


def cross_attn(h, mem, gain, mem_gain, wq, wk, wv, wo, layer):
    d = h.shape[1]
    k = norm_matmul(mem, mem_gain, wk, d, layer=layer)
    v = norm_matmul(mem, mem_gain, wv, d, layer=layer)
    q = norm_matmul_resident(h, gain, wq, layer)
    return matmul_residual([attention(q, k, v)], wo, h, layer=layer)


def kernel(x, mem, mix_norm, xattn_norm, mem_norm, ffn_norm, final_norm, ev_w_in, ev_b_gates, ev_pool_w, ev_pool_scale, ev_head_norm, ev_w_out, od_w_in, od_conv_w, od_w_out, xa_wq, xa_wk, xa_wv, xa_wo, rt_group_w, rt_group_b, rt_expert_w, rt_expert_b, ex_w_gate, ex_w_up, ex_w_down):
    depth = mix_norm.shape[0]
    h = x[0]
    m = mem[0]
    for layer in range(depth):
        j = layer // 2
        if layer % 2 == 0:
            h = even_mixer(h, mix_norm[layer], ev_w_in, ev_b_gates[j], ev_pool_w[j], ev_pool_scale[j],
                           ev_head_norm[j], ev_w_out, j)
        else:
            h = odd_mixer(h, mix_norm[layer], od_w_in, od_conv_w[j], od_w_out, j)
        h = cross_attn(h, m, xattn_norm[layer], mem_norm[layer], xa_wq, xa_wk, xa_wv, xa_wo, layer)
        h = hier_moe(h, ffn_norm[layer], rt_group_w[layer], rt_group_b[layer], rt_expert_w[layer],
                     rt_expert_b[layer], ex_w_gate, ex_w_up, ex_w_down, layer,
                     final_gain=final_norm if layer == depth - 1 else None)
    return h[None]
```

```python
import functools

import jax
import jax.numpy as jnp
from jax import lax
from jax.experimental import pallas as pl
from jax.experimental.pallas import tpu as pltpu

F32 = jnp.float32
BF = jnp.bfloat16
EPS = 1e-6

POOL_WINDOWS = (2, 4, 8, 16)
N_POOL_GROUPS = 4
MLSTM_HEADS = 4
FORGET_LANE0 = MLSTM_HEADS
XATTN_HEADS = 4
N_GROUPS = 4
EXPERTS_PER_GROUP = 8
N_EXPERTS = N_GROUPS * EXPERTS_PER_GROUP
TOP_K = 2
CONV_WIDTH = 3

LANES = 128
SUBLANES = 8
DMA_QUEUE_ROWS = 0
DMA_QUEUE_BULK = 1

ROW_TILE = 1024
COL_TILE = 1024
OUT_ROW_TILE = 512
CAST_ROWS = 64
CAST_UNROLL = 4
MOE_COL_CHUNK = 256
WEIGHT_DMA_PARTS = 4
MLSTM_CHUNK = 256
POOL_ROWS = 256
POOL_HALO = 128
CONV_ROWS = 256
CONV_COLS = 512
CONV_HALO = 16
ROUTER_ROWS = 512
MOE_BLOCK = 256
COMBINE_ROWS = 256


def _cparams(semantics, vmem_mib):
    return pltpu.CompilerParams(dimension_semantics=semantics, vmem_limit_bytes=vmem_mib * 1024 * 1024)


def _sigmoid(x):
    return 1.0 / (1.0 + jnp.exp(-x))


def _log_sigmoid(x):
    return jnp.minimum(x, 0.0) - jnp.log(1.0 + jnp.exp(-jnp.abs(x)))


def _rms(x, g):
    ms = jnp.mean(x * x, axis=-1, keepdims=True)
    return x * lax.rsqrt(ms + EPS) * g


def _norm_mm_kernel(x_ref, g_ref, w_ref, o_ref, xn_ref):
    @pl.when(pl.program_id(1) == 0)
    def _():
        xn_ref[...] = _rms(x_ref[...], g_ref[...]).astype(BF)

    o_ref[...] = jnp.dot(xn_ref[...], w_ref[...].astype(BF), preferred_element_type=F32).astype(o_ref.dtype)


_NT = (((1,), (1,)), ((), ()))


def _norm_mm_gates_t_kernel(x_ref, g_ref, wt_ref, wgt_ref, bg_ref, o_ref, gates_ref, xn_ref):
    @pl.when(pl.program_id(1) == 0)
    def _():
        xn = _rms(x_ref[...], g_ref[...]).astype(BF)
        xn_ref[...] = xn
        wg = wgt_ref[...]
        wg = jnp.concatenate([wg, jnp.zeros((LANES - wg.shape[0], wg.shape[1]), F32)], axis=0).astype(BF)
        gates_ref[...] = lax.dot_general(xn, wg, _NT, preferred_element_type=F32) + bg_ref[...]

    o_ref[...] = lax.dot_general(xn_ref[...], wt_ref[...].astype(BF), _NT,
                                 preferred_element_type=F32).astype(o_ref.dtype)


def norm_matmul_gates_t(x, gain, wt, n_cols, n_gates, gate_b, layer):
    n, k = x.shape
    tm = min(ROW_TILE, n)
    tn = COL_TILE
    return pl.pallas_call(
        _norm_mm_gates_t_kernel, grid=(n // tm, n_cols // tn),
        in_specs=[pl.BlockSpec((tm, k), lambda i, j: (i, 0)), pl.BlockSpec((1, k), lambda i, j: (0, 0)),
                  pl.BlockSpec((None, tn, k), lambda i, j: (layer, j, 0)),
                  pl.BlockSpec((None, n_gates, k), lambda i, j: (layer, n_cols // n_gates, 0)),
                  pl.BlockSpec((1, LANES), lambda i, j: (0, 0))],
        out_specs=[pl.BlockSpec((tm, tn), lambda i, j: (i, j)), pl.BlockSpec((tm, LANES), lambda i, j: (i, 0))],
        out_shape=[jax.ShapeDtypeStruct((n, n_cols), BF), jax.ShapeDtypeStruct((n, LANES), F32)],
        scratch_shapes=[pltpu.VMEM((tm, k), BF)],
        compiler_params=_cparams(("parallel", "arbitrary"), 56), name="norm_matmul_gates",
    )(x, gain.reshape(1, k), wt, wt, gate_b)


def _stacked(w, layer):
    return (w[None], 0) if layer is None else (w, layer)


def norm_matmul(x, gain, w, n_cols, layer=None):
    n, k = x.shape
    w, li = _stacked(w, layer)
    tm = min(ROW_TILE, n)
    tn = COL_TILE
    return pl.pallas_call(
        _norm_mm_kernel, grid=(n // tm, n_cols // tn),
        in_specs=[pl.BlockSpec((tm, k), lambda i, j: (i, 0)), pl.BlockSpec((1, k), lambda i, j: (0, 0)),
                  pl.BlockSpec((None, k, tn), lambda i, j: (li, 0, j))],
        out_specs=pl.BlockSpec((tm, tn), lambda i, j: (i, j)),
        out_shape=jax.ShapeDtypeStruct((n, n_cols), BF), scratch_shapes=[pltpu.VMEM((tm, k), BF)],
        compiler_params=_cparams(("parallel", "arbitrary"), 56), name="norm_matmul",
    )(x, gain.reshape(1, k), w)


def _fill_bf16(dst_ref, src_ref):
    def body(i, carry):
        r = pl.multiple_of(i * CAST_ROWS, CAST_ROWS)
        dst_ref[pl.ds(r, CAST_ROWS), :] = src_ref[pl.ds(r, CAST_ROWS), :].astype(BF)
        return carry

    lax.fori_loop(0, src_ref.shape[0] // CAST_ROWS, body, 0, unroll=CAST_UNROLL)


def _q_attention_kernel(x_ref, g_ref, w_ref, k_ref, v_ref, o_ref, wb_ref, q_ref, *, hd):
    @pl.when(pl.program_id(0) == 0)
    def _():
        _fill_bf16(wb_ref, w_ref)

    xn = _rms(x_ref[...], g_ref[...]).astype(BF)
    q_ref[...] = jnp.dot(xn, wb_ref[...], preferred_element_type=F32).astype(BF)
    scale = hd ** -0.5
    for h in range(XATTN_HEADS):
        hs = slice(h * hd, (h + 1) * hd)
        s = lax.dot_general(q_ref[:, hs], k_ref[:, hs], _NT, preferred_element_type=F32) * scale
        e = jnp.exp(s - jnp.max(s, axis=-1, keepdims=True))
        p = (e / jnp.sum(e, axis=-1, keepdims=True)).astype(BF)
        o_ref[:, hs] = jnp.dot(p, v_ref[:, hs], preferred_element_type=F32).astype(o_ref.dtype)


def q_attention(x, gain, wq, k, v, layer):
    n, kd = x.shape
    d = wq.shape[2]
    m = k.shape[0]
    tm = min(OUT_ROW_TILE, n)
    return pl.pallas_call(
        functools.partial(_q_attention_kernel, hd=d // XATTN_HEADS), grid=(n // tm,),
        in_specs=[pl.BlockSpec((tm, kd), lambda i: (i, 0)), pl.BlockSpec((1, kd), lambda i: (0, 0)),
                  pl.BlockSpec((None, kd, d), lambda i: (layer, 0, 0), pipeline_mode=pl.Buffered(1)),
                  pl.BlockSpec((m, d), lambda i: (0, 0)), pl.BlockSpec((m, d), lambda i: (0, 0))],
        out_specs=pl.BlockSpec((tm, d), lambda i: (i, 0)), out_shape=jax.ShapeDtypeStruct((n, d), BF),
        scratch_shapes=[pltpu.VMEM((kd, d), BF), pltpu.VMEM((tm, d), BF)],
        compiler_params=_cparams(("arbitrary",), 52), name="q_attention",
    )(x, gain.reshape(1, kd), wq, k, v)


def _mm_res_kernel(*refs, nparts):
    xs = refs[:nparts]
    w_ref, res_ref, o_ref, wb_ref = refs[nparts:]

    @pl.when(pl.program_id(0) == 0)
    def _():
        _fill_bf16(wb_ref, w_ref)

    acc = res_ref[...]
    k0 = 0
    for x_ref in xs:
        kp = x_ref.shape[1]
        acc = acc + jnp.dot(x_ref[...], wb_ref[k0:k0 + kp, :], preferred_element_type=F32)
        k0 += kp
    o_ref[...] = acc


def matmul_residual(xs, w, res, layer=None):
    n, d = res.shape
    w, li = _stacked(w, layer)
    k = w.shape[1]
    tm = min(OUT_ROW_TILE, n)
    in_specs = [pl.BlockSpec((tm, x.shape[1]), lambda i: (i, 0)) for x in xs]
    in_specs += [pl.BlockSpec((None, k, d), lambda i: (li, 0, 0), pipeline_mode=pl.Buffered(1)),
                 pl.BlockSpec((tm, d), lambda i: (i, 0))]
    return pl.pallas_call(
        functools.partial(_mm_res_kernel, nparts=len(xs)), grid=(n // tm,), in_specs=in_specs,
        out_specs=pl.BlockSpec((tm, d), lambda i: (i, 0)), out_shape=jax.ShapeDtypeStruct((n, d), F32),
        scratch_shapes=[pltpu.VMEM((k, d), BF)],
        compiler_params=_cparams(("arbitrary",), 52), name="matmul_residual",
    )(*xs, w, res)


def _pool_kernel(cur_ref, prev_ref, w_ref, sc_ref, o_ref, *, tp, gdim):
    i = pl.program_id(0)
    dist = lax.broadcasted_iota(jnp.int32, (tp, tp), 0) - lax.broadcasted_iota(jnp.int32, (tp, tp), 1)
    distp = (lax.broadcasted_iota(jnp.int32, (tp, POOL_HALO), 0) + POOL_HALO
             - lax.broadcasted_iota(jnp.int32, (tp, POOL_HALO), 1))
    pos = i * tp + lax.broadcasted_iota(jnp.int32, (tp, 1), 0)
    for j, win in enumerate(POOL_WINDOWS):
        gs = slice(j * gdim, (j + 1) * gdim)
        cur = cur_ref[:, gs]
        band = jnp.where(dist >= 0, jnp.where(dist < win, 1.0, 0.0), 0.0).astype(BF)
        bandp = jnp.where(distp < jnp.where(i > 0, win, 0), 1.0, 0.0).astype(BF)
        s = (jnp.dot(band, cur, preferred_element_type=F32)
             + jnp.dot(bandp, prev_ref[:, gs], preferred_element_type=F32))
        cnt = jnp.minimum(pos + 1, win).astype(F32)
        d = s / cnt - cur.astype(F32)
        y = jnp.dot(d.astype(BF), w_ref[j].astype(BF), preferred_element_type=F32) * sc_ref[:, gs]
        o_ref[:, gs] = y.astype(o_ref.dtype)


def pool_mixer(z, pool_w, pool_scale):
    n = z.shape[0]
    gdim = pool_w.shape[-1]
    width = N_POOL_GROUPS * gdim
    tp = min(POOL_ROWS, n)
    halo_blocks = tp // POOL_HALO
    return pl.pallas_call(
        functools.partial(_pool_kernel, tp=tp, gdim=gdim), grid=(n // tp,),
        in_specs=[
            pl.BlockSpec((tp, width), lambda i: (i, 0)),
            pl.BlockSpec((POOL_HALO, width), lambda i: (jnp.maximum(i * halo_blocks - 1, 0), 0)),
            pl.BlockSpec((N_POOL_GROUPS, gdim, gdim), lambda i: (0, 0, 0)),
            pl.BlockSpec((1, width), lambda i: (0, 0)),
        ],
        out_specs=pl.BlockSpec((tp, width), lambda i: (i, 0)),
        out_shape=jax.ShapeDtypeStruct((n, width), BF),
        compiler_params=_cparams(("parallel",), 32), name="pool_mixer",
    )(z, z, pool_w, pool_scale.reshape(1, width))


def _mlstm_kernel(q_ref, k_ref, v_ref, o_ref, g_ref, gain_ref, y_ref, ct_ref, m_ref, *, chunk, dh):
    c = pl.program_id(0)

    @pl.when(c == 0)
    def _():
        ct_ref[...] = jnp.zeros_like(ct_ref)
        m_ref[...] = jnp.zeros_like(m_ref)

    g = g_ref[...]
    lf = _log_sigmoid(g)
    row = lax.broadcasted_iota(jnp.int32, (chunk, chunk), 0)
    col = lax.broadcasted_iota(jnp.int32, (chunk, chunk), 1)
    causal = col <= row
    ltri = jnp.where(causal, 1.0, 0.0).astype(BF)
    hi = lf.astype(BF)
    r1 = lf - hi.astype(F32)
    mid = r1.astype(BF)
    lo = (r1 - mid.astype(F32)).astype(BF)
    bcum = (jnp.dot(ltri, hi, preferred_element_type=F32) + jnp.dot(ltri, mid, preferred_element_type=F32)
            + jnp.dot(ltri, lo, preferred_element_type=F32))
    g_t = g.T
    b_t = bcum.T
    ones_col = jnp.where(lax.broadcasted_iota(jnp.int32, (chunk, LANES), 1) == 0, 1.0, 0.0).astype(BF)

    for h in range(MLSTM_HEADS):
        hs = slice(h * dh, (h + 1) * dh)
        fl = FORGET_LANE0 + h
        bc = bcum[:, fl:fl + 1]
        br = b_t[fl:fl + 1, :]
        ir = g_t[h:h + 1, :]
        b_last = bcum[chunk - 1:chunk, fl:fl + 1]
        m_prev = m_ref[h][:, 0:1]

        dmat = jnp.where(causal, bc + (ir - br), -jnp.inf)
        inter = bc + m_prev
        m_t = jnp.maximum(jnp.max(dmat, axis=1, keepdims=True), inter)
        w_inter = jnp.exp(inter - m_t)
        p = jnp.exp(dmat - m_t)

        qh = q_ref[:, hs] * (dh ** -0.5)
        kh = k_ref[:, hs]
        v_aug = jnp.concatenate([v_ref[:, hs], ones_col], axis=1)
        s = lax.dot_general(qh, kh, (((1,), (1,)), ((), ())), preferred_element_type=F32)
        sc = (s * p).astype(BF)
        ct = ct_ref[h]
        num_aug = (w_inter * jnp.dot(qh, ct.astype(BF), preferred_element_type=F32)
                   + jnp.dot(sc, v_aug, preferred_element_type=F32))
        num = num_aug[:, :dh]
        den = num_aug[:, dh:dh + 1]
        hout = num / jnp.maximum(jnp.abs(den), jnp.exp(-m_t))

        yn = _rms(hout, gain_ref[:, hs])
        y_ref[:, hs] = (_sigmoid(o_ref[:, hs].astype(F32)) * yn).astype(y_ref.dtype)

        d_end = b_last - br + ir
        m_new = jnp.maximum(b_last + m_prev, jnp.max(d_end, axis=1, keepdims=True))
        a_prev = jnp.exp(b_last + m_prev - m_new)
        a_s = jnp.exp(d_end - m_new)
        k_t = (kh.astype(F32).T * a_s).astype(BF)
        ct_ref[h] = a_prev * ct + jnp.dot(k_t, v_aug, preferred_element_type=F32)
        m_ref[h] = jnp.broadcast_to(m_new, (1, LANES))


def mlstm_mixer(z, gates, head_gain, col0):
    n = z.shape[0]
    width = head_gain.shape[0]
    dh = width // MLSTM_HEADS
    chunk = min(MLSTM_CHUNK, n)
    base = col0 // width
    qkvo = [pl.BlockSpec((chunk, width), lambda c, p=p: (c, base + p)) for p in range(4)]
    return pl.pallas_call(
        functools.partial(_mlstm_kernel, chunk=chunk, dh=dh), grid=(n // chunk,),
        in_specs=qkvo + [pl.BlockSpec((chunk, LANES), lambda c: (c, 0)), pl.BlockSpec((1, width), lambda c: (0, 0))],
        out_specs=pl.BlockSpec((chunk, width), lambda c: (c, 0)),
        out_shape=jax.ShapeDtypeStruct((n, width), BF),
        scratch_shapes=[pltpu.VMEM((MLSTM_HEADS, dh, dh + LANES), F32), pltpu.VMEM((MLSTM_HEADS, 1, LANES), F32)],
        compiler_params=_cparams(("arbitrary",), 32), name="mlstm_mixer",
    )(z, z, z, z, gates, head_gain.reshape(1, width))


def _gated_conv(b, c, u, cp, up, w, first_block):
    zc = c.astype(F32) * u.astype(F32)
    zp = jnp.where(first_block, 0.0, cp.astype(F32) * up.astype(F32))
    row = lax.broadcasted_iota(jnp.int32, zc.shape, 0)
    acc = w[CONV_WIDTH - 1:CONV_WIDTH, :] * zc
    for back in range(1, CONV_WIDTH):
        shifted = pltpu.roll(zc, back, 0)
        for r in range(back):
            shifted = jnp.where(row == r, zp[CONV_HALO - back + r:CONV_HALO - back + r + 1, :], shifted)
        acc = acc + w[CONV_WIDTH - 1 - back:CONV_WIDTH - back, :] * shifted
    return b.astype(F32) * acc


def _conv_mm_res_kernel(b_ref, c_ref, u_ref, cp_ref, up_ref, cw_ref, w_ref, res_ref, o_ref, wb_ref, xs_ref):
    i = pl.program_id(0)

    @pl.when(i == 0)
    def _():
        _fill_bf16(wb_ref, w_ref)

    for c0 in range(0, xs_ref.shape[1], CONV_COLS):
        cs = slice(c0, c0 + CONV_COLS)
        xs_ref[:, cs] = _gated_conv(b_ref[:, cs], c_ref[:, cs], u_ref[:, cs], cp_ref[:, cs], up_ref[:, cs],
                                    cw_ref[:, cs], i == 0).astype(BF)
    o_ref[...] = res_ref[...] + jnp.dot(xs_ref[...], wb_ref[...], preferred_element_type=F32)


def conv_matmul_residual(z, conv_w, w, res, layer):
    n, d = res.shape
    tm = min(CONV_ROWS, n)
    halo_blocks = tm // CONV_HALO
    cur = lambda part: pl.BlockSpec((tm, d), lambda i: (i, part))
    prev = lambda part: pl.BlockSpec((CONV_HALO, d), lambda i: (jnp.maximum(i * halo_blocks - 1, 0), part))
    return pl.pallas_call(
        _conv_mm_res_kernel, grid=(n // tm,),
        in_specs=[cur(0), cur(1), cur(2), prev(1), prev(2), pl.BlockSpec((CONV_WIDTH, d), lambda i: (0, 0)),
                  pl.BlockSpec((None, d, d), lambda i: (layer, 0, 0), pipeline_mode=pl.Buffered(1)),
                  pl.BlockSpec((tm, d), lambda i: (i, 0))],
        out_specs=pl.BlockSpec((tm, d), lambda i: (i, 0)), out_shape=jax.ShapeDtypeStruct((n, d), F32),
        scratch_shapes=[pltpu.VMEM((d, d), BF), pltpu.VMEM((tm, d), BF)],
        compiler_params=_cparams(("arbitrary",), 52), name="conv_matmul_residual",
    )(z, z, z, z, z, conv_w, w, res)


def _router_kernel(x_ref, g_ref, wr_ref, br_ref, xn_ref, ids_ref, wts_ref, cnt_ref, carry_ref, whi_ref, wlo_ref,
                   *, tr):
    @pl.when(pl.program_id(0) == 0)
    def _():
        carry_ref[...] = jnp.zeros_like(carry_ref)

    @pl.when(pl.program_id(0) == 0)
    def _():
        w = wr_ref[...]
        w_hi = w.astype(BF)
        whi_ref[...] = w_hi
        wlo_ref[...] = (w - w_hi.astype(F32)).astype(BF)

    xn = _rms(x_ref[...], g_ref[...])
    xn_ref[...] = xn
    x_hi = xn.astype(BF)
    x_lo = (xn - x_hi.astype(F32)).astype(BF)
    logits = (jnp.dot(x_hi, whi_ref[...], preferred_element_type=F32)
              + jnp.dot(x_lo, whi_ref[...], preferred_element_type=F32)
              + jnp.dot(x_hi, wlo_ref[...], preferred_element_type=F32)) + br_ref[...]
    lane = lax.broadcasted_iota(jnp.int32, (tr, LANES), 1).astype(F32)
    neg = -jnp.inf

    def first_argmax(vals):
        top = jnp.max(vals, axis=-1, keepdims=True)
        return top, jnp.min(jnp.where(vals == top, lane, float(LANES)), axis=-1, keepdims=True)

    gl = jnp.where(lane < N_GROUPS, logits, neg)
    gmax, grp = first_argmax(gl)
    g_prob = 1.0 / jnp.sum(jnp.exp(gl - gmax), axis=-1, keepdims=True)
    lo = N_GROUPS + EXPERTS_PER_GROUP * grp
    el = jnp.where(lane >= lo, jnp.where(lane < lo + EXPERTS_PER_GROUP, logits, neg), neg)
    v1, l1 = first_argmax(el)
    v2, l2 = first_argmax(jnp.where(lane == l1, neg, el))
    e2 = jnp.exp(v2 - v1)
    w1 = g_prob / (1.0 + e2)
    w2 = g_prob * e2 / (1.0 + e2)
    hot1 = lane == l1
    hot2 = lane == l2
    hot = jnp.where(hot1, 1.0, jnp.where(hot2, 1.0, 0.0))
    earlier = (lax.broadcasted_iota(jnp.int32, (tr, tr), 1) < lax.broadcasted_iota(jnp.int32, (tr, tr), 0))
    before = jnp.dot(jnp.where(earlier, 1.0, 0.0).astype(BF), hot.astype(BF), preferred_element_type=F32)
    before = before + carry_ref[0:1, :]
    r1 = jnp.sum(jnp.where(hot1, before, 0.0), axis=-1, keepdims=True)
    r2 = jnp.sum(jnp.where(hot2, before, 0.0), axis=-1, keepdims=True)
    carry_ref[0:1, :] = carry_ref[0:1, :] + jnp.sum(hot, axis=0, keepdims=True)
    ids = jnp.where(lane == 0, l1 - N_GROUPS, jnp.where(lane == 1, l2 - N_GROUPS,
                    jnp.where(lane == 2, r1, jnp.where(lane == 3, r2, 0.0))))
    ids_ref[...] = ids.astype(jnp.int32)
    wts_ref[...] = jnp.where(lane == 0, w1, jnp.where(lane == 1, w2, 0.0))
    cnt_ref[...] = carry_ref[...]


def route(h, gain, w_router, b_router):
    n, d = h.shape
    tr = min(ROUTER_ROWS, n)
    return pl.pallas_call(
        functools.partial(_router_kernel, tr=tr), grid=(n // tr,),
        in_specs=[pl.BlockSpec((tr, d), lambda i: (i, 0)), pl.BlockSpec((1, d), lambda i: (0, 0)),
                  pl.BlockSpec((d, LANES), lambda i: (0, 0)), pl.BlockSpec((1, LANES), lambda i: (0, 0))],
        out_specs=[pl.BlockSpec((tr, d), lambda i: (i, 0)),
                   pl.BlockSpec((tr, LANES), lambda i: (i, 0)), pl.BlockSpec((tr, LANES), lambda i: (i, 0)),
                   pl.BlockSpec((SUBLANES, LANES), lambda i: (0, 0))],
        out_shape=[jax.ShapeDtypeStruct((n, d), F32), jax.ShapeDtypeStruct((n, LANES), jnp.int32),
                   jax.ShapeDtypeStruct((n, LANES), F32), jax.ShapeDtypeStruct((SUBLANES, LANES), F32)],
        scratch_shapes=[pltpu.VMEM((SUBLANES, LANES), F32), pltpu.VMEM((d, LANES), BF), pltpu.VMEM((d, LANES), BF)],
        compiler_params=_cparams(("arbitrary",), 40), name="moe_router",
    )(h, gain.reshape(1, d), w_router, b_router)


def _dest_kernel(ids_ref, start_ref, o_ref, *, tr):
    ids = ids_ref[...].astype(F32)
    lane = lax.broadcasted_iota(jnp.int32, (tr, LANES), 1).astype(F32)
    start = start_ref[...]
    rows = []
    for k in range(TOP_K):
        first = jnp.sum(jnp.where(lane == ids[:, k:k + 1], start, 0.0), axis=-1, keepdims=True)
        rows.append(first + ids[:, TOP_K + k:TOP_K + k + 1])
    packed = jnp.where(lane == 0, rows[0], jnp.where(lane == 1, rows[1], 0.0))
    o_ref[...] = packed.T[0:SUBLANES, :].astype(jnp.int32)


def assignment_rows(ids, start_rows):
    n = ids.shape[0]
    tr = min(ROUTER_ROWS, n)
    out = pl.pallas_call(
        functools.partial(_dest_kernel, tr=tr), grid=(n // tr,),
        in_specs=[pl.BlockSpec((tr, LANES), lambda i: (i, 0)), pl.BlockSpec((1, LANES), lambda i: (0, 0))],
        out_specs=pl.BlockSpec((SUBLANES, tr), lambda i: (0, i)),
        out_shape=jax.ShapeDtypeStruct((SUBLANES, n), jnp.int32),
        compiler_params=_cparams(("parallel",), 32), name="moe_assignment_rows",
    )(ids, start_rows)
    return out[:TOP_K]


def _row_copy(src_hbm, row, dst, dst_row, sem):
    return pltpu.make_async_copy(src_hbm.at[pl.ds(row, 1), :], dst.at[pl.ds(dst_row, 1), :], sem)


def _rows_wait(src_hbm, dst, sem):
    pltpu.make_async_copy(src_hbm.at[pl.ds(0, dst.shape[0]), :], dst, sem).wait()


def _is_new_expert(blk_ref, b):
    return jnp.logical_or(b == 0, blk_ref[b] != blk_ref[jnp.maximum(b - 1, 0)])


def _weight_copies(w_hbm, layer, expert, stage, sem):
    rows = stage.shape[0] // WEIGHT_DMA_PARTS
    return [pltpu.make_async_copy(w_hbm.at[layer, expert, pl.ds(p * rows, rows), :],
                                  stage.at[pl.ds(p * rows, rows), :], sem) for p in range(WEIGHT_DMA_PARTS)]


def _moe_expert_kernel(blk_ref, nxt_ref, nact_ref, rtok_ref, xn_hbm, wg_hbm, wu_hbm, wd_hbm, ys_ref,
                       buf, stage_g, stage_u, stage_d, wgb, wub, wdb, xb, hid, sem, wsem, *, tb, nb, layer):
    b = pl.program_id(0)
    nact = nact_ref[0]
    slot = b % 2
    f = hid.shape[1]
    d = ys_ref.shape[1]
    up_chunks = f // MOE_COL_CHUNK
    rows_per_chunk = tb // up_chunks

    def expert_copies(e):
        return (_weight_copies(wg_hbm, layer, e, stage_g, wsem.at[0])
                + _weight_copies(wu_hbm, layer, e, stage_u, wsem.at[1])
                + _weight_copies(wd_hbm, layer, e, stage_d, wsem.at[2]))

    @pl.when(b == 0)
    def _():
        for cp in expert_copies(blk_ref[0]):
            cp.start(priority=DMA_QUEUE_BULK)

        def body(r, carry):
            _row_copy(xn_hbm, rtok_ref[r], buf.at[0], r, sem.at[0]).start(priority=DMA_QUEUE_ROWS)
            return carry

        lax.fori_loop(0, tb, body, 0, unroll=8)

    @pl.when(jnp.logical_and(b < nact, _is_new_expert(blk_ref, b)))
    def _():
        for cp in expert_copies(blk_ref[b]):
            cp.wait()
        _fill_bf16(wgb, stage_g)
        _fill_bf16(wub, stage_u)
        _fill_bf16(wdb, stage_d)

        @pl.when(nxt_ref[b] >= 0)
        def _():
            for cp in expert_copies(nxt_ref[b]):
                cp.start(priority=DMA_QUEUE_BULK)

    def compute(prefetch_next):
        _rows_wait(xn_hbm, buf.at[slot], sem.at[slot])
        xb[...] = buf[slot].astype(BF)
        for c in range(up_chunks):
            if prefetch_next:
                for r in range(c * rows_per_chunk, (c + 1) * rows_per_chunk):
                    _row_copy(xn_hbm, rtok_ref[(b + 1) * tb + r], buf.at[1 - slot], r,
                              sem.at[1 - slot]).start(priority=DMA_QUEUE_ROWS)
            cs = slice(c * MOE_COL_CHUNK, (c + 1) * MOE_COL_CHUNK)
            gate = jnp.dot(xb[...], wgb[:, cs], preferred_element_type=F32)
            up = jnp.dot(xb[...], wub[:, cs], preferred_element_type=F32)
            hid[:, cs] = (gate * _sigmoid(gate) * up).astype(hid.dtype)
        for c in range(0, d, MOE_COL_CHUNK):
            cs = slice(c, c + MOE_COL_CHUNK)
            ys_ref[:, cs] = jnp.dot(hid[...], wdb[:, cs], preferred_element_type=F32)

    @pl.when(jnp.logical_and(b < nact, b + 1 < nb))
    def _():
        compute(True)

    @pl.when(jnp.logical_and(b < nact, b + 1 == nb))
    def _():
        compute(False)

    @pl.when(b >= nact)
    def _():
        ys_ref[...] = jnp.zeros_like(ys_ref)

    @pl.when(b == nact)
    def _():
        _rows_wait(xn_hbm, buf.at[slot], sem.at[slot])


def _combine_kernel(dest_ref, ys_hbm, h_ref, w_ref, gain_ref, o_ref, buf, sem, *, tc, n, final_norm):
    i = pl.program_id(0)

    def issue(blk, slot):
        base = blk * tc

        def body(r, carry):
            for k in range(TOP_K):
                _row_copy(ys_hbm, dest_ref[k * n + base + r], buf.at[slot], k * tc + r,
                          sem.at[slot]).start(priority=k)
            return carry

        lax.fori_loop(0, tc, body, 0, unroll=8)

    @pl.when(i == 0)
    def _():
        issue(0, 0)

    @pl.when(i + 1 < pl.num_programs(0))
    def _():
        issue(i + 1, (i + 1) % 2)

    slot = i % 2
    _rows_wait(ys_hbm, buf.at[slot], sem.at[slot])
    out = h_ref[...] + (w_ref[:, 0:1] * buf[slot, 0:tc, :] + w_ref[:, 1:2] * buf[slot, tc:TOP_K * tc, :])
    if final_norm:
        out = _rms(out, gain_ref[...])
    o_ref[...] = out


def hier_moe(h, ffn_gain, wg_r, bg_r, we_r, be_r, w_gate, w_up, w_down, layer, final_gain=None):
    n, d = h.shape
    f = w_gate.shape[-1]
    a = n * TOP_K
    tb = MOE_BLOCK
    nb = a // tb + N_EXPERTS

    pad = LANES - N_GROUPS - N_EXPERTS
    w_router = jnp.concatenate([wg_r, we_r, jnp.zeros((d, pad), F32)], axis=1)
    b_router = jnp.concatenate([bg_r, be_r, jnp.zeros((pad,), F32)]).reshape(1, LANES)
    xn, ids, wts, cnt = route(h, ffn_gain, w_router, b_router)

    counts = cnt[0, N_GROUPS:N_GROUPS + N_EXPERTS].astype(jnp.int32)
    nblk = (counts + tb - 1) // tb
    bend = jnp.cumsum(nblk)
    nact = bend[-1]
    start_rows = jnp.pad(((bend - nblk) * tb).astype(F32), (0, LANES - N_EXPERTS)).reshape(1, LANES)
    dest = assignment_rows(ids, start_rows).reshape(a)
    blk = jnp.minimum(jnp.arange(nb, dtype=jnp.int32), jnp.maximum(nact - 1, 0))
    owner = lambda bi: jnp.minimum(jnp.searchsorted(bend, bi, side='right'), N_EXPERTS - 1).astype(jnp.int32)
    blk_e = owner(blk)
    after = bend[blk_e]
    nxt_e = jnp.where(after < nact, owner(after), -1)
    tok = jnp.tile(jnp.arange(n, dtype=jnp.int32), TOP_K)
    row_tok = (jnp.arange(nb * tb, dtype=jnp.int32) % n).at[dest].set(tok)
    nact1 = nact.reshape(1).astype(jnp.int32)

    any_space = pl.BlockSpec(memory_space=pl.ANY)
    ys = pl.pallas_call(
        functools.partial(_moe_expert_kernel, tb=tb, nb=nb, layer=layer),
        grid_spec=pltpu.PrefetchScalarGridSpec(
            num_scalar_prefetch=4, grid=(nb,),
            in_specs=[any_space, any_space, any_space, any_space],
            out_specs=pl.BlockSpec((tb, d), lambda b, *_: (b, 0)),
            scratch_shapes=[pltpu.VMEM((2, tb, d), F32),
                            pltpu.VMEM((d, f), F32), pltpu.VMEM((d, f), F32), pltpu.VMEM((f, d), F32),
                            pltpu.VMEM((d, f), BF), pltpu.VMEM((d, f), BF), pltpu.VMEM((f, d), BF),
                            pltpu.VMEM((tb, d), BF), pltpu.VMEM((tb, f), BF),
                            pltpu.SemaphoreType.DMA((2,)), pltpu.SemaphoreType.DMA((3,))]),
        out_shape=jax.ShapeDtypeStruct((nb * tb, d), F32),
        compiler_params=_cparams(("arbitrary",), 58), name="moe_experts",
    )(blk_e, nxt_e, nact1, row_tok, xn, w_gate, w_up, w_down)

    tc = min(COMBINE_ROWS, n)
    gain = (final_gain if final_gain is not None else ffn_gain).reshape(1, d)
    return pl.pallas_call(
        functools.partial(_combine_kernel, tc=tc, n=n, final_norm=final_gain is not None),
        grid_spec=pltpu.PrefetchScalarGridSpec(
            num_scalar_prefetch=1, grid=(n // tc,),
            in_specs=[pl.BlockSpec(memory_space=pl.ANY), pl.BlockSpec((tc, d), lambda i, *_: (i, 0)),
                      pl.BlockSpec((tc, LANES), lambda i, *_: (i, 0)), pl.BlockSpec((1, d), lambda i, *_: (0, 0))],
            out_specs=pl.BlockSpec((tc, d), lambda i, *_: (i, 0)),
            scratch_shapes=[pltpu.VMEM((2, TOP_K * tc, d), F32), pltpu.SemaphoreType.DMA((2,))]),
        out_shape=jax.ShapeDtypeStruct((n, d), F32),
        compiler_params=_cparams(("arbitrary",), 40), name="moe_combine",
    )(dest, ys, h, wts, gain)


def even_mixer(h, gain, w_in, b_gates, pool_w, pool_scale, head_gain, w_out, j):
    d = h.shape[1]
    pool_width = N_POOL_GROUPS * pool_w.shape[-1]
    mlstm_width = head_gain.shape[0]
    main_cols = pool_width + 4 * mlstm_width
    n_gates = 2 * MLSTM_HEADS
    gate_b = jnp.pad(b_gates, (0, LANES - n_gates)).reshape(1, LANES)
    z, gates = norm_matmul_gates_t(h, gain, jnp.swapaxes(w_in, 1, 2), main_cols, n_gates, gate_b, j)
    y_p = pool_mixer(z, pool_w, pool_scale)
    y_m = mlstm_mixer(z, gates, head_gain, pool_width)
    assert pool_width == mlstm_width and pool_width + mlstm_width == d
    return matmul_residual([y_p, y_m], w_out, h, layer=j)


def odd_mixer(h, gain, w_in, conv_w, w_out, j):
    z = norm_matmul(h, gain, w_in, w_in.shape[2], layer=j)
    return conv_matmul_residual(z, conv_w, w_out, h, j)


def cross_attn(h, mem, gain, mem_gain, wq, wk, wv, wo, layer):
    d = h.shape[1]
    k = norm_matmul(mem, mem_gain, wk, d, layer=layer)
    v = norm_matmul(mem, mem_gain, wv, d, layer=layer)
    return matmul_residual([q_attention(h, gain, wq, k, v, layer)], wo, h, layer=layer)


def kernel(x, mem, mix_norm, xattn_norm, mem_norm, ffn_norm, final_norm, ev_w_in, ev_b_gates, ev_pool_w, ev_pool_scale, ev_head_norm, ev_w_out, od_w_in, od_conv_w, od_w_out, xa_wq, xa_wk, xa_wv, xa_wo, rt_group_w, rt_group_b, rt_expert_w, rt_expert_b, ex_w_gate, ex_w_up, ex_w_down):
    depth = mix_norm.shape[0]
    h = x[0]
    m = mem[0]
    for layer in range(depth):
        j = layer // 2
        if layer % 2 == 0:
            h = even_mixer(h, mix_norm[layer], ev_w_in, ev_b_gates[j], ev_pool_w[j], ev_pool_scale[j],
                           ev_head_norm[j], ev_w_out, j)
        else:
            h = odd_mixer(h, mix_norm[layer], od_w_in, od_conv_w[j], od_w_out, j)
        h = cross_attn(h, m, xattn_norm[layer], mem_norm[layer], xa_wq, xa_wk, xa_wv, xa_wo, layer)
        h = hier_moe(h, ffn_norm[layer], rt_group_w[layer], rt_group_b[layer], rt_expert_w[layer],
                     rt_expert_b[layer], ex_w_gate, ex_w_up, ex_w_down, layer,
                     final_gain=final_norm if layer == depth - 1 else None)
    return h[None]
```

```python
import functools

import jax
import jax.numpy as jnp
from jax import lax
from jax.experimental import pallas as pl
from jax.experimental.pallas import tpu as pltpu

F32 = jnp.float32
BF = jnp.bfloat16
EPS = 1e-6

POOL_WINDOWS = (2, 4, 8, 16)
N_POOL_GROUPS = 4
MLSTM_HEADS = 4
FORGET_LANE0 = MLSTM_HEADS
XATTN_HEADS = 4
N_GROUPS = 4
EXPERTS_PER_GROUP = 8
N_EXPERTS = N_GROUPS * EXPERTS_PER_GROUP
TOP_K = 2
CONV_WIDTH = 3

LANES = 128
SUBLANES = 8
DMA_QUEUE_ROWS = 0
DMA_QUEUE_BULK = 1

ROW_TILE = 1024
COL_TILE = 1024
OUT_ROW_TILE = 512
CAST_ROWS = 64
CAST_UNROLL = 4
MOE_COL_CHUNK = 256
WEIGHT_DMA_PARTS = 4
MLSTM_CHUNK = 256
POOL_ROWS = 256
POOL_HALO = 128
CONV_ROWS = 256
CONV_COLS = 512
CONV_HALO = 16
ROUTER_ROWS = 512
GATHER_AHEAD = 2
COMBINE_ROWS = 256


def _moe_block_rows(assignments):
    mean_rows = assignments // N_EXPERTS
    return -(-(mean_rows * 9 // 16) // 16) * 16


def _cparams(semantics, vmem_mib):
    return pltpu.CompilerParams(dimension_semantics=semantics, vmem_limit_bytes=vmem_mib * 1024 * 1024)


def _sigmoid(x):
    return 1.0 / (1.0 + jnp.exp(-x))


def _log_sigmoid(x):
    return jnp.minimum(x, 0.0) - jnp.log(1.0 + jnp.exp(-jnp.abs(x)))


def _rms(x, g):
    ms = jnp.mean(x * x, axis=-1, keepdims=True)
    return x * lax.rsqrt(ms + EPS) * g


def _norm_mm_kernel(x_ref, g_ref, w_ref, o_ref, xn_ref):
    @pl.when(pl.program_id(1) == 0)
    def _():
        xn_ref[...] = _rms(x_ref[...], g_ref[...]).astype(BF)

    o_ref[...] = jnp.dot(xn_ref[...], w_ref[...].astype(BF), preferred_element_type=F32).astype(o_ref.dtype)


_NT = (((1,), (1,)), ((), ()))


def _norm_mm_gates_t_kernel(x_ref, g_ref, wt_ref, wgt_ref, bg_ref, o_ref, gates_ref, xn_ref):
    @pl.when(pl.program_id(1) == 0)
    def _():
        xn = _rms(x_ref[...], g_ref[...]).astype(BF)
        xn_ref[...] = xn
        wg = wgt_ref[...]
        wg = jnp.concatenate([wg, jnp.zeros((LANES - wg.shape[0], wg.shape[1]), F32)], axis=0).astype(BF)
        gates_ref[...] = lax.dot_general(xn, wg, _NT, preferred_element_type=F32) + bg_ref[...]

    o_ref[...] = lax.dot_general(xn_ref[...], wt_ref[...].astype(BF), _NT,
                                 preferred_element_type=F32).astype(o_ref.dtype)


def norm_matmul_gates_t(x, gain, wt, n_cols, n_gates, gate_b, layer):
    n, k = x.shape
    tm = min(ROW_TILE, n)
    tn = COL_TILE
    return pl.pallas_call(
        _norm_mm_gates_t_kernel, grid=(n // tm, n_cols // tn),
        in_specs=[pl.BlockSpec((tm, k), lambda i, j: (i, 0)), pl.BlockSpec((1, k), lambda i, j: (0, 0)),
                  pl.BlockSpec((None, tn, k), lambda i, j: (layer, j, 0)),
                  pl.BlockSpec((None, n_gates, k), lambda i, j: (layer, n_cols // n_gates, 0)),
                  pl.BlockSpec((1, LANES), lambda i, j: (0, 0))],
        out_specs=[pl.BlockSpec((tm, tn), lambda i, j: (i, j)), pl.BlockSpec((tm, LANES), lambda i, j: (i, 0))],
        out_shape=[jax.ShapeDtypeStruct((n, n_cols), BF), jax.ShapeDtypeStruct((n, LANES), F32)],
        scratch_shapes=[pltpu.VMEM((tm, k), BF)],
        compiler_params=_cparams(("parallel", "arbitrary"), 56), name="norm_matmul_gates",
    )(x, gain.reshape(1, k), wt, wt, gate_b)


def _stacked(w, layer):
    return (w[None], 0) if layer is None else (w, layer)


def norm_matmul(x, gain, w, n_cols, layer=None):
    n, k = x.shape
    w, li = _stacked(w, layer)
    tm = min(ROW_TILE, n)
    tn = COL_TILE
    return pl.pallas_call(
        _norm_mm_kernel, grid=(n // tm, n_cols // tn),
        in_specs=[pl.BlockSpec((tm, k), lambda i, j: (i, 0)), pl.BlockSpec((1, k), lambda i, j: (0, 0)),
                  pl.BlockSpec((None, k, tn), lambda i, j: (li, 0, j))],
        out_specs=pl.BlockSpec((tm, tn), lambda i, j: (i, j)),
        out_shape=jax.ShapeDtypeStruct((n, n_cols), BF), scratch_shapes=[pltpu.VMEM((tm, k), BF)],
        compiler_params=_cparams(("parallel", "arbitrary"), 56), name="norm_matmul",
    )(x, gain.reshape(1, k), w)


def _fill_bf16(dst_ref, src_ref):
    def body(i, carry):
        r = pl.multiple_of(i * CAST_ROWS, CAST_ROWS)
        dst_ref[pl.ds(r, CAST_ROWS), :] = src_ref[pl.ds(r, CAST_ROWS), :].astype(BF)
        return carry

    lax.fori_loop(0, src_ref.shape[0] // CAST_ROWS, body, 0, unroll=CAST_UNROLL)


def _q_attention_kernel(x_ref, g_ref, w_ref, k_ref, v_ref, o_ref, wb_ref, q_ref, *, hd):
    @pl.when(pl.program_id(0) == 0)
    def _():
        _fill_bf16(wb_ref, w_ref)

    xn = _rms(x_ref[...], g_ref[...]).astype(BF)
    q_ref[...] = jnp.dot(xn, wb_ref[...], preferred_element_type=F32).astype(BF)
    scale = hd ** -0.5
    for h in range(XATTN_HEADS):
        hs = slice(h * hd, (h + 1) * hd)
        s = lax.dot_general(q_ref[:, hs], k_ref[:, hs], _NT, preferred_element_type=F32) * scale
        e = jnp.exp(s - jnp.max(s, axis=-1, keepdims=True))
        p = (e / jnp.sum(e, axis=-1, keepdims=True)).astype(BF)
        o_ref[:, hs] = jnp.dot(p, v_ref[:, hs], preferred_element_type=F32).astype(o_ref.dtype)


def q_attention(x, gain, wq, k, v, layer):
    n, kd = x.shape
    d = wq.shape[2]
    m = k.shape[0]
    tm = min(OUT_ROW_TILE, n)
    return pl.pallas_call(
        functools.partial(_q_attention_kernel, hd=d // XATTN_HEADS), grid=(n // tm,),
        in_specs=[pl.BlockSpec((tm, kd), lambda i: (i, 0)), pl.BlockSpec((1, kd), lambda i: (0, 0)),
                  pl.BlockSpec((None, kd, d), lambda i: (layer, 0, 0), pipeline_mode=pl.Buffered(1)),
                  pl.BlockSpec((m, d), lambda i: (0, 0)), pl.BlockSpec((m, d), lambda i: (0, 0))],
        out_specs=pl.BlockSpec((tm, d), lambda i: (i, 0)), out_shape=jax.ShapeDtypeStruct((n, d), BF),
        scratch_shapes=[pltpu.VMEM((kd, d), BF), pltpu.VMEM((tm, d), BF)],
        compiler_params=_cparams(("arbitrary",), 52), name="q_attention",
    )(x, gain.reshape(1, kd), wq, k, v)


def _mm_res_kernel(*refs, nparts):
    xs = refs[:nparts]
    w_ref, res_ref, o_ref, wb_ref = refs[nparts:]

    @pl.when(pl.program_id(0) == 0)
    def _():
        _fill_bf16(wb_ref, w_ref)

    acc = res_ref[...]
    k0 = 0
    for x_ref in xs:
        kp = x_ref.shape[1]
        acc = acc + jnp.dot(x_ref[...], wb_ref[k0:k0 + kp, :], preferred_element_type=F32)
        k0 += kp
    o_ref[...] = acc


def matmul_residual(xs, w, res, layer=None):
    n, d = res.shape
    w, li = _stacked(w, layer)
    k = w.shape[1]
    tm = min(OUT_ROW_TILE, n)
    in_specs = [pl.BlockSpec((tm, x.shape[1]), lambda i: (i, 0)) for x in xs]
    in_specs += [pl.BlockSpec((None, k, d), lambda i: (li, 0, 0), pipeline_mode=pl.Buffered(1)),
                 pl.BlockSpec((tm, d), lambda i: (i, 0))]
    return pl.pallas_call(
        functools.partial(_mm_res_kernel, nparts=len(xs)), grid=(n // tm,), in_specs=in_specs,
        out_specs=pl.BlockSpec((tm, d), lambda i: (i, 0)), out_shape=jax.ShapeDtypeStruct((n, d), F32),
        scratch_shapes=[pltpu.VMEM((k, d), BF)],
        compiler_params=_cparams(("arbitrary",), 52), name="matmul_residual",
    )(*xs, w, res)


def _pool_kernel(cur_ref, prev_ref, w_ref, sc_ref, o_ref, *, tp, gdim):
    i = pl.program_id(0)
    dist = lax.broadcasted_iota(jnp.int32, (tp, tp), 0) - lax.broadcasted_iota(jnp.int32, (tp, tp), 1)
    distp = (lax.broadcasted_iota(jnp.int32, (tp, POOL_HALO), 0) + POOL_HALO
             - lax.broadcasted_iota(jnp.int32, (tp, POOL_HALO), 1))
    pos = i * tp + lax.broadcasted_iota(jnp.int32, (tp, 1), 0)
    for j, win in enumerate(POOL_WINDOWS):
        gs = slice(j * gdim, (j + 1) * gdim)
        cur = cur_ref[:, gs]
        band = jnp.where(dist >= 0, jnp.where(dist < win, 1.0, 0.0), 0.0).astype(BF)
        bandp = jnp.where(distp < jnp.where(i > 0, win, 0), 1.0, 0.0).astype(BF)
        s = (jnp.dot(band, cur, preferred_element_type=F32)
             + jnp.dot(bandp, prev_ref[:, gs], preferred_element_type=F32))
        cnt = jnp.minimum(pos + 1, win).astype(F32)
        d = s / cnt - cur.astype(F32)
        y = jnp.dot(d.astype(BF), w_ref[j].astype(BF), preferred_element_type=F32) * sc_ref[:, gs]
        o_ref[:, gs] = y.astype(o_ref.dtype)


def pool_mixer(z, pool_w, pool_scale):
    n = z.shape[0]
    gdim = pool_w.shape[-1]
    width = N_POOL_GROUPS * gdim
    tp = min(POOL_ROWS, n)
    halo_blocks = tp // POOL_HALO
    return pl.pallas_call(
        functools.partial(_pool_kernel, tp=tp, gdim=gdim), grid=(n // tp,),
        in_specs=[
            pl.BlockSpec((tp, width), lambda i: (i, 0)),
            pl.BlockSpec((POOL_HALO, width), lambda i: (jnp.maximum(i * halo_blocks - 1, 0), 0)),
            pl.BlockSpec((N_POOL_GROUPS, gdim, gdim), lambda i: (0, 0, 0)),
            pl.BlockSpec((1, width), lambda i: (0, 0)),
        ],
        out_specs=pl.BlockSpec((tp, width), lambda i: (i, 0)),
        out_shape=jax.ShapeDtypeStruct((n, width), BF),
        compiler_params=_cparams(("parallel",), 32), name="pool_mixer",
    )(z, z, pool_w, pool_scale.reshape(1, width))


def _mlstm_kernel(q_ref, k_ref, v_ref, o_ref, g_ref, gain_ref, y_ref, ct_ref, m_ref, *, chunk, dh):
    c = pl.program_id(0)

    @pl.when(c == 0)
    def _():
        ct_ref[...] = jnp.zeros_like(ct_ref)
        m_ref[...] = jnp.zeros_like(m_ref)

    g = g_ref[...]
    lf = _log_sigmoid(g)
    row = lax.broadcasted_iota(jnp.int32, (chunk, chunk), 0)
    col = lax.broadcasted_iota(jnp.int32, (chunk, chunk), 1)
    causal = col <= row
    ltri = jnp.where(causal, 1.0, 0.0).astype(BF)
    hi = lf.astype(BF)
    r1 = lf - hi.astype(F32)
    mid = r1.astype(BF)
    lo = (r1 - mid.astype(F32)).astype(BF)
    bcum = (jnp.dot(ltri, hi, preferred_element_type=F32) + jnp.dot(ltri, mid, preferred_element_type=F32)
            + jnp.dot(ltri, lo, preferred_element_type=F32))
    g_t = g.T
    b_t = bcum.T
    ones_col = jnp.where(lax.broadcasted_iota(jnp.int32, (chunk, LANES), 1) == 0, 1.0, 0.0).astype(BF)

    for h in range(MLSTM_HEADS):
        hs = slice(h * dh, (h + 1) * dh)
        fl = FORGET_LANE0 + h
        bc = bcum[:, fl:fl + 1]
        br = b_t[fl:fl + 1, :]
        ir = g_t[h:h + 1, :]
        b_last = bcum[chunk - 1:chunk, fl:fl + 1]
        m_prev = m_ref[h][:, 0:1]

        dmat = jnp.where(causal, bc + (ir - br), -jnp.inf)
        inter = bc + m_prev
        m_t = jnp.maximum(jnp.max(dmat, axis=1, keepdims=True), inter)
        w_inter = jnp.exp(inter - m_t)
        p = jnp.exp(dmat - m_t)

        qh = q_ref[:, hs] * (dh ** -0.5)
        kh = k_ref[:, hs]
        v_aug = jnp.concatenate([v_ref[:, hs], ones_col], axis=1)
        s = lax.dot_general(qh, kh, (((1,), (1,)), ((), ())), preferred_element_type=F32)
        sc = (s * p).astype(BF)
        ct = ct_ref[h]
        num_aug = (w_inter * jnp.dot(qh, ct.astype(BF), preferred_element_type=F32)
                   + jnp.dot(sc, v_aug, preferred_element_type=F32))
        num = num_aug[:, :dh]
        den = num_aug[:, dh:dh + 1]
        hout = num / jnp.maximum(jnp.abs(den), jnp.exp(-m_t))

        yn = _rms(hout, gain_ref[:, hs])
        y_ref[:, hs] = (_sigmoid(o_ref[:, hs].astype(F32)) * yn).astype(y_ref.dtype)

        d_end = b_last - br + ir
        m_new = jnp.maximum(b_last + m_prev, jnp.max(d_end, axis=1, keepdims=True))
        a_prev = jnp.exp(b_last + m_prev - m_new)
        a_s = jnp.exp(d_end - m_new)
        k_t = (kh.astype(F32).T * a_s).astype(BF)
        ct_ref[h] = a_prev * ct + jnp.dot(k_t, v_aug, preferred_element_type=F32)
        m_ref[h] = jnp.broadcast_to(m_new, (1, LANES))


def mlstm_mixer(z, gates, head_gain, col0):
    n = z.shape[0]
    width = head_gain.shape[0]
    dh = width // MLSTM_HEADS
    chunk = min(MLSTM_CHUNK, n)
    base = col0 // width
    qkvo = [pl.BlockSpec((chunk, width), lambda c, p=p: (c, base + p)) for p in range(4)]
    return pl.pallas_call(
        functools.partial(_mlstm_kernel, chunk=chunk, dh=dh), grid=(n // chunk,),
        in_specs=qkvo + [pl.BlockSpec((chunk, LANES), lambda c: (c, 0)), pl.BlockSpec((1, width), lambda c: (0, 0))],
        out_specs=pl.BlockSpec((chunk, width), lambda c: (c, 0)),
        out_shape=jax.ShapeDtypeStruct((n, width), BF),
        scratch_shapes=[pltpu.VMEM((MLSTM_HEADS, dh, dh + LANES), F32), pltpu.VMEM((MLSTM_HEADS, 1, LANES), F32)],
        compiler_params=_cparams(("arbitrary",), 32), name="mlstm_mixer",
    )(z, z, z, z, gates, head_gain.reshape(1, width))


def _gated_conv(b, c, u, cp, up, w, first_block):
    zc = c.astype(F32) * u.astype(F32)
    zp = jnp.where(first_block, 0.0, cp.astype(F32) * up.astype(F32))
    row = lax.broadcasted_iota(jnp.int32, zc.shape, 0)
    acc = w[CONV_WIDTH - 1:CONV_WIDTH, :] * zc
    for back in range(1, CONV_WIDTH):
        shifted = pltpu.roll(zc, back, 0)
        for r in range(back):
            shifted = jnp.where(row == r, zp[CONV_HALO - back + r:CONV_HALO - back + r + 1, :], shifted)
        acc = acc + w[CONV_WIDTH - 1 - back:CONV_WIDTH - back, :] * shifted
    return b.astype(F32) * acc


def _conv_mm_res_kernel(b_ref, c_ref, u_ref, cp_ref, up_ref, cw_ref, w_ref, res_ref, o_ref, wb_ref, xs_ref):
    i = pl.program_id(0)

    @pl.when(i == 0)
    def _():
        _fill_bf16(wb_ref, w_ref)

    for c0 in range(0, xs_ref.shape[1], CONV_COLS):
        cs = slice(c0, c0 + CONV_COLS)
        xs_ref[:, cs] = _gated_conv(b_ref[:, cs], c_ref[:, cs], u_ref[:, cs], cp_ref[:, cs], up_ref[:, cs],
                                    cw_ref[:, cs], i == 0).astype(BF)
    o_ref[...] = res_ref[...] + jnp.dot(xs_ref[...], wb_ref[...], preferred_element_type=F32)


def conv_matmul_residual(z, conv_w, w, res, layer):
    n, d = res.shape
    tm = min(CONV_ROWS, n)
    halo_blocks = tm // CONV_HALO
    cur = lambda part: pl.BlockSpec((tm, d), lambda i: (i, part))
    prev = lambda part: pl.BlockSpec((CONV_HALO, d), lambda i: (jnp.maximum(i * halo_blocks - 1, 0), part))
    return pl.pallas_call(
        _conv_mm_res_kernel, grid=(n // tm,),
        in_specs=[cur(0), cur(1), cur(2), prev(1), prev(2), pl.BlockSpec((CONV_WIDTH, d), lambda i: (0, 0)),
                  pl.BlockSpec((None, d, d), lambda i: (layer, 0, 0), pipeline_mode=pl.Buffered(1)),
                  pl.BlockSpec((tm, d), lambda i: (i, 0))],
        out_specs=pl.BlockSpec((tm, d), lambda i: (i, 0)), out_shape=jax.ShapeDtypeStruct((n, d), F32),
        scratch_shapes=[pltpu.VMEM((d, d), BF), pltpu.VMEM((tm, d), BF)],
        compiler_params=_cparams(("arbitrary",), 52), name="conv_matmul_residual",
    )(z, z, z, z, z, conv_w, w, res)


def _router_kernel(x_ref, g_ref, wr_ref, br_ref, xn_ref, ids_ref, wts_ref, cnt_ref, carry_ref, whi_ref, wlo_ref,
                   *, tr):
    @pl.when(pl.program_id(0) == 0)
    def _():
        carry_ref[...] = jnp.zeros_like(carry_ref)

    @pl.when(pl.program_id(0) == 0)
    def _():
        w = wr_ref[...]
        w_hi = w.astype(BF)
        whi_ref[...] = w_hi
        wlo_ref[...] = (w - w_hi.astype(F32)).astype(BF)

    xn = _rms(x_ref[...], g_ref[...])
    xn_ref[...] = xn
    x_hi = xn.astype(BF)
    x_lo = (xn - x_hi.astype(F32)).astype(BF)
    logits = (jnp.dot(x_hi, whi_ref[...], preferred_element_type=F32)
              + jnp.dot(x_lo, whi_ref[...], preferred_element_type=F32)
              + jnp.dot(x_hi, wlo_ref[...], preferred_element_type=F32)) + br_ref[...]
    lane = lax.broadcasted_iota(jnp.int32, (tr, LANES), 1).astype(F32)
    neg = -jnp.inf

    def first_argmax(vals):
        top = jnp.max(vals, axis=-1, keepdims=True)
        return top, jnp.min(jnp.where(vals == top, lane, float(LANES)), axis=-1, keepdims=True)

    gl = jnp.where(lane < N_GROUPS, logits, neg)
    gmax, grp = first_argmax(gl)
    g_prob = 1.0 / jnp.sum(jnp.exp(gl - gmax), axis=-1, keepdims=True)
    lo = N_GROUPS + EXPERTS_PER_GROUP * grp
    el = jnp.where(lane >= lo, jnp.where(lane < lo + EXPERTS_PER_GROUP, logits, neg), neg)
    v1, l1 = first_argmax(el)
    v2, l2 = first_argmax(jnp.where(lane == l1, neg, el))
    e2 = jnp.exp(v2 - v1)
    w1 = g_prob / (1.0 + e2)
    w2 = g_prob * e2 / (1.0 + e2)
    hot1 = lane == l1
    hot2 = lane == l2
    hot = jnp.where(hot1, 1.0, jnp.where(hot2, 1.0, 0.0))
    earlier = (lax.broadcasted_iota(jnp.int32, (tr, tr), 1) < lax.broadcasted_iota(jnp.int32, (tr, tr), 0))
    before = jnp.dot(jnp.where(earlier, 1.0, 0.0).astype(BF), hot.astype(BF), preferred_element_type=F32)
    before = before + carry_ref[0:1, :]
    r1 = jnp.sum(jnp.where(hot1, before, 0.0), axis=-1, keepdims=True)
    r2 = jnp.sum(jnp.where(hot2, before, 0.0), axis=-1, keepdims=True)
    carry_ref[0:1, :] = carry_ref[0:1, :] + jnp.sum(hot, axis=0, keepdims=True)
    ids = jnp.where(lane == 0, l1 - N_GROUPS, jnp.where(lane == 1, l2 - N_GROUPS,
                    jnp.where(lane == 2, r1, jnp.where(lane == 3, r2, 0.0))))
    ids_ref[...] = ids.astype(jnp.int32)
    wts_ref[...] = jnp.where(lane == 0, w1, jnp.where(lane == 1, w2, 0.0))
    cnt_ref[...] = carry_ref[...]


def route(h, gain, w_router, b_router):
    n, d = h.shape
    tr = min(ROUTER_ROWS, n)
    return pl.pallas_call(
        functools.partial(_router_kernel, tr=tr), grid=(n // tr,),
        in_specs=[pl.BlockSpec((tr, d), lambda i: (i, 0)), pl.BlockSpec((1, d), lambda i: (0, 0)),
                  pl.BlockSpec((d, LANES), lambda i: (0, 0)), pl.BlockSpec((1, LANES), lambda i: (0, 0))],
        out_specs=[pl.BlockSpec((tr, d), lambda i: (i, 0)),
                   pl.BlockSpec((tr, LANES), lambda i: (i, 0)), pl.BlockSpec((tr, LANES), lambda i: (i, 0)),
                   pl.BlockSpec((SUBLANES, LANES), lambda i: (0, 0))],
        out_shape=[jax.ShapeDtypeStruct((n, d), F32), jax.ShapeDtypeStruct((n, LANES), jnp.int32),
                   jax.ShapeDtypeStruct((n, LANES), F32), jax.ShapeDtypeStruct((SUBLANES, LANES), F32)],
        scratch_shapes=[pltpu.VMEM((SUBLANES, LANES), F32), pltpu.VMEM((d, LANES), BF), pltpu.VMEM((d, LANES), BF)],
        compiler_params=_cparams(("arbitrary",), 40), name="moe_router",
    )(h, gain.reshape(1, d), w_router, b_router)


def _dest_kernel(ids_ref, start_ref, o_ref, *, tr):
    ids = ids_ref[...].astype(F32)
    lane = lax.broadcasted_iota(jnp.int32, (tr, LANES), 1).astype(F32)
    start = start_ref[...]
    rows = []
    for k in range(TOP_K):
        first = jnp.sum(jnp.where(lane == ids[:, k:k + 1], start, 0.0), axis=-1, keepdims=True)
        rows.append(first + ids[:, TOP_K + k:TOP_K + k + 1])
    packed = jnp.where(lane == 0, rows[0], jnp.where(lane == 1, rows[1], 0.0))
    o_ref[...] = packed.T[0:SUBLANES, :].astype(jnp.int32)


def assignment_rows(ids, start_rows):
    n = ids.shape[0]
    tr = min(ROUTER_ROWS, n)
    out = pl.pallas_call(
        functools.partial(_dest_kernel, tr=tr), grid=(n // tr,),
        in_specs=[pl.BlockSpec((tr, LANES), lambda i: (i, 0)), pl.BlockSpec((1, LANES), lambda i: (0, 0))],
        out_specs=pl.BlockSpec((SUBLANES, tr), lambda i: (0, i)),
        out_shape=jax.ShapeDtypeStruct((SUBLANES, n), jnp.int32),
        compiler_params=_cparams(("parallel",), 32), name="moe_assignment_rows",
    )(ids, start_rows)
    return out[:TOP_K]


def _row_copy(src_hbm, row, dst, dst_row, sem):
    return pltpu.make_async_copy(src_hbm.at[pl.ds(row, 1), :], dst.at[pl.ds(dst_row, 1), :], sem)


def _rows_wait(src_hbm, dst, sem):
    pltpu.make_async_copy(src_hbm.at[pl.ds(0, dst.shape[0]), :], dst, sem).wait()


def _is_new_expert(blk_ref, b):
    return jnp.logical_or(b == 0, blk_ref[b] != blk_ref[jnp.maximum(b - 1, 0)])


def _weight_copies(w_hbm, layer, expert, stage, sem):
    rows = stage.shape[0] // WEIGHT_DMA_PARTS
    return [pltpu.make_async_copy(w_hbm.at[layer, expert, pl.ds(p * rows, rows), :],
                                  stage.at[pl.ds(p * rows, rows), :], sem) for p in range(WEIGHT_DMA_PARTS)]


def _moe_expert_kernel(blk_ref, nxt_ref, nact_ref, rtok_ref, xn_hbm, wg_hbm, wu_hbm, wd_hbm, ys_ref,
                       buf, stage_g, stage_u, stage_d, wgb, wub, wdb, xb, hid, sem, wsem, *, tb, layer):
    b = pl.program_id(0)
    nact = nact_ref[0]
    nslots = GATHER_AHEAD + 1
    slot = b % nslots
    f = hid.shape[1]
    d = ys_ref.shape[1]
    up_chunks = f // MOE_COL_CHUNK
    rows_per_chunk = tb // up_chunks

    def expert_copies(e):
        return (_weight_copies(wg_hbm, layer, e, stage_g, wsem.at[0])
                + _weight_copies(wu_hbm, layer, e, stage_u, wsem.at[1])
                + _weight_copies(wd_hbm, layer, e, stage_d, wsem.at[2]))

    @pl.when(b == 0)
    def _():
        for cp in expert_copies(blk_ref[0]):
            cp.start(priority=DMA_QUEUE_BULK)
        for ahead in range(GATHER_AHEAD):
            def body(r, carry, ahead=ahead):
                _row_copy(xn_hbm, rtok_ref[ahead * tb + r], buf.at[ahead], r,
                          sem.at[ahead]).start(priority=DMA_QUEUE_ROWS)
                return carry

            lax.fori_loop(0, tb, body, 0, unroll=8)

    @pl.when(jnp.logical_and(b < nact, _is_new_expert(blk_ref, b)))
    def _():
        for cp in expert_copies(blk_ref[b]):
            cp.wait()
        _fill_bf16(wgb, stage_g)
        _fill_bf16(wub, stage_u)
        _fill_bf16(wdb, stage_d)

        @pl.when(nxt_ref[b] >= 0)
        def _():
            for cp in expert_copies(nxt_ref[b]):
                cp.start(priority=DMA_QUEUE_BULK)

    @pl.when(b < nact)
    def _():
        _rows_wait(xn_hbm, buf.at[slot], sem.at[slot])
        xb[...] = buf[slot].astype(BF)
        ahead_slot = (b + GATHER_AHEAD) % nslots
        for c in range(up_chunks):
            for r in range(c * rows_per_chunk, (c + 1) * rows_per_chunk):
                _row_copy(xn_hbm, rtok_ref[(b + GATHER_AHEAD) * tb + r], buf.at[ahead_slot], r,
                          sem.at[ahead_slot]).start(priority=DMA_QUEUE_ROWS)
            cs = slice(c * MOE_COL_CHUNK, (c + 1) * MOE_COL_CHUNK)
            gate = jnp.dot(xb[...], wgb[:, cs], preferred_element_type=F32)
            up = jnp.dot(xb[...], wub[:, cs], preferred_element_type=F32)
            hid[:, cs] = (gate * _sigmoid(gate) * up).astype(hid.dtype)
        for c in range(0, d, MOE_COL_CHUNK):
            cs = slice(c, c + MOE_COL_CHUNK)
            ys_ref[:, cs] = jnp.dot(hid[...], wdb[:, cs], preferred_element_type=F32)

    @pl.when(b >= nact)
    def _():
        ys_ref[...] = jnp.zeros_like(ys_ref)

    @pl.when(jnp.logical_and(b >= nact, b < nact + GATHER_AHEAD))
    def _():
        _rows_wait(xn_hbm, buf.at[slot], sem.at[slot])


def _combine_kernel(dest_ref, ys_hbm, h_ref, w_ref, gain_ref, o_ref, buf, sem, *, tc, n, final_norm):
    i = pl.program_id(0)

    def issue(blk, slot):
        base = blk * tc

        def body(r, carry):
            for k in range(TOP_K):
                _row_copy(ys_hbm, dest_ref[k * n + base + r], buf.at[slot], k * tc + r,
                          sem.at[slot]).start(priority=k)
            return carry

        lax.fori_loop(0, tc, body, 0, unroll=8)

    @pl.when(i == 0)
    def _():
        issue(0, 0)

    @pl.when(i + 1 < pl.num_programs(0))
    def _():
        issue(i + 1, (i + 1) % 2)

    slot = i % 2
    _rows_wait(ys_hbm, buf.at[slot], sem.at[slot])
    out = h_ref[...] + (w_ref[:, 0:1] * buf[slot, 0:tc, :] + w_ref[:, 1:2] * buf[slot, tc:TOP_K * tc, :])
    if final_norm:
        out = _rms(out, gain_ref[...])
    o_ref[...] = out


def hier_moe(h, ffn_gain, wg_r, bg_r, we_r, be_r, w_gate, w_up, w_down, layer, final_gain=None):
    n, d = h.shape
    f = w_gate.shape[-1]
    a = n * TOP_K
    tb = _moe_block_rows(a)
    nb = a // tb + N_EXPERTS + GATHER_AHEAD

    pad = LANES - N_GROUPS - N_EXPERTS
    w_router = jnp.concatenate([wg_r, we_r, jnp.zeros((d, pad), F32)], axis=1)
    b_router = jnp.concatenate([bg_r, be_r, jnp.zeros((pad,), F32)]).reshape(1, LANES)
    xn, ids, wts, cnt = route(h, ffn_gain, w_router, b_router)

    counts = cnt[0, N_GROUPS:N_GROUPS + N_EXPERTS].astype(jnp.int32)
    nblk = (counts + tb - 1) // tb
    bend = jnp.cumsum(nblk)
    nact = bend[-1]
    start_rows = jnp.pad(((bend - nblk) * tb).astype(F32), (0, LANES - N_EXPERTS)).reshape(1, LANES)
    dest = assignment_rows(ids, start_rows).reshape(a)
    blk = jnp.minimum(jnp.arange(nb, dtype=jnp.int32), jnp.maximum(nact - 1, 0))
    owner = lambda bi: jnp.minimum(jnp.searchsorted(bend, bi, side='right'), N_EXPERTS - 1).astype(jnp.int32)
    blk_e = owner(blk)
    after = bend[blk_e]
    nxt_e = jnp.where(after < nact, owner(after), -1)
    tok = jnp.tile(jnp.arange(n, dtype=jnp.int32), TOP_K)
    row_tok = (jnp.arange(nb * tb, dtype=jnp.int32) % n).at[dest].set(tok)
    nact1 = nact.reshape(1).astype(jnp.int32)

    any_space = pl.BlockSpec(memory_space=pl.ANY)
    ys = pl.pallas_call(
        functools.partial(_moe_expert_kernel, tb=tb, layer=layer),
        grid_spec=pltpu.PrefetchScalarGridSpec(
            num_scalar_prefetch=4, grid=(nb,),
            in_specs=[any_space, any_space, any_space, any_space],
            out_specs=pl.BlockSpec((tb, d), lambda b, *_: (b, 0)),
            scratch_shapes=[pltpu.VMEM((GATHER_AHEAD + 1, tb, d), F32),
                            pltpu.VMEM((d, f), F32), pltpu.VMEM((d, f), F32), pltpu.VMEM((f, d), F32),
                            pltpu.VMEM((d, f), BF), pltpu.VMEM((d, f), BF), pltpu.VMEM((f, d), BF),
                            pltpu.VMEM((tb, d), BF), pltpu.VMEM((tb, f), BF),
                            pltpu.SemaphoreType.DMA((GATHER_AHEAD + 1,)), pltpu.SemaphoreType.DMA((3,))]),
        out_shape=jax.ShapeDtypeStruct((nb * tb, d), F32),
        compiler_params=_cparams(("arbitrary",), 58), name="moe_experts",
    )(blk_e, nxt_e, nact1, row_tok, xn, w_gate, w_up, w_down)

    tc = min(COMBINE_ROWS, n)
    gain = (final_gain if final_gain is not None else ffn_gain).reshape(1, d)
    return pl.pallas_call(
        functools.partial(_combine_kernel, tc=tc, n=n, final_norm=final_gain is not None),
        grid_spec=pltpu.PrefetchScalarGridSpec(
            num_scalar_prefetch=1, grid=(n // tc,),
            in_specs=[pl.BlockSpec(memory_space=pl.ANY), pl.BlockSpec((tc, d), lambda i, *_: (i, 0)),
                      pl.BlockSpec((tc, LANES), lambda i, *_: (i, 0)), pl.BlockSpec((1, d), lambda i, *_: (0, 0))],
            out_specs=pl.BlockSpec((tc, d), lambda i, *_: (i, 0)),
            scratch_shapes=[pltpu.VMEM((2, TOP_K * tc, d), F32), pltpu.SemaphoreType.DMA((2,))]),
        out_shape=jax.ShapeDtypeStruct((n, d), F32),
        compiler_params=_cparams(("arbitrary",), 40), name="moe_combine",
    )(dest, ys, h, wts, gain)


def even_mixer(h, gain, w_in, b_gates, pool_w, pool_scale, head_gain, w_out, j):
    d = h.shape[1]
    pool_width = N_POOL_GROUPS * pool_w.shape[-1]
    mlstm_width = head_gain.shape[0]
    main_cols = pool_width + 4 * mlstm_width
    n_gates = 2 * MLSTM_HEADS
    gate_b = jnp.pad(b_gates, (0, LANES - n_gates)).reshape(1, LANES)
    z, gates = norm_matmul_gates_t(h, gain, jnp.swapaxes(w_in, 1, 2), main_cols, n_gates, gate_b, j)
    y_p = pool_mixer(z, pool_w, pool_scale)
    y_m = mlstm_mixer(z, gates, head_gain, pool_width)
    assert pool_width == mlstm_width and pool_width + mlstm_width == d
    return matmul_residual([y_p, y_m], w_out, h, layer=j)


def odd_mixer(h, gain, w_in, conv_w, w_out, j):
    z = norm_matmul(h, gain, w_in, w_in.shape[2], layer=j)
    return conv_matmul_residual(z, conv_w, w_out, h, j)


def cross_attn(h, mem, gain, mem_gain, wq, wk, wv, wo, layer):
    d = h.shape[1]
    k = norm_matmul(mem, mem_gain, wk, d, layer=layer)
    v = norm_matmul(mem, mem_gain, wv, d, layer=layer)
    return matmul_residual([q_attention(h, gain, wq, k, v, layer)], wo, h, layer=layer)


def kernel(x, mem, mix_norm, xattn_norm, mem_norm, ffn_norm, final_norm, ev_w_in, ev_b_gates, ev_pool_w, ev_pool_scale, ev_head_norm, ev_w_out, od_w_in, od_conv_w, od_w_out, xa_wq, xa_wk, xa_wv, xa_wo, rt_group_w, rt_group_b, rt_expert_w, rt_expert_b, ex_w_gate, ex_w_up, ex_w_down):
    depth = mix_norm.shape[0]
    h = x[0]
    m = mem[0]
    for layer in range(depth):
        j = layer // 2
        if layer % 2 == 0:
            h = even_mixer(h, mix_norm[layer], ev_w_in, ev_b_gates[j], ev_pool_w[j], ev_pool_scale[j],
                           ev_head_norm[j], ev_w_out, j)
        else:
            h = odd_mixer(h, mix_norm[layer], od_w_in, od_conv_w[j], od_w_out, j)
        h = cross_attn(h, m, xattn_norm[layer], mem_norm[layer], xa_wq, xa_wk, xa_wv, xa_wo, layer)
        h = hier_moe(h, ffn_norm[layer], rt_group_w[layer], rt_group_b[layer], rt_expert_w[layer],
                     rt_expert_b[layer], ex_w_gate, ex_w_up, ex_w_down, layer,
                     final_gain=final_norm if layer == depth - 1 else None)
    return h[None]
```

```python
import functools

import jax
import jax.numpy as jnp
from jax import lax
from jax.experimental import pallas as pl
from jax.experimental.pallas import tpu as pltpu

F32 = jnp.float32
BF = jnp.bfloat16
EPS = 1e-6

POOL_WINDOWS = (2, 4, 8, 16)
N_POOL_GROUPS = 4
MLSTM_HEADS = 4
FORGET_LANE0 = MLSTM_HEADS
XATTN_HEADS = 4
N_GROUPS = 4
EXPERTS_PER_GROUP = 8
N_EXPERTS = N_GROUPS * EXPERTS_PER_GROUP
TOP_K = 2
CONV_WIDTH = 3

LANES = 128
SUBLANES = 8
DMA_QUEUE_ROWS = 0
DMA_QUEUE_BULK = 1

ROW_TILE = 1024
COL_TILE = 1024
OUT_ROW_TILE = 512
CAST_ROWS = 64
CAST_UNROLL = 4
MOE_COL_CHUNK = 256
UNIT_ROWS = 1024
UNIT_COLS = 512
UNITS_PER_MATRIX = 4
RING_SLOTS = 6
CONVERT_ROWS = 256
MLSTM_CHUNK = 256
POOL_ROWS = 256
POOL_HALO = 128
CONV_ROWS = 256
CONV_COLS = 512
CONV_HALO = 16
ROUTER_ROWS = 512
GATHER_AHEAD = 2
COMBINE_ROWS = 256


def _moe_block_rows(assignments):
    mean_rows = assignments // N_EXPERTS
    return -(-(mean_rows * 9 // 16) // 16) * 16


def _cparams(semantics, vmem_mib):
    return pltpu.CompilerParams(dimension_semantics=semantics, vmem_limit_bytes=vmem_mib * 1024 * 1024)


def _sigmoid(x):
    return 1.0 / (1.0 + jnp.exp(-x))


def _log_sigmoid(x):
    return jnp.minimum(x, 0.0) - jnp.log(1.0 + jnp.exp(-jnp.abs(x)))


def _rms(x, g):
    ms = jnp.mean(x * x, axis=-1, keepdims=True)
    return x * lax.rsqrt(ms + EPS) * g


def _norm_mm_kernel(x_ref, g_ref, w_ref, o_ref, xn_ref):
    @pl.when(pl.program_id(1) == 0)
    def _():
        xn_ref[...] = _rms(x_ref[...], g_ref[...]).astype(BF)

    o_ref[...] = jnp.dot(xn_ref[...], w_ref[...].astype(BF), preferred_element_type=F32).astype(o_ref.dtype)


_NT = (((1,), (1,)), ((), ()))


def _norm_mm_gates_t_kernel(x_ref, g_ref, wt_ref, wgt_ref, bg_ref, o_ref, gates_ref, xn_ref):
    @pl.when(pl.program_id(1) == 0)
    def _():
        xn = _rms(x_ref[...], g_ref[...]).astype(BF)
        xn_ref[...] = xn
        wg = wgt_ref[...]
        wg = jnp.concatenate([wg, jnp.zeros((LANES - wg.shape[0], wg.shape[1]), F32)], axis=0).astype(BF)
        gates_ref[...] = lax.dot_general(xn, wg, _NT, preferred_element_type=F32) + bg_ref[...]

    o_ref[...] = lax.dot_general(xn_ref[...], wt_ref[...].astype(BF), _NT,
                                 preferred_element_type=F32).astype(o_ref.dtype)


def norm_matmul_gates_t(x, gain, wt, n_cols, n_gates, gate_b, layer):
    n, k = x.shape
    tm = min(ROW_TILE, n)
    tn = COL_TILE
    return pl.pallas_call(
        _norm_mm_gates_t_kernel, grid=(n // tm, n_cols // tn),
        in_specs=[pl.BlockSpec((tm, k), lambda i, j: (i, 0)), pl.BlockSpec((1, k), lambda i, j: (0, 0)),
                  pl.BlockSpec((None, tn, k), lambda i, j: (layer, j, 0)),
                  pl.BlockSpec((None, n_gates, k), lambda i, j: (layer, n_cols // n_gates, 0)),
                  pl.BlockSpec((1, LANES), lambda i, j: (0, 0))],
        out_specs=[pl.BlockSpec((tm, tn), lambda i, j: (i, j)), pl.BlockSpec((tm, LANES), lambda i, j: (i, 0))],
        out_shape=[jax.ShapeDtypeStruct((n, n_cols), BF), jax.ShapeDtypeStruct((n, LANES), F32)],
        scratch_shapes=[pltpu.VMEM((tm, k), BF)],
        compiler_params=_cparams(("parallel", "arbitrary"), 56), name="norm_matmul_gates",
    )(x, gain.reshape(1, k), wt, wt, gate_b)


def _stacked(w, layer):
    return (w[None], 0) if layer is None else (w, layer)


def norm_matmul(x, gain, w, n_cols, layer=None):
    n, k = x.shape
    w, li = _stacked(w, layer)
    tm = min(ROW_TILE, n)
    tn = COL_TILE
    return pl.pallas_call(
        _norm_mm_kernel, grid=(n // tm, n_cols // tn),
        in_specs=[pl.BlockSpec((tm, k), lambda i, j: (i, 0)), pl.BlockSpec((1, k), lambda i, j: (0, 0)),
                  pl.BlockSpec((None, k, tn), lambda i, j: (li, 0, j))],
        out_specs=pl.BlockSpec((tm, tn), lambda i, j: (i, j)),
        out_shape=jax.ShapeDtypeStruct((n, n_cols), BF), scratch_shapes=[pltpu.VMEM((tm, k), BF)],
        compiler_params=_cparams(("parallel", "arbitrary"), 56), name="norm_matmul",
    )(x, gain.reshape(1, k), w)


def _fill_bf16(dst_ref, src_ref):
    def body(i, carry):
        r = pl.multiple_of(i * CAST_ROWS, CAST_ROWS)
        dst_ref[pl.ds(r, CAST_ROWS), :] = src_ref[pl.ds(r, CAST_ROWS), :].astype(BF)
        return carry

    lax.fori_loop(0, src_ref.shape[0] // CAST_ROWS, body, 0, unroll=CAST_UNROLL)


def _q_attention_kernel(x_ref, g_ref, w_ref, k_ref, v_ref, o_ref, wb_ref, q_ref, *, hd):
    @pl.when(pl.program_id(0) == 0)
    def _():
        _fill_bf16(wb_ref, w_ref)

    xn = _rms(x_ref[...], g_ref[...]).astype(BF)
    q_ref[...] = jnp.dot(xn, wb_ref[...], preferred_element_type=F32).astype(BF)
    scale = hd ** -0.5
    for h in range(XATTN_HEADS):
        hs = slice(h * hd, (h + 1) * hd)
        s = lax.dot_general(q_ref[:, hs], k_ref[:, hs], _NT, preferred_element_type=F32) * scale
        e = jnp.exp(s - jnp.max(s, axis=-1, keepdims=True))
        p = (e / jnp.sum(e, axis=-1, keepdims=True)).astype(BF)
        o_ref[:, hs] = jnp.dot(p, v_ref[:, hs], preferred_element_type=F32).astype(o_ref.dtype)


def q_attention(x, gain, wq, k, v, layer):
    n, kd = x.shape
    d = wq.shape[2]
    m = k.shape[0]
    tm = min(OUT_ROW_TILE, n)
    return pl.pallas_call(
        functools.partial(_q_attention_kernel, hd=d // XATTN_HEADS), grid=(n // tm,),
        in_specs=[pl.BlockSpec((tm, kd), lambda i: (i, 0)), pl.BlockSpec((1, kd), lambda i: (0, 0)),
                  pl.BlockSpec((None, kd, d), lambda i: (layer, 0, 0), pipeline_mode=pl.Buffered(1)),
                  pl.BlockSpec((m, d), lambda i: (0, 0)), pl.BlockSpec((m, d), lambda i: (0, 0))],
        out_specs=pl.BlockSpec((tm, d), lambda i: (i, 0)), out_shape=jax.ShapeDtypeStruct((n, d), BF),
        scratch_shapes=[pltpu.VMEM((kd, d), BF), pltpu.VMEM((tm, d), BF)],
        compiler_params=_cparams(("arbitrary",), 52), name="q_attention",
    )(x, gain.reshape(1, kd), wq, k, v)


def _mm_res_kernel(*refs, nparts):
    xs = refs[:nparts]
    w_ref, res_ref, o_ref, wb_ref = refs[nparts:]

    @pl.when(pl.program_id(0) == 0)
    def _():
        _fill_bf16(wb_ref, w_ref)

    acc = res_ref[...]
    k0 = 0
    for x_ref in xs:
        kp = x_ref.shape[1]
        acc = acc + jnp.dot(x_ref[...], wb_ref[k0:k0 + kp, :], preferred_element_type=F32)
        k0 += kp
    o_ref[...] = acc


def matmul_residual(xs, w, res, layer=None):
    n, d = res.shape
    w, li = _stacked(w, layer)
    k = w.shape[1]
    tm = min(OUT_ROW_TILE, n)
    in_specs = [pl.BlockSpec((tm, x.shape[1]), lambda i: (i, 0)) for x in xs]
    in_specs += [pl.BlockSpec((None, k, d), lambda i: (li, 0, 0), pipeline_mode=pl.Buffered(1)),
                 pl.BlockSpec((tm, d), lambda i: (i, 0))]
    return pl.pallas_call(
        functools.partial(_mm_res_kernel, nparts=len(xs)), grid=(n // tm,), in_specs=in_specs,
        out_specs=pl.BlockSpec((tm, d), lambda i: (i, 0)), out_shape=jax.ShapeDtypeStruct((n, d), F32),
        scratch_shapes=[pltpu.VMEM((k, d), BF)],
        compiler_params=_cparams(("arbitrary",), 52), name="matmul_residual",
    )(*xs, w, res)


def _pool_kernel(cur_ref, prev_ref, w_ref, sc_ref, o_ref, *, tp, gdim):
    i = pl.program_id(0)
    dist = lax.broadcasted_iota(jnp.int32, (tp, tp), 0) - lax.broadcasted_iota(jnp.int32, (tp, tp), 1)
    distp = (lax.broadcasted_iota(jnp.int32, (tp, POOL_HALO), 0) + POOL_HALO
             - lax.broadcasted_iota(jnp.int32, (tp, POOL_HALO), 1))
    pos = i * tp + lax.broadcasted_iota(jnp.int32, (tp, 1), 0)
    for j, win in enumerate(POOL_WINDOWS):
        gs = slice(j * gdim, (j + 1) * gdim)
        cur = cur_ref[:, gs]
        band = jnp.where(dist >= 0, jnp.where(dist < win, 1.0, 0.0), 0.0).astype(BF)
        bandp = jnp.where(distp < jnp.where(i > 0, win, 0), 1.0, 0.0).astype(BF)
        s = (jnp.dot(band, cur, preferred_element_type=F32)
             + jnp.dot(bandp, prev_ref[:, gs], preferred_element_type=F32))
        cnt = jnp.minimum(pos + 1, win).astype(F32)
        d = s / cnt - cur.astype(F32)
        y = jnp.dot(d.astype(BF), w_ref[j].astype(BF), preferred_element_type=F32) * sc_ref[:, gs]
        o_ref[:, gs] = y.astype(o_ref.dtype)


def pool_mixer(z, pool_w, pool_scale):
    n = z.shape[0]
    gdim = pool_w.shape[-1]
    width = N_POOL_GROUPS * gdim
    tp = min(POOL_ROWS, n)
    halo_blocks = tp // POOL_HALO
    return pl.pallas_call(
        functools.partial(_pool_kernel, tp=tp, gdim=gdim), grid=(n // tp,),
        in_specs=[
            pl.BlockSpec((tp, width), lambda i: (i, 0)),
            pl.BlockSpec((POOL_HALO, width), lambda i: (jnp.maximum(i * halo_blocks - 1, 0), 0)),
            pl.BlockSpec((N_POOL_GROUPS, gdim, gdim), lambda i: (0, 0, 0)),
            pl.BlockSpec((1, width), lambda i: (0, 0)),
        ],
        out_specs=pl.BlockSpec((tp, width), lambda i: (i, 0)),
        out_shape=jax.ShapeDtypeStruct((n, width), BF),
        compiler_params=_cparams(("parallel",), 32), name="pool_mixer",
    )(z, z, pool_w, pool_scale.reshape(1, width))


def _mlstm_kernel(q_ref, k_ref, v_ref, o_ref, g_ref, gain_ref, y_ref, ct_ref, m_ref, *, chunk, dh):
    c = pl.program_id(0)

    @pl.when(c == 0)
    def _():
        ct_ref[...] = jnp.zeros_like(ct_ref)
        m_ref[...] = jnp.zeros_like(m_ref)

    g = g_ref[...]
    lf = _log_sigmoid(g)
    row = lax.broadcasted_iota(jnp.int32, (chunk, chunk), 0)
    col = lax.broadcasted_iota(jnp.int32, (chunk, chunk), 1)
    causal = col <= row
    ltri = jnp.where(causal, 1.0, 0.0).astype(BF)
    hi = lf.astype(BF)
    r1 = lf - hi.astype(F32)
    mid = r1.astype(BF)
    lo = (r1 - mid.astype(F32)).astype(BF)
    bcum = (jnp.dot(ltri, hi, preferred_element_type=F32) + jnp.dot(ltri, mid, preferred_element_type=F32)
            + jnp.dot(ltri, lo, preferred_element_type=F32))
    g_t = g.T
    b_t = bcum.T
    ones_col = jnp.where(lax.broadcasted_iota(jnp.int32, (chunk, LANES), 1) == 0, 1.0, 0.0).astype(BF)

    for h in range(MLSTM_HEADS):
        hs = slice(h * dh, (h + 1) * dh)
        fl = FORGET_LANE0 + h
        bc = bcum[:, fl:fl + 1]
        br = b_t[fl:fl + 1, :]
        ir = g_t[h:h + 1, :]
        b_last = bcum[chunk - 1:chunk, fl:fl + 1]
        m_prev = m_ref[h][:, 0:1]

        dmat = jnp.where(causal, bc + (ir - br), -jnp.inf)
        inter = bc + m_prev
        m_t = jnp.maximum(jnp.max(dmat, axis=1, keepdims=True), inter)
        w_inter = jnp.exp(inter - m_t)
        p = jnp.exp(dmat - m_t)

        qh = q_ref[:, hs] * (dh ** -0.5)
        kh = k_ref[:, hs]
        v_aug = jnp.concatenate([v_ref[:, hs], ones_col], axis=1)
        s = lax.dot_general(qh, kh, (((1,), (1,)), ((), ())), preferred_element_type=F32)
        sc = (s * p).astype(BF)
        ct = ct_ref[h]
        num_aug = (w_inter * jnp.dot(qh, ct.astype(BF), preferred_element_type=F32)
                   + jnp.dot(sc, v_aug, preferred_element_type=F32))
        num = num_aug[:, :dh]
        den = num_aug[:, dh:dh + 1]
        hout = num / jnp.maximum(jnp.abs(den), jnp.exp(-m_t))

        yn = _rms(hout, gain_ref[:, hs])
        y_ref[:, hs] = (_sigmoid(o_ref[:, hs].astype(F32)) * yn).astype(y_ref.dtype)

        d_end = b_last - br + ir
        m_new = jnp.maximum(b_last + m_prev, jnp.max(d_end, axis=1, keepdims=True))
        a_prev = jnp.exp(b_last + m_prev - m_new)
        a_s = jnp.exp(d_end - m_new)
        k_t = (kh.astype(F32).T * a_s).astype(BF)
        ct_ref[h] = a_prev * ct + jnp.dot(k_t, v_aug, preferred_element_type=F32)
        m_ref[h] = jnp.broadcast_to(m_new, (1, LANES))


def mlstm_mixer(z, gates, head_gain, col0):
    n = z.shape[0]
    width = head_gain.shape[0]
    dh = width // MLSTM_HEADS
    chunk = min(MLSTM_CHUNK, n)
    base = col0 // width
    qkvo = [pl.BlockSpec((chunk, width), lambda c, p=p: (c, base + p)) for p in range(4)]
    return pl.pallas_call(
        functools.partial(_mlstm_kernel, chunk=chunk, dh=dh), grid=(n // chunk,),
        in_specs=qkvo + [pl.BlockSpec((chunk, LANES), lambda c: (c, 0)), pl.BlockSpec((1, width), lambda c: (0, 0))],
        out_specs=pl.BlockSpec((chunk, width), lambda c: (c, 0)),
        out_shape=jax.ShapeDtypeStruct((n, width), BF),
        scratch_shapes=[pltpu.VMEM((MLSTM_HEADS, dh, dh + LANES), F32), pltpu.VMEM((MLSTM_HEADS, 1, LANES), F32)],
        compiler_params=_cparams(("arbitrary",), 32), name="mlstm_mixer",
    )(z, z, z, z, gates, head_gain.reshape(1, width))


def _gated_conv(b, c, u, cp, up, w, first_block):
    zc = c.astype(F32) * u.astype(F32)
    zp = jnp.where(first_block, 0.0, cp.astype(F32) * up.astype(F32))
    row = lax.broadcasted_iota(jnp.int32, zc.shape, 0)
    acc = w[CONV_WIDTH - 1:CONV_WIDTH, :] * zc
    for back in range(1, CONV_WIDTH):
        shifted = pltpu.roll(zc, back, 0)
        for r in range(back):
            shifted = jnp.where(row == r, zp[CONV_HALO - back + r:CONV_HALO - back + r + 1, :], shifted)
        acc = acc + w[CONV_WIDTH - 1 - back:CONV_WIDTH - back, :] * shifted
    return b.astype(F32) * acc


def _conv_mm_res_kernel(b_ref, c_ref, u_ref, cp_ref, up_ref, cw_ref, w_ref, res_ref, o_ref, wb_ref, xs_ref):
    i = pl.program_id(0)

    @pl.when(i == 0)
    def _():
        _fill_bf16(wb_ref, w_ref)

    for c0 in range(0, xs_ref.shape[1], CONV_COLS):
        cs = slice(c0, c0 + CONV_COLS)
        xs_ref[:, cs] = _gated_conv(b_ref[:, cs], c_ref[:, cs], u_ref[:, cs], cp_ref[:, cs], up_ref[:, cs],
                                    cw_ref[:, cs], i == 0).astype(BF)
    o_ref[...] = res_ref[...] + jnp.dot(xs_ref[...], wb_ref[...], preferred_element_type=F32)


def conv_matmul_residual(z, conv_w, w, res, layer):
    n, d = res.shape
    tm = min(CONV_ROWS, n)
    halo_blocks = tm // CONV_HALO
    cur = lambda part: pl.BlockSpec((tm, d), lambda i: (i, part))
    prev = lambda part: pl.BlockSpec((CONV_HALO, d), lambda i: (jnp.maximum(i * halo_blocks - 1, 0), part))
    return pl.pallas_call(
        _conv_mm_res_kernel, grid=(n // tm,),
        in_specs=[cur(0), cur(1), cur(2), prev(1), prev(2), pl.BlockSpec((CONV_WIDTH, d), lambda i: (0, 0)),
                  pl.BlockSpec((None, d, d), lambda i: (layer, 0, 0), pipeline_mode=pl.Buffered(1)),
                  pl.BlockSpec((tm, d), lambda i: (i, 0))],
        out_specs=pl.BlockSpec((tm, d), lambda i: (i, 0)), out_shape=jax.ShapeDtypeStruct((n, d), F32),
        scratch_shapes=[pltpu.VMEM((d, d), BF), pltpu.VMEM((tm, d), BF)],
        compiler_params=_cparams(("arbitrary",), 52), name="conv_matmul_residual",
    )(z, z, z, z, z, conv_w, w, res)


def _router_kernel(x_ref, g_ref, wr_ref, br_ref, xn_ref, ids_ref, wts_ref, cnt_ref, carry_ref, whi_ref, wlo_ref,
                   *, tr):
    @pl.when(pl.program_id(0) == 0)
    def _():
        carry_ref[...] = jnp.zeros_like(carry_ref)

    @pl.when(pl.program_id(0) == 0)
    def _():
        w = wr_ref[...]
        w_hi = w.astype(BF)
        whi_ref[...] = w_hi
        wlo_ref[...] = (w - w_hi.astype(F32)).astype(BF)

    xn = _rms(x_ref[...], g_ref[...])
    xn_ref[...] = xn
    x_hi = xn.astype(BF)
    x_lo = (xn - x_hi.astype(F32)).astype(BF)
    logits = (jnp.dot(x_hi, whi_ref[...], preferred_element_type=F32)
              + jnp.dot(x_lo, whi_ref[...], preferred_element_type=F32)
              + jnp.dot(x_hi, wlo_ref[...], preferred_element_type=F32)) + br_ref[...]
    lane = lax.broadcasted_iota(jnp.int32, (tr, LANES), 1).astype(F32)
    neg = -jnp.inf

    def first_argmax(vals):
        top = jnp.max(vals, axis=-1, keepdims=True)
        return top, jnp.min(jnp.where(vals == top, lane, float(LANES)), axis=-1, keepdims=True)

    gl = jnp.where(lane < N_GROUPS, logits, neg)
    gmax, grp = first_argmax(gl)
    g_prob = 1.0 / jnp.sum(jnp.exp(gl - gmax), axis=-1, keepdims=True)
    lo = N_GROUPS + EXPERTS_PER_GROUP * grp
    el = jnp.where(lane >= lo, jnp.where(lane < lo + EXPERTS_PER_GROUP, logits, neg), neg)
    v1, l1 = first_argmax(el)
    v2, l2 = first_argmax(jnp.where(lane == l1, neg, el))
    e2 = jnp.exp(v2 - v1)
    w1 = g_prob / (1.0 + e2)
    w2 = g_prob * e2 / (1.0 + e2)
    hot1 = lane == l1
    hot2 = lane == l2
    hot = jnp.where(hot1, 1.0, jnp.where(hot2, 1.0, 0.0))
    earlier = (lax.broadcasted_iota(jnp.int32, (tr, tr), 1) < lax.broadcasted_iota(jnp.int32, (tr, tr), 0))
    before = jnp.dot(jnp.where(earlier, 1.0, 0.0).astype(BF), hot.astype(BF), preferred_element_type=F32)
    before = before + carry_ref[0:1, :]
    r1 = jnp.sum(jnp.where(hot1, before, 0.0), axis=-1, keepdims=True)
    r2 = jnp.sum(jnp.where(hot2, before, 0.0), axis=-1, keepdims=True)
    carry_ref[0:1, :] = carry_ref[0:1, :] + jnp.sum(hot, axis=0, keepdims=True)
    ids = jnp.where(lane == 0, l1 - N_GROUPS, jnp.where(lane == 1, l2 - N_GROUPS,
                    jnp.where(lane == 2, r1, jnp.where(lane == 3, r2, 0.0))))
    ids_ref[...] = ids.astype(jnp.int32)
    wts_ref[...] = jnp.where(lane == 0, w1, jnp.where(lane == 1, w2, 0.0))
    cnt_ref[...] = carry_ref[...]


def route(h, gain, w_router, b_router):
    n, d = h.shape
    tr = min(ROUTER_ROWS, n)
    return pl.pallas_call(
        functools.partial(_router_kernel, tr=tr), grid=(n // tr,),
        in_specs=[pl.BlockSpec((tr, d), lambda i: (i, 0)), pl.BlockSpec((1, d), lambda i: (0, 0)),
                  pl.BlockSpec((d, LANES), lambda i: (0, 0)), pl.BlockSpec((1, LANES), lambda i: (0, 0))],
        out_specs=[pl.BlockSpec((tr, d), lambda i: (i, 0)),
                   pl.BlockSpec((tr, LANES), lambda i: (i, 0)), pl.BlockSpec((tr, LANES), lambda i: (i, 0)),
                   pl.BlockSpec((SUBLANES, LANES), lambda i: (0, 0))],
        out_shape=[jax.ShapeDtypeStruct((n, d), F32), jax.ShapeDtypeStruct((n, LANES), jnp.int32),
                   jax.ShapeDtypeStruct((n, LANES), F32), jax.ShapeDtypeStruct((SUBLANES, LANES), F32)],
        scratch_shapes=[pltpu.VMEM((SUBLANES, LANES), F32), pltpu.VMEM((d, LANES), BF), pltpu.VMEM((d, LANES), BF)],
        compiler_params=_cparams(("arbitrary",), 40), name="moe_router",
    )(h, gain.reshape(1, d), w_router, b_router)


def _dest_kernel(ids_ref, start_ref, o_ref, *, tr):
    ids = ids_ref[...].astype(F32)
    lane = lax.broadcasted_iota(jnp.int32, (tr, LANES), 1).astype(F32)
    start = start_ref[...]
    rows = []
    for k in range(TOP_K):
        first = jnp.sum(jnp.where(lane == ids[:, k:k + 1], start, 0.0), axis=-1, keepdims=True)
        rows.append(first + ids[:, TOP_K + k:TOP_K + k + 1])
    packed = jnp.where(lane == 0, rows[0], jnp.where(lane == 1, rows[1], 0.0))
    o_ref[...] = packed.T[0:SUBLANES, :].astype(jnp.int32)


def assignment_rows(ids, start_rows):
    n = ids.shape[0]
    tr = min(ROUTER_ROWS, n)
    out = pl.pallas_call(
        functools.partial(_dest_kernel, tr=tr), grid=(n // tr,),
        in_specs=[pl.BlockSpec((tr, LANES), lambda i: (i, 0)), pl.BlockSpec((1, LANES), lambda i: (0, 0))],
        out_specs=pl.BlockSpec((SUBLANES, tr), lambda i: (0, i)),
        out_shape=jax.ShapeDtypeStruct((SUBLANES, n), jnp.int32),
        compiler_params=_cparams(("parallel",), 32), name="moe_assignment_rows",
    )(ids, start_rows)
    return out[:TOP_K]


def _row_copy(src_hbm, row, dst, dst_row, sem):
    return pltpu.make_async_copy(src_hbm.at[pl.ds(row, 1), :], dst.at[pl.ds(dst_row, 1), :], sem)


def _rows_wait(src_hbm, dst, sem):
    pltpu.make_async_copy(src_hbm.at[pl.ds(0, dst.shape[0]), :], dst, sem).wait()


def _weight_unit(u):
    m, q = divmod(u, UNITS_PER_MATRIX)
    if m < 2:
        return m, (q // 2) * UNIT_ROWS, (q % 2) * UNIT_COLS
    return m, 0, q * UNIT_COLS


def _moe_expert_kernel(blk_ref, kin_ref, single_ref, set_ref, n1_ref, n2_ref, nact_ref, rtok_ref,
                       xn_hbm, wg_hbm, wu_hbm, wd_hbm, ys_ref,
                       buf, ring, wgb, wub, wdb, xb, hid, sem, rsem, *, tb, layer):
    b = pl.program_id(0)
    nact = nact_ref[0]
    nslots = GATHER_AHEAD + 1
    slot = b % nslots
    f = hid.shape[1]
    d = ys_ref.shape[1]
    up_chunks = f // MOE_COL_CHUNK
    down_chunks = d // MOE_COL_CHUNK
    rows_per_chunk = tb // up_chunks
    w_hbm = (wg_hbm, wu_hbm, wd_hbm)
    w_sets = (wgb, wub, wdb)
    cur = set_ref[b]

    def unit_copy(u, expert):
        m, r0, c0 = _weight_unit(u)
        s = u % RING_SLOTS
        return pltpu.make_async_copy(w_hbm[m].at[layer, expert, pl.ds(r0, UNIT_ROWS), pl.ds(c0, UNIT_COLS)],
                                     ring.at[s], rsem.at[s])

    def unit_convert(u, wset):
        m, r0, c0 = _weight_unit(u)
        s = u % RING_SLOTS
        for r in range(0, UNIT_ROWS, CONVERT_ROWS):
            w_sets[m][wset, r0 + r:r0 + r + CONVERT_ROWS, c0:c0 + UNIT_COLS] = (
                ring[s, r:r + CONVERT_ROWS, :].astype(BF))

    def advance(u, expert, wset, u_next, expert_next):
        unit_copy(u, expert).wait()
        unit_convert(u, wset)
        unit_copy(u_next, expert_next).start(priority=DMA_QUEUE_BULK)

    half = RING_SLOTS
    n_units = 2 * RING_SLOTS

    @pl.when(b == 0)
    def _():
        e0 = blk_ref[0]
        for ahead in range(GATHER_AHEAD):
            def body(r, carry, ahead=ahead):
                _row_copy(xn_hbm, rtok_ref[ahead * tb + r], buf.at[ahead], r,
                          sem.at[ahead]).start(priority=DMA_QUEUE_ROWS)
                return carry

            lax.fori_loop(0, tb, body, 0, unroll=8)
        for u in range(half):
            unit_copy(u, e0).start(priority=DMA_QUEUE_BULK)
        for u in range(half):
            advance(u, e0, cur, u + half, e0)
        for u in range(half, n_units):
            advance(u, e0, cur, u - half, n1_ref[0])

    @pl.when(jnp.logical_and(jnp.logical_and(b > 0, b < nact), single_ref[b] == 1))
    def _():
        for u in range(half, n_units):
            advance(u, blk_ref[b], cur, u - half, n1_ref[b])

    def compute(stage):
        _rows_wait(xn_hbm, buf.at[slot], sem.at[slot])
        xb[...] = buf[slot].astype(BF)
        ahead_slot = (b + GATHER_AHEAD) % nslots
        nxt = n1_ref[b]

        def convert_one(j):
            if stage == 0:
                advance(j, nxt, 1 - cur, j + half, nxt)
            elif stage == 1:
                advance(j + half, nxt, 1 - cur, j, n2_ref[b])

        for c in range(up_chunks):
            convert_one(c)
            for r in range(c * rows_per_chunk, (c + 1) * rows_per_chunk):
                _row_copy(xn_hbm, rtok_ref[(b + GATHER_AHEAD) * tb + r], buf.at[ahead_slot], r,
                          sem.at[ahead_slot]).start(priority=DMA_QUEUE_ROWS)
            cs = slice(c * MOE_COL_CHUNK, (c + 1) * MOE_COL_CHUNK)
            gate = jnp.dot(xb[...], wgb[cur, :, cs], preferred_element_type=F32)
            up = jnp.dot(xb[...], wub[cur, :, cs], preferred_element_type=F32)
            hid[:, cs] = (gate * _sigmoid(gate) * up).astype(hid.dtype)
        for c in range(down_chunks):
            if c % (down_chunks // (half - up_chunks)) == 0:
                convert_one(up_chunks + c // (down_chunks // (half - up_chunks)))
            cs = slice(c * MOE_COL_CHUNK, (c + 1) * MOE_COL_CHUNK)
            ys_ref[:, cs] = jnp.dot(hid[...], wdb[cur, :, cs], preferred_element_type=F32)

    for stage in range(3):
        in_stage = kin_ref[b] == stage if stage < 2 else kin_ref[b] >= stage

        @pl.when(jnp.logical_and(b < nact, in_stage))
        def _(stage=stage):
            compute(stage)

    @pl.when(b >= nact)
    def _():
        ys_ref[...] = jnp.zeros_like(ys_ref)

    @pl.when(jnp.logical_and(b >= nact, b < nact + GATHER_AHEAD))
    def _():
        _rows_wait(xn_hbm, buf.at[slot], sem.at[slot])

    @pl.when(b == nact)
    def _():
        for u in range(half):
            unit_copy(u, blk_ref[b]).wait()


def _combine_kernel(dest_ref, ys_hbm, h_ref, w_ref, gain_ref, o_ref, buf, sem, *, tc, n, final_norm):
    i = pl.program_id(0)

    def issue(blk, slot):
        base = blk * tc

        def body(r, carry):
            for k in range(TOP_K):
                _row_copy(ys_hbm, dest_ref[k * n + base + r], buf.at[slot], k * tc + r,
                          sem.at[slot]).start(priority=k)
            return carry

        lax.fori_loop(0, tc, body, 0, unroll=8)

    @pl.when(i == 0)
    def _():
        issue(0, 0)

    @pl.when(i + 1 < pl.num_programs(0))
    def _():
        issue(i + 1, (i + 1) % 2)

    slot = i % 2
    _rows_wait(ys_hbm, buf.at[slot], sem.at[slot])
    out = h_ref[...] + (w_ref[:, 0:1] * buf[slot, 0:tc, :] + w_ref[:, 1:2] * buf[slot, tc:TOP_K * tc, :])
    if final_norm:
        out = _rms(out, gain_ref[...])
    o_ref[...] = out


def hier_moe(h, ffn_gain, wg_r, bg_r, we_r, be_r, w_gate, w_up, w_down, layer, final_gain=None):
    n, d = h.shape
    f = w_gate.shape[-1]
    a = n * TOP_K
    assert (d, f) == (2 * UNIT_ROWS, 2 * UNIT_COLS), "weight unit tiling is laid out for 2x2 / 1x4 units per matrix"
    tb = _moe_block_rows(a)
    nb = a // tb + N_EXPERTS + GATHER_AHEAD

    pad = LANES - N_GROUPS - N_EXPERTS
    w_router = jnp.concatenate([wg_r, we_r, jnp.zeros((d, pad), F32)], axis=1)
    b_router = jnp.concatenate([bg_r, be_r, jnp.zeros((pad,), F32)]).reshape(1, LANES)
    xn, ids, wts, cnt = route(h, ffn_gain, w_router, b_router)

    counts = cnt[0, N_GROUPS:N_GROUPS + N_EXPERTS].astype(jnp.int32)
    nblk = (counts + tb - 1) // tb
    bend = jnp.cumsum(nblk)
    nact = bend[-1]
    start_rows = jnp.pad(((bend - nblk) * tb).astype(F32), (0, LANES - N_EXPERTS)).reshape(1, LANES)
    dest = assignment_rows(ids, start_rows).reshape(a)
    blk = jnp.minimum(jnp.arange(nb, dtype=jnp.int32), jnp.maximum(nact - 1, 0))
    owner = lambda bi: jnp.minimum(jnp.sum((bend[None, :] <= bi[:, None]).astype(jnp.int32), axis=1), N_EXPERTS - 1)
    blk_e = owner(blk)
    k_in_e = blk - (bend - nblk)[blk_e]
    following = lambda e: jnp.where(bend[e] < nact, owner(bend[e]), e)
    nxt1 = following(blk_e)
    nxt2 = following(nxt1)
    prev_e = jnp.concatenate([blk_e[:1], blk_e[:-1]])
    prev_single = ((k_in_e == 0) & (nblk[prev_e] == 1)).astype(jnp.int32)
    wset = (jnp.cumsum((k_in_e == 0).astype(jnp.int32)) - 1) % 2
    tok = jnp.tile(jnp.arange(n, dtype=jnp.int32), TOP_K)
    row_tok = (jnp.arange(nb * tb, dtype=jnp.int32) % n).at[dest].set(tok)
    nact1 = nact.reshape(1).astype(jnp.int32)

    any_space = pl.BlockSpec(memory_space=pl.ANY)
    ys = pl.pallas_call(
        functools.partial(_moe_expert_kernel, tb=tb, layer=layer),
        grid_spec=pltpu.PrefetchScalarGridSpec(
            num_scalar_prefetch=8, grid=(nb,),
            in_specs=[any_space, any_space, any_space, any_space],
            out_specs=pl.BlockSpec((tb, d), lambda b, *_: (b, 0)),
            scratch_shapes=[pltpu.VMEM((GATHER_AHEAD + 1, tb, d), F32),
                            pltpu.VMEM((RING_SLOTS, UNIT_ROWS, UNIT_COLS), F32),
                            pltpu.VMEM((2, d, f), BF), pltpu.VMEM((2, d, f), BF), pltpu.VMEM((2, f, d), BF),
                            pltpu.VMEM((tb, d), BF), pltpu.VMEM((tb, f), BF),
                            pltpu.SemaphoreType.DMA((GATHER_AHEAD + 1,)), pltpu.SemaphoreType.DMA((RING_SLOTS,))]),
        out_shape=jax.ShapeDtypeStruct((nb * tb, d), F32),
        compiler_params=_cparams(("arbitrary",), 58), name="moe_experts",
    )(blk_e, k_in_e, prev_single, wset, nxt1, nxt2, nact1, row_tok, xn, w_gate, w_up, w_down)

    tc = min(COMBINE_ROWS, n)
    gain = (final_gain if final_gain is not None else ffn_gain).reshape(1, d)
    return pl.pallas_call(
        functools.partial(_combine_kernel, tc=tc, n=n, final_norm=final_gain is not None),
        grid_spec=pltpu.PrefetchScalarGridSpec(
            num_scalar_prefetch=1, grid=(n // tc,),
            in_specs=[pl.BlockSpec(memory_space=pl.ANY), pl.BlockSpec((tc, d), lambda i, *_: (i, 0)),
                      pl.BlockSpec((tc, LANES), lambda i, *_: (i, 0)), pl.BlockSpec((1, d), lambda i, *_: (0, 0))],
            out_specs=pl.BlockSpec((tc, d), lambda i, *_: (i, 0)),
            scratch_shapes=[pltpu.VMEM((2, TOP_K * tc, d), F32), pltpu.SemaphoreType.DMA((2,))]),
        out_shape=jax.ShapeDtypeStruct((n, d), F32),
        compiler_params=_cparams(("arbitrary",), 40), name="moe_combine",
    )(dest, ys, h, wts, gain)


def even_mixer(h, gain, w_in, b_gates, pool_w, pool_scale, head_gain, w_out, j):
    d = h.shape[1]
    pool_width = N_POOL_GROUPS * pool_w.shape[-1]
    mlstm_width = head_gain.shape[0]
    main_cols = pool_width + 4 * mlstm_width
    n_gates = 2 * MLSTM_HEADS
    gate_b = jnp.pad(b_gates, (0, LANES - n_gates)).reshape(1, LANES)
    z, gates = norm_matmul_gates_t(h, gain, jnp.swapaxes(w_in, 1, 2), main_cols, n_gates, gate_b, j)
    y_p = pool_mixer(z, pool_w, pool_scale)
    y_m = mlstm_mixer(z, gates, head_gain, pool_width)
    assert pool_width == mlstm_width and pool_width + mlstm_width == d
    return matmul_residual([y_p, y_m], w_out, h, layer=j)


def odd_mixer(h, gain, w_in, conv_w, w_out, j):
    z = norm_matmul(h, gain, w_in, w_in.shape[2], layer=j)
    return conv_matmul_residual(z, conv_w, w_out, h, j)


def cross_attn(h, mem, gain, mem_gain, wq, wk, wv, wo, layer):
    d = h.shape[1]
    k = norm_matmul(mem, mem_gain, wk, d, layer=layer)
    v = norm_matmul(mem, mem_gain, wv, d, layer=layer)
    return matmul_residual([q_attention(h, gain, wq, k, v, layer)], wo, h, layer=layer)


def kernel(x, mem, mix_norm, xattn_norm, mem_norm, ffn_norm, final_norm, ev_w_in, ev_b_gates, ev_pool_w, ev_pool_scale, ev_head_norm, ev_w_out, od_w_in, od_conv_w, od_w_out, xa_wq, xa_wk, xa_wv, xa_wo, rt_group_w, rt_group_b, rt_expert_w, rt_expert_b, ex_w_gate, ex_w_up, ex_w_down):
    depth = mix_norm.shape[0]
    h = x[0]
    m = mem[0]
    for layer in range(depth):
        j = layer // 2
        if layer % 2 == 0:
            h = even_mixer(h, mix_norm[layer], ev_w_in, ev_b_gates[j], ev_pool_w[j], ev_pool_scale[j],
                           ev_head_norm[j], ev_w_out, j)
        else:
            h = odd_mixer(h, mix_norm[layer], od_w_in, od_conv_w[j], od_w_out, j)
        h = cross_attn(h, m, xattn_norm[layer], mem_norm[layer], xa_wq, xa_wk, xa_wv, xa_wo, layer)
        h = hier_moe(h, ffn_norm[layer], rt_group_w[layer], rt_group_b[layer], rt_expert_w[layer],
                     rt_expert_b[layer], ex_w_gate, ex_w_up, ex_w_down, layer,
                     final_gain=final_norm if layer == depth - 1 else None)
    return h[None]
```

```python
import functools

import jax
import jax.numpy as jnp
from jax import lax
from jax.experimental import pallas as pl
from jax.experimental.pallas import tpu as pltpu

F32 = jnp.float32
BF = jnp.bfloat16
EPS = 1e-6

POOL_WINDOWS = (2, 4, 8, 16)
N_POOL_GROUPS = 4
MLSTM_HEADS = 4
FORGET_LANE0 = MLSTM_HEADS
XATTN_HEADS = 4
N_GROUPS = 4
EXPERTS_PER_GROUP = 8
N_EXPERTS = N_GROUPS * EXPERTS_PER_GROUP
TOP_K = 2
CONV_WIDTH = 3

LANES = 128
SUBLANES = 8
DMA_QUEUE_ROWS = 0
DMA_QUEUE_BULK = 1

ROW_TILE = 1024
COL_TILE = 1024
OUT_ROW_TILE = 512
CAST_ROWS = 64
CAST_UNROLL = 4
MOE_COL_CHUNK = 256
WEIGHT_DMA_PARTS = 4
MLSTM_CHUNK = 256
POOL_ROWS = 256
POOL_HALO = 128
CONV_ROWS = 256
CONV_COLS = 512
CONV_HALO = 16
ROUTER_ROWS = 512
GATHER_AHEAD = 2
COMBINE_ROWS = 256


def _moe_block_rows(assignments):
    mean_rows = assignments // N_EXPERTS
    return -(-(mean_rows * 9 // 16) // 16) * 16


def _cparams(semantics, vmem_mib):
    return pltpu.CompilerParams(dimension_semantics=semantics, vmem_limit_bytes=vmem_mib * 1024 * 1024)


def _sigmoid(x):
    return 1.0 / (1.0 + jnp.exp(-x))


def _log_sigmoid(x):
    return jnp.minimum(x, 0.0) - jnp.log(1.0 + jnp.exp(-jnp.abs(x)))


def _rms(x, g):
    ms = jnp.mean(x * x, axis=-1, keepdims=True)
    return x * lax.rsqrt(ms + EPS) * g


def _norm_mm_kernel(x_ref, g_ref, w_ref, o_ref, xn_ref):
    @pl.when(pl.program_id(1) == 0)
    def _():
        xn_ref[...] = _rms(x_ref[...], g_ref[...]).astype(BF)

    o_ref[...] = jnp.dot(xn_ref[...], w_ref[...].astype(BF), preferred_element_type=F32).astype(o_ref.dtype)


_NT = (((1,), (1,)), ((), ()))


def _norm_mm_gates_t_kernel(x_ref, g_ref, wt_ref, wgt_ref, bg_ref, o_ref, gates_ref, xn_ref):
    @pl.when(pl.program_id(1) == 0)
    def _():
        xn = _rms(x_ref[...], g_ref[...]).astype(BF)
        xn_ref[...] = xn
        wg = wgt_ref[...]
        wg = jnp.concatenate([wg, jnp.zeros((LANES - wg.shape[0], wg.shape[1]), F32)], axis=0).astype(BF)
        gates_ref[...] = lax.dot_general(xn, wg, _NT, preferred_element_type=F32) + bg_ref[...]

    o_ref[...] = lax.dot_general(xn_ref[...], wt_ref[...].astype(BF), _NT,
                                 preferred_element_type=F32).astype(o_ref.dtype)


def norm_matmul_gates_t(x, gain, wt, n_cols, n_gates, gate_b, layer):
    n, k = x.shape
    tm = min(ROW_TILE, n)
    tn = COL_TILE
    return pl.pallas_call(
        _norm_mm_gates_t_kernel, grid=(n // tm, n_cols // tn),
        in_specs=[pl.BlockSpec((tm, k), lambda i, j: (i, 0)), pl.BlockSpec((1, k), lambda i, j: (0, 0)),
                  pl.BlockSpec((None, tn, k), lambda i, j: (layer, j, 0)),
                  pl.BlockSpec((None, n_gates, k), lambda i, j: (layer, n_cols // n_gates, 0)),
                  pl.BlockSpec((1, LANES), lambda i, j: (0, 0))],
        out_specs=[pl.BlockSpec((tm, tn), lambda i, j: (i, j)), pl.BlockSpec((tm, LANES), lambda i, j: (i, 0))],
        out_shape=[jax.ShapeDtypeStruct((n, n_cols), BF), jax.ShapeDtypeStruct((n, LANES), F32)],
        scratch_shapes=[pltpu.VMEM((tm, k), BF)],
        compiler_params=_cparams(("parallel", "arbitrary"), 56), name="norm_matmul_gates",
    )(x, gain.reshape(1, k), wt, wt, gate_b)


def _stacked(w, layer):
    return (w[None], 0) if layer is None else (w, layer)


def norm_matmul(x, gain, w, n_cols, layer=None):
    n, k = x.shape
    w, li = _stacked(w, layer)
    tm = min(ROW_TILE, n)
    tn = COL_TILE
    return pl.pallas_call(
        _norm_mm_kernel, grid=(n // tm, n_cols // tn),
        in_specs=[pl.BlockSpec((tm, k), lambda i, j: (i, 0)), pl.BlockSpec((1, k), lambda i, j: (0, 0)),
                  pl.BlockSpec((None, k, tn), lambda i, j: (li, 0, j))],
        out_specs=pl.BlockSpec((tm, tn), lambda i, j: (i, j)),
        out_shape=jax.ShapeDtypeStruct((n, n_cols), BF), scratch_shapes=[pltpu.VMEM((tm, k), BF)],
        compiler_params=_cparams(("parallel", "arbitrary"), 56), name="norm_matmul",
    )(x, gain.reshape(1, k), w)


def _fill_bf16(dst_ref, src_ref):
    def body(i, carry):
        r = pl.multiple_of(i * CAST_ROWS, CAST_ROWS)
        dst_ref[pl.ds(r, CAST_ROWS), :] = src_ref[pl.ds(r, CAST_ROWS), :].astype(BF)
        return carry

    lax.fori_loop(0, src_ref.shape[0] // CAST_ROWS, body, 0, unroll=CAST_UNROLL)


def _q_attention_kernel(x_ref, g_ref, w_ref, k_ref, v_ref, o_ref, wb_ref, q_ref, *, hd):
    @pl.when(pl.program_id(0) == 0)
    def _():
        _fill_bf16(wb_ref, w_ref)

    xn = _rms(x_ref[...], g_ref[...]).astype(BF)
    q_ref[...] = jnp.dot(xn, wb_ref[...], preferred_element_type=F32).astype(BF)
    scale = hd ** -0.5
    for h in range(XATTN_HEADS):
        hs = slice(h * hd, (h + 1) * hd)
        s = lax.dot_general(q_ref[:, hs], k_ref[:, hs], _NT, preferred_element_type=F32) * scale
        e = jnp.exp(s - jnp.max(s, axis=-1, keepdims=True))
        p = (e / jnp.sum(e, axis=-1, keepdims=True)).astype(BF)
        o_ref[:, hs] = jnp.dot(p, v_ref[:, hs], preferred_element_type=F32).astype(o_ref.dtype)


def q_attention(x, gain, wq, k, v, layer):
    n, kd = x.shape
    d = wq.shape[2]
    m = k.shape[0]
    tm = min(OUT_ROW_TILE, n)
    return pl.pallas_call(
        functools.partial(_q_attention_kernel, hd=d // XATTN_HEADS), grid=(n // tm,),
        in_specs=[pl.BlockSpec((tm, kd), lambda i: (i, 0)), pl.BlockSpec((1, kd), lambda i: (0, 0)),
                  pl.BlockSpec((None, kd, d), lambda i: (layer, 0, 0), pipeline_mode=pl.Buffered(1)),
                  pl.BlockSpec((m, d), lambda i: (0, 0)), pl.BlockSpec((m, d), lambda i: (0, 0))],
        out_specs=pl.BlockSpec((tm, d), lambda i: (i, 0)), out_shape=jax.ShapeDtypeStruct((n, d), BF),
        scratch_shapes=[pltpu.VMEM((kd, d), BF), pltpu.VMEM((tm, d), BF)],
        compiler_params=_cparams(("arbitrary",), 52), name="q_attention",
    )(x, gain.reshape(1, kd), wq, k, v)


def _mm_res_kernel(*refs, nparts):
    xs = refs[:nparts]
    w_ref, res_ref, o_ref, wb_ref = refs[nparts:]

    @pl.when(pl.program_id(0) == 0)
    def _():
        _fill_bf16(wb_ref, w_ref)

    acc = res_ref[...]
    k0 = 0
    for x_ref in xs:
        kp = x_ref.shape[1]
        acc = acc + jnp.dot(x_ref[...], wb_ref[k0:k0 + kp, :], preferred_element_type=F32)
        k0 += kp
    o_ref[...] = acc


def matmul_residual(xs, w, res, layer=None):
    n, d = res.shape
    w, li = _stacked(w, layer)
    k = w.shape[1]
    tm = min(OUT_ROW_TILE, n)
    in_specs = [pl.BlockSpec((tm, x.shape[1]), lambda i: (i, 0)) for x in xs]
    in_specs += [pl.BlockSpec((None, k, d), lambda i: (li, 0, 0), pipeline_mode=pl.Buffered(1)),
                 pl.BlockSpec((tm, d), lambda i: (i, 0))]
    return pl.pallas_call(
        functools.partial(_mm_res_kernel, nparts=len(xs)), grid=(n // tm,), in_specs=in_specs,
        out_specs=pl.BlockSpec((tm, d), lambda i: (i, 0)), out_shape=jax.ShapeDtypeStruct((n, d), F32),
        scratch_shapes=[pltpu.VMEM((k, d), BF)],
        compiler_params=_cparams(("arbitrary",), 52), name="matmul_residual",
    )(*xs, w, res)


def _pool_kernel(cur_ref, prev_ref, w_ref, sc_ref, o_ref, *, tp, gdim):
    i = pl.program_id(0)
    dist = lax.broadcasted_iota(jnp.int32, (tp, tp), 0) - lax.broadcasted_iota(jnp.int32, (tp, tp), 1)
    distp = (lax.broadcasted_iota(jnp.int32, (tp, POOL_HALO), 0) + POOL_HALO
             - lax.broadcasted_iota(jnp.int32, (tp, POOL_HALO), 1))
    pos = i * tp + lax.broadcasted_iota(jnp.int32, (tp, 1), 0)
    for j, win in enumerate(POOL_WINDOWS):
        gs = slice(j * gdim, (j + 1) * gdim)
        cur = cur_ref[:, gs]
        band = jnp.where(dist >= 0, jnp.where(dist < win, 1.0, 0.0), 0.0).astype(BF)
        bandp = jnp.where(distp < jnp.where(i > 0, win, 0), 1.0, 0.0).astype(BF)
        s = (jnp.dot(band, cur, preferred_element_type=F32)
             + jnp.dot(bandp, prev_ref[:, gs], preferred_element_type=F32))
        cnt = jnp.minimum(pos + 1, win).astype(F32)
        d = s / cnt - cur.astype(F32)
        y = jnp.dot(d.astype(BF), w_ref[j].astype(BF), preferred_element_type=F32) * sc_ref[:, gs]
        o_ref[:, gs] = y.astype(o_ref.dtype)


def pool_mixer(z, pool_w, pool_scale):
    n = z.shape[0]
    gdim = pool_w.shape[-1]
    width = N_POOL_GROUPS * gdim
    tp = min(POOL_ROWS, n)
    halo_blocks = tp // POOL_HALO
    return pl.pallas_call(
        functools.partial(_pool_kernel, tp=tp, gdim=gdim), grid=(n // tp,),
        in_specs=[
            pl.BlockSpec((tp, width), lambda i: (i, 0)),
            pl.BlockSpec((POOL_HALO, width), lambda i: (jnp.maximum(i * halo_blocks - 1, 0), 0)),
            pl.BlockSpec((N_POOL_GROUPS, gdim, gdim), lambda i: (0, 0, 0)),
            pl.BlockSpec((1, width), lambda i: (0, 0)),
        ],
        out_specs=pl.BlockSpec((tp, width), lambda i: (i, 0)),
        out_shape=jax.ShapeDtypeStruct((n, width), BF),
        compiler_params=_cparams(("parallel",), 32), name="pool_mixer",
    )(z, z, pool_w, pool_scale.reshape(1, width))


def _mlstm_kernel(q_ref, k_ref, v_ref, o_ref, g_ref, gain_ref, y_ref, ct_ref, m_ref, *, chunk, dh):
    c = pl.program_id(0)

    @pl.when(c == 0)
    def _():
        ct_ref[...] = jnp.zeros_like(ct_ref)
        m_ref[...] = jnp.zeros_like(m_ref)

    g = g_ref[...]
    lf = _log_sigmoid(g)
    row = lax.broadcasted_iota(jnp.int32, (chunk, chunk), 0)
    col = lax.broadcasted_iota(jnp.int32, (chunk, chunk), 1)
    causal = col <= row
    ltri = jnp.where(causal, 1.0, 0.0).astype(BF)
    hi = lf.astype(BF)
    r1 = lf - hi.astype(F32)
    mid = r1.astype(BF)
    lo = (r1 - mid.astype(F32)).astype(BF)
    bcum = (jnp.dot(ltri, hi, preferred_element_type=F32) + jnp.dot(ltri, mid, preferred_element_type=F32)
            + jnp.dot(ltri, lo, preferred_element_type=F32))
    g_t = g.T
    b_t = bcum.T
    ones_col = jnp.where(lax.broadcasted_iota(jnp.int32, (chunk, LANES), 1) == 0, 1.0, 0.0).astype(BF)

    for h in range(MLSTM_HEADS):
        hs = slice(h * dh, (h + 1) * dh)
        fl = FORGET_LANE0 + h
        bc = bcum[:, fl:fl + 1]
        br = b_t[fl:fl + 1, :]
        ir = g_t[h:h + 1, :]
        b_last = bcum[chunk - 1:chunk, fl:fl + 1]
        m_prev = m_ref[h][:, 0:1]

        dmat = jnp.where(causal, bc + (ir - br), -jnp.inf)
        inter = bc + m_prev
        m_t = jnp.maximum(jnp.max(dmat, axis=1, keepdims=True), inter)
        w_inter = jnp.exp(inter - m_t)
        p = jnp.exp(dmat - m_t)

        qh = q_ref[:, hs] * (dh ** -0.5)
        kh = k_ref[:, hs]
        v_aug = jnp.concatenate([v_ref[:, hs], ones_col], axis=1)
        s = lax.dot_general(qh, kh, (((1,), (1,)), ((), ())), preferred_element_type=F32)
        sc = (s * p).astype(BF)
        ct = ct_ref[h]
        num_aug = (w_inter * jnp.dot(qh, ct.astype(BF), preferred_element_type=F32)
                   + jnp.dot(sc, v_aug, preferred_element_type=F32))
        num = num_aug[:, :dh]
        den = num_aug[:, dh:dh + 1]
        hout = num / jnp.maximum(jnp.abs(den), jnp.exp(-m_t))

        yn = _rms(hout, gain_ref[:, hs])
        y_ref[:, hs] = (_sigmoid(o_ref[:, hs].astype(F32)) * yn).astype(y_ref.dtype)

        d_end = b_last - br + ir
        m_new = jnp.maximum(b_last + m_prev, jnp.max(d_end, axis=1, keepdims=True))
        a_prev = jnp.exp(b_last + m_prev - m_new)
        a_s = jnp.exp(d_end - m_new)
        k_t = (kh.astype(F32).T * a_s).astype(BF)
        ct_ref[h] = a_prev * ct + jnp.dot(k_t, v_aug, preferred_element_type=F32)
        m_ref[h] = jnp.broadcast_to(m_new, (1, LANES))


def mlstm_mixer(z, gates, head_gain, col0):
    n = z.shape[0]
    width = head_gain.shape[0]
    dh = width // MLSTM_HEADS
    chunk = min(MLSTM_CHUNK, n)
    base = col0 // width
    qkvo = [pl.BlockSpec((chunk, width), lambda c, p=p: (c, base + p)) for p in range(4)]
    return pl.pallas_call(
        functools.partial(_mlstm_kernel, chunk=chunk, dh=dh), grid=(n // chunk,),
        in_specs=qkvo + [pl.BlockSpec((chunk, LANES), lambda c: (c, 0)), pl.BlockSpec((1, width), lambda c: (0, 0))],
        out_specs=pl.BlockSpec((chunk, width), lambda c: (c, 0)),
        out_shape=jax.ShapeDtypeStruct((n, width), BF),
        scratch_shapes=[pltpu.VMEM((MLSTM_HEADS, dh, dh + LANES), F32), pltpu.VMEM((MLSTM_HEADS, 1, LANES), F32)],
        compiler_params=_cparams(("arbitrary",), 32), name="mlstm_mixer",
    )(z, z, z, z, gates, head_gain.reshape(1, width))


def _gated_conv(b, c, u, cp, up, w, first_block):
    zc = c.astype(F32) * u.astype(F32)
    zp = jnp.where(first_block, 0.0, cp.astype(F32) * up.astype(F32))
    row = lax.broadcasted_iota(jnp.int32, zc.shape, 0)
    acc = w[CONV_WIDTH - 1:CONV_WIDTH, :] * zc
    for back in range(1, CONV_WIDTH):
        shifted = pltpu.roll(zc, back, 0)
        for r in range(back):
            shifted = jnp.where(row == r, zp[CONV_HALO - back + r:CONV_HALO - back + r + 1, :], shifted)
        acc = acc + w[CONV_WIDTH - 1 - back:CONV_WIDTH - back, :] * shifted
    return b.astype(F32) * acc


def _conv_mm_res_kernel(b_ref, c_ref, u_ref, cp_ref, up_ref, cw_ref, w_ref, res_ref, o_ref, wb_ref, xs_ref):
    i = pl.program_id(0)

    @pl.when(i == 0)
    def _():
        _fill_bf16(wb_ref, w_ref)

    for c0 in range(0, xs_ref.shape[1], CONV_COLS):
        cs = slice(c0, c0 + CONV_COLS)
        xs_ref[:, cs] = _gated_conv(b_ref[:, cs], c_ref[:, cs], u_ref[:, cs], cp_ref[:, cs], up_ref[:, cs],
                                    cw_ref[:, cs], i == 0).astype(BF)
    o_ref[...] = res_ref[...] + jnp.dot(xs_ref[...], wb_ref[...], preferred_element_type=F32)


def conv_matmul_residual(z, conv_w, w, res, layer):
    n, d = res.shape
    tm = min(CONV_ROWS, n)
    halo_blocks = tm // CONV_HALO
    cur = lambda part: pl.BlockSpec((tm, d), lambda i: (i, part))
    prev = lambda part: pl.BlockSpec((CONV_HALO, d), lambda i: (jnp.maximum(i * halo_blocks - 1, 0), part))
    return pl.pallas_call(
        _conv_mm_res_kernel, grid=(n // tm,),
        in_specs=[cur(0), cur(1), cur(2), prev(1), prev(2), pl.BlockSpec((CONV_WIDTH, d), lambda i: (0, 0)),
                  pl.BlockSpec((None, d, d), lambda i: (layer, 0, 0), pipeline_mode=pl.Buffered(1)),
                  pl.BlockSpec((tm, d), lambda i: (i, 0))],
        out_specs=pl.BlockSpec((tm, d), lambda i: (i, 0)), out_shape=jax.ShapeDtypeStruct((n, d), F32),
        scratch_shapes=[pltpu.VMEM((d, d), BF), pltpu.VMEM((tm, d), BF)],
        compiler_params=_cparams(("arbitrary",), 52), name="conv_matmul_residual",
    )(z, z, z, z, z, conv_w, w, res)


def _router_kernel(x_ref, g_ref, wr_ref, br_ref, xn_ref, ids_ref, wts_ref, cnt_ref, carry_ref, whi_ref, wlo_ref,
                   *, tr):
    @pl.when(pl.program_id(0) == 0)
    def _():
        carry_ref[...] = jnp.zeros_like(carry_ref)

    @pl.when(pl.program_id(0) == 0)
    def _():
        w = wr_ref[...]
        w_hi = w.astype(BF)
        whi_ref[...] = w_hi
        wlo_ref[...] = (w - w_hi.astype(F32)).astype(BF)

    xn = _rms(x_ref[...], g_ref[...])
    xn_ref[...] = xn
    x_hi = xn.astype(BF)
    x_lo = (xn - x_hi.astype(F32)).astype(BF)
    logits = (jnp.dot(x_hi, whi_ref[...], preferred_element_type=F32)
              + jnp.dot(x_lo, whi_ref[...], preferred_element_type=F32)
              + jnp.dot(x_hi, wlo_ref[...], preferred_element_type=F32)) + br_ref[...]
    lane = lax.broadcasted_iota(jnp.int32, (tr, LANES), 1).astype(F32)
    neg = -jnp.inf

    def first_argmax(vals):
        top = jnp.max(vals, axis=-1, keepdims=True)
        return top, jnp.min(jnp.where(vals == top, lane, float(LANES)), axis=-1, keepdims=True)

    gl = jnp.where(lane < N_GROUPS, logits, neg)
    gmax, grp = first_argmax(gl)
    g_prob = 1.0 / jnp.sum(jnp.exp(gl - gmax), axis=-1, keepdims=True)
    lo = N_GROUPS + EXPERTS_PER_GROUP * grp
    el = jnp.where(lane >= lo, jnp.where(lane < lo + EXPERTS_PER_GROUP, logits, neg), neg)
    v1, l1 = first_argmax(el)
    v2, l2 = first_argmax(jnp.where(lane == l1, neg, el))
    e2 = jnp.exp(v2 - v1)
    w1 = g_prob / (1.0 + e2)
    w2 = g_prob * e2 / (1.0 + e2)
    hot1 = lane == l1
    hot2 = lane == l2
    hot = jnp.where(hot1, 1.0, jnp.where(hot2, 1.0, 0.0))
    earlier = (lax.broadcasted_iota(jnp.int32, (tr, tr), 1) < lax.broadcasted_iota(jnp.int32, (tr, tr), 0))
    before = jnp.dot(jnp.where(earlier, 1.0, 0.0).astype(BF), hot.astype(BF), preferred_element_type=F32)
    before = before + carry_ref[0:1, :]
    r1 = jnp.sum(jnp.where(hot1, before, 0.0), axis=-1, keepdims=True)
    r2 = jnp.sum(jnp.where(hot2, before, 0.0), axis=-1, keepdims=True)
    carry_ref[0:1, :] = carry_ref[0:1, :] + jnp.sum(hot, axis=0, keepdims=True)
    ids = jnp.where(lane == 0, l1 - N_GROUPS, jnp.where(lane == 1, l2 - N_GROUPS,
                    jnp.where(lane == 2, r1, jnp.where(lane == 3, r2, 0.0))))
    ids_ref[...] = ids.astype(jnp.int32)
    wts_ref[...] = jnp.where(lane == 0, w1, jnp.where(lane == 1, w2, 0.0))
    cnt_ref[...] = carry_ref[...]


def route(h, gain, w_router, b_router):
    n, d = h.shape
    tr = min(ROUTER_ROWS, n)
    return pl.pallas_call(
        functools.partial(_router_kernel, tr=tr), grid=(n // tr,),
        in_specs=[pl.BlockSpec((tr, d), lambda i: (i, 0)), pl.BlockSpec((1, d), lambda i: (0, 0)),
                  pl.BlockSpec((d, LANES), lambda i: (0, 0)), pl.BlockSpec((1, LANES), lambda i: (0, 0))],
        out_specs=[pl.BlockSpec((tr, d), lambda i: (i, 0)),
                   pl.BlockSpec((tr, LANES), lambda i: (i, 0)), pl.BlockSpec((tr, LANES), lambda i: (i, 0)),
                   pl.BlockSpec((SUBLANES, LANES), lambda i: (0, 0))],
        out_shape=[jax.ShapeDtypeStruct((n, d), F32), jax.ShapeDtypeStruct((n, LANES), jnp.int32),
                   jax.ShapeDtypeStruct((n, LANES), F32), jax.ShapeDtypeStruct((SUBLANES, LANES), F32)],
        scratch_shapes=[pltpu.VMEM((SUBLANES, LANES), F32), pltpu.VMEM((d, LANES), BF), pltpu.VMEM((d, LANES), BF)],
        compiler_params=_cparams(("arbitrary",), 40), name="moe_router",
    )(h, gain.reshape(1, d), w_router, b_router)


def _dest_kernel(ids_ref, start_ref, o_ref, *, tr):
    ids = ids_ref[...].astype(F32)
    lane = lax.broadcasted_iota(jnp.int32, (tr, LANES), 1).astype(F32)
    start = start_ref[...]
    rows = []
    for k in range(TOP_K):
        first = jnp.sum(jnp.where(lane == ids[:, k:k + 1], start, 0.0), axis=-1, keepdims=True)
        rows.append(first + ids[:, TOP_K + k:TOP_K + k + 1])
    packed = jnp.where(lane == 0, rows[0], jnp.where(lane == 1, rows[1], 0.0))
    o_ref[...] = packed.T[0:SUBLANES, :].astype(jnp.int32)


def assignment_rows(ids, start_rows):
    n = ids.shape[0]
    tr = min(ROUTER_ROWS, n)
    out = pl.pallas_call(
        functools.partial(_dest_kernel, tr=tr), grid=(n // tr,),
        in_specs=[pl.BlockSpec((tr, LANES), lambda i: (i, 0)), pl.BlockSpec((1, LANES), lambda i: (0, 0))],
        out_specs=pl.BlockSpec((SUBLANES, tr), lambda i: (0, i)),
        out_shape=jax.ShapeDtypeStruct((SUBLANES, n), jnp.int32),
        compiler_params=_cparams(("parallel",), 32), name="moe_assignment_rows",
    )(ids, start_rows)
    return out[:TOP_K]


def _row_copy(src_hbm, row, dst, dst_row, sem):
    return pltpu.make_async_copy(src_hbm.at[pl.ds(row, 1), :], dst.at[pl.ds(dst_row, 1), :], sem)


def _rows_wait(src_hbm, dst, sem):
    pltpu.make_async_copy(src_hbm.at[pl.ds(0, dst.shape[0]), :], dst, sem).wait()


def _expert_weight_copies(w_hbm, layer, expert, stage, wset, sem):
    rows = stage.shape[1] // WEIGHT_DMA_PARTS
    return [pltpu.make_async_copy(w_hbm.at[layer, expert, pl.ds(p * rows, rows), :],
                                  stage.at[wset, pl.ds(p * rows, rows), :], sem.at[wset])
            for p in range(WEIGHT_DMA_PARTS)]


def _stream_expert_weights(b, nact, blk_ref, kin_ref, set_ref, nxt_ref, copies):
    cur = set_ref[b]

    @pl.when(b == 0)
    def _():
        for cp in copies(blk_ref[0], cur):
            cp.start(priority=DMA_QUEUE_BULK)

    @pl.when(jnp.logical_and(b < nact, kin_ref[b] == 0))
    def _():
        for cp in copies(blk_ref[b], cur):
            cp.wait()

        @pl.when(nxt_ref[b] != blk_ref[b])
        def _():
            for cp in copies(nxt_ref[b], 1 - cur):
                cp.start(priority=DMA_QUEUE_BULK)

    return cur


def _moe_up_kernel(blk_ref, kin_ref, set_ref, nxt_ref, nact_ref, rtok_ref, xn_hbm, wg_hbm, wu_hbm, hid_ref,
                   buf, stage_g, stage_u, xb, sem, wsem, *, tb, layer):
    b = pl.program_id(0)
    nact = nact_ref[0]
    nslots = GATHER_AHEAD + 1
    slot = b % nslots
    f = hid_ref.shape[1]
    up_chunks = f // MOE_COL_CHUNK
    rows_per_chunk = tb // up_chunks

    @pl.when(b == 0)
    def _():
        for ahead in range(GATHER_AHEAD):
            def body(r, carry, ahead=ahead):
                _row_copy(xn_hbm, rtok_ref[ahead * tb + r], buf.at[ahead], r,
                          sem.at[ahead]).start(priority=DMA_QUEUE_ROWS)
                return carry

            lax.fori_loop(0, tb, body, 0, unroll=8)

    def copies(e, wset):
        return (_expert_weight_copies(wg_hbm, layer, e, stage_g, wset, wsem.at[0])
                + _expert_weight_copies(wu_hbm, layer, e, stage_u, wset, wsem.at[1]))

    cur = _stream_expert_weights(b, nact, blk_ref, kin_ref, set_ref, nxt_ref, copies)

    @pl.when(b < nact)
    def _():
        _rows_wait(xn_hbm, buf.at[slot], sem.at[slot])
        xb[...] = buf[slot].astype(BF)
        ahead_slot = (b + GATHER_AHEAD) % nslots
        for c in range(up_chunks):
            for r in range(c * rows_per_chunk, (c + 1) * rows_per_chunk):
                _row_copy(xn_hbm, rtok_ref[(b + GATHER_AHEAD) * tb + r], buf.at[ahead_slot], r,
                          sem.at[ahead_slot]).start(priority=DMA_QUEUE_ROWS)
            cs = slice(c * MOE_COL_CHUNK, (c + 1) * MOE_COL_CHUNK)
            gate = jnp.dot(xb[...], stage_g[cur, :, cs].astype(BF), preferred_element_type=F32)
            up = jnp.dot(xb[...], stage_u[cur, :, cs].astype(BF), preferred_element_type=F32)
            hid_ref[:, cs] = (gate * _sigmoid(gate) * up).astype(hid_ref.dtype)

    @pl.when(b >= nact)
    def _():
        hid_ref[...] = jnp.zeros_like(hid_ref)

    @pl.when(jnp.logical_and(b >= nact, b < nact + GATHER_AHEAD))
    def _():
        _rows_wait(xn_hbm, buf.at[slot], sem.at[slot])


def _moe_down_kernel(blk_ref, kin_ref, set_ref, nxt_ref, nact_ref, hid_ref, wd_hbm, ys_ref, stage_d, wsem, *, layer):
    b = pl.program_id(0)
    nact = nact_ref[0]
    d = ys_ref.shape[1]

    def copies(e, wset):
        return _expert_weight_copies(wd_hbm, layer, e, stage_d, wset, wsem)

    cur = _stream_expert_weights(b, nact, blk_ref, kin_ref, set_ref, nxt_ref, copies)

    @pl.when(b < nact)
    def _():
        for c in range(d // MOE_COL_CHUNK):
            cs = slice(c * MOE_COL_CHUNK, (c + 1) * MOE_COL_CHUNK)
            ys_ref[:, cs] = jnp.dot(hid_ref[...], stage_d[cur, :, cs].astype(BF), preferred_element_type=F32)

    @pl.when(b >= nact)
    def _():
        ys_ref[...] = jnp.zeros_like(ys_ref)


def _combine_kernel(dest_ref, ys_hbm, h_ref, w_ref, gain_ref, o_ref, buf, sem, *, tc, n, final_norm):
    i = pl.program_id(0)

    def issue(blk, slot):
        base = blk * tc

        def body(r, carry):
            for k in range(TOP_K):
                _row_copy(ys_hbm, dest_ref[k * n + base + r], buf.at[slot], k * tc + r,
                          sem.at[slot]).start(priority=k)
            return carry

        lax.fori_loop(0, tc, body, 0, unroll=8)

    @pl.when(i == 0)
    def _():
        issue(0, 0)

    @pl.when(i + 1 < pl.num_programs(0))
    def _():
        issue(i + 1, (i + 1) % 2)

    slot = i % 2
    _rows_wait(ys_hbm, buf.at[slot], sem.at[slot])
    out = h_ref[...] + (w_ref[:, 0:1] * buf[slot, 0:tc, :] + w_ref[:, 1:2] * buf[slot, tc:TOP_K * tc, :])
    if final_norm:
        out = _rms(out, gain_ref[...])
    o_ref[...] = out


def hier_moe(h, ffn_gain, wg_r, bg_r, we_r, be_r, w_gate, w_up, w_down, layer, final_gain=None):
    n, d = h.shape
    f = w_gate.shape[-1]
    a = n * TOP_K
    tb = _moe_block_rows(a)
    nb = a // tb + N_EXPERTS + GATHER_AHEAD

    pad = LANES - N_GROUPS - N_EXPERTS
    w_router = jnp.concatenate([wg_r, we_r, jnp.zeros((d, pad), F32)], axis=1)
    b_router = jnp.concatenate([bg_r, be_r, jnp.zeros((pad,), F32)]).reshape(1, LANES)
    xn, ids, wts, cnt = route(h, ffn_gain, w_router, b_router)

    counts = cnt[0, N_GROUPS:N_GROUPS + N_EXPERTS].astype(jnp.int32)
    nblk = (counts + tb - 1) // tb
    bend = jnp.cumsum(nblk)
    nact = bend[-1]
    start_rows = jnp.pad(((bend - nblk) * tb).astype(F32), (0, LANES - N_EXPERTS)).reshape(1, LANES)
    dest = assignment_rows(ids, start_rows).reshape(a)
    blk = jnp.minimum(jnp.arange(nb, dtype=jnp.int32), jnp.maximum(nact - 1, 0))
    owner = lambda bi: jnp.minimum(jnp.sum((bend[None, :] <= bi[:, None]).astype(jnp.int32), axis=1), N_EXPERTS - 1)
    blk_e = owner(blk)
    k_in_e = blk - (bend - nblk)[blk_e]
    following = lambda e: jnp.where(bend[e] < nact, owner(bend[e]), e)
    nxt_e = following(blk_e)
    wset = (jnp.cumsum((k_in_e == 0).astype(jnp.int32)) - 1) % 2
    tok = jnp.tile(jnp.arange(n, dtype=jnp.int32), TOP_K)
    row_tok = (jnp.arange(nb * tb, dtype=jnp.int32) % n).at[dest].set(tok)
    nact1 = nact.reshape(1).astype(jnp.int32)

    any_space = pl.BlockSpec(memory_space=pl.ANY)
    hid = pl.pallas_call(
        functools.partial(_moe_up_kernel, tb=tb, layer=layer),
        grid_spec=pltpu.PrefetchScalarGridSpec(
            num_scalar_prefetch=6, grid=(nb,),
            in_specs=[any_space, any_space, any_space],
            out_specs=pl.BlockSpec((tb, f), lambda b, *_: (b, 0)),
            scratch_shapes=[pltpu.VMEM((GATHER_AHEAD + 1, tb, d), F32),
                            pltpu.VMEM((2, d, f), F32), pltpu.VMEM((2, d, f), F32), pltpu.VMEM((tb, d), BF),
                            pltpu.SemaphoreType.DMA((GATHER_AHEAD + 1,)), pltpu.SemaphoreType.DMA((2, 2))]),
        out_shape=jax.ShapeDtypeStruct((nb * tb, f), BF),
        compiler_params=_cparams(("arbitrary",), 56), name="moe_up",
    )(blk_e, k_in_e, wset, nxt_e, nact1, row_tok, xn, w_gate, w_up)

    ys = pl.pallas_call(
        functools.partial(_moe_down_kernel, layer=layer),
        grid_spec=pltpu.PrefetchScalarGridSpec(
            num_scalar_prefetch=5, grid=(nb,),
            in_specs=[pl.BlockSpec((tb, f), lambda b, *_: (b, 0)), any_space],
            out_specs=pl.BlockSpec((tb, d), lambda b, *_: (b, 0)),
            scratch_shapes=[pltpu.VMEM((2, f, d), F32), pltpu.SemaphoreType.DMA((2,))]),
        out_shape=jax.ShapeDtypeStruct((nb * tb, d), F32),
        compiler_params=_cparams(("arbitrary",), 40), name="moe_down",
    )(blk_e, k_in_e, wset, nxt_e, nact1, hid, w_down)

    tc = min(COMBINE_ROWS, n)
    gain = (final_gain if final_gain is not None else ffn_gain).reshape(1, d)
    return pl.pallas_call(
        functools.partial(_combine_kernel, tc=tc, n=n, final_norm=final_gain is not None),
        grid_spec=pltpu.PrefetchScalarGridSpec(
            num_scalar_prefetch=1, grid=(n // tc,),
            in_specs=[pl.BlockSpec(memory_space=pl.ANY), pl.BlockSpec((tc, d), lambda i, *_: (i, 0)),
                      pl.BlockSpec((tc, LANES), lambda i, *_: (i, 0)), pl.BlockSpec((1, d), lambda i, *_: (0, 0))],
            out_specs=pl.BlockSpec((tc, d), lambda i, *_: (i, 0)),
            scratch_shapes=[pltpu.VMEM((2, TOP_K * tc, d), F32), pltpu.SemaphoreType.DMA((2,))]),
        out_shape=jax.ShapeDtypeStruct((n, d), F32),
        compiler_params=_cparams(("arbitrary",), 40), name="moe_combine",
    )(dest, ys, h, wts, gain)


def even_mixer(h, gain, w_in, b_gates, pool_w, pool_scale, head_gain, w_out, j):
    d = h.shape[1]
    pool_width = N_POOL_GROUPS * pool_w.shape[-1]
    mlstm_width = head_gain.shape[0]
    main_cols = pool_width + 4 * mlstm_width
    n_gates = 2 * MLSTM_HEADS
    gate_b = jnp.pad(b_gates, (0, LANES - n_gates)).reshape(1, LANES)
    z, gates = norm_matmul_gates_t(h, gain, jnp.swapaxes(w_in, 1, 2), main_cols, n_gates, gate_b, j)
    y_p = pool_mixer(z, pool_w, pool_scale)
    y_m = mlstm_mixer(z, gates, head_gain, pool_width)
    assert pool_width == mlstm_width and pool_width + mlstm_width == d
    return matmul_residual([y_p, y_m], w_out, h, layer=j)


def odd_mixer(h, gain, w_in, conv_w, w_out, j):
    z = norm_matmul(h, gain, w_in, w_in.shape[2], layer=j)
    return conv_matmul_residual(z, conv_w, w_out, h, j)


def cross_attn(h, mem, gain, mem_gain, wq, wk, wv, wo, layer):
    d = h.shape[1]
    k = norm_matmul(mem, mem_gain, wk, d, layer=layer)
    v = norm_matmul(mem, mem_gain, wv, d, layer=layer)
    return matmul_residual([q_attention(h, gain, wq, k, v, layer)], wo, h, layer=layer)


def kernel(x, mem, mix_norm, xattn_norm, mem_norm, ffn_norm, final_norm, ev_w_in, ev_b_gates, ev_pool_w, ev_pool_scale, ev_head_norm, ev_w_out, od_w_in, od_conv_w, od_w_out, xa_wq, xa_wk, xa_wv, xa_wo, rt_group_w, rt_group_b, rt_expert_w, rt_expert_b, ex_w_gate, ex_w_up, ex_w_down):
    depth = mix_norm.shape[0]
    h = x[0]
    m = mem[0]
    for layer in range(depth):
        j = layer // 2
        if layer % 2 == 0:
            h = even_mixer(h, mix_norm[layer], ev_w_in, ev_b_gates[j], ev_pool_w[j], ev_pool_scale[j],
                           ev_head_norm[j], ev_w_out, j)
        else:
            h = odd_mixer(h, mix_norm[layer], od_w_in, od_conv_w[j], od_w_out, j)
        h = cross_attn(h, m, xattn_norm[layer], mem_norm[layer], xa_wq, xa_wk, xa_wv, xa_wo, layer)
        h = hier_moe(h, ffn_norm[layer], rt_group_w[layer], rt_group_b[layer], rt_expert_w[layer],
                     rt_expert_b[layer], ex_w_gate, ex_w_up, ex_w_down, layer,
                     final_gain=final_norm if layer == depth - 1 else None)
    return h[None]
```

```python
import functools

import jax
import jax.numpy as jnp
from jax import lax
from jax.experimental import pallas as pl
from jax.experimental.pallas import tpu as pltpu

F32 = jnp.float32
BF = jnp.bfloat16
EPS = 1e-6

POOL_WINDOWS = (2, 4, 8, 16)
N_POOL_GROUPS = 4
MLSTM_HEADS = 4
FORGET_LANE0 = MLSTM_HEADS
XATTN_HEADS = 4
N_GROUPS = 4
EXPERTS_PER_GROUP = 8
N_EXPERTS = N_GROUPS * EXPERTS_PER_GROUP
TOP_K = 2
CONV_WIDTH = 3

LANES = 128
SUBLANES = 8
DMA_QUEUE_ROWS = 0
DMA_QUEUE_BULK = 1

ROW_TILE = 1024
COL_TILE = 1024
OUT_ROW_TILE = 512
CAST_ROWS = 64
CAST_UNROLL = 4
MOE_COL_CHUNK = 256
WEIGHT_DMA_PARTS = 4
MLSTM_CHUNK = 256
POOL_ROWS = 256
POOL_HALO = 128
CONV_ROWS = 256
CONV_COLS = 512
CONV_HALO = 16
ROUTER_ROWS = 512
GATHER_AHEAD = 2
COMBINE_ROWS = 256


def _moe_block_rows(assignments):
    mean_rows = assignments // N_EXPERTS
    return -(-(mean_rows * 9 // 16) // 16) * 16


def _cparams(semantics, vmem_mib):
    return pltpu.CompilerParams(dimension_semantics=semantics, vmem_limit_bytes=vmem_mib * 1024 * 1024)


def _sigmoid(x):
    return 1.0 / (1.0 + jnp.exp(-x))


def _log_sigmoid(x):
    return jnp.minimum(x, 0.0) - jnp.log(1.0 + jnp.exp(-jnp.abs(x)))


def _rms(x, g):
    ms = jnp.mean(x * x, axis=-1, keepdims=True)
    return x * lax.rsqrt(ms + EPS) * g


def _pack_halves(x):
    half = x.shape[1] // 2
    lo = pltpu.bitcast(x[:, :half].astype(BF).astype(F32), jnp.uint32)
    hi = pltpu.bitcast(x[:, half:].astype(BF).astype(F32), jnp.uint32)
    return hi | lax.shift_right_logical(lo, jnp.uint32(16))


def _unpack_halves(w):
    lo = pltpu.bitcast(lax.shift_left(w, jnp.uint32(16)), F32)
    hi = pltpu.bitcast(w & jnp.uint32(0xFFFF0000), F32)
    return lo, hi


def _norm_mm_kernel(x_ref, g_ref, w_ref, o_ref, xn_ref):
    @pl.when(pl.program_id(1) == 0)
    def _():
        xn_ref[...] = _rms(x_ref[...], g_ref[...]).astype(BF)

    o_ref[...] = jnp.dot(xn_ref[...], w_ref[...].astype(BF), preferred_element_type=F32).astype(o_ref.dtype)


_NT = (((1,), (1,)), ((), ()))


def _norm_mm_gates_t_kernel(x_ref, g_ref, wt_ref, wgt_ref, bg_ref, o_ref, gates_ref, xn_ref):
    @pl.when(pl.program_id(1) == 0)
    def _():
        xn = _rms(x_ref[...], g_ref[...]).astype(BF)
        xn_ref[...] = xn
        wg = wgt_ref[...]
        wg = jnp.concatenate([wg, jnp.zeros((LANES - wg.shape[0], wg.shape[1]), F32)], axis=0).astype(BF)
        gates_ref[...] = lax.dot_general(xn, wg, _NT, preferred_element_type=F32) + bg_ref[...]

    o_ref[...] = lax.dot_general(xn_ref[...], wt_ref[...].astype(BF), _NT,
                                 preferred_element_type=F32).astype(o_ref.dtype)


def norm_matmul_gates_t(x, gain, wt, n_cols, n_gates, gate_b, layer):
    n, k = x.shape
    tm = min(ROW_TILE, n)
    tn = COL_TILE
    return pl.pallas_call(
        _norm_mm_gates_t_kernel, grid=(n // tm, n_cols // tn),
        in_specs=[pl.BlockSpec((tm, k), lambda i, j: (i, 0)), pl.BlockSpec((1, k), lambda i, j: (0, 0)),
                  pl.BlockSpec((None, tn, k), lambda i, j: (layer, j, 0)),
                  pl.BlockSpec((None, n_gates, k), lambda i, j: (layer, n_cols // n_gates, 0)),
                  pl.BlockSpec((1, LANES), lambda i, j: (0, 0))],
        out_specs=[pl.BlockSpec((tm, tn), lambda i, j: (i, j)), pl.BlockSpec((tm, LANES), lambda i, j: (i, 0))],
        out_shape=[jax.ShapeDtypeStruct((n, n_cols), BF), jax.ShapeDtypeStruct((n, LANES), F32)],
        scratch_shapes=[pltpu.VMEM((tm, k), BF)],
        compiler_params=_cparams(("parallel", "arbitrary"), 56), name="norm_matmul_gates",
    )(x, gain.reshape(1, k), wt, wt, gate_b)


def _stacked(w, layer):
    return (w[None], 0) if layer is None else (w, layer)


def norm_matmul(x, gain, w, n_cols, layer=None):
    n, k = x.shape
    w, li = _stacked(w, layer)
    tm = min(ROW_TILE, n)
    tn = COL_TILE
    return pl.pallas_call(
        _norm_mm_kernel, grid=(n // tm, n_cols // tn),
        in_specs=[pl.BlockSpec((tm, k), lambda i, j: (i, 0)), pl.BlockSpec((1, k), lambda i, j: (0, 0)),
                  pl.BlockSpec((None, k, tn), lambda i, j: (li, 0, j))],
        out_specs=pl.BlockSpec((tm, tn), lambda i, j: (i, j)),
        out_shape=jax.ShapeDtypeStruct((n, n_cols), BF), scratch_shapes=[pltpu.VMEM((tm, k), BF)],
        compiler_params=_cparams(("parallel", "arbitrary"), 56), name="norm_matmul",
    )(x, gain.reshape(1, k), w)


def _fill_bf16(dst_ref, src_ref):
    def body(i, carry):
        r = pl.multiple_of(i * CAST_ROWS, CAST_ROWS)
        dst_ref[pl.ds(r, CAST_ROWS), :] = src_ref[pl.ds(r, CAST_ROWS), :].astype(BF)
        return carry

    lax.fori_loop(0, src_ref.shape[0] // CAST_ROWS, body, 0, unroll=CAST_UNROLL)


def _q_attention_kernel(x_ref, g_ref, w_ref, k_ref, v_ref, o_ref, wb_ref, q_ref, *, hd):
    @pl.when(pl.program_id(0) == 0)
    def _():
        _fill_bf16(wb_ref, w_ref)

    xn = _rms(x_ref[...], g_ref[...]).astype(BF)
    q_ref[...] = jnp.dot(xn, wb_ref[...], preferred_element_type=F32).astype(BF)
    scale = hd ** -0.5
    for h in range(XATTN_HEADS):
        hs = slice(h * hd, (h + 1) * hd)
        s = lax.dot_general(q_ref[:, hs], k_ref[:, hs], _NT, preferred_element_type=F32) * scale
        e = jnp.exp(s - jnp.max(s, axis=-1, keepdims=True))
        p = (e / jnp.sum(e, axis=-1, keepdims=True)).astype(BF)
        o_ref[:, hs] = jnp.dot(p, v_ref[:, hs], preferred_element_type=F32).astype(o_ref.dtype)


def q_attention(x, gain, wq, k, v, layer):
    n, kd = x.shape
    d = wq.shape[2]
    m = k.shape[0]
    tm = min(OUT_ROW_TILE, n)
    return pl.pallas_call(
        functools.partial(_q_attention_kernel, hd=d // XATTN_HEADS), grid=(n // tm,),
        in_specs=[pl.BlockSpec((tm, kd), lambda i: (i, 0)), pl.BlockSpec((1, kd), lambda i: (0, 0)),
                  pl.BlockSpec((None, kd, d), lambda i: (layer, 0, 0), pipeline_mode=pl.Buffered(1)),
                  pl.BlockSpec((m, d), lambda i: (0, 0)), pl.BlockSpec((m, d), lambda i: (0, 0))],
        out_specs=pl.BlockSpec((tm, d), lambda i: (i, 0)), out_shape=jax.ShapeDtypeStruct((n, d), BF),
        scratch_shapes=[pltpu.VMEM((kd, d), BF), pltpu.VMEM((tm, d), BF)],
        compiler_params=_cparams(("arbitrary",), 52), name="q_attention",
    )(x, gain.reshape(1, kd), wq, k, v)


def _mm_res_kernel(*refs, nparts):
    xs = refs[:nparts]
    w_ref, res_ref, o_ref, wb_ref = refs[nparts:]

    @pl.when(pl.program_id(0) == 0)
    def _():
        _fill_bf16(wb_ref, w_ref)

    acc = res_ref[...]
    k0 = 0
    for x_ref in xs:
        kp = x_ref.shape[1]
        acc = acc + jnp.dot(x_ref[...], wb_ref[k0:k0 + kp, :], preferred_element_type=F32)
        k0 += kp
    o_ref[...] = acc


def matmul_residual(xs, w, res, layer=None):
    n, d = res.shape
    w, li = _stacked(w, layer)
    k = w.shape[1]
    tm = min(OUT_ROW_TILE, n)
    in_specs = [pl.BlockSpec((tm, x.shape[1]), lambda i: (i, 0)) for x in xs]
    in_specs += [pl.BlockSpec((None, k, d), lambda i: (li, 0, 0), pipeline_mode=pl.Buffered(1)),
                 pl.BlockSpec((tm, d), lambda i: (i, 0))]
    return pl.pallas_call(
        functools.partial(_mm_res_kernel, nparts=len(xs)), grid=(n // tm,), in_specs=in_specs,
        out_specs=pl.BlockSpec((tm, d), lambda i: (i, 0)), out_shape=jax.ShapeDtypeStruct((n, d), F32),
        scratch_shapes=[pltpu.VMEM((k, d), BF)],
        compiler_params=_cparams(("arbitrary",), 52), name="matmul_residual",
    )(*xs, w, res)


def _pool_kernel(cur_ref, prev_ref, w_ref, sc_ref, o_ref, *, tp, gdim):
    i = pl.program_id(0)
    dist = lax.broadcasted_iota(jnp.int32, (tp, tp), 0) - lax.broadcasted_iota(jnp.int32, (tp, tp), 1)
    distp = (lax.broadcasted_iota(jnp.int32, (tp, POOL_HALO), 0) + POOL_HALO
             - lax.broadcasted_iota(jnp.int32, (tp, POOL_HALO), 1))
    pos = i * tp + lax.broadcasted_iota(jnp.int32, (tp, 1), 0)
    for j, win in enumerate(POOL_WINDOWS):
        gs = slice(j * gdim, (j + 1) * gdim)
        cur = cur_ref[:, gs]
        band = jnp.where(dist >= 0, jnp.where(dist < win, 1.0, 0.0), 0.0).astype(BF)
        bandp = jnp.where(distp < jnp.where(i > 0, win, 0), 1.0, 0.0).astype(BF)
        s = (jnp.dot(band, cur, preferred_element_type=F32)
             + jnp.dot(bandp, prev_ref[:, gs], preferred_element_type=F32))
        cnt = jnp.minimum(pos + 1, win).astype(F32)
        d = s / cnt - cur.astype(F32)
        y = jnp.dot(d.astype(BF), w_ref[j].astype(BF), preferred_element_type=F32) * sc_ref[:, gs]
        o_ref[:, gs] = y.astype(o_ref.dtype)


def pool_mixer(z, pool_w, pool_scale):
    n = z.shape[0]
    gdim = pool_w.shape[-1]
    width = N_POOL_GROUPS * gdim
    tp = min(POOL_ROWS, n)
    halo_blocks = tp // POOL_HALO
    return pl.pallas_call(
        functools.partial(_pool_kernel, tp=tp, gdim=gdim), grid=(n // tp,),
        in_specs=[
            pl.BlockSpec((tp, width), lambda i: (i, 0)),
            pl.BlockSpec((POOL_HALO, width), lambda i: (jnp.maximum(i * halo_blocks - 1, 0), 0)),
            pl.BlockSpec((N_POOL_GROUPS, gdim, gdim), lambda i: (0, 0, 0)),
            pl.BlockSpec((1, width), lambda i: (0, 0)),
        ],
        out_specs=pl.BlockSpec((tp, width), lambda i: (i, 0)),
        out_shape=jax.ShapeDtypeStruct((n, width), BF),
        compiler_params=_cparams(("parallel",), 32), name="pool_mixer",
    )(z, z, pool_w, pool_scale.reshape(1, width))


def _mlstm_kernel(q_ref, k_ref, v_ref, o_ref, g_ref, gain_ref, y_ref, ct_ref, m_ref, *, chunk, dh):
    c = pl.program_id(0)

    @pl.when(c == 0)
    def _():
        ct_ref[...] = jnp.zeros_like(ct_ref)
        m_ref[...] = jnp.zeros_like(m_ref)

    g = g_ref[...]
    lf = _log_sigmoid(g)
    row = lax.broadcasted_iota(jnp.int32, (chunk, chunk), 0)
    col = lax.broadcasted_iota(jnp.int32, (chunk, chunk), 1)
    causal = col <= row
    ltri = jnp.where(causal, 1.0, 0.0).astype(BF)
    hi = lf.astype(BF)
    r1 = lf - hi.astype(F32)
    mid = r1.astype(BF)
    lo = (r1 - mid.astype(F32)).astype(BF)
    bcum = (jnp.dot(ltri, hi, preferred_element_type=F32) + jnp.dot(ltri, mid, preferred_element_type=F32)
            + jnp.dot(ltri, lo, preferred_element_type=F32))
    g_t = g.T
    b_t = bcum.T
    ones_col = jnp.where(lax.broadcasted_iota(jnp.int32, (chunk, LANES), 1) == 0, 1.0, 0.0).astype(BF)

    for h in range(MLSTM_HEADS):
        hs = slice(h * dh, (h + 1) * dh)
        fl = FORGET_LANE0 + h
        bc = bcum[:, fl:fl + 1]
        br = b_t[fl:fl + 1, :]
        ir = g_t[h:h + 1, :]
        b_last = bcum[chunk - 1:chunk, fl:fl + 1]
        m_prev = m_ref[h][:, 0:1]

        dmat = jnp.where(causal, bc + (ir - br), -jnp.inf)
        inter = bc + m_prev
        m_t = jnp.maximum(jnp.max(dmat, axis=1, keepdims=True), inter)
        w_inter = jnp.exp(inter - m_t)
        p = jnp.exp(dmat - m_t)

        qh = q_ref[:, hs] * (dh ** -0.5)
        kh = k_ref[:, hs]
        v_aug = jnp.concatenate([v_ref[:, hs], ones_col], axis=1)
        s = lax.dot_general(qh, kh, (((1,), (1,)), ((), ())), preferred_element_type=F32)
        sc = (s * p).astype(BF)
        ct = ct_ref[h]
        num_aug = (w_inter * jnp.dot(qh, ct.astype(BF), preferred_element_type=F32)
                   + jnp.dot(sc, v_aug, preferred_element_type=F32))
        num = num_aug[:, :dh]
        den = num_aug[:, dh:dh + 1]
        hout = num / jnp.maximum(jnp.abs(den), jnp.exp(-m_t))

        yn = _rms(hout, gain_ref[:, hs])
        y_ref[:, hs] = (_sigmoid(o_ref[:, hs].astype(F32)) * yn).astype(y_ref.dtype)

        d_end = b_last - br + ir
        m_new = jnp.maximum(b_last + m_prev, jnp.max(d_end, axis=1, keepdims=True))
        a_prev = jnp.exp(b_last + m_prev - m_new)
        a_s = jnp.exp(d_end - m_new)
        k_t = (kh.astype(F32).T * a_s).astype(BF)
        ct_ref[h] = a_prev * ct + jnp.dot(k_t, v_aug, preferred_element_type=F32)
        m_ref[h] = jnp.broadcast_to(m_new, (1, LANES))


def mlstm_mixer(z, gates, head_gain, col0):
    n = z.shape[0]
    width = head_gain.shape[0]
    dh = width // MLSTM_HEADS
    chunk = min(MLSTM_CHUNK, n)
    base = col0 // width
    qkvo = [pl.BlockSpec((chunk, width), lambda c, p=p: (c, base + p)) for p in range(4)]
    return pl.pallas_call(
        functools.partial(_mlstm_kernel, chunk=chunk, dh=dh), grid=(n // chunk,),
        in_specs=qkvo + [pl.BlockSpec((chunk, LANES), lambda c: (c, 0)), pl.BlockSpec((1, width), lambda c: (0, 0))],
        out_specs=pl.BlockSpec((chunk, width), lambda c: (c, 0)),
        out_shape=jax.ShapeDtypeStruct((n, width), BF),
        scratch_shapes=[pltpu.VMEM((MLSTM_HEADS, dh, dh + LANES), F32), pltpu.VMEM((MLSTM_HEADS, 1, LANES), F32)],
        compiler_params=_cparams(("arbitrary",), 32), name="mlstm_mixer",
    )(z, z, z, z, gates, head_gain.reshape(1, width))


def _gated_conv(b, c, u, cp, up, w, first_block):
    zc = c.astype(F32) * u.astype(F32)
    zp = jnp.where(first_block, 0.0, cp.astype(F32) * up.astype(F32))
    row = lax.broadcasted_iota(jnp.int32, zc.shape, 0)
    acc = w[CONV_WIDTH - 1:CONV_WIDTH, :] * zc
    for back in range(1, CONV_WIDTH):
        shifted = pltpu.roll(zc, back, 0)
        for r in range(back):
            shifted = jnp.where(row == r, zp[CONV_HALO - back + r:CONV_HALO - back + r + 1, :], shifted)
        acc = acc + w[CONV_WIDTH - 1 - back:CONV_WIDTH - back, :] * shifted
    return b.astype(F32) * acc


def _conv_mm_res_kernel(b_ref, c_ref, u_ref, cp_ref, up_ref, cw_ref, w_ref, res_ref, o_ref, wb_ref, xs_ref):
    i = pl.program_id(0)

    @pl.when(i == 0)
    def _():
        _fill_bf16(wb_ref, w_ref)

    for c0 in range(0, xs_ref.shape[1], CONV_COLS):
        cs = slice(c0, c0 + CONV_COLS)
        xs_ref[:, cs] = _gated_conv(b_ref[:, cs], c_ref[:, cs], u_ref[:, cs], cp_ref[:, cs], up_ref[:, cs],
                                    cw_ref[:, cs], i == 0).astype(BF)
    o_ref[...] = res_ref[...] + jnp.dot(xs_ref[...], wb_ref[...], preferred_element_type=F32)


def conv_matmul_residual(z, conv_w, w, res, layer):
    n, d = res.shape
    tm = min(CONV_ROWS, n)
    halo_blocks = tm // CONV_HALO
    cur = lambda part: pl.BlockSpec((tm, d), lambda i: (i, part))
    prev = lambda part: pl.BlockSpec((CONV_HALO, d), lambda i: (jnp.maximum(i * halo_blocks - 1, 0), part))
    return pl.pallas_call(
        _conv_mm_res_kernel, grid=(n // tm,),
        in_specs=[cur(0), cur(1), cur(2), prev(1), prev(2), pl.BlockSpec((CONV_WIDTH, d), lambda i: (0, 0)),
                  pl.BlockSpec((None, d, d), lambda i: (layer, 0, 0), pipeline_mode=pl.Buffered(1)),
                  pl.BlockSpec((tm, d), lambda i: (i, 0))],
        out_specs=pl.BlockSpec((tm, d), lambda i: (i, 0)), out_shape=jax.ShapeDtypeStruct((n, d), F32),
        scratch_shapes=[pltpu.VMEM((d, d), BF), pltpu.VMEM((tm, d), BF)],
        compiler_params=_cparams(("arbitrary",), 52), name="conv_matmul_residual",
    )(z, z, z, z, z, conv_w, w, res)


def _router_kernel(x_ref, g_ref, wr_ref, br_ref, xn_ref, ids_ref, wts_ref, cnt_ref, carry_ref, whi_ref, wlo_ref,
                   *, tr):
    @pl.when(pl.program_id(0) == 0)
    def _():
        carry_ref[...] = jnp.zeros_like(carry_ref)

    @pl.when(pl.program_id(0) == 0)
    def _():
        w = wr_ref[...]
        w_hi = w.astype(BF)
        whi_ref[...] = w_hi
        wlo_ref[...] = (w - w_hi.astype(F32)).astype(BF)

    xn = _rms(x_ref[...], g_ref[...])
    xn_ref[...] = _pack_halves(xn)
    x_hi = xn.astype(BF)
    x_lo = (xn - x_hi.astype(F32)).astype(BF)
    logits = (jnp.dot(x_hi, whi_ref[...], preferred_element_type=F32)
              + jnp.dot(x_lo, whi_ref[...], preferred_element_type=F32)
              + jnp.dot(x_hi, wlo_ref[...], preferred_element_type=F32)) + br_ref[...]
    lane = lax.broadcasted_iota(jnp.int32, (tr, LANES), 1).astype(F32)
    neg = -jnp.inf

    def first_argmax(vals):
        top = jnp.max(vals, axis=-1, keepdims=True)
        return top, jnp.min(jnp.where(vals == top, lane, float(LANES)), axis=-1, keepdims=True)

    gl = jnp.where(lane < N_GROUPS, logits, neg)
    gmax, grp = first_argmax(gl)
    g_prob = 1.0 / jnp.sum(jnp.exp(gl - gmax), axis=-1, keepdims=True)
    lo = N_GROUPS + EXPERTS_PER_GROUP * grp
    el = jnp.where(lane >= lo, jnp.where(lane < lo + EXPERTS_PER_GROUP, logits, neg), neg)
    v1, l1 = first_argmax(el)
    v2, l2 = first_argmax(jnp.where(lane == l1, neg, el))
    e2 = jnp.exp(v2 - v1)
    w1 = g_prob / (1.0 + e2)
    w2 = g_prob * e2 / (1.0 + e2)
    hot1 = lane == l1
    hot2 = lane == l2
    hot = jnp.where(hot1, 1.0, jnp.where(hot2, 1.0, 0.0))
    earlier = (lax.broadcasted_iota(jnp.int32, (tr, tr), 1) < lax.broadcasted_iota(jnp.int32, (tr, tr), 0))
    before = jnp.dot(jnp.where(earlier, 1.0, 0.0).astype(BF), hot.astype(BF), preferred_element_type=F32)
    before = before + carry_ref[0:1, :]
    r1 = jnp.sum(jnp.where(hot1, before, 0.0), axis=-1, keepdims=True)
    r2 = jnp.sum(jnp.where(hot2, before, 0.0), axis=-1, keepdims=True)
    carry_ref[0:1, :] = carry_ref[0:1, :] + jnp.sum(hot, axis=0, keepdims=True)
    ids = jnp.where(lane == 0, l1 - N_GROUPS, jnp.where(lane == 1, l2 - N_GROUPS,
                    jnp.where(lane == 2, r1, jnp.where(lane == 3, r2, 0.0))))
    ids_ref[...] = ids.astype(jnp.int32)
    wts_ref[...] = jnp.where(lane == 0, w1, jnp.where(lane == 1, w2, 0.0))
    cnt_ref[...] = carry_ref[...]


def route(h, gain, w_router, b_router):
    n, d = h.shape
    tr = min(ROUTER_ROWS, n)
    return pl.pallas_call(
        functools.partial(_router_kernel, tr=tr), grid=(n // tr,),
        in_specs=[pl.BlockSpec((tr, d), lambda i: (i, 0)), pl.BlockSpec((1, d), lambda i: (0, 0)),
                  pl.BlockSpec((d, LANES), lambda i: (0, 0)), pl.BlockSpec((1, LANES), lambda i: (0, 0))],
        out_specs=[pl.BlockSpec((tr, d // 2), lambda i: (i, 0)),
                   pl.BlockSpec((tr, LANES), lambda i: (i, 0)), pl.BlockSpec((tr, LANES), lambda i: (i, 0)),
                   pl.BlockSpec((SUBLANES, LANES), lambda i: (0, 0))],
        out_shape=[jax.ShapeDtypeStruct((n, d // 2), jnp.uint32), jax.ShapeDtypeStruct((n, LANES), jnp.int32),
                   jax.ShapeDtypeStruct((n, LANES), F32), jax.ShapeDtypeStruct((SUBLANES, LANES), F32)],
        scratch_shapes=[pltpu.VMEM((SUBLANES, LANES), F32), pltpu.VMEM((d, LANES), BF), pltpu.VMEM((d, LANES), BF)],
        compiler_params=_cparams(("arbitrary",), 40), name="moe_router",
    )(h, gain.reshape(1, d), w_router, b_router)


def _dest_kernel(ids_ref, start_ref, o_ref, *, tr):
    ids = ids_ref[...].astype(F32)
    lane = lax.broadcasted_iota(jnp.int32, (tr, LANES), 1).astype(F32)
    start = start_ref[...]
    rows = []
    for k in range(TOP_K):
        first = jnp.sum(jnp.where(lane == ids[:, k:k + 1], start, 0.0), axis=-1, keepdims=True)
        rows.append(first + ids[:, TOP_K + k:TOP_K + k + 1])
    packed = jnp.where(lane == 0, rows[0], jnp.where(lane == 1, rows[1], 0.0))
    o_ref[...] = packed.T[0:SUBLANES, :].astype(jnp.int32)


def assignment_rows(ids, start_rows):
    n = ids.shape[0]
    tr = min(ROUTER_ROWS, n)
    out = pl.pallas_call(
        functools.partial(_dest_kernel, tr=tr), grid=(n // tr,),
        in_specs=[pl.BlockSpec((tr, LANES), lambda i: (i, 0)), pl.BlockSpec((1, LANES), lambda i: (0, 0))],
        out_specs=pl.BlockSpec((SUBLANES, tr), lambda i: (0, i)),
        out_shape=jax.ShapeDtypeStruct((SUBLANES, n), jnp.int32),
        compiler_params=_cparams(("parallel",), 32), name="moe_assignment_rows",
    )(ids, start_rows)
    return out[:TOP_K]


def _row_copy(src_hbm, row, dst, dst_row, sem):
    return pltpu.make_async_copy(src_hbm.at[pl.ds(row, 1), :], dst.at[pl.ds(dst_row, 1), :], sem)


def _rows_wait(src_hbm, dst, sem):
    pltpu.make_async_copy(src_hbm.at[pl.ds(0, dst.shape[0]), :], dst, sem).wait()


def _expert_weight_copies(w_hbm, layer, expert, stage, wset, sem):
    rows = stage.shape[1] // WEIGHT_DMA_PARTS
    return [pltpu.make_async_copy(w_hbm.at[layer, expert, pl.ds(p * rows, rows), :],
                                  stage.at[wset, pl.ds(p * rows, rows), :], sem.at[wset])
            for p in range(WEIGHT_DMA_PARTS)]


def _stream_expert_weights(b, nact, blk_ref, kin_ref, set_ref, nxt_ref, copies):
    cur = set_ref[b]

    @pl.when(b == 0)
    def _():
        for cp in copies(blk_ref[0], cur):
            cp.start(priority=DMA_QUEUE_BULK)

    @pl.when(jnp.logical_and(b < nact, kin_ref[b] == 0))
    def _():
        for cp in copies(blk_ref[b], cur):
            cp.wait()

        @pl.when(nxt_ref[b] != blk_ref[b])
        def _():
            for cp in copies(nxt_ref[b], 1 - cur):
                cp.start(priority=DMA_QUEUE_BULK)

    return cur


def _moe_up_kernel(blk_ref, kin_ref, set_ref, nxt_ref, nact_ref, rtok_ref, xn_hbm, wg_hbm, wu_hbm, hid_ref,
                   buf, stage_g, stage_u, xb, sem, wsem, *, tb, layer):
    b = pl.program_id(0)
    nact = nact_ref[0]
    nslots = GATHER_AHEAD + 1
    slot = b % nslots
    f = hid_ref.shape[1]
    up_chunks = f // MOE_COL_CHUNK
    rows_per_chunk = tb // up_chunks

    @pl.when(b == 0)
    def _():
        for ahead in range(GATHER_AHEAD):
            def body(r, carry, ahead=ahead):
                _row_copy(xn_hbm, rtok_ref[ahead * tb + r], buf.at[ahead], r,
                          sem.at[ahead]).start(priority=DMA_QUEUE_ROWS)
                return carry

            lax.fori_loop(0, tb, body, 0, unroll=8)

    def copies(e, wset):
        return (_expert_weight_copies(wg_hbm, layer, e, stage_g, wset, wsem.at[0])
                + _expert_weight_copies(wu_hbm, layer, e, stage_u, wset, wsem.at[1]))

    cur = _stream_expert_weights(b, nact, blk_ref, kin_ref, set_ref, nxt_ref, copies)

    @pl.when(b < nact)
    def _():
        _rows_wait(xn_hbm, buf.at[slot], sem.at[slot])
        x_lo, x_hi = _unpack_halves(buf[slot])
        half = x_lo.shape[1]
        xb[:, :half] = x_lo.astype(BF)
        xb[:, half:] = x_hi.astype(BF)
        ahead_slot = (b + GATHER_AHEAD) % nslots
        for c in range(up_chunks):
            for r in range(c * rows_per_chunk, (c + 1) * rows_per_chunk):
                _row_copy(xn_hbm, rtok_ref[(b + GATHER_AHEAD) * tb + r], buf.at[ahead_slot], r,
                          sem.at[ahead_slot]).start(priority=DMA_QUEUE_ROWS)
            cs = slice(c * MOE_COL_CHUNK, (c + 1) * MOE_COL_CHUNK)
            gate = jnp.dot(xb[...], stage_g[cur, :, cs].astype(BF), preferred_element_type=F32)
            up = jnp.dot(xb[...], stage_u[cur, :, cs].astype(BF), preferred_element_type=F32)
            hid_ref[:, cs] = (gate * _sigmoid(gate) * up).astype(hid_ref.dtype)

    @pl.when(b >= nact)
    def _():
        hid_ref[...] = jnp.zeros_like(hid_ref)

    @pl.when(jnp.logical_and(b >= nact, b < nact + GATHER_AHEAD))
    def _():
        _rows_wait(xn_hbm, buf.at[slot], sem.at[slot])


def _moe_down_kernel(blk_ref, kin_ref, set_ref, nxt_ref, nact_ref, hid_ref, wd_hbm, ys_ref, stage_d, wsem, *, layer):
    b = pl.program_id(0)
    nact = nact_ref[0]
    half = ys_ref.shape[1]

    def copies(e, wset):
        return _expert_weight_copies(wd_hbm, layer, e, stage_d, wset, wsem)

    cur = _stream_expert_weights(b, nact, blk_ref, kin_ref, set_ref, nxt_ref, copies)

    @pl.when(b < nact)
    def _():
        for c in range(half // MOE_COL_CHUNK):
            cs = slice(c * MOE_COL_CHUNK, (c + 1) * MOE_COL_CHUNK)
            hs = slice(half + c * MOE_COL_CHUNK, half + (c + 1) * MOE_COL_CHUNK)
            lo = jnp.dot(hid_ref[...], stage_d[cur, :, cs].astype(BF), preferred_element_type=F32)
            hi = jnp.dot(hid_ref[...], stage_d[cur, :, hs].astype(BF), preferred_element_type=F32)
            ys_ref[:, cs] = _pack_halves(jnp.concatenate([lo, hi], axis=1))

    @pl.when(b >= nact)
    def _():
        ys_ref[...] = jnp.zeros_like(ys_ref)


def _combine_kernel(dest_ref, ys_hbm, h_ref, w_ref, gain_ref, o_ref, buf, sem, *, tc, n, final_norm):
    i = pl.program_id(0)

    def issue(blk, slot):
        base = blk * tc

        def body(r, carry):
            for k in range(TOP_K):
                _row_copy(ys_hbm, dest_ref[k * n + base + r], buf.at[slot], k * tc + r,
                          sem.at[slot]).start(priority=k)
            return carry

        lax.fori_loop(0, tc, body, 0, unroll=8)

    @pl.when(i == 0)
    def _():
        issue(0, 0)

    @pl.when(i + 1 < pl.num_programs(0))
    def _():
        issue(i + 1, (i + 1) % 2)

    slot = i % 2
    _rows_wait(ys_hbm, buf.at[slot], sem.at[slot])
    a_lo, a_hi = _unpack_halves(buf[slot, 0:tc, :])
    b_lo, b_hi = _unpack_halves(buf[slot, tc:TOP_K * tc, :])
    w0 = w_ref[:, 0:1]
    w1 = w_ref[:, 1:2]
    out = h_ref[...] + jnp.concatenate([w0 * a_lo + w1 * b_lo, w0 * a_hi + w1 * b_hi], axis=1)
    if final_norm:
        out = _rms(out, gain_ref[...])
    o_ref[...] = out


def hier_moe(h, ffn_gain, wg_r, bg_r, we_r, be_r, w_gate, w_up, w_down, layer, final_gain=None):
    n, d = h.shape
    f = w_gate.shape[-1]
    a = n * TOP_K
    tb = _moe_block_rows(a)
    nb = a // tb + N_EXPERTS + GATHER_AHEAD

    pad = LANES - N_GROUPS - N_EXPERTS
    w_router = jnp.concatenate([wg_r, we_r, jnp.zeros((d, pad), F32)], axis=1)
    b_router = jnp.concatenate([bg_r, be_r, jnp.zeros((pad,), F32)]).reshape(1, LANES)
    xn, ids, wts, cnt = route(h, ffn_gain, w_router, b_router)

    counts = cnt[0, N_GROUPS:N_GROUPS + N_EXPERTS].astype(jnp.int32)
    nblk = (counts + tb - 1) // tb
    bend = jnp.cumsum(nblk)
    nact = bend[-1]
    start_rows = jnp.pad(((bend - nblk) * tb).astype(F32), (0, LANES - N_EXPERTS)).reshape(1, LANES)
    dest = assignment_rows(ids, start_rows).reshape(a)
    blk = jnp.minimum(jnp.arange(nb, dtype=jnp.int32), jnp.maximum(nact - 1, 0))
    owner = lambda bi: jnp.minimum(jnp.sum((bend[None, :] <= bi[:, None]).astype(jnp.int32), axis=1), N_EXPERTS - 1)
    blk_e = owner(blk)
    k_in_e = blk - (bend - nblk)[blk_e]
    following = lambda e: jnp.where(bend[e] < nact, owner(bend[e]), e)
    nxt_e = following(blk_e)
    wset = (jnp.cumsum((k_in_e == 0).astype(jnp.int32)) - 1) % 2
    tok = jnp.tile(jnp.arange(n, dtype=jnp.int32), TOP_K)
    row_tok = (jnp.arange(nb * tb, dtype=jnp.int32) % n).at[dest].set(tok)
    nact1 = nact.reshape(1).astype(jnp.int32)

    any_space = pl.BlockSpec(memory_space=pl.ANY)
    hid = pl.pallas_call(
        functools.partial(_moe_up_kernel, tb=tb, layer=layer),
        grid_spec=pltpu.PrefetchScalarGridSpec(
            num_scalar_prefetch=6, grid=(nb,),
            in_specs=[any_space, any_space, any_space],
            out_specs=pl.BlockSpec((tb, f), lambda b, *_: (b, 0)),
            scratch_shapes=[pltpu.VMEM((GATHER_AHEAD + 1, tb, d // 2), jnp.uint32),
                            pltpu.VMEM((2, d, f), F32), pltpu.VMEM((2, d, f), F32), pltpu.VMEM((tb, d), BF),
                            pltpu.SemaphoreType.DMA((GATHER_AHEAD + 1,)), pltpu.SemaphoreType.DMA((2, 2))]),
        out_shape=jax.ShapeDtypeStruct((nb * tb, f), BF),
        compiler_params=_cparams(("arbitrary",), 56), name="moe_up",
    )(blk_e, k_in_e, wset, nxt_e, nact1, row_tok, xn, w_gate, w_up)

    ys = pl.pallas_call(
        functools.partial(_moe_down_kernel, layer=layer),
        grid_spec=pltpu.PrefetchScalarGridSpec(
            num_scalar_prefetch=5, grid=(nb,),
            in_specs=[pl.BlockSpec((tb, f), lambda b, *_: (b, 0)), any_space],
            out_specs=pl.BlockSpec((tb, d // 2), lambda b, *_: (b, 0)),
            scratch_shapes=[pltpu.VMEM((2, f, d), F32), pltpu.SemaphoreType.DMA((2,))]),
        out_shape=jax.ShapeDtypeStruct((nb * tb, d // 2), jnp.uint32),
        compiler_params=_cparams(("arbitrary",), 40), name="moe_down",
    )(blk_e, k_in_e, wset, nxt_e, nact1, hid, w_down)

    tc = min(COMBINE_ROWS, n)
    gain = (final_gain if final_gain is not None else ffn_gain).reshape(1, d)
    return pl.pallas_call(
        functools.partial(_combine_kernel, tc=tc, n=n, final_norm=final_gain is not None),
        grid_spec=pltpu.PrefetchScalarGridSpec(
            num_scalar_prefetch=1, grid=(n // tc,),
            in_specs=[pl.BlockSpec(memory_space=pl.ANY), pl.BlockSpec((tc, d), lambda i, *_: (i, 0)),
                      pl.BlockSpec((tc, LANES), lambda i, *_: (i, 0)), pl.BlockSpec((1, d), lambda i, *_: (0, 0))],
            out_specs=pl.BlockSpec((tc, d), lambda i, *_: (i, 0)),
            scratch_shapes=[pltpu.VMEM((2, TOP_K * tc, d // 2), jnp.uint32), pltpu.SemaphoreType.DMA((2,))]),
        out_shape=jax.ShapeDtypeStruct((n, d), F32),
        compiler_params=_cparams(("arbitrary",), 40), name="moe_combine",
    )(dest, ys, h, wts, gain)


def even_mixer(h, gain, w_in, b_gates, pool_w, pool_scale, head_gain, w_out, j):
    d = h.shape[1]
    pool_width = N_POOL_GROUPS * pool_w.shape[-1]
    mlstm_width = head_gain.shape[0]
    main_cols = pool_width + 4 * mlstm_width
    n_gates = 2 * MLSTM_HEADS
    gate_b = jnp.pad(b_gates, (0, LANES - n_gates)).reshape(1, LANES)
    z, gates = norm_matmul_gates_t(h, gain, jnp.swapaxes(w_in, 1, 2), main_cols, n_gates, gate_b, j)
    y_p = pool_mixer(z, pool_w, pool_scale)
    y_m = mlstm_mixer(z, gates, head_gain, pool_width)
    assert pool_width == mlstm_width and pool_width + mlstm_width == d
    return matmul_residual([y_p, y_m], w_out, h, layer=j)


def odd_mixer(h, gain, w_in, conv_w, w_out, j):
    z = norm_matmul(h, gain, w_in, w_in.shape[2], layer=j)
    return conv_matmul_residual(z, conv_w, w_out, h, j)


def cross_attn(h, mem, gain, mem_gain, wq, wk, wv, wo, layer):
    d = h.shape[1]
    k = norm_matmul(mem, mem_gain, wk, d, layer=layer)
    v = norm_matmul(mem, mem_gain, wv, d, layer=layer)
    return matmul_residual([q_attention(h, gain, wq, k, v, layer)], wo, h, layer=layer)


def kernel(x, mem, mix_norm, xattn_norm, mem_norm, ffn_norm, final_norm, ev_w_in, ev_b_gates, ev_pool_w, ev_pool_scale, ev_head_norm, ev_w_out, od_w_in, od_conv_w, od_w_out, xa_wq, xa_wk, xa_wv, xa_wo, rt_group_w, rt_group_b, rt_expert_w, rt_expert_b, ex_w_gate, ex_w_up, ex_w_down):
    depth = mix_norm.shape[0]
    h = x[0]
    m = mem[0]
    for layer in range(depth):
        j = layer // 2
        if layer % 2 == 0:
            h = even_mixer(h, mix_norm[layer], ev_w_in, ev_b_gates[j], ev_pool_w[j], ev_pool_scale[j],
                           ev_head_norm[j], ev_w_out, j)
        else:
            h = odd_mixer(h, mix_norm[layer], od_w_in, od_conv_w[j], od_w_out, j)
        h = cross_attn(h, m, xattn_norm[layer], mem_norm[layer], xa_wq, xa_wk, xa_wv, xa_wo, layer)
        h = hier_moe(h, ffn_norm[layer], rt_group_w[layer], rt_group_b[layer], rt_expert_w[layer],
                     rt_expert_b[layer], ex_w_gate, ex_w_up, ex_w_down, layer,
                     final_gain=final_norm if layer == depth - 1 else None)
    return h[None]
```

```python
import functools

import jax
import jax.numpy as jnp
from jax import lax
from jax.experimental import pallas as pl
from jax.experimental.pallas import tpu as pltpu

F32 = jnp.float32
BF = jnp.bfloat16
EPS = 1e-6

POOL_WINDOWS = (2, 4, 8, 16)
N_POOL_GROUPS = 4
MLSTM_HEADS = 4
FORGET_LANE0 = MLSTM_HEADS
XATTN_HEADS = 4
N_GROUPS = 4
EXPERTS_PER_GROUP = 8
N_EXPERTS = N_GROUPS * EXPERTS_PER_GROUP
TOP_K = 2
CONV_WIDTH = 3

LANES = 128
SUBLANES = 8
DMA_QUEUE_ROWS = 0
DMA_QUEUE_BULK = 1

ROW_TILE = 1024
COL_TILE = 1024
OUT_ROW_TILE = 512
CAST_ROWS = 64
CAST_UNROLL = 4
MOE_COL_CHUNK = 256
WEIGHT_DMA_PARTS = 4
MLSTM_CHUNK = 256
POOL_ROWS = 256
POOL_HALO = 128
CONV_ROWS = 256
CONV_COLS = 512
CONV_HALO = 16
ROUTER_ROWS = 512
GATHER_AHEAD = 2
COMBINE_ROWS = 256


def _moe_block_rows(assignments):
    mean_rows = assignments // N_EXPERTS
    return -(-(mean_rows * 9 // 8) // 16) * 16


def _cparams(semantics, vmem_mib):
    return pltpu.CompilerParams(dimension_semantics=semantics, vmem_limit_bytes=vmem_mib * 1024 * 1024)


def _sigmoid(x):
    return 1.0 / (1.0 + jnp.exp(-x))


def _log_sigmoid(x):
    return jnp.minimum(x, 0.0) - jnp.log(1.0 + jnp.exp(-jnp.abs(x)))


def _rms(x, g):
    ms = jnp.mean(x * x, axis=-1, keepdims=True)
    return x * lax.rsqrt(ms + EPS) * g


def _pack_halves(x):
    half = x.shape[1] // 2
    lo = pltpu.bitcast(x[:, :half].astype(BF).astype(F32), jnp.uint32)
    hi = pltpu.bitcast(x[:, half:].astype(BF).astype(F32), jnp.uint32)
    return hi | lax.shift_right_logical(lo, jnp.uint32(16))


def _unpack_halves(w):
    lo = pltpu.bitcast(lax.shift_left(w, jnp.uint32(16)), F32)
    hi = pltpu.bitcast(w & jnp.uint32(0xFFFF0000), F32)
    return lo, hi


def _norm_mm_kernel(x_ref, g_ref, w_ref, o_ref, xn_ref):
    @pl.when(pl.program_id(1) == 0)
    def _():
        xn_ref[...] = _rms(x_ref[...], g_ref[...]).astype(BF)

    o_ref[...] = jnp.dot(xn_ref[...], w_ref[...].astype(BF), preferred_element_type=F32).astype(o_ref.dtype)


_NT = (((1,), (1,)), ((), ()))


def _norm_mm_gates_t_kernel(x_ref, g_ref, wt_ref, wgt_ref, bg_ref, o_ref, gates_ref, xn_ref):
    @pl.when(pl.program_id(1) == 0)
    def _():
        xn = _rms(x_ref[...], g_ref[...]).astype(BF)
        xn_ref[...] = xn
        wg = wgt_ref[...]
        wg = jnp.concatenate([wg, jnp.zeros((LANES - wg.shape[0], wg.shape[1]), F32)], axis=0).astype(BF)
        gates_ref[...] = lax.dot_general(xn, wg, _NT, preferred_element_type=F32) + bg_ref[...]

    o_ref[...] = lax.dot_general(xn_ref[...], wt_ref[...].astype(BF), _NT,
                                 preferred_element_type=F32).astype(o_ref.dtype)


def norm_matmul_gates_t(x, gain, wt, n_cols, n_gates, gate_b, layer):
    n, k = x.shape
    tm = min(ROW_TILE, n)
    tn = COL_TILE
    return pl.pallas_call(
        _norm_mm_gates_t_kernel, grid=(n // tm, n_cols // tn),
        in_specs=[pl.BlockSpec((tm, k), lambda i, j: (i, 0)), pl.BlockSpec((1, k), lambda i, j: (0, 0)),
                  pl.BlockSpec((None, tn, k), lambda i, j: (layer, j, 0)),
                  pl.BlockSpec((None, n_gates, k), lambda i, j: (layer, n_cols // n_gates, 0)),
                  pl.BlockSpec((1, LANES), lambda i, j: (0, 0))],
        out_specs=[pl.BlockSpec((tm, tn), lambda i, j: (i, j)), pl.BlockSpec((tm, LANES), lambda i, j: (i, 0))],
        out_shape=[jax.ShapeDtypeStruct((n, n_cols), BF), jax.ShapeDtypeStruct((n, LANES), F32)],
        scratch_shapes=[pltpu.VMEM((tm, k), BF)],
        compiler_params=_cparams(("parallel", "arbitrary"), 56), name="norm_matmul_gates",
    )(x, gain.reshape(1, k), wt, wt, gate_b)


def _stacked(w, layer):
    return (w[None], 0) if layer is None else (w, layer)


def norm_matmul(x, gain, w, n_cols, layer=None):
    n, k = x.shape
    w, li = _stacked(w, layer)
    tm = min(ROW_TILE, n)
    tn = COL_TILE
    return pl.pallas_call(
        _norm_mm_kernel, grid=(n // tm, n_cols // tn),
        in_specs=[pl.BlockSpec((tm, k), lambda i, j: (i, 0)), pl.BlockSpec((1, k), lambda i, j: (0, 0)),
                  pl.BlockSpec((None, k, tn), lambda i, j: (li, 0, j))],
        out_specs=pl.BlockSpec((tm, tn), lambda i, j: (i, j)),
        out_shape=jax.ShapeDtypeStruct((n, n_cols), BF), scratch_shapes=[pltpu.VMEM((tm, k), BF)],
        compiler_params=_cparams(("parallel", "arbitrary"), 56), name="norm_matmul",
    )(x, gain.reshape(1, k), w)


def _fill_bf16(dst_ref, src_ref):
    def body(i, carry):
        r = pl.multiple_of(i * CAST_ROWS, CAST_ROWS)
        dst_ref[pl.ds(r, CAST_ROWS), :] = src_ref[pl.ds(r, CAST_ROWS), :].astype(BF)
        return carry

    lax.fori_loop(0, src_ref.shape[0] // CAST_ROWS, body, 0, unroll=CAST_UNROLL)


def _q_attention_kernel(x_ref, g_ref, w_ref, k_ref, v_ref, o_ref, wb_ref, q_ref, *, hd):
    @pl.when(pl.program_id(0) == 0)
    def _():
        _fill_bf16(wb_ref, w_ref)

    xn = _rms(x_ref[...], g_ref[...]).astype(BF)
    q_ref[...] = jnp.dot(xn, wb_ref[...], preferred_element_type=F32).astype(BF)
    scale = hd ** -0.5
    for h in range(XATTN_HEADS):
        hs = slice(h * hd, (h + 1) * hd)
        s = lax.dot_general(q_ref[:, hs], k_ref[:, hs], _NT, preferred_element_type=F32) * scale
        e = jnp.exp(s - jnp.max(s, axis=-1, keepdims=True))
        p = (e / jnp.sum(e, axis=-1, keepdims=True)).astype(BF)
        o_ref[:, hs] = jnp.dot(p, v_ref[:, hs], preferred_element_type=F32).astype(o_ref.dtype)


def q_attention(x, gain, wq, k, v, layer):
    n, kd = x.shape
    d = wq.shape[2]
    m = k.shape[0]
    tm = min(OUT_ROW_TILE, n)
    return pl.pallas_call(
        functools.partial(_q_attention_kernel, hd=d // XATTN_HEADS), grid=(n // tm,),
        in_specs=[pl.BlockSpec((tm, kd), lambda i: (i, 0)), pl.BlockSpec((1, kd), lambda i: (0, 0)),
                  pl.BlockSpec((None, kd, d), lambda i: (layer, 0, 0), pipeline_mode=pl.Buffered(1)),
                  pl.BlockSpec((m, d), lambda i: (0, 0)), pl.BlockSpec((m, d), lambda i: (0, 0))],
        out_specs=pl.BlockSpec((tm, d), lambda i: (i, 0)), out_shape=jax.ShapeDtypeStruct((n, d), BF),
        scratch_shapes=[pltpu.VMEM((kd, d), BF), pltpu.VMEM((tm, d), BF)],
        compiler_params=_cparams(("arbitrary",), 52), name="q_attention",
    )(x, gain.reshape(1, kd), wq, k, v)


def _mm_res_kernel(*refs, nparts):
    xs = refs[:nparts]
    w_ref, res_ref, o_ref, wb_ref = refs[nparts:]

    @pl.when(pl.program_id(0) == 0)
    def _():
        _fill_bf16(wb_ref, w_ref)

    acc = res_ref[...]
    k0 = 0
    for x_ref in xs:
        kp = x_ref.shape[1]
        acc = acc + jnp.dot(x_ref[...], wb_ref[k0:k0 + kp, :], preferred_element_type=F32)
        k0 += kp
    o_ref[...] = acc


def matmul_residual(xs, w, res, layer=None):
    n, d = res.shape
    w, li = _stacked(w, layer)
    k = w.shape[1]
    tm = min(OUT_ROW_TILE, n)
    in_specs = [pl.BlockSpec((tm, x.shape[1]), lambda i: (i, 0)) for x in xs]
    in_specs += [pl.BlockSpec((None, k, d), lambda i: (li, 0, 0), pipeline_mode=pl.Buffered(1)),
                 pl.BlockSpec((tm, d), lambda i: (i, 0))]
    return pl.pallas_call(
        functools.partial(_mm_res_kernel, nparts=len(xs)), grid=(n // tm,), in_specs=in_specs,
        out_specs=pl.BlockSpec((tm, d), lambda i: (i, 0)), out_shape=jax.ShapeDtypeStruct((n, d), F32),
        scratch_shapes=[pltpu.VMEM((k, d), BF)],
        compiler_params=_cparams(("arbitrary",), 52), name="matmul_residual",
    )(*xs, w, res)


def _pool_kernel(cur_ref, prev_ref, w_ref, sc_ref, o_ref, *, tp, gdim):
    i = pl.program_id(0)
    dist = lax.broadcasted_iota(jnp.int32, (tp, tp), 0) - lax.broadcasted_iota(jnp.int32, (tp, tp), 1)
    distp = (lax.broadcasted_iota(jnp.int32, (tp, POOL_HALO), 0) + POOL_HALO
             - lax.broadcasted_iota(jnp.int32, (tp, POOL_HALO), 1))
    pos = i * tp + lax.broadcasted_iota(jnp.int32, (tp, 1), 0)
    for j, win in enumerate(POOL_WINDOWS):
        gs = slice(j * gdim, (j + 1) * gdim)
        cur = cur_ref[:, gs]
        band = jnp.where(dist >= 0, jnp.where(dist < win, 1.0, 0.0), 0.0).astype(BF)
        bandp = jnp.where(distp < jnp.where(i > 0, win, 0), 1.0, 0.0).astype(BF)
        s = (jnp.dot(band, cur, preferred_element_type=F32)
             + jnp.dot(bandp, prev_ref[:, gs], preferred_element_type=F32))
        cnt = jnp.minimum(pos + 1, win).astype(F32)
        d = s / cnt - cur.astype(F32)
        y = jnp.dot(d.astype(BF), w_ref[j].astype(BF), preferred_element_type=F32) * sc_ref[:, gs]
        o_ref[:, gs] = y.astype(o_ref.dtype)


def pool_mixer(z, pool_w, pool_scale):
    n = z.shape[0]
    gdim = pool_w.shape[-1]
    width = N_POOL_GROUPS * gdim
    tp = min(POOL_ROWS, n)
    halo_blocks = tp // POOL_HALO
    return pl.pallas_call(
        functools.partial(_pool_kernel, tp=tp, gdim=gdim), grid=(n // tp,),
        in_specs=[
            pl.BlockSpec((tp, width), lambda i: (i, 0)),
            pl.BlockSpec((POOL_HALO, width), lambda i: (jnp.maximum(i * halo_blocks - 1, 0), 0)),
            pl.BlockSpec((N_POOL_GROUPS, gdim, gdim), lambda i: (0, 0, 0)),
            pl.BlockSpec((1, width), lambda i: (0, 0)),
        ],
        out_specs=pl.BlockSpec((tp, width), lambda i: (i, 0)),
        out_shape=jax.ShapeDtypeStruct((n, width), BF),
        compiler_params=_cparams(("parallel",), 32), name="pool_mixer",
    )(z, z, pool_w, pool_scale.reshape(1, width))


def _mlstm_kernel(q_ref, k_ref, v_ref, o_ref, g_ref, gain_ref, y_ref, ct_ref, m_ref, *, chunk, dh):
    c = pl.program_id(0)

    @pl.when(c == 0)
    def _():
        ct_ref[...] = jnp.zeros_like(ct_ref)
        m_ref[...] = jnp.zeros_like(m_ref)

    g = g_ref[...]
    lf = _log_sigmoid(g)
    row = lax.broadcasted_iota(jnp.int32, (chunk, chunk), 0)
    col = lax.broadcasted_iota(jnp.int32, (chunk, chunk), 1)
    causal = col <= row
    ltri = jnp.where(causal, 1.0, 0.0).astype(BF)
    hi = lf.astype(BF)
    r1 = lf - hi.astype(F32)
    mid = r1.astype(BF)
    lo = (r1 - mid.astype(F32)).astype(BF)
    bcum = (jnp.dot(ltri, hi, preferred_element_type=F32) + jnp.dot(ltri, mid, preferred_element_type=F32)
            + jnp.dot(ltri, lo, preferred_element_type=F32))
    g_t = g.T
    b_t = bcum.T
    ones_col = jnp.where(lax.broadcasted_iota(jnp.int32, (chunk, LANES), 1) == 0, 1.0, 0.0).astype(BF)

    for h in range(MLSTM_HEADS):
        hs = slice(h * dh, (h + 1) * dh)
        fl = FORGET_LANE0 + h
        bc = bcum[:, fl:fl + 1]
        br = b_t[fl:fl + 1, :]
        ir = g_t[h:h + 1, :]
        b_last = bcum[chunk - 1:chunk, fl:fl + 1]
        m_prev = m_ref[h][:, 0:1]

        dmat = jnp.where(causal, bc + (ir - br), -jnp.inf)
        inter = bc + m_prev
        m_t = jnp.maximum(jnp.max(dmat, axis=1, keepdims=True), inter)
        w_inter = jnp.exp(inter - m_t)
        p = jnp.exp(dmat - m_t)

        qh = q_ref[:, hs] * (dh ** -0.5)
        kh = k_ref[:, hs]
        v_aug = jnp.concatenate([v_ref[:, hs], ones_col], axis=1)
        s = lax.dot_general(qh, kh, (((1,), (1,)), ((), ())), preferred_element_type=F32)
        sc = (s * p).astype(BF)
        ct = ct_ref[h]
        num_aug = (w_inter * jnp.dot(qh, ct.astype(BF), preferred_element_type=F32)
                   + jnp.dot(sc, v_aug, preferred_element_type=F32))
        num = num_aug[:, :dh]
        den = num_aug[:, dh:dh + 1]
        hout = num / jnp.maximum(jnp.abs(den), jnp.exp(-m_t))

        yn = _rms(hout, gain_ref[:, hs])
        y_ref[:, hs] = (_sigmoid(o_ref[:, hs].astype(F32)) * yn).astype(y_ref.dtype)

        d_end = b_last - br + ir
        m_new = jnp.maximum(b_last + m_prev, jnp.max(d_end, axis=1, keepdims=True))
        a_prev = jnp.exp(b_last + m_prev - m_new)
        a_s = jnp.exp(d_end - m_new)
        k_t = (kh.astype(F32).T * a_s).astype(BF)
        ct_ref[h] = a_prev * ct + jnp.dot(k_t, v_aug, preferred_element_type=F32)
        m_ref[h] = jnp.broadcast_to(m_new, (1, LANES))


def mlstm_mixer(z, gates, head_gain, col0):
    n = z.shape[0]
    width = head_gain.shape[0]
    dh = width // MLSTM_HEADS
    chunk = min(MLSTM_CHUNK, n)
    base = col0 // width
    qkvo = [pl.BlockSpec((chunk, width), lambda c, p=p: (c, base + p)) for p in range(4)]
    return pl.pallas_call(
        functools.partial(_mlstm_kernel, chunk=chunk, dh=dh), grid=(n // chunk,),
        in_specs=qkvo + [pl.BlockSpec((chunk, LANES), lambda c: (c, 0)), pl.BlockSpec((1, width), lambda c: (0, 0))],
        out_specs=pl.BlockSpec((chunk, width), lambda c: (c, 0)),
        out_shape=jax.ShapeDtypeStruct((n, width), BF),
        scratch_shapes=[pltpu.VMEM((MLSTM_HEADS, dh, dh + LANES), F32), pltpu.VMEM((MLSTM_HEADS, 1, LANES), F32)],
        compiler_params=_cparams(("arbitrary",), 32), name="mlstm_mixer",
    )(z, z, z, z, gates, head_gain.reshape(1, width))


def _gated_conv(b, c, u, cp, up, w, first_block):
    zc = c.astype(F32) * u.astype(F32)
    zp = jnp.where(first_block, 0.0, cp.astype(F32) * up.astype(F32))
    row = lax.broadcasted_iota(jnp.int32, zc.shape, 0)
    acc = w[CONV_WIDTH - 1:CONV_WIDTH, :] * zc
    for back in range(1, CONV_WIDTH):
        shifted = pltpu.roll(zc, back, 0)
        for r in range(back):
            shifted = jnp.where(row == r, zp[CONV_HALO - back + r:CONV_HALO - back + r + 1, :], shifted)
        acc = acc + w[CONV_WIDTH - 1 - back:CONV_WIDTH - back, :] * shifted
    return b.astype(F32) * acc


def _conv_mm_res_kernel(b_ref, c_ref, u_ref, cp_ref, up_ref, cw_ref, w_ref, res_ref, o_ref, wb_ref, xs_ref):
    i = pl.program_id(0)

    @pl.when(i == 0)
    def _():
        _fill_bf16(wb_ref, w_ref)

    for c0 in range(0, xs_ref.shape[1], CONV_COLS):
        cs = slice(c0, c0 + CONV_COLS)
        xs_ref[:, cs] = _gated_conv(b_ref[:, cs], c_ref[:, cs], u_ref[:, cs], cp_ref[:, cs], up_ref[:, cs],
                                    cw_ref[:, cs], i == 0).astype(BF)
    o_ref[...] = res_ref[...] + jnp.dot(xs_ref[...], wb_ref[...], preferred_element_type=F32)


def conv_matmul_residual(z, conv_w, w, res, layer):
    n, d = res.shape
    tm = min(CONV_ROWS, n)
    halo_blocks = tm // CONV_HALO
    cur = lambda part: pl.BlockSpec((tm, d), lambda i: (i, part))
    prev = lambda part: pl.BlockSpec((CONV_HALO, d), lambda i: (jnp.maximum(i * halo_blocks - 1, 0), part))
    return pl.pallas_call(
        _conv_mm_res_kernel, grid=(n // tm,),
        in_specs=[cur(0), cur(1), cur(2), prev(1), prev(2), pl.BlockSpec((CONV_WIDTH, d), lambda i: (0, 0)),
                  pl.BlockSpec((None, d, d), lambda i: (layer, 0, 0), pipeline_mode=pl.Buffered(1)),
                  pl.BlockSpec((tm, d), lambda i: (i, 0))],
        out_specs=pl.BlockSpec((tm, d), lambda i: (i, 0)), out_shape=jax.ShapeDtypeStruct((n, d), F32),
        scratch_shapes=[pltpu.VMEM((d, d), BF), pltpu.VMEM((tm, d), BF)],
        compiler_params=_cparams(("arbitrary",), 52), name="conv_matmul_residual",
    )(z, z, z, z, z, conv_w, w, res)


def _router_kernel(x_ref, g_ref, wr_ref, br_ref, xn_ref, ids_ref, wts_ref, cnt_ref, carry_ref, whi_ref, wlo_ref,
                   *, tr):
    @pl.when(pl.program_id(0) == 0)
    def _():
        carry_ref[...] = jnp.zeros_like(carry_ref)

    @pl.when(pl.program_id(0) == 0)
    def _():
        w = wr_ref[...]
        w_hi = w.astype(BF)
        whi_ref[...] = w_hi
        wlo_ref[...] = (w - w_hi.astype(F32)).astype(BF)

    xn = _rms(x_ref[...], g_ref[...])
    xn_ref[...] = _pack_halves(xn)
    x_hi = xn.astype(BF)
    x_lo = (xn - x_hi.astype(F32)).astype(BF)
    logits = (jnp.dot(x_hi, whi_ref[...], preferred_element_type=F32)
              + jnp.dot(x_lo, whi_ref[...], preferred_element_type=F32)
              + jnp.dot(x_hi, wlo_ref[...], preferred_element_type=F32)) + br_ref[...]
    lane = lax.broadcasted_iota(jnp.int32, (tr, LANES), 1).astype(F32)
    neg = -jnp.inf

    def first_argmax(vals):
        top = jnp.max(vals, axis=-1, keepdims=True)
        return top, jnp.min(jnp.where(vals == top, lane, float(LANES)), axis=-1, keepdims=True)

    gl = jnp.where(lane < N_GROUPS, logits, neg)
    gmax, grp = first_argmax(gl)
    g_prob = 1.0 / jnp.sum(jnp.exp(gl - gmax), axis=-1, keepdims=True)
    lo = N_GROUPS + EXPERTS_PER_GROUP * grp
    el = jnp.where(lane >= lo, jnp.where(lane < lo + EXPERTS_PER_GROUP, logits, neg), neg)
    v1, l1 = first_argmax(el)
    v2, l2 = first_argmax(jnp.where(lane == l1, neg, el))
    e2 = jnp.exp(v2 - v1)
    w1 = g_prob / (1.0 + e2)
    w2 = g_prob * e2 / (1.0 + e2)
    hot1 = lane == l1
    hot2 = lane == l2
    hot = jnp.where(hot1, 1.0, jnp.where(hot2, 1.0, 0.0))
    earlier = (lax.broadcasted_iota(jnp.int32, (tr, tr), 1) < lax.broadcasted_iota(jnp.int32, (tr, tr), 0))
    before = jnp.dot(jnp.where(earlier, 1.0, 0.0).astype(BF), hot.astype(BF), preferred_element_type=F32)
    before = before + carry_ref[0:1, :]
    r1 = jnp.sum(jnp.where(hot1, before, 0.0), axis=-1, keepdims=True)
    r2 = jnp.sum(jnp.where(hot2, before, 0.0), axis=-1, keepdims=True)
    carry_ref[0:1, :] = carry_ref[0:1, :] + jnp.sum(hot, axis=0, keepdims=True)
    ids = jnp.where(lane == 0, l1 - N_GROUPS, jnp.where(lane == 1, l2 - N_GROUPS,
                    jnp.where(lane == 2, r1, jnp.where(lane == 3, r2, 0.0))))
    ids_ref[...] = ids.astype(jnp.int32)
    wts_ref[...] = jnp.where(lane == 0, w1, jnp.where(lane == 1, w2, 0.0))
    cnt_ref[...] = carry_ref[...]


def route(h, gain, w_router, b_router):
    n, d = h.shape
    tr = min(ROUTER_ROWS, n)
    return pl.pallas_call(
        functools.partial(_router_kernel, tr=tr), grid=(n // tr,),
        in_specs=[pl.BlockSpec((tr, d), lambda i: (i, 0)), pl.BlockSpec((1, d), lambda i: (0, 0)),
                  pl.BlockSpec((d, LANES), lambda i: (0, 0)), pl.BlockSpec((1, LANES), lambda i: (0, 0))],
        out_specs=[pl.BlockSpec((tr, d // 2), lambda i: (i, 0)),
                   pl.BlockSpec((tr, LANES), lambda i: (i, 0)), pl.BlockSpec((tr, LANES), lambda i: (i, 0)),
                   pl.BlockSpec((SUBLANES, LANES), lambda i: (0, 0))],
        out_shape=[jax.ShapeDtypeStruct((n, d // 2), jnp.uint32), jax.ShapeDtypeStruct((n, LANES), jnp.int32),
                   jax.ShapeDtypeStruct((n, LANES), F32), jax.ShapeDtypeStruct((SUBLANES, LANES), F32)],
        scratch_shapes=[pltpu.VMEM((SUBLANES, LANES), F32), pltpu.VMEM((d, LANES), BF), pltpu.VMEM((d, LANES), BF)],
        compiler_params=_cparams(("arbitrary",), 40), name="moe_router",
    )(h, gain.reshape(1, d), w_router, b_router)


def _dest_kernel(ids_ref, start_ref, o_ref, *, tr):
    ids = ids_ref[...].astype(F32)
    lane = lax.broadcasted_iota(jnp.int32, (tr, LANES), 1).astype(F32)
    start = start_ref[...]
    rows = []
    for k in range(TOP_K):
        first = jnp.sum(jnp.where(lane == ids[:, k:k + 1], start, 0.0), axis=-1, keepdims=True)
        rows.append(first + ids[:, TOP_K + k:TOP_K + k + 1])
    packed = jnp.where(lane == 0, rows[0], jnp.where(lane == 1, rows[1], 0.0))
    o_ref[...] = packed.T[0:SUBLANES, :].astype(jnp.int32)


def assignment_rows(ids, start_rows):
    n = ids.shape[0]
    tr = min(ROUTER_ROWS, n)
    out = pl.pallas_call(
        functools.partial(_dest_kernel, tr=tr), grid=(n // tr,),
        in_specs=[pl.BlockSpec((tr, LANES), lambda i: (i, 0)), pl.BlockSpec((1, LANES), lambda i: (0, 0))],
        out_specs=pl.BlockSpec((SUBLANES, tr), lambda i: (0, i)),
        out_shape=jax.ShapeDtypeStruct((SUBLANES, n), jnp.int32),
        compiler_params=_cparams(("parallel",), 32), name="moe_assignment_rows",
    )(ids, start_rows)
    return out[:TOP_K]


def _row_copy(src_hbm, row, dst, dst_row, sem):
    return pltpu.make_async_copy(src_hbm.at[pl.ds(row, 1), :], dst.at[pl.ds(dst_row, 1), :], sem)


def _rows_wait(src_hbm, dst, sem):
    pltpu.make_async_copy(src_hbm.at[pl.ds(0, dst.shape[0]), :], dst, sem).wait()


def _expert_weight_copies(w_hbm, layer, expert, stage, wset, sem):
    rows = stage.shape[1] // WEIGHT_DMA_PARTS
    return [pltpu.make_async_copy(w_hbm.at[layer, expert, pl.ds(p * rows, rows), :],
                                  stage.at[wset, pl.ds(p * rows, rows), :], sem.at[wset])
            for p in range(WEIGHT_DMA_PARTS)]


def _stream_expert_weights(b, nact, blk_ref, kin_ref, set_ref, nxt_ref, copies):
    cur = set_ref[b]

    @pl.when(b == 0)
    def _():
        for cp in copies(blk_ref[0], cur):
            cp.start(priority=DMA_QUEUE_BULK)

    @pl.when(jnp.logical_and(b < nact, kin_ref[b] == 0))
    def _():
        for cp in copies(blk_ref[b], cur):
            cp.wait()

        @pl.when(nxt_ref[b] != blk_ref[b])
        def _():
            for cp in copies(nxt_ref[b], 1 - cur):
                cp.start(priority=DMA_QUEUE_BULK)

    return cur


def _moe_up_kernel(blk_ref, kin_ref, set_ref, nxt_ref, nact_ref, rtok_ref, xn_hbm, wg_hbm, wu_hbm, hid_ref,
                   buf, stage_g, stage_u, xb, sem, wsem, *, tb, layer):
    b = pl.program_id(0)
    nact = nact_ref[0]
    nslots = GATHER_AHEAD + 1
    slot = b % nslots
    f = hid_ref.shape[1]
    up_chunks = f // MOE_COL_CHUNK
    rows_per_chunk = tb // up_chunks

    @pl.when(b == 0)
    def _():
        for ahead in range(GATHER_AHEAD):
            def body(r, carry, ahead=ahead):
                _row_copy(xn_hbm, rtok_ref[ahead * tb + r], buf.at[ahead], r,
                          sem.at[ahead]).start(priority=DMA_QUEUE_ROWS)
                return carry

            lax.fori_loop(0, tb, body, 0, unroll=8)

    def copies(e, wset):
        return (_expert_weight_copies(wg_hbm, layer, e, stage_g, wset, wsem.at[0])
                + _expert_weight_copies(wu_hbm, layer, e, stage_u, wset, wsem.at[1]))

    cur = _stream_expert_weights(b, nact, blk_ref, kin_ref, set_ref, nxt_ref, copies)

    @pl.when(b < nact)
    def _():
        _rows_wait(xn_hbm, buf.at[slot], sem.at[slot])
        x_lo, x_hi = _unpack_halves(buf[slot])
        half = x_lo.shape[1]
        xb[:, :half] = x_lo.astype(BF)
        xb[:, half:] = x_hi.astype(BF)
        ahead_slot = (b + GATHER_AHEAD) % nslots
        for c in range(up_chunks):
            for r in range(c * rows_per_chunk, (c + 1) * rows_per_chunk):
                _row_copy(xn_hbm, rtok_ref[(b + GATHER_AHEAD) * tb + r], buf.at[ahead_slot], r,
                          sem.at[ahead_slot]).start(priority=DMA_QUEUE_ROWS)
            cs = slice(c * MOE_COL_CHUNK, (c + 1) * MOE_COL_CHUNK)
            gate = jnp.dot(xb[...], stage_g[cur, :, cs].astype(BF), preferred_element_type=F32)
            up = jnp.dot(xb[...], stage_u[cur, :, cs].astype(BF), preferred_element_type=F32)
            hid_ref[:, cs] = (gate * _sigmoid(gate) * up).astype(hid_ref.dtype)

    @pl.when(b >= nact)
    def _():
        hid_ref[...] = jnp.zeros_like(hid_ref)

    @pl.when(jnp.logical_and(b >= nact, b < nact + GATHER_AHEAD))
    def _():
        _rows_wait(xn_hbm, buf.at[slot], sem.at[slot])


def _moe_down_kernel(blk_ref, kin_ref, set_ref, nxt_ref, nact_ref, hid_ref, wd_hbm, ys_ref, stage_d, wsem, *, layer):
    b = pl.program_id(0)
    nact = nact_ref[0]
    half = ys_ref.shape[1]

    def copies(e, wset):
        return _expert_weight_copies(wd_hbm, layer, e, stage_d, wset, wsem)

    cur = _stream_expert_weights(b, nact, blk_ref, kin_ref, set_ref, nxt_ref, copies)

    @pl.when(b < nact)
    def _():
        for c in range(half // MOE_COL_CHUNK):
            cs = slice(c * MOE_COL_CHUNK, (c + 1) * MOE_COL_CHUNK)
            hs = slice(half + c * MOE_COL_CHUNK, half + (c + 1) * MOE_COL_CHUNK)
            lo = jnp.dot(hid_ref[...], stage_d[cur, :, cs].astype(BF), preferred_element_type=F32)
            hi = jnp.dot(hid_ref[...], stage_d[cur, :, hs].astype(BF), preferred_element_type=F32)
            ys_ref[:, cs] = _pack_halves(jnp.concatenate([lo, hi], axis=1))

    @pl.when(b >= nact)
    def _():
        ys_ref[...] = jnp.zeros_like(ys_ref)


def _combine_kernel(dest_ref, ys_hbm, h_ref, w_ref, gain_ref, o_ref, buf, sem, *, tc, n, final_norm):
    i = pl.program_id(0)

    def issue(blk, slot):
        base = blk * tc

        def body(r, carry):
            for k in range(TOP_K):
                _row_copy(ys_hbm, dest_ref[k * n + base + r], buf.at[slot], k * tc + r,
                          sem.at[slot]).start(priority=k)
            return carry

        lax.fori_loop(0, tc, body, 0, unroll=8)

    @pl.when(i == 0)
    def _():
        issue(0, 0)

    @pl.when(i + 1 < pl.num_programs(0))
    def _():
        issue(i + 1, (i + 1) % 2)

    slot = i % 2
    _rows_wait(ys_hbm, buf.at[slot], sem.at[slot])
    a_lo, a_hi = _unpack_halves(buf[slot, 0:tc, :])
    b_lo, b_hi = _unpack_halves(buf[slot, tc:TOP_K * tc, :])
    w0 = w_ref[:, 0:1]
    w1 = w_ref[:, 1:2]
    out = h_ref[...] + jnp.concatenate([w0 * a_lo + w1 * b_lo, w0 * a_hi + w1 * b_hi], axis=1)
    if final_norm:
        out = _rms(out, gain_ref[...])
    o_ref[...] = out


def hier_moe(h, ffn_gain, wg_r, bg_r, we_r, be_r, w_gate, w_up, w_down, layer, final_gain=None):
    n, d = h.shape
    f = w_gate.shape[-1]
    a = n * TOP_K
    tb = _moe_block_rows(a)
    nb = a // tb + N_EXPERTS + GATHER_AHEAD

    pad = LANES - N_GROUPS - N_EXPERTS
    w_router = jnp.concatenate([wg_r, we_r, jnp.zeros((d, pad), F32)], axis=1)
    b_router = jnp.concatenate([bg_r, be_r, jnp.zeros((pad,), F32)]).reshape(1, LANES)
    xn, ids, wts, cnt = route(h, ffn_gain, w_router, b_router)

    counts = cnt[0, N_GROUPS:N_GROUPS + N_EXPERTS].astype(jnp.int32)
    nblk = (counts + tb - 1) // tb
    bend = jnp.cumsum(nblk)
    nact = bend[-1]
    start_rows = jnp.pad(((bend - nblk) * tb).astype(F32), (0, LANES - N_EXPERTS)).reshape(1, LANES)
    dest = assignment_rows(ids, start_rows).reshape(a)
    blk = jnp.minimum(jnp.arange(nb, dtype=jnp.int32), jnp.maximum(nact - 1, 0))
    owner = lambda bi: jnp.minimum(jnp.sum((bend[None, :] <= bi[:, None]).astype(jnp.int32), axis=1), N_EXPERTS - 1)
    blk_e = owner(blk)
    k_in_e = blk - (bend - nblk)[blk_e]
    following = lambda e: jnp.where(bend[e] < nact, owner(bend[e]), e)
    nxt_e = following(blk_e)
    wset = (jnp.cumsum((k_in_e == 0).astype(jnp.int32)) - 1) % 2
    tok = jnp.tile(jnp.arange(n, dtype=jnp.int32), TOP_K)
    row_tok = (jnp.arange(nb * tb, dtype=jnp.int32) % n).at[dest].set(tok)
    nact1 = nact.reshape(1).astype(jnp.int32)

    any_space = pl.BlockSpec(memory_space=pl.ANY)
    hid = pl.pallas_call(
        functools.partial(_moe_up_kernel, tb=tb, layer=layer),
        grid_spec=pltpu.PrefetchScalarGridSpec(
            num_scalar_prefetch=6, grid=(nb,),
            in_specs=[any_space, any_space, any_space],
            out_specs=pl.BlockSpec((tb, f), lambda b, *_: (b, 0)),
            scratch_shapes=[pltpu.VMEM((GATHER_AHEAD + 1, tb, d // 2), jnp.uint32),
                            pltpu.VMEM((2, d, f), F32), pltpu.VMEM((2, d, f), F32), pltpu.VMEM((tb, d), BF),
                            pltpu.SemaphoreType.DMA((GATHER_AHEAD + 1,)), pltpu.SemaphoreType.DMA((2, 2))]),
        out_shape=jax.ShapeDtypeStruct((nb * tb, f), BF),
        compiler_params=_cparams(("arbitrary",), 56), name="moe_up",
    )(blk_e, k_in_e, wset, nxt_e, nact1, row_tok, xn, w_gate, w_up)

    ys = pl.pallas_call(
        functools.partial(_moe_down_kernel, layer=layer),
        grid_spec=pltpu.PrefetchScalarGridSpec(
            num_scalar_prefetch=5, grid=(nb,),
            in_specs=[pl.BlockSpec((tb, f), lambda b, *_: (b, 0)), any_space],
            out_specs=pl.BlockSpec((tb, d // 2), lambda b, *_: (b, 0)),
            scratch_shapes=[pltpu.VMEM((2, f, d), F32), pltpu.SemaphoreType.DMA((2,))]),
        out_shape=jax.ShapeDtypeStruct((nb * tb, d // 2), jnp.uint32),
        compiler_params=_cparams(("arbitrary",), 40), name="moe_down",
    )(blk_e, k_in_e, wset, nxt_e, nact1, hid, w_down)

    tc = min(COMBINE_ROWS, n)
    gain = (final_gain if final_gain is not None else ffn_gain).reshape(1, d)
    return pl.pallas_call(
        functools.partial(_combine_kernel, tc=tc, n=n, final_norm=final_gain is not None),
        grid_spec=pltpu.PrefetchScalarGridSpec(
            num_scalar_prefetch=1, grid=(n // tc,),
            in_specs=[pl.BlockSpec(memory_space=pl.ANY), pl.BlockSpec((tc, d), lambda i, *_: (i, 0)),
                      pl.BlockSpec((tc, LANES), lambda i, *_: (i, 0)), pl.BlockSpec((1, d), lambda i, *_: (0, 0))],
            out_specs=pl.BlockSpec((tc, d), lambda i, *_: (i, 0)),
            scratch_shapes=[pltpu.VMEM((2, TOP_K * tc, d // 2), jnp.uint32), pltpu.SemaphoreType.DMA((2,))]),
        out_shape=jax.ShapeDtypeStruct((n, d), F32),
        compiler_params=_cparams(("arbitrary",), 40), name="moe_combine",
    )(dest, ys, h, wts, gain)


def even_mixer(h, gain, w_in, b_gates, pool_w, pool_scale, head_gain, w_out, j):
    d = h.shape[1]
    pool_width = N_POOL_GROUPS * pool_w.shape[-1]
    mlstm_width = head_gain.shape[0]
    main_cols = pool_width + 4 * mlstm_width
    n_gates = 2 * MLSTM_HEADS
    gate_b = jnp.pad(b_gates, (0, LANES - n_gates)).reshape(1, LANES)
    z, gates = norm_matmul_gates_t(h, gain, jnp.swapaxes(w_in, 1, 2), main_cols, n_gates, gate_b, j)
    y_p = pool_mixer(z, pool_w, pool_scale)
    y_m = mlstm_mixer(z, gates, head_gain, pool_width)
    assert pool_width == mlstm_width and pool_width + mlstm_width == d
    return matmul_residual([y_p, y_m], w_out, h, layer=j)


def odd_mixer(h, gain, w_in, conv_w, w_out, j):
    z = norm_matmul(h, gain, w_in, w_in.shape[2], layer=j)
    return conv_matmul_residual(z, conv_w, w_out, h, j)


def cross_attn(h, mem, gain, mem_gain, wq, wk, wv, wo, layer):
    d = h.shape[1]
    k = norm_matmul(mem, mem_gain, wk, d, layer=layer)
    v = norm_matmul(mem, mem_gain, wv, d, layer=layer)
    return matmul_residual([q_attention(h, gain, wq, k, v, layer)], wo, h, layer=layer)


def kernel(x, mem, mix_norm, xattn_norm, mem_norm, ffn_norm, final_norm, ev_w_in, ev_b_gates, ev_pool_w, ev_pool_scale, ev_head_norm, ev_w_out, od_w_in, od_conv_w, od_w_out, xa_wq, xa_wk, xa_wv, xa_wo, rt_group_w, rt_group_b, rt_expert_w, rt_expert_b, ex_w_gate, ex_w_up, ex_w_down):
    depth = mix_norm.shape[0]
    h = x[0]
    m = mem[0]
    for layer in range(depth):
        j = layer // 2
        if layer % 2 == 0:
            h = even_mixer(h, mix_norm[layer], ev_w_in, ev_b_gates[j], ev_pool_w[j], ev_pool_scale[j],
                           ev_head_norm[j], ev_w_out, j)
        else:
            h = odd_mixer(h, mix_norm[layer], od_w_in, od_conv_w[j], od_w_out, j)
        h = cross_attn(h, m, xattn_norm[layer], mem_norm[layer], xa_wq, xa_wk, xa_wv, xa_wo, layer)
        h = hier_moe(h, ffn_norm[layer], rt_group_w[layer], rt_group_b[layer], rt_expert_w[layer],
                     rt_expert_b[layer], ex_w_gate, ex_w_up, ex_w_down, layer,
                     final_gain=final_norm if layer == depth - 1 else None)
    return h[None]
```

```python
import functools

import jax
import jax.numpy as jnp
from jax import lax
from jax.experimental import pallas as pl
from jax.experimental.pallas import tpu as pltpu

F32 = jnp.float32
BF = jnp.bfloat16
EPS = 1e-6

POOL_WINDOWS = (2, 4, 8, 16)
N_POOL_GROUPS = 4
MLSTM_HEADS = 4
FORGET_LANE0 = MLSTM_HEADS
XATTN_HEADS = 4
N_GROUPS = 4
EXPERTS_PER_GROUP = 8
N_EXPERTS = N_GROUPS * EXPERTS_PER_GROUP
TOP_K = 2
CONV_WIDTH = 3

LANES = 128
SUBLANES = 8
DMA_QUEUE_ROWS = 0
DMA_QUEUE_BULK = 1

ROW_TILE = 1024
COL_TILE = 1024
OUT_ROW_TILE = 512
CAST_ROWS = 64
CAST_UNROLL = 4
MOE_COL_CHUNK = 256
WEIGHT_DMA_PARTS = 4
MLSTM_CHUNK = 256
POOL_ROWS = 256
POOL_HALO = 128
CONV_ROWS = 256
CONV_COLS = 512
CONV_HALO = 16
ROUTER_ROWS = 512
GATHER_AHEAD = 2
COMBINE_ROWS = 256


def _moe_block_rows(assignments):
    mean_rows = assignments // N_EXPERTS
    return -(-(mean_rows * 9 // 16) // 16) * 16


def _cparams(semantics, vmem_mib):
    return pltpu.CompilerParams(dimension_semantics=semantics, vmem_limit_bytes=vmem_mib * 1024 * 1024)


def _sigmoid(x):
    return 1.0 / (1.0 + jnp.exp(-x))


def _log_sigmoid(x):
    return jnp.minimum(x, 0.0) - jnp.log(1.0 + jnp.exp(-jnp.abs(x)))


def _rms(x, g):
    ms = jnp.mean(x * x, axis=-1, keepdims=True)
    return x * lax.rsqrt(ms + EPS) * g


def _pack_halves(x):
    half = x.shape[1] // 2
    lo = pltpu.bitcast(x[:, :half].astype(BF).astype(F32), jnp.uint32)
    hi = pltpu.bitcast(x[:, half:].astype(BF).astype(F32), jnp.uint32)
    return hi | lax.shift_right_logical(lo, jnp.uint32(16))


def _unpack_halves(w):
    lo = pltpu.bitcast(lax.shift_left(w, jnp.uint32(16)), F32)
    hi = pltpu.bitcast(w & jnp.uint32(0xFFFF0000), F32)
    return lo, hi


def _norm_mm_kernel(x_ref, g_ref, w_ref, o_ref, xn_ref):
    @pl.when(pl.program_id(1) == 0)
    def _():
        xn_ref[...] = _rms(x_ref[...], g_ref[...]).astype(BF)

    o_ref[...] = jnp.dot(xn_ref[...], w_ref[...].astype(BF), preferred_element_type=F32).astype(o_ref.dtype)


_NT = (((1,), (1,)), ((), ()))


def _norm_mm_gates_t_kernel(x_ref, g_ref, wt_ref, wgt_ref, bg_ref, o_ref, gates_ref, xn_ref):
    @pl.when(pl.program_id(1) == 0)
    def _():
        xn = _rms(x_ref[...], g_ref[...]).astype(BF)
        xn_ref[...] = xn
        wg = wgt_ref[...]
        wg = jnp.concatenate([wg, jnp.zeros((LANES - wg.shape[0], wg.shape[1]), F32)], axis=0).astype(BF)
        gates_ref[...] = lax.dot_general(xn, wg, _NT, preferred_element_type=F32) + bg_ref[...]

    o_ref[...] = lax.dot_general(xn_ref[...], wt_ref[...].astype(BF), _NT,
                                 preferred_element_type=F32).astype(o_ref.dtype)


def norm_matmul_gates_t(x, gain, wt, n_cols, n_gates, gate_b, layer):
    n, k = x.shape
    tm = min(ROW_TILE, n)
    tn = COL_TILE
    return pl.pallas_call(
        _norm_mm_gates_t_kernel, grid=(n // tm, n_cols // tn),
        in_specs=[pl.BlockSpec((tm, k), lambda i, j: (i, 0)), pl.BlockSpec((1, k), lambda i, j: (0, 0)),
                  pl.BlockSpec((None, tn, k), lambda i, j: (layer, j, 0)),
                  pl.BlockSpec((None, n_gates, k), lambda i, j: (layer, n_cols // n_gates, 0)),
                  pl.BlockSpec((1, LANES), lambda i, j: (0, 0))],
        out_specs=[pl.BlockSpec((tm, tn), lambda i, j: (i, j)), pl.BlockSpec((tm, LANES), lambda i, j: (i, 0))],
        out_shape=[jax.ShapeDtypeStruct((n, n_cols), BF), jax.ShapeDtypeStruct((n, LANES), F32)],
        scratch_shapes=[pltpu.VMEM((tm, k), BF)],
        compiler_params=_cparams(("parallel", "arbitrary"), 56), name="norm_matmul_gates",
    )(x, gain.reshape(1, k), wt, wt, gate_b)


def _stacked(w, layer):
    return (w[None], 0) if layer is None else (w, layer)


def norm_matmul(x, gain, w, n_cols, layer=None):
    n, k = x.shape
    w, li = _stacked(w, layer)
    tm = min(ROW_TILE, n)
    tn = COL_TILE
    return pl.pallas_call(
        _norm_mm_kernel, grid=(n // tm, n_cols // tn),
        in_specs=[pl.BlockSpec((tm, k), lambda i, j: (i, 0)), pl.BlockSpec((1, k), lambda i, j: (0, 0)),
                  pl.BlockSpec((None, k, tn), lambda i, j: (li, 0, j))],
        out_specs=pl.BlockSpec((tm, tn), lambda i, j: (i, j)),
        out_shape=jax.ShapeDtypeStruct((n, n_cols), BF), scratch_shapes=[pltpu.VMEM((tm, k), BF)],
        compiler_params=_cparams(("parallel", "arbitrary"), 56), name="norm_matmul",
    )(x, gain.reshape(1, k), w)


def _fill_bf16(dst_ref, src_ref):
    def body(i, carry):
        r = pl.multiple_of(i * CAST_ROWS, CAST_ROWS)
        dst_ref[pl.ds(r, CAST_ROWS), :] = src_ref[pl.ds(r, CAST_ROWS), :].astype(BF)
        return carry

    lax.fori_loop(0, src_ref.shape[0] // CAST_ROWS, body, 0, unroll=CAST_UNROLL)


def _q_attention_kernel(x_ref, g_ref, w_ref, k_ref, v_ref, o_ref, wb_ref, q_ref, *, hd):
    @pl.when(pl.program_id(0) == 0)
    def _():
        _fill_bf16(wb_ref, w_ref)

    xn = _rms(x_ref[...], g_ref[...]).astype(BF)
    q_ref[...] = jnp.dot(xn, wb_ref[...], preferred_element_type=F32).astype(BF)
    scale = hd ** -0.5
    for h in range(XATTN_HEADS):
        hs = slice(h * hd, (h + 1) * hd)
        s = lax.dot_general(q_ref[:, hs], k_ref[:, hs], _NT, preferred_element_type=F32) * scale
        e = jnp.exp(s - jnp.max(s, axis=-1, keepdims=True))
        p = (e / jnp.sum(e, axis=-1, keepdims=True)).astype(BF)
        o_ref[:, hs] = jnp.dot(p, v_ref[:, hs], preferred_element_type=F32).astype(o_ref.dtype)


def q_attention(x, gain, wq, k, v, layer):
    n, kd = x.shape
    d = wq.shape[2]
    m = k.shape[0]
    tm = min(OUT_ROW_TILE, n)
    return pl.pallas_call(
        functools.partial(_q_attention_kernel, hd=d // XATTN_HEADS), grid=(n // tm,),
        in_specs=[pl.BlockSpec((tm, kd), lambda i: (i, 0)), pl.BlockSpec((1, kd), lambda i: (0, 0)),
                  pl.BlockSpec((None, kd, d), lambda i: (layer, 0, 0), pipeline_mode=pl.Buffered(1)),
                  pl.BlockSpec((m, d), lambda i: (0, 0)), pl.BlockSpec((m, d), lambda i: (0, 0))],
        out_specs=pl.BlockSpec((tm, d), lambda i: (i, 0)), out_shape=jax.ShapeDtypeStruct((n, d), BF),
        scratch_shapes=[pltpu.VMEM((kd, d), BF), pltpu.VMEM((tm, d), BF)],
        compiler_params=_cparams(("arbitrary",), 52), name="q_attention",
    )(x, gain.reshape(1, kd), wq, k, v)


def _mm_res_kernel(*refs, nparts):
    xs = refs[:nparts]
    w_ref, res_ref, o_ref, wb_ref = refs[nparts:]

    @pl.when(pl.program_id(0) == 0)
    def _():
        _fill_bf16(wb_ref, w_ref)

    acc = res_ref[...]
    k0 = 0
    for x_ref in xs:
        kp = x_ref.shape[1]
        acc = acc + jnp.dot(x_ref[...], wb_ref[k0:k0 + kp, :], preferred_element_type=F32)
        k0 += kp
    o_ref[...] = acc


def matmul_residual(xs, w, res, layer=None):
    n, d = res.shape
    w, li = _stacked(w, layer)
    k = w.shape[1]
    tm = min(OUT_ROW_TILE, n)
    in_specs = [pl.BlockSpec((tm, x.shape[1]), lambda i: (i, 0)) for x in xs]
    in_specs += [pl.BlockSpec((None, k, d), lambda i: (li, 0, 0), pipeline_mode=pl.Buffered(1)),
                 pl.BlockSpec((tm, d), lambda i: (i, 0))]
    return pl.pallas_call(
        functools.partial(_mm_res_kernel, nparts=len(xs)), grid=(n // tm,), in_specs=in_specs,
        out_specs=pl.BlockSpec((tm, d), lambda i: (i, 0)), out_shape=jax.ShapeDtypeStruct((n, d), F32),
        scratch_shapes=[pltpu.VMEM((k, d), BF)],
        compiler_params=_cparams(("arbitrary",), 52), name="matmul_residual",
    )(*xs, w, res)


def _pool_kernel(cur_ref, prev_ref, w_ref, sc_ref, o_ref, *, tp, gdim):
    i = pl.program_id(0)
    dist = lax.broadcasted_iota(jnp.int32, (tp, tp), 0) - lax.broadcasted_iota(jnp.int32, (tp, tp), 1)
    distp = (lax.broadcasted_iota(jnp.int32, (tp, POOL_HALO), 0) + POOL_HALO
             - lax.broadcasted_iota(jnp.int32, (tp, POOL_HALO), 1))
    pos = i * tp + lax.broadcasted_iota(jnp.int32, (tp, 1), 0)
    for j, win in enumerate(POOL_WINDOWS):
        gs = slice(j * gdim, (j + 1) * gdim)
        cur = cur_ref[:, gs]
        band = jnp.where(dist >= 0, jnp.where(dist < win, 1.0, 0.0), 0.0).astype(BF)
        bandp = jnp.where(distp < jnp.where(i > 0, win, 0), 1.0, 0.0).astype(BF)
        s = (jnp.dot(band, cur, preferred_element_type=F32)
             + jnp.dot(bandp, prev_ref[:, gs], preferred_element_type=F32))
        cnt = jnp.minimum(pos + 1, win).astype(F32)
        d = s / cnt - cur.astype(F32)
        y = jnp.dot(d.astype(BF), w_ref[j].astype(BF), preferred_element_type=F32) * sc_ref[:, gs]
        o_ref[:, gs] = y.astype(o_ref.dtype)


def pool_mixer(z, pool_w, pool_scale):
    n = z.shape[0]
    gdim = pool_w.shape[-1]
    width = N_POOL_GROUPS * gdim
    tp = min(POOL_ROWS, n)
    halo_blocks = tp // POOL_HALO
    return pl.pallas_call(
        functools.partial(_pool_kernel, tp=tp, gdim=gdim), grid=(n // tp,),
        in_specs=[
            pl.BlockSpec((tp, width), lambda i: (i, 0)),
            pl.BlockSpec((POOL_HALO, width), lambda i: (jnp.maximum(i * halo_blocks - 1, 0), 0)),
            pl.BlockSpec((N_POOL_GROUPS, gdim, gdim), lambda i: (0, 0, 0)),
            pl.BlockSpec((1, width), lambda i: (0, 0)),
        ],
        out_specs=pl.BlockSpec((tp, width), lambda i: (i, 0)),
        out_shape=jax.ShapeDtypeStruct((n, width), BF),
        compiler_params=_cparams(("parallel",), 32), name="pool_mixer",
    )(z, z, pool_w, pool_scale.reshape(1, width))


def _mlstm_kernel(q_ref, k_ref, v_ref, o_ref, g_ref, gain_ref, y_ref, ct_ref, m_ref, *, chunk, dh):
    c = pl.program_id(0)

    @pl.when(c == 0)
    def _():
        ct_ref[...] = jnp.zeros_like(ct_ref)
        m_ref[...] = jnp.zeros_like(m_ref)

    g = g_ref[...]
    lf = _log_sigmoid(g)
    row = lax.broadcasted_iota(jnp.int32, (chunk, chunk), 0)
    col = lax.broadcasted_iota(jnp.int32, (chunk, chunk), 1)
    causal = col <= row
    ltri = jnp.where(causal, 1.0, 0.0).astype(BF)
    hi = lf.astype(BF)
    r1 = lf - hi.astype(F32)
    mid = r1.astype(BF)
    lo = (r1 - mid.astype(F32)).astype(BF)
    bcum = (jnp.dot(ltri, hi, preferred_element_type=F32) + jnp.dot(ltri, mid, preferred_element_type=F32)
            + jnp.dot(ltri, lo, preferred_element_type=F32))
    g_t = g.T
    b_t = bcum.T
    ones_col = jnp.where(lax.broadcasted_iota(jnp.int32, (chunk, LANES), 1) == 0, 1.0, 0.0).astype(BF)

    for h in range(MLSTM_HEADS):
        hs = slice(h * dh, (h + 1) * dh)
        fl = FORGET_LANE0 + h
        bc = bcum[:, fl:fl + 1]
        br = b_t[fl:fl + 1, :]
        ir = g_t[h:h + 1, :]
        b_last = bcum[chunk - 1:chunk, fl:fl + 1]
        m_prev = m_ref[h][:, 0:1]

        dmat = jnp.where(causal, bc + (ir - br), -jnp.inf)
        inter = bc + m_prev
        m_t = jnp.maximum(jnp.max(dmat, axis=1, keepdims=True), inter)
        w_inter = jnp.exp(inter - m_t)
        p = jnp.exp(dmat - m_t)

        qh = q_ref[:, hs] * (dh ** -0.5)
        kh = k_ref[:, hs]
        v_aug = jnp.concatenate([v_ref[:, hs], ones_col], axis=1)
        s = lax.dot_general(qh, kh, (((1,), (1,)), ((), ())), preferred_element_type=F32)
        sc = (s * p).astype(BF)
        ct = ct_ref[h]
        num_aug = (w_inter * jnp.dot(qh, ct.astype(BF), preferred_element_type=F32)
                   + jnp.dot(sc, v_aug, preferred_element_type=F32))
        num = num_aug[:, :dh]
        den = num_aug[:, dh:dh + 1]
        hout = num / jnp.maximum(jnp.abs(den), jnp.exp(-m_t))

        yn = _rms(hout, gain_ref[:, hs])
        y_ref[:, hs] = (_sigmoid(o_ref[:, hs].astype(F32)) * yn).astype(y_ref.dtype)

        d_end = b_last - br + ir
        m_new = jnp.maximum(b_last + m_prev, jnp.max(d_end, axis=1, keepdims=True))
        a_prev = jnp.exp(b_last + m_prev - m_new)
        a_s = jnp.exp(d_end - m_new)
        k_t = (kh.astype(F32).T * a_s).astype(BF)
        ct_ref[h] = a_prev * ct + jnp.dot(k_t, v_aug, preferred_element_type=F32)
        m_ref[h] = jnp.broadcast_to(m_new, (1, LANES))


def mlstm_mixer(z, gates, head_gain, col0):
    n = z.shape[0]
    width = head_gain.shape[0]
    dh = width // MLSTM_HEADS
    chunk = min(MLSTM_CHUNK, n)
    base = col0 // width
    qkvo = [pl.BlockSpec((chunk, width), lambda c, p=p: (c, base + p)) for p in range(4)]
    return pl.pallas_call(
        functools.partial(_mlstm_kernel, chunk=chunk, dh=dh), grid=(n // chunk,),
        in_specs=qkvo + [pl.BlockSpec((chunk, LANES), lambda c: (c, 0)), pl.BlockSpec((1, width), lambda c: (0, 0))],
        out_specs=pl.BlockSpec((chunk, width), lambda c: (c, 0)),
        out_shape=jax.ShapeDtypeStruct((n, width), BF),
        scratch_shapes=[pltpu.VMEM((MLSTM_HEADS, dh, dh + LANES), F32), pltpu.VMEM((MLSTM_HEADS, 1, LANES), F32)],
        compiler_params=_cparams(("arbitrary",), 32), name="mlstm_mixer",
    )(z, z, z, z, gates, head_gain.reshape(1, width))


def _gated_conv(b, c, u, cp, up, w, first_block):
    zc = c.astype(F32) * u.astype(F32)
    zp = jnp.where(first_block, 0.0, cp.astype(F32) * up.astype(F32))
    row = lax.broadcasted_iota(jnp.int32, zc.shape, 0)
    acc = w[CONV_WIDTH - 1:CONV_WIDTH, :] * zc
    for back in range(1, CONV_WIDTH):
        shifted = pltpu.roll(zc, back, 0)
        for r in range(back):
            shifted = jnp.where(row == r, zp[CONV_HALO - back + r:CONV_HALO - back + r + 1, :], shifted)
        acc = acc + w[CONV_WIDTH - 1 - back:CONV_WIDTH - back, :] * shifted
    return b.astype(F32) * acc


def _conv_mm_res_kernel(b_ref, c_ref, u_ref, cp_ref, up_ref, cw_ref, w_ref, res_ref, o_ref, wb_ref, xs_ref):
    i = pl.program_id(0)

    @pl.when(i == 0)
    def _():
        _fill_bf16(wb_ref, w_ref)

    for c0 in range(0, xs_ref.shape[1], CONV_COLS):
        cs = slice(c0, c0 + CONV_COLS)
        xs_ref[:, cs] = _gated_conv(b_ref[:, cs], c_ref[:, cs], u_ref[:, cs], cp_ref[:, cs], up_ref[:, cs],
                                    cw_ref[:, cs], i == 0).astype(BF)
    o_ref[...] = res_ref[...] + jnp.dot(xs_ref[...], wb_ref[...], preferred_element_type=F32)


def conv_matmul_residual(z, conv_w, w, res, layer):
    n, d = res.shape
    tm = min(CONV_ROWS, n)
    halo_blocks = tm // CONV_HALO
    cur = lambda part: pl.BlockSpec((tm, d), lambda i: (i, part))
    prev = lambda part: pl.BlockSpec((CONV_HALO, d), lambda i: (jnp.maximum(i * halo_blocks - 1, 0), part))
    return pl.pallas_call(
        _conv_mm_res_kernel, grid=(n // tm,),
        in_specs=[cur(0), cur(1), cur(2), prev(1), prev(2), pl.BlockSpec((CONV_WIDTH, d), lambda i: (0, 0)),
                  pl.BlockSpec((None, d, d), lambda i: (layer, 0, 0), pipeline_mode=pl.Buffered(1)),
                  pl.BlockSpec((tm, d), lambda i: (i, 0))],
        out_specs=pl.BlockSpec((tm, d), lambda i: (i, 0)), out_shape=jax.ShapeDtypeStruct((n, d), F32),
        scratch_shapes=[pltpu.VMEM((d, d), BF), pltpu.VMEM((tm, d), BF)],
        compiler_params=_cparams(("arbitrary",), 52), name="conv_matmul_residual",
    )(z, z, z, z, z, conv_w, w, res)


def _router_kernel(x_ref, g_ref, wr_ref, br_ref, xn_ref, ids_ref, wts_ref, cnt_ref, carry_ref, whi_ref, wlo_ref,
                   *, tr):
    @pl.when(pl.program_id(0) == 0)
    def _():
        carry_ref[...] = jnp.zeros_like(carry_ref)

    @pl.when(pl.program_id(0) == 0)
    def _():
        w = wr_ref[...]
        w_hi = w.astype(BF)
        whi_ref[...] = w_hi
        wlo_ref[...] = (w - w_hi.astype(F32)).astype(BF)

    xn = _rms(x_ref[...], g_ref[...])
    xn_ref[...] = _pack_halves(xn)
    x_hi = xn.astype(BF)
    x_lo = (xn - x_hi.astype(F32)).astype(BF)
    logits = (jnp.dot(x_hi, whi_ref[...], preferred_element_type=F32)
              + jnp.dot(x_lo, whi_ref[...], preferred_element_type=F32)
              + jnp.dot(x_hi, wlo_ref[...], preferred_element_type=F32)) + br_ref[...]
    lane = lax.broadcasted_iota(jnp.int32, (tr, LANES), 1).astype(F32)
    neg = -jnp.inf

    def first_argmax(vals):
        top = jnp.max(vals, axis=-1, keepdims=True)
        return top, jnp.min(jnp.where(vals == top, lane, float(LANES)), axis=-1, keepdims=True)

    gl = jnp.where(lane < N_GROUPS, logits, neg)
    gmax, grp = first_argmax(gl)
    g_prob = 1.0 / jnp.sum(jnp.exp(gl - gmax), axis=-1, keepdims=True)
    lo = N_GROUPS + EXPERTS_PER_GROUP * grp
    el = jnp.where(lane >= lo, jnp.where(lane < lo + EXPERTS_PER_GROUP, logits, neg), neg)
    v1, l1 = first_argmax(el)
    v2, l2 = first_argmax(jnp.where(lane == l1, neg, el))
    e2 = jnp.exp(v2 - v1)
    w1 = g_prob / (1.0 + e2)
    w2 = g_prob * e2 / (1.0 + e2)
    hot1 = lane == l1
    hot2 = lane == l2
    hot = jnp.where(hot1, 1.0, jnp.where(hot2, 1.0, 0.0))
    earlier = (lax.broadcasted_iota(jnp.int32, (tr, tr), 1) < lax.broadcasted_iota(jnp.int32, (tr, tr), 0))
    before = jnp.dot(jnp.where(earlier, 1.0, 0.0).astype(BF), hot.astype(BF), preferred_element_type=F32)
    before = before + carry_ref[0:1, :]
    r1 = jnp.sum(jnp.where(hot1, before, 0.0), axis=-1, keepdims=True)
    r2 = jnp.sum(jnp.where(hot2, before, 0.0), axis=-1, keepdims=True)
    carry_ref[0:1, :] = carry_ref[0:1, :] + jnp.sum(hot, axis=0, keepdims=True)
    ids = jnp.where(lane == 0, l1 - N_GROUPS, jnp.where(lane == 1, l2 - N_GROUPS,
                    jnp.where(lane == 2, r1, jnp.where(lane == 3, r2, 0.0))))
    ids_ref[...] = ids.astype(jnp.int32)
    wts_ref[...] = jnp.where(lane == 0, w1, jnp.where(lane == 1, w2, 0.0))
    cnt_ref[...] = carry_ref[...]


def route(h, gain, w_router, b_router):
    n, d = h.shape
    tr = min(ROUTER_ROWS, n)
    return pl.pallas_call(
        functools.partial(_router_kernel, tr=tr), grid=(n // tr,),
        in_specs=[pl.BlockSpec((tr, d), lambda i: (i, 0)), pl.BlockSpec((1, d), lambda i: (0, 0)),
                  pl.BlockSpec((d, LANES), lambda i: (0, 0)), pl.BlockSpec((1, LANES), lambda i: (0, 0))],
        out_specs=[pl.BlockSpec((tr, d // 2), lambda i: (i, 0)),
                   pl.BlockSpec((tr, LANES), lambda i: (i, 0)), pl.BlockSpec((tr, LANES), lambda i: (i, 0)),
                   pl.BlockSpec((SUBLANES, LANES), lambda i: (0, 0))],
        out_shape=[jax.ShapeDtypeStruct((n, d // 2), jnp.uint32), jax.ShapeDtypeStruct((n, LANES), jnp.int32),
                   jax.ShapeDtypeStruct((n, LANES), F32), jax.ShapeDtypeStruct((SUBLANES, LANES), F32)],
        scratch_shapes=[pltpu.VMEM((SUBLANES, LANES), F32), pltpu.VMEM((d, LANES), BF), pltpu.VMEM((d, LANES), BF)],
        compiler_params=_cparams(("arbitrary",), 40), name="moe_router",
    )(h, gain.reshape(1, d), w_router, b_router)


def _dest_kernel(ids_ref, start_ref, o_ref, *, tr):
    ids = ids_ref[...].astype(F32)
    lane = lax.broadcasted_iota(jnp.int32, (tr, LANES), 1).astype(F32)
    start = start_ref[...]
    rows = []
    for k in range(TOP_K):
        first = jnp.sum(jnp.where(lane == ids[:, k:k + 1], start, 0.0), axis=-1, keepdims=True)
        rows.append(first + ids[:, TOP_K + k:TOP_K + k + 1])
    packed = jnp.where(lane == 0, rows[0], jnp.where(lane == 1, rows[1], 0.0))
    o_ref[...] = packed.T[0:SUBLANES, :].astype(jnp.int32)


def assignment_rows(ids, start_rows):
    n = ids.shape[0]
    tr = min(ROUTER_ROWS, n)
    out = pl.pallas_call(
        functools.partial(_dest_kernel, tr=tr), grid=(n // tr,),
        in_specs=[pl.BlockSpec((tr, LANES), lambda i: (i, 0)), pl.BlockSpec((1, LANES), lambda i: (0, 0))],
        out_specs=pl.BlockSpec((SUBLANES, tr), lambda i: (0, i)),
        out_shape=jax.ShapeDtypeStruct((SUBLANES, n), jnp.int32),
        compiler_params=_cparams(("parallel",), 32), name="moe_assignment_rows",
    )(ids, start_rows)
    return out[:TOP_K]


def _row_copy(src_hbm, row, dst, dst_row, sem):
    return pltpu.make_async_copy(src_hbm.at[pl.ds(row, 1), :], dst.at[pl.ds(dst_row, 1), :], sem)


def _rows_wait(src_hbm, dst, sem):
    pltpu.make_async_copy(src_hbm.at[pl.ds(0, dst.shape[0]), :], dst, sem).wait()


def _expert_weight_copies(w_hbm, layer, expert, stage, wset, sem):
    rows = stage.shape[1] // WEIGHT_DMA_PARTS
    return [pltpu.make_async_copy(w_hbm.at[layer, expert, pl.ds(p * rows, rows), :],
                                  stage.at[wset, pl.ds(p * rows, rows), :], sem.at[wset])
            for p in range(WEIGHT_DMA_PARTS)]


def _stream_expert_weights(b, nact, blk_ref, kin_ref, set_ref, nxt_ref, copies):
    cur = set_ref[b]

    @pl.when(b == 0)
    def _():
        for cp in copies(blk_ref[0], cur):
            cp.start(priority=DMA_QUEUE_BULK)

    @pl.when(jnp.logical_and(b < nact, kin_ref[b] == 0))
    def _():
        for cp in copies(blk_ref[b], cur):
            cp.wait()

        @pl.when(nxt_ref[b] != blk_ref[b])
        def _():
            for cp in copies(nxt_ref[b], 1 - cur):
                cp.start(priority=DMA_QUEUE_BULK)

    return cur


def _moe_up_kernel(blk_ref, kin_ref, set_ref, nxt_ref, nact_ref, rtok_ref, xn_hbm, wg_hbm, wu_hbm, hid_ref,
                   buf, stage_g, stage_u, xb, sem, wsem, *, tb, layer):
    b = pl.program_id(0)
    nact = nact_ref[0]
    nslots = GATHER_AHEAD + 1
    slot = b % nslots
    f = hid_ref.shape[1]
    up_chunks = f // MOE_COL_CHUNK
    rows_per_chunk = tb // up_chunks

    @pl.when(b == 0)
    def _():
        for ahead in range(GATHER_AHEAD):
            def body(r, carry, ahead=ahead):
                _row_copy(xn_hbm, rtok_ref[ahead * tb + r], buf.at[ahead], r,
                          sem.at[ahead]).start(priority=DMA_QUEUE_ROWS)
                return carry

            lax.fori_loop(0, tb, body, 0, unroll=8)

    def copies(e, wset):
        return (_expert_weight_copies(wg_hbm, layer, e, stage_g, wset, wsem.at[0])
                + _expert_weight_copies(wu_hbm, layer, e, stage_u, wset, wsem.at[1]))

    cur = _stream_expert_weights(b, nact, blk_ref, kin_ref, set_ref, nxt_ref, copies)

    @pl.when(b < nact)
    def _():
        _rows_wait(xn_hbm, buf.at[slot], sem.at[slot])
        x_lo, x_hi = _unpack_halves(buf[slot])
        half = x_lo.shape[1]
        xb[:, :half] = x_lo.astype(BF)
        xb[:, half:] = x_hi.astype(BF)
        ahead_slot = (b + GATHER_AHEAD) % nslots
        for c in range(up_chunks):
            for r in range(c * rows_per_chunk, (c + 1) * rows_per_chunk):
                _row_copy(xn_hbm, rtok_ref[(b + GATHER_AHEAD) * tb + r], buf.at[ahead_slot], r,
                          sem.at[ahead_slot]).start(priority=DMA_QUEUE_ROWS)
            cs = slice(c * MOE_COL_CHUNK, (c + 1) * MOE_COL_CHUNK)
            gate = jnp.dot(xb[...], stage_g[cur, :, cs].astype(BF), preferred_element_type=F32)
            up = jnp.dot(xb[...], stage_u[cur, :, cs].astype(BF), preferred_element_type=F32)
            hid_ref[:, cs] = (gate * _sigmoid(gate) * up).astype(hid_ref.dtype)

    @pl.when(b >= nact)
    def _():
        hid_ref[...] = jnp.zeros_like(hid_ref)

    @pl.when(jnp.logical_and(b >= nact, b < nact + GATHER_AHEAD))
    def _():
        _rows_wait(xn_hbm, buf.at[slot], sem.at[slot])


def _moe_down_kernel(blk_ref, kin_ref, set_ref, nxt_ref, nact_ref, hid_ref, wd_hbm, ys_ref, stage_d, wsem, *, layer):
    b = pl.program_id(0)
    nact = nact_ref[0]
    half = ys_ref.shape[1]

    def copies(e, wset):
        return _expert_weight_copies(wd_hbm, layer, e, stage_d, wset, wsem)

    cur = _stream_expert_weights(b, nact, blk_ref, kin_ref, set_ref, nxt_ref, copies)

    @pl.when(b < nact)
    def _():
        for c in range(half // MOE_COL_CHUNK):
            cs = slice(c * MOE_COL_CHUNK, (c + 1) * MOE_COL_CHUNK)
            hs = slice(half + c * MOE_COL_CHUNK, half + (c + 1) * MOE_COL_CHUNK)
            lo = jnp.dot(hid_ref[...], stage_d[cur, :, cs].astype(BF), preferred_element_type=F32)
            hi = jnp.dot(hid_ref[...], stage_d[cur, :, hs].astype(BF), preferred_element_type=F32)
            ys_ref[:, cs] = _pack_halves(jnp.concatenate([lo, hi], axis=1))

    @pl.when(b >= nact)
    def _():
        ys_ref[...] = jnp.zeros_like(ys_ref)


def _combine_kernel(dest_ref, ys_hbm, h_ref, w_ref, gain_ref, o_ref, buf, sem, *, tc, n, final_norm):
    i = pl.program_id(0)

    def row_copy(blk, slot, r, k):
        return _row_copy(ys_hbm, dest_ref[k * n + blk * tc + r], buf.at[slot], k * tc + r, sem.at[slot])

    @pl.when(i == 0)
    def _():
        def body(r, carry):
            for k in range(TOP_K):
                row_copy(0, 0, r, k).start(priority=k)
            return carry

        lax.fori_loop(0, tc, body, 0, unroll=8)

    @pl.when(i + 1 < pl.num_programs(0))
    def _():
        for r in range(tc):
            for k in range(TOP_K):
                row_copy(i + 1, (i + 1) % 2, r, k).start(priority=k)

    slot = i % 2
    _rows_wait(ys_hbm, buf.at[slot], sem.at[slot])
    a_lo, a_hi = _unpack_halves(buf[slot, 0:tc, :])
    b_lo, b_hi = _unpack_halves(buf[slot, tc:TOP_K * tc, :])
    w0 = w_ref[:, 0:1]
    w1 = w_ref[:, 1:2]
    out = h_ref[...] + jnp.concatenate([w0 * a_lo + w1 * b_lo, w0 * a_hi + w1 * b_hi], axis=1)
    if final_norm:
        out = _rms(out, gain_ref[...])
    o_ref[...] = out


def hier_moe(h, ffn_gain, wg_r, bg_r, we_r, be_r, w_gate, w_up, w_down, layer, final_gain=None):
    n, d = h.shape
    f = w_gate.shape[-1]
    a = n * TOP_K
    tb = _moe_block_rows(a)
    nb = a // tb + N_EXPERTS + GATHER_AHEAD

    pad = LANES - N_GROUPS - N_EXPERTS
    w_router = jnp.concatenate([wg_r, we_r, jnp.zeros((d, pad), F32)], axis=1)
    b_router = jnp.concatenate([bg_r, be_r, jnp.zeros((pad,), F32)]).reshape(1, LANES)
    xn, ids, wts, cnt = route(h, ffn_gain, w_router, b_router)

    counts = cnt[0, N_GROUPS:N_GROUPS + N_EXPERTS].astype(jnp.int32)
    nblk = (counts + tb - 1) // tb
    bend = jnp.cumsum(nblk)
    nact = bend[-1]
    start_rows = jnp.pad(((bend - nblk) * tb).astype(F32), (0, LANES - N_EXPERTS)).reshape(1, LANES)
    dest = assignment_rows(ids, start_rows).reshape(a)
    blk = jnp.minimum(jnp.arange(nb, dtype=jnp.int32), jnp.maximum(nact - 1, 0))
    owner = lambda bi: jnp.minimum(jnp.sum((bend[None, :] <= bi[:, None]).astype(jnp.int32), axis=1), N_EXPERTS - 1)
    blk_e = owner(blk)
    k_in_e = blk - (bend - nblk)[blk_e]
    following = lambda e: jnp.where(bend[e] < nact, owner(bend[e]), e)
    nxt_e = following(blk_e)
    wset = (jnp.cumsum((k_in_e == 0).astype(jnp.int32)) - 1) % 2
    tok = jnp.tile(jnp.arange(n, dtype=jnp.int32), TOP_K)
    row_tok = (jnp.arange(nb * tb, dtype=jnp.int32) % n).at[dest].set(tok)
    nact1 = nact.reshape(1).astype(jnp.int32)

    any_space = pl.BlockSpec(memory_space=pl.ANY)
    hid = pl.pallas_call(
        functools.partial(_moe_up_kernel, tb=tb, layer=layer),
        grid_spec=pltpu.PrefetchScalarGridSpec(
            num_scalar_prefetch=6, grid=(nb,),
            in_specs=[any_space, any_space, any_space],
            out_specs=pl.BlockSpec((tb, f), lambda b, *_: (b, 0)),
            scratch_shapes=[pltpu.VMEM((GATHER_AHEAD + 1, tb, d // 2), jnp.uint32),
                            pltpu.VMEM((2, d, f), F32), pltpu.VMEM((2, d, f), F32), pltpu.VMEM((tb, d), BF),
                            pltpu.SemaphoreType.DMA((GATHER_AHEAD + 1,)), pltpu.SemaphoreType.DMA((2, 2))]),
        out_shape=jax.ShapeDtypeStruct((nb * tb, f), BF),
        compiler_params=_cparams(("arbitrary",), 56), name="moe_up",
    )(blk_e, k_in_e, wset, nxt_e, nact1, row_tok, xn, w_gate, w_up)

    ys = pl.pallas_call(
        functools.partial(_moe_down_kernel, layer=layer),
        grid_spec=pltpu.PrefetchScalarGridSpec(
            num_scalar_prefetch=5, grid=(nb,),
            in_specs=[pl.BlockSpec((tb, f), lambda b, *_: (b, 0)), any_space],
            out_specs=pl.BlockSpec((tb, d // 2), lambda b, *_: (b, 0)),
            scratch_shapes=[pltpu.VMEM((2, f, d), F32), pltpu.SemaphoreType.DMA((2,))]),
        out_shape=jax.ShapeDtypeStruct((nb * tb, d // 2), jnp.uint32),
        compiler_params=_cparams(("arbitrary",), 40), name="moe_down",
    )(blk_e, k_in_e, wset, nxt_e, nact1, hid, w_down)

    tc = min(COMBINE_ROWS, n)
    gain = (final_gain if final_gain is not None else ffn_gain).reshape(1, d)
    return pl.pallas_call(
        functools.partial(_combine_kernel, tc=tc, n=n, final_norm=final_gain is not None),
        grid_spec=pltpu.PrefetchScalarGridSpec(
            num_scalar_prefetch=1, grid=(n // tc,),
            in_specs=[pl.BlockSpec(memory_space=pl.ANY), pl.BlockSpec((tc, d), lambda i, *_: (i, 0)),
                      pl.BlockSpec((tc, LANES), lambda i, *_: (i, 0)), pl.BlockSpec((1, d), lambda i, *_: (0, 0))],
            out_specs=pl.BlockSpec((tc, d), lambda i, *_: (i, 0)),
            scratch_shapes=[pltpu.VMEM((2, TOP_K * tc, d // 2), jnp.uint32), pltpu.SemaphoreType.DMA((2,))]),
        out_shape=jax.ShapeDtypeStruct((n, d), F32),
        compiler_params=_cparams(("arbitrary",), 40), name="moe_combine",
    )(dest, ys, h, wts, gain)


def even_mixer(h, gain, w_in, b_gates, pool_w, pool_scale, head_gain, w_out, j):
    d = h.shape[1]
    pool_width = N_POOL_GROUPS * pool_w.shape[-1]
    mlstm_width = head_gain.shape[0]
    main_cols = pool_width + 4 * mlstm_width
    n_gates = 2 * MLSTM_HEADS
    gate_b = jnp.pad(b_gates, (0, LANES - n_gates)).reshape(1, LANES)
    z, gates = norm_matmul_gates_t(h, gain, jnp.swapaxes(w_in, 1, 2), main_cols, n_gates, gate_b, j)
    y_p = pool_mixer(z, pool_w, pool_scale)
    y_m = mlstm_mixer(z, gates, head_gain, pool_width)
    assert pool_width == mlstm_width and pool_width + mlstm_width == d
    return matmul_residual([y_p, y_m], w_out, h, layer=j)


def odd_mixer(h, gain, w_in, conv_w, w_out, j):
    z = norm_matmul(h, gain, w_in, w_in.shape[2], layer=j)
    return conv_matmul_residual(z, conv_w, w_out, h, j)


def cross_attn(h, mem, gain, mem_gain, wq, wk, wv, wo, layer):
    d = h.shape[1]
    k = norm_matmul(mem, mem_gain, wk, d, layer=layer)
    v = norm_matmul(mem, mem_gain, wv, d, layer=layer)
    return matmul_residual([q_attention(h, gain, wq, k, v, layer)], wo, h, layer=layer)


def kernel(x, mem, mix_norm, xattn_norm, mem_norm, ffn_norm, final_norm, ev_w_in, ev_b_gates, ev_pool_w, ev_pool_scale, ev_head_norm, ev_w_out, od_w_in, od_conv_w, od_w_out, xa_wq, xa_wk, xa_wv, xa_wo, rt_group_w, rt_group_b, rt_expert_w, rt_expert_b, ex_w_gate, ex_w_up, ex_w_down):
    depth = mix_norm.shape[0]
    h = x[0]
    m = mem[0]
    for layer in range(depth):
        j = layer // 2
        if layer % 2 == 0:
            h = even_mixer(h, mix_norm[layer], ev_w_in, ev_b_gates[j], ev_pool_w[j], ev_pool_scale[j],
                           ev_head_norm[j], ev_w_out, j)
        else:
            h = odd_mixer(h, mix_norm[layer], od_w_in, od_conv_w[j], od_w_out, j)
        h = cross_attn(h, m, xattn_norm[layer], mem_norm[layer], xa_wq, xa_wk, xa_wv, xa_wo, layer)
        h = hier_moe(h, ffn_norm[layer], rt_group_w[layer], rt_group_b[layer], rt_expert_w[layer],
                     rt_expert_b[layer], ex_w_gate, ex_w_up, ex_w_down, layer,
                     final_gain=final_norm if layer == depth - 1 else None)
    return h[None]
```

```python
import functools

import jax
import jax.numpy as jnp
from jax import lax
from jax.experimental import pallas as pl
from jax.experimental.pallas import tpu as pltpu

F32 = jnp.float32
BF = jnp.bfloat16
EPS = 1e-6

POOL_WINDOWS = (2, 4, 8, 16)
N_POOL_GROUPS = 4
MLSTM_HEADS = 4
FORGET_LANE0 = MLSTM_HEADS
XATTN_HEADS = 4
N_GROUPS = 4
EXPERTS_PER_GROUP = 8
N_EXPERTS = N_GROUPS * EXPERTS_PER_GROUP
TOP_K = 2
CONV_WIDTH = 3

LANES = 128
SUBLANES = 8
DMA_QUEUE_ROWS = 0
DMA_QUEUE_BULK = 1

ROW_TILE = 1024
COL_TILE = 1024
OUT_ROW_TILE = 512
MOE_COL_CHUNK = 256
WEIGHT_DMA_PARTS = 4
MLSTM_CHUNK = 256
POOL_ROWS = 256
POOL_HALO = 128
CONV_ROWS = 512
CONV_COLS = 512
CONV_HALO = 16
ROUTER_ROWS = 512
GATHER_AHEAD = 2
COMBINE_ROWS = 256


def _moe_block_rows(assignments):
    mean_rows = assignments // N_EXPERTS
    return -(-(mean_rows * 9 // 16) // 16) * 16


def _cparams(semantics, vmem_mib):
    return pltpu.CompilerParams(dimension_semantics=semantics, vmem_limit_bytes=vmem_mib * 1024 * 1024)


def _sigmoid(x):
    return 1.0 / (1.0 + jnp.exp(-x))


def _log_sigmoid(x):
    return jnp.minimum(x, 0.0) - jnp.log(1.0 + jnp.exp(-jnp.abs(x)))


def _rms(x, g):
    ms = jnp.mean(x * x, axis=-1, keepdims=True)
    return x * lax.rsqrt(ms + EPS) * g


def _pack_halves(x):
    half = x.shape[1] // 2
    lo = pltpu.bitcast(x[:, :half].astype(BF).astype(F32), jnp.uint32)
    hi = pltpu.bitcast(x[:, half:].astype(BF).astype(F32), jnp.uint32)
    return hi | lax.shift_right_logical(lo, jnp.uint32(16))


def _unpack_halves(w):
    lo = pltpu.bitcast(lax.shift_left(w, jnp.uint32(16)), F32)
    hi = pltpu.bitcast(w & jnp.uint32(0xFFFF0000), F32)
    return lo, hi


def _norm_mm_kernel(x_ref, g_ref, w_ref, o_ref, xn_ref):
    @pl.when(pl.program_id(1) == 0)
    def _():
        xn_ref[...] = _rms(x_ref[...], g_ref[...]).astype(BF)

    o_ref[...] = jnp.dot(xn_ref[...], w_ref[...].astype(BF), preferred_element_type=F32).astype(o_ref.dtype)


_NT = (((1,), (1,)), ((), ()))


def _norm_mm_gates_t_kernel(x_ref, g_ref, wt_ref, wgt_ref, bg_ref, o_ref, gates_ref, xn_ref):
    @pl.when(pl.program_id(1) == 0)
    def _():
        xn = _rms(x_ref[...], g_ref[...]).astype(BF)
        xn_ref[...] = xn
        wg = wgt_ref[...]
        wg = jnp.concatenate([wg, jnp.zeros((LANES - wg.shape[0], wg.shape[1]), F32)], axis=0).astype(BF)
        gates_ref[...] = lax.dot_general(xn, wg, _NT, preferred_element_type=F32) + bg_ref[...]

    o_ref[...] = lax.dot_general(xn_ref[...], wt_ref[...].astype(BF), _NT,
                                 preferred_element_type=F32).astype(o_ref.dtype)


def norm_matmul_gates_t(x, gain, wt, n_cols, n_gates, gate_b, layer):
    n, k = x.shape
    tm = min(ROW_TILE, n)
    tn = COL_TILE
    return pl.pallas_call(
        _norm_mm_gates_t_kernel, grid=(n // tm, n_cols // tn),
        in_specs=[pl.BlockSpec((tm, k), lambda i, j: (i, 0)), pl.BlockSpec((1, k), lambda i, j: (0, 0)),
                  pl.BlockSpec((None, tn, k), lambda i, j: (layer, j, 0)),
                  pl.BlockSpec((None, n_gates, k), lambda i, j: (layer, n_cols // n_gates, 0)),
                  pl.BlockSpec((1, LANES), lambda i, j: (0, 0))],
        out_specs=[pl.BlockSpec((tm, tn), lambda i, j: (i, j)), pl.BlockSpec((tm, LANES), lambda i, j: (i, 0))],
        out_shape=[jax.ShapeDtypeStruct((n, n_cols), BF), jax.ShapeDtypeStruct((n, LANES), F32)],
        scratch_shapes=[pltpu.VMEM((tm, k), BF)],
        compiler_params=_cparams(("parallel", "arbitrary"), 56), name="norm_matmul_gates",
    )(x, gain.reshape(1, k), wt, wt, gate_b)


def _stacked(w, layer):
    return (w[None], 0) if layer is None else (w, layer)


def norm_matmul(x, gain, w, n_cols, layer=None):
    n, k = x.shape
    w, li = _stacked(w, layer)
    tm = min(ROW_TILE, n)
    tn = COL_TILE
    return pl.pallas_call(
        _norm_mm_kernel, grid=(n // tm, n_cols // tn),
        in_specs=[pl.BlockSpec((tm, k), lambda i, j: (i, 0)), pl.BlockSpec((1, k), lambda i, j: (0, 0)),
                  pl.BlockSpec((None, k, tn), lambda i, j: (li, 0, j))],
        out_specs=pl.BlockSpec((tm, tn), lambda i, j: (i, j)),
        out_shape=jax.ShapeDtypeStruct((n, n_cols), BF), scratch_shapes=[pltpu.VMEM((tm, k), BF)],
        compiler_params=_cparams(("parallel", "arbitrary"), 56), name="norm_matmul",
    )(x, gain.reshape(1, k), w)


def _q_attention_kernel(x_ref, g_ref, w_ref, k_ref, v_ref, o_ref, q_ref, *, hd):
    xn = _rms(x_ref[...], g_ref[...]).astype(BF)
    q_ref[...] = jnp.dot(xn, w_ref[...].astype(BF), preferred_element_type=F32).astype(BF)
    scale = hd ** -0.5
    for h in range(XATTN_HEADS):
        hs = slice(h * hd, (h + 1) * hd)
        s = lax.dot_general(q_ref[:, hs], k_ref[:, hs], _NT, preferred_element_type=F32) * scale
        e = jnp.exp(s - jnp.max(s, axis=-1, keepdims=True))
        p = (e / jnp.sum(e, axis=-1, keepdims=True)).astype(BF)
        o_ref[:, hs] = jnp.dot(p, v_ref[:, hs], preferred_element_type=F32).astype(o_ref.dtype)


def q_attention(x, gain, wq, k, v, layer):
    n, kd = x.shape
    d = wq.shape[2]
    m = k.shape[0]
    tm = min(OUT_ROW_TILE, n)
    return pl.pallas_call(
        functools.partial(_q_attention_kernel, hd=d // XATTN_HEADS), grid=(n // tm,),
        in_specs=[pl.BlockSpec((tm, kd), lambda i: (i, 0)), pl.BlockSpec((1, kd), lambda i: (0, 0)),
                  pl.BlockSpec((None, kd, d), lambda i: (layer, 0, 0), pipeline_mode=pl.Buffered(1)),
                  pl.BlockSpec((m, d), lambda i: (0, 0)), pl.BlockSpec((m, d), lambda i: (0, 0))],
        out_specs=pl.BlockSpec((tm, d), lambda i: (i, 0)), out_shape=jax.ShapeDtypeStruct((n, d), BF),
        scratch_shapes=[pltpu.VMEM((tm, d), BF)],
        compiler_params=_cparams(("parallel",), 52), name="q_attention",
    )(x, gain.reshape(1, kd), wq, k, v)


def _mm_res_kernel(*refs, nparts):
    xs = refs[:nparts]
    w_ref, res_ref, o_ref = refs[nparts:]
    acc = res_ref[...]
    k0 = 0
    for x_ref in xs:
        kp = x_ref.shape[1]
        acc = acc + jnp.dot(x_ref[...], w_ref[k0:k0 + kp, :].astype(BF), preferred_element_type=F32)
        k0 += kp
    o_ref[...] = acc


def matmul_residual(xs, w, res, layer=None):
    n, d = res.shape
    w, li = _stacked(w, layer)
    k = w.shape[1]
    tm = min(OUT_ROW_TILE, n)
    in_specs = [pl.BlockSpec((tm, x.shape[1]), lambda i: (i, 0)) for x in xs]
    in_specs += [pl.BlockSpec((None, k, d), lambda i: (li, 0, 0), pipeline_mode=pl.Buffered(1)),
                 pl.BlockSpec((tm, d), lambda i: (i, 0))]
    return pl.pallas_call(
        functools.partial(_mm_res_kernel, nparts=len(xs)), grid=(n // tm,), in_specs=in_specs,
        out_specs=pl.BlockSpec((tm, d), lambda i: (i, 0)), out_shape=jax.ShapeDtypeStruct((n, d), F32),
        compiler_params=_cparams(("parallel",), 52), name="matmul_residual",
    )(*xs, w, res)


def _pool_kernel(cur_ref, prev_ref, w_ref, sc_ref, o_ref, *, tp, gdim):
    i = pl.program_id(0)
    dist = lax.broadcasted_iota(jnp.int32, (tp, tp), 0) - lax.broadcasted_iota(jnp.int32, (tp, tp), 1)
    distp = (lax.broadcasted_iota(jnp.int32, (tp, POOL_HALO), 0) + POOL_HALO
             - lax.broadcasted_iota(jnp.int32, (tp, POOL_HALO), 1))
    pos = i * tp + lax.broadcasted_iota(jnp.int32, (tp, 1), 0)
    for j, win in enumerate(POOL_WINDOWS):
        gs = slice(j * gdim, (j + 1) * gdim)
        cur = cur_ref[:, gs]
        band = jnp.where(dist >= 0, jnp.where(dist < win, 1.0, 0.0), 0.0).astype(BF)
        bandp = jnp.where(distp < jnp.where(i > 0, win, 0), 1.0, 0.0).astype(BF)
        s = (jnp.dot(band, cur, preferred_element_type=F32)
             + jnp.dot(bandp, prev_ref[:, gs], preferred_element_type=F32))
        cnt = jnp.minimum(pos + 1, win).astype(F32)
        d = s / cnt - cur.astype(F32)
        y = jnp.dot(d.astype(BF), w_ref[j].astype(BF), preferred_element_type=F32) * sc_ref[:, gs]
        o_ref[:, gs] = y.astype(o_ref.dtype)


def pool_mixer(z, pool_w, pool_scale):
    n = z.shape[0]
    gdim = pool_w.shape[-1]
    width = N_POOL_GROUPS * gdim
    tp = min(POOL_ROWS, n)
    halo_blocks = tp // POOL_HALO
    return pl.pallas_call(
        functools.partial(_pool_kernel, tp=tp, gdim=gdim), grid=(n // tp,),
        in_specs=[
            pl.BlockSpec((tp, width), lambda i: (i, 0)),
            pl.BlockSpec((POOL_HALO, width), lambda i: (jnp.maximum(i * halo_blocks - 1, 0), 0)),
            pl.BlockSpec((N_POOL_GROUPS, gdim, gdim), lambda i: (0, 0, 0)),
            pl.BlockSpec((1, width), lambda i: (0, 0)),
        ],
        out_specs=pl.BlockSpec((tp, width), lambda i: (i, 0)),
        out_shape=jax.ShapeDtypeStruct((n, width), BF),
        compiler_params=_cparams(("parallel",), 32), name="pool_mixer",
    )(z, z, pool_w, pool_scale.reshape(1, width))


def _mlstm_kernel(q_ref, k_ref, v_ref, o_ref, g_ref, gain_ref, y_ref, ct_ref, m_ref, *, chunk, dh):
    c = pl.program_id(0)

    @pl.when(c == 0)
    def _():
        ct_ref[...] = jnp.zeros_like(ct_ref)
        m_ref[...] = jnp.zeros_like(m_ref)

    g = g_ref[...]
    lf = _log_sigmoid(g)
    row = lax.broadcasted_iota(jnp.int32, (chunk, chunk), 0)
    col = lax.broadcasted_iota(jnp.int32, (chunk, chunk), 1)
    causal = col <= row
    ltri = jnp.where(causal, 1.0, 0.0).astype(BF)
    hi = lf.astype(BF)
    r1 = lf - hi.astype(F32)
    mid = r1.astype(BF)
    lo = (r1 - mid.astype(F32)).astype(BF)
    bcum = (jnp.dot(ltri, hi, preferred_element_type=F32) + jnp.dot(ltri, mid, preferred_element_type=F32)
            + jnp.dot(ltri, lo, preferred_element_type=F32))
    g_t = g.T
    b_t = bcum.T
    ones_col = jnp.where(lax.broadcasted_iota(jnp.int32, (chunk, LANES), 1) == 0, 1.0, 0.0).astype(BF)

    for h in range(MLSTM_HEADS):
        hs = slice(h * dh, (h + 1) * dh)
        fl = FORGET_LANE0 + h
        bc = bcum[:, fl:fl + 1]
        br = b_t[fl:fl + 1, :]
        ir = g_t[h:h + 1, :]
        b_last = bcum[chunk - 1:chunk, fl:fl + 1]
        m_prev = m_ref[h][:, 0:1]

        dmat = jnp.where(causal, bc + (ir - br), -jnp.inf)
        inter = bc + m_prev
        m_t = jnp.maximum(jnp.max(dmat, axis=1, keepdims=True), inter)
        w_inter = jnp.exp(inter - m_t)
        p = jnp.exp(dmat - m_t)

        qh = q_ref[:, hs] * (dh ** -0.5)
        kh = k_ref[:, hs]
        v_aug = jnp.concatenate([v_ref[:, hs], ones_col], axis=1)
        s = lax.dot_general(qh, kh, (((1,), (1,)), ((), ())), preferred_element_type=F32)
        sc = (s * p).astype(BF)
        ct = ct_ref[h]
        num_aug = (w_inter * jnp.dot(qh, ct.astype(BF), preferred_element_type=F32)
                   + jnp.dot(sc, v_aug, preferred_element_type=F32))
        num = num_aug[:, :dh]
        den = num_aug[:, dh:dh + 1]
        hout = num / jnp.maximum(jnp.abs(den), jnp.exp(-m_t))

        yn = _rms(hout, gain_ref[:, hs])
        y_ref[:, hs] = (_sigmoid(o_ref[:, hs].astype(F32)) * yn).astype(y_ref.dtype)

        d_end = b_last - br + ir
        m_new = jnp.maximum(b_last + m_prev, jnp.max(d_end, axis=1, keepdims=True))
        a_prev = jnp.exp(b_last + m_prev - m_new)
        a_s = jnp.exp(d_end - m_new)
        k_t = (kh.astype(F32).T * a_s).astype(BF)
        ct_ref[h] = a_prev * ct + jnp.dot(k_t, v_aug, preferred_element_type=F32)
        m_ref[h] = jnp.broadcast_to(m_new, (1, LANES))


def mlstm_mixer(z, gates, head_gain, col0):
    n = z.shape[0]
    width = head_gain.shape[0]
    dh = width // MLSTM_HEADS
    chunk = min(MLSTM_CHUNK, n)
    base = col0 // width
    qkvo = [pl.BlockSpec((chunk, width), lambda c, p=p: (c, base + p)) for p in range(4)]
    return pl.pallas_call(
        functools.partial(_mlstm_kernel, chunk=chunk, dh=dh), grid=(n // chunk,),
        in_specs=qkvo + [pl.BlockSpec((chunk, LANES), lambda c: (c, 0)), pl.BlockSpec((1, width), lambda c: (0, 0))],
        out_specs=pl.BlockSpec((chunk, width), lambda c: (c, 0)),
        out_shape=jax.ShapeDtypeStruct((n, width), BF),
        scratch_shapes=[pltpu.VMEM((MLSTM_HEADS, dh, dh + LANES), F32), pltpu.VMEM((MLSTM_HEADS, 1, LANES), F32)],
        compiler_params=_cparams(("arbitrary",), 32), name="mlstm_mixer",
    )(z, z, z, z, gates, head_gain.reshape(1, width))


def _gated_conv(b, c, u, cp, up, w, first_block):
    zc = c.astype(F32) * u.astype(F32)
    zp = jnp.where(first_block, 0.0, cp.astype(F32) * up.astype(F32))
    row = lax.broadcasted_iota(jnp.int32, zc.shape, 0)
    acc = w[CONV_WIDTH - 1:CONV_WIDTH, :] * zc
    for back in range(1, CONV_WIDTH):
        shifted = pltpu.roll(zc, back, 0)
        for r in range(back):
            shifted = jnp.where(row == r, zp[CONV_HALO - back + r:CONV_HALO - back + r + 1, :], shifted)
        acc = acc + w[CONV_WIDTH - 1 - back:CONV_WIDTH - back, :] * shifted
    return b.astype(F32) * acc


def _conv_mm_res_kernel(b_ref, c_ref, u_ref, cp_ref, up_ref, cw_ref, w_ref, res_ref, o_ref, xs_ref):
    first_block = pl.program_id(0) == 0
    for c0 in range(0, xs_ref.shape[1], CONV_COLS):
        cs = slice(c0, c0 + CONV_COLS)
        xs_ref[:, cs] = _gated_conv(b_ref[:, cs], c_ref[:, cs], u_ref[:, cs], cp_ref[:, cs], up_ref[:, cs],
                                    cw_ref[:, cs], first_block).astype(BF)
    o_ref[...] = res_ref[...] + jnp.dot(xs_ref[...], w_ref[...].astype(BF), preferred_element_type=F32)


def conv_matmul_residual(z, conv_w, w, res, layer):
    n, d = res.shape
    tm = min(CONV_ROWS, n)
    halo_blocks = tm // CONV_HALO
    cur = lambda part: pl.BlockSpec((tm, d), lambda i: (i, part))
    prev = lambda part: pl.BlockSpec((CONV_HALO, d), lambda i: (jnp.maximum(i * halo_blocks - 1, 0), part))
    return pl.pallas_call(
        _conv_mm_res_kernel, grid=(n // tm,),
        in_specs=[cur(0), cur(1), cur(2), prev(1), prev(2), pl.BlockSpec((CONV_WIDTH, d), lambda i: (0, 0)),
                  pl.BlockSpec((None, d, d), lambda i: (layer, 0, 0), pipeline_mode=pl.Buffered(1)),
                  pl.BlockSpec((tm, d), lambda i: (i, 0))],
        out_specs=pl.BlockSpec((tm, d), lambda i: (i, 0)), out_shape=jax.ShapeDtypeStruct((n, d), F32),
        scratch_shapes=[pltpu.VMEM((tm, d), BF)],
        compiler_params=_cparams(("parallel",), 52), name="conv_matmul_residual",
    )(z, z, z, z, z, conv_w, w, res)


def _router_kernel(x_ref, g_ref, wr_ref, br_ref, xn_ref, ids_ref, wts_ref, cnt_ref, carry_ref, w2_ref, *, tr):
    @pl.when(pl.program_id(0) == 0)
    def _():
        carry_ref[...] = jnp.zeros_like(carry_ref)

    @pl.when(pl.program_id(0) == 0)
    def _():
        w = wr_ref[...]
        w_hi = w.astype(BF)
        w2_ref[:, :LANES] = w_hi
        w2_ref[:, LANES:] = (w - w_hi.astype(F32)).astype(BF)

    xn = _rms(x_ref[...], g_ref[...])
    xn_ref[...] = _pack_halves(xn)
    x_hi = xn.astype(BF)
    x_lo = (xn - x_hi.astype(F32)).astype(BF)
    p_hi = jnp.dot(x_hi, w2_ref[...], preferred_element_type=F32)
    p_lo = jnp.dot(x_lo, w2_ref[...], preferred_element_type=F32)
    logits = (p_hi[:, :LANES] + p_hi[:, LANES:]) + (p_lo[:, :LANES] + p_lo[:, LANES:]) + br_ref[...]
    lane = lax.broadcasted_iota(jnp.int32, (tr, LANES), 1).astype(F32)
    neg = -jnp.inf

    def first_argmax(vals):
        top = jnp.max(vals, axis=-1, keepdims=True)
        return top, jnp.min(jnp.where(vals == top, lane, float(LANES)), axis=-1, keepdims=True)

    gl = jnp.where(lane < N_GROUPS, logits, neg)
    gmax, grp = first_argmax(gl)
    g_prob = 1.0 / jnp.sum(jnp.exp(gl - gmax), axis=-1, keepdims=True)
    lo = N_GROUPS + EXPERTS_PER_GROUP * grp
    el = jnp.where(lane >= lo, jnp.where(lane < lo + EXPERTS_PER_GROUP, logits, neg), neg)
    v1, l1 = first_argmax(el)
    v2, l2 = first_argmax(jnp.where(lane == l1, neg, el))
    e2 = jnp.exp(v2 - v1)
    w1 = g_prob / (1.0 + e2)
    w2 = g_prob * e2 / (1.0 + e2)
    hot1 = lane == l1
    hot2 = lane == l2
    hot = jnp.where(hot1, 1.0, jnp.where(hot2, 1.0, 0.0))
    earlier = (lax.broadcasted_iota(jnp.int32, (tr, tr), 1) < lax.broadcasted_iota(jnp.int32, (tr, tr), 0))
    before = jnp.dot(jnp.where(earlier, 1.0, 0.0).astype(BF), hot.astype(BF), preferred_element_type=F32)
    before = before + carry_ref[0:1, :]
    r1 = jnp.sum(jnp.where(hot1, before, 0.0), axis=-1, keepdims=True)
    r2 = jnp.sum(jnp.where(hot2, before, 0.0), axis=-1, keepdims=True)
    carry_ref[0:1, :] = carry_ref[0:1, :] + jnp.sum(hot, axis=0, keepdims=True)
    ids = jnp.where(lane == 0, l1 - N_GROUPS, jnp.where(lane == 1, l2 - N_GROUPS,
                    jnp.where(lane == 2, r1, jnp.where(lane == 3, r2, 0.0))))
    ids_ref[...] = ids.astype(jnp.int32)
    wts_ref[...] = jnp.where(lane == 0, w1, jnp.where(lane == 1, w2, 0.0))
    cnt_ref[...] = carry_ref[...]


def route(h, gain, w_router, b_router):
    n, d = h.shape
    tr = min(ROUTER_ROWS, n)
    return pl.pallas_call(
        functools.partial(_router_kernel, tr=tr), grid=(n // tr,),
        in_specs=[pl.BlockSpec((tr, d), lambda i: (i, 0)), pl.BlockSpec((1, d), lambda i: (0, 0)),
                  pl.BlockSpec((d, LANES), lambda i: (0, 0)), pl.BlockSpec((1, LANES), lambda i: (0, 0))],
        out_specs=[pl.BlockSpec((tr, d // 2), lambda i: (i, 0)),
                   pl.BlockSpec((tr, LANES), lambda i: (i, 0)), pl.BlockSpec((tr, LANES), lambda i: (i, 0)),
                   pl.BlockSpec((SUBLANES, LANES), lambda i: (0, 0))],
        out_shape=[jax.ShapeDtypeStruct((n, d // 2), jnp.uint32), jax.ShapeDtypeStruct((n, LANES), jnp.int32),
                   jax.ShapeDtypeStruct((n, LANES), F32), jax.ShapeDtypeStruct((SUBLANES, LANES), F32)],
        scratch_shapes=[pltpu.VMEM((SUBLANES, LANES), F32), pltpu.VMEM((d, 2 * LANES), BF)],
        compiler_params=_cparams(("arbitrary",), 40), name="moe_router",
    )(h, gain.reshape(1, d), w_router, b_router)


def _dest_kernel(ids_ref, start_ref, o_ref, *, tr):
    ids = ids_ref[...].astype(F32)
    lane = lax.broadcasted_iota(jnp.int32, (tr, LANES), 1).astype(F32)
    start = start_ref[...]
    rows = []
    for k in range(TOP_K):
        first = jnp.sum(jnp.where(lane == ids[:, k:k + 1], start, 0.0), axis=-1, keepdims=True)
        rows.append(first + ids[:, TOP_K + k:TOP_K + k + 1])
    packed = jnp.where(lane == 0, rows[0], jnp.where(lane == 1, rows[1], 0.0))
    o_ref[...] = packed.T[0:SUBLANES, :].astype(jnp.int32)


def assignment_rows(ids, start_rows):
    n = ids.shape[0]
    tr = min(ROUTER_ROWS, n)
    out = pl.pallas_call(
        functools.partial(_dest_kernel, tr=tr), grid=(n // tr,),
        in_specs=[pl.BlockSpec((tr, LANES), lambda i: (i, 0)), pl.BlockSpec((1, LANES), lambda i: (0, 0))],
        out_specs=pl.BlockSpec((SUBLANES, tr), lambda i: (0, i)),
        out_shape=jax.ShapeDtypeStruct((SUBLANES, n), jnp.int32),
        compiler_params=_cparams(("parallel",), 32), name="moe_assignment_rows",
    )(ids, start_rows)
    return out[:TOP_K]


def _row_copy(src_hbm, row, dst, dst_row, sem):
    return pltpu.make_async_copy(src_hbm.at[pl.ds(row, 1), :], dst.at[pl.ds(dst_row, 1), :], sem)


def _rows_wait(src_hbm, dst, sem):
    pltpu.make_async_copy(src_hbm.at[pl.ds(0, dst.shape[0]), :], dst, sem).wait()


def _expert_weight_copies(w_hbm, layer, expert, stage, wset, sem):
    rows = stage.shape[1] // WEIGHT_DMA_PARTS
    return [pltpu.make_async_copy(w_hbm.at[layer, expert, pl.ds(p * rows, rows), :],
                                  stage.at[wset, pl.ds(p * rows, rows), :], sem.at[wset])
            for p in range(WEIGHT_DMA_PARTS)]


def _stream_expert_weights(b, nact, blk_ref, kin_ref, set_ref, nxt_ref, copies):
    cur = set_ref[b]

    @pl.when(b == 0)
    def _():
        for cp in copies(blk_ref[0], cur):
            cp.start(priority=DMA_QUEUE_BULK)

    @pl.when(jnp.logical_and(b < nact, kin_ref[b] == 0))
    def _():
        for cp in copies(blk_ref[b], cur):
            cp.wait()

        @pl.when(nxt_ref[b] != blk_ref[b])
        def _():
            for cp in copies(nxt_ref[b], 1 - cur):
                cp.start(priority=DMA_QUEUE_BULK)

    return cur


def _moe_up_kernel(blk_ref, kin_ref, set_ref, nxt_ref, nact_ref, rtok_ref, xn_hbm, wg_hbm, wu_hbm, hid_ref,
                   buf, stage_g, stage_u, xb, sem, wsem, *, tb, layer):
    b = pl.program_id(0)
    nact = nact_ref[0]
    nslots = GATHER_AHEAD + 1
    slot = b % nslots
    f = hid_ref.shape[1]
    up_chunks = f // MOE_COL_CHUNK
    rows_per_chunk = tb // up_chunks

    @pl.when(b == 0)
    def _():
        for ahead in range(GATHER_AHEAD):
            def body(r, carry, ahead=ahead):
                _row_copy(xn_hbm, rtok_ref[ahead * tb + r], buf.at[ahead], r,
                          sem.at[ahead]).start(priority=DMA_QUEUE_ROWS)
                return carry

            lax.fori_loop(0, tb, body, 0, unroll=8)

    def copies(e, wset):
        return (_expert_weight_copies(wg_hbm, layer, e, stage_g, wset, wsem.at[0])
                + _expert_weight_copies(wu_hbm, layer, e, stage_u, wset, wsem.at[1]))

    cur = _stream_expert_weights(b, nact, blk_ref, kin_ref, set_ref, nxt_ref, copies)

    @pl.when(b < nact)
    def _():
        _rows_wait(xn_hbm, buf.at[slot], sem.at[slot])
        x_lo, x_hi = _unpack_halves(buf[slot])
        half = x_lo.shape[1]
        xb[:, :half] = x_lo.astype(BF)
        xb[:, half:] = x_hi.astype(BF)
        ahead_slot = (b + GATHER_AHEAD) % nslots
        for c in range(up_chunks):
            for r in range(c * rows_per_chunk, (c + 1) * rows_per_chunk):
                _row_copy(xn_hbm, rtok_ref[(b + GATHER_AHEAD) * tb + r], buf.at[ahead_slot], r,
                          sem.at[ahead_slot]).start(priority=DMA_QUEUE_ROWS)
            cs = slice(c * MOE_COL_CHUNK, (c + 1) * MOE_COL_CHUNK)
            gate = jnp.dot(xb[...], stage_g[cur, :, cs].astype(BF), preferred_element_type=F32)
            up = jnp.dot(xb[...], stage_u[cur, :, cs].astype(BF), preferred_element_type=F32)
            hid_ref[:, cs] = (gate * _sigmoid(gate) * up).astype(hid_ref.dtype)

    @pl.when(b >= nact)
    def _():
        hid_ref[...] = jnp.zeros_like(hid_ref)

    @pl.when(jnp.logical_and(b >= nact, b < nact + GATHER_AHEAD))
    def _():
        _rows_wait(xn_hbm, buf.at[slot], sem.at[slot])


def _moe_down_kernel(blk_ref, kin_ref, set_ref, nxt_ref, nact_ref, hid_ref, wd_hbm, ys_ref, stage_d, wsem, *, layer):
    b = pl.program_id(0)
    nact = nact_ref[0]
    half = ys_ref.shape[1]

    def copies(e, wset):
        return _expert_weight_copies(wd_hbm, layer, e, stage_d, wset, wsem)

    cur = _stream_expert_weights(b, nact, blk_ref, kin_ref, set_ref, nxt_ref, copies)

    @pl.when(b < nact)
    def _():
        for c in range(half // MOE_COL_CHUNK):
            cs = slice(c * MOE_COL_CHUNK, (c + 1) * MOE_COL_CHUNK)
            hs = slice(half + c * MOE_COL_CHUNK, half + (c + 1) * MOE_COL_CHUNK)
            lo = jnp.dot(hid_ref[...], stage_d[cur, :, cs].astype(BF), preferred_element_type=F32)
            hi = jnp.dot(hid_ref[...], stage_d[cur, :, hs].astype(BF), preferred_element_type=F32)
            ys_ref[:, cs] = _pack_halves(jnp.concatenate([lo, hi], axis=1))

    @pl.when(b >= nact)
    def _():
        ys_ref[...] = jnp.zeros_like(ys_ref)


def _combine_kernel(dest_ref, ys_hbm, h_ref, w_ref, gain_ref, o_ref, buf, sem, *, tc, n, final_norm):
    i = pl.program_id(0)

    def row_copy(blk, slot, r, k):
        return _row_copy(ys_hbm, dest_ref[k * n + blk * tc + r], buf.at[slot], k * tc + r, sem.at[slot])

    @pl.when(i == 0)
    def _():
        def body(r, carry):
            for k in range(TOP_K):
                row_copy(0, 0, r, k).start(priority=k)
            return carry

        lax.fori_loop(0, tc, body, 0, unroll=8)

    @pl.when(i + 1 < pl.num_programs(0))
    def _():
        for r in range(tc):
            for k in range(TOP_K):
                row_copy(i + 1, (i + 1) % 2, r, k).start(priority=k)

    slot = i % 2
    _rows_wait(ys_hbm, buf.at[slot], sem.at[slot])
    a_lo, a_hi = _unpack_halves(buf[slot, 0:tc, :])
    b_lo, b_hi = _unpack_halves(buf[slot, tc:TOP_K * tc, :])
    w0 = w_ref[:, 0:1]
    w1 = w_ref[:, 1:2]
    out = h_ref[...] + jnp.concatenate([w0 * a_lo + w1 * b_lo, w0 * a_hi + w1 * b_hi], axis=1)
    if final_norm:
        out = _rms(out, gain_ref[...])
    o_ref[...] = out


def hier_moe(h, ffn_gain, wg_r, bg_r, we_r, be_r, w_gate, w_up, w_down, layer, final_gain=None):
    n, d = h.shape
    f = w_gate.shape[-1]
    a = n * TOP_K
    tb = _moe_block_rows(a)
    nb = a // tb + N_EXPERTS + GATHER_AHEAD

    pad = LANES - N_GROUPS - N_EXPERTS
    w_router = jnp.concatenate([wg_r, we_r, jnp.zeros((d, pad), F32)], axis=1)
    b_router = jnp.concatenate([bg_r, be_r, jnp.zeros((pad,), F32)]).reshape(1, LANES)
    xn, ids, wts, cnt = route(h, ffn_gain, w_router, b_router)

    counts = cnt[0, N_GROUPS:N_GROUPS + N_EXPERTS].astype(jnp.int32)
    nblk = (counts + tb - 1) // tb
    bend = jnp.cumsum(nblk)
    nact = bend[-1]
    start_rows = jnp.pad(((bend - nblk) * tb).astype(F32), (0, LANES - N_EXPERTS)).reshape(1, LANES)
    dest = assignment_rows(ids, start_rows).reshape(a)
    blk = jnp.minimum(jnp.arange(nb, dtype=jnp.int32), jnp.maximum(nact - 1, 0))
    owner = lambda bi: jnp.minimum(jnp.sum((bend[None, :] <= bi[:, None]).astype(jnp.int32), axis=1), N_EXPERTS - 1)
    blk_e = owner(blk)
    k_in_e = blk - (bend - nblk)[blk_e]
    following = lambda e: jnp.where(bend[e] < nact, owner(bend[e]), e)
    nxt_e = following(blk_e)
    wset = (jnp.cumsum((k_in_e == 0).astype(jnp.int32)) - 1) % 2
    tok = jnp.tile(jnp.arange(n, dtype=jnp.int32), TOP_K)
    row_tok = (jnp.arange(nb * tb, dtype=jnp.int32) % n).at[dest].set(tok)
    nact1 = nact.reshape(1).astype(jnp.int32)

    any_space = pl.BlockSpec(memory_space=pl.ANY)
    hid = pl.pallas_call(
        functools.partial(_moe_up_kernel, tb=tb, layer=layer),
        grid_spec=pltpu.PrefetchScalarGridSpec(
            num_scalar_prefetch=6, grid=(nb,),
            in_specs=[any_space, any_space, any_space],
            out_specs=pl.BlockSpec((tb, f), lambda b, *_: (b, 0)),
            scratch_shapes=[pltpu.VMEM((GATHER_AHEAD + 1, tb, d // 2), jnp.uint32),
                            pltpu.VMEM((2, d, f), F32), pltpu.VMEM((2, d, f), F32), pltpu.VMEM((tb, d), BF),
                            pltpu.SemaphoreType.DMA((GATHER_AHEAD + 1,)), pltpu.SemaphoreType.DMA((2, 2))]),
        out_shape=jax.ShapeDtypeStruct((nb * tb, f), BF),
        compiler_params=_cparams(("arbitrary",), 56), name="moe_up",
    )(blk_e, k_in_e, wset, nxt_e, nact1, row_tok, xn, w_gate, w_up)

    ys = pl.pallas_call(
        functools.partial(_moe_down_kernel, layer=layer),
        grid_spec=pltpu.PrefetchScalarGridSpec(
            num_scalar_prefetch=5, grid=(nb,),
            in_specs=[pl.BlockSpec((tb, f), lambda b, *_: (b, 0)), any_space],
            out_specs=pl.BlockSpec((tb, d // 2), lambda b, *_: (b, 0)),
            scratch_shapes=[pltpu.VMEM((2, f, d), F32), pltpu.SemaphoreType.DMA((2,))]),
        out_shape=jax.ShapeDtypeStruct((nb * tb, d // 2), jnp.uint32),
        compiler_params=_cparams(("arbitrary",), 40), name="moe_down",
    )(blk_e, k_in_e, wset, nxt_e, nact1, hid, w_down)

    tc = min(COMBINE_ROWS, n)
    gain = (final_gain if final_gain is not None else ffn_gain).reshape(1, d)
    return pl.pallas_call(
        functools.partial(_combine_kernel, tc=tc, n=n, final_norm=final_gain is not None),
        grid_spec=pltpu.PrefetchScalarGridSpec(
            num_scalar_prefetch=1, grid=(n // tc,),
            in_specs=[pl.BlockSpec(memory_space=pl.ANY), pl.BlockSpec((tc, d), lambda i, *_: (i, 0)),
                      pl.BlockSpec((tc, LANES), lambda i, *_: (i, 0)), pl.BlockSpec((1, d), lambda i, *_: (0, 0))],
            out_specs=pl.BlockSpec((tc, d), lambda i, *_: (i, 0)),
            scratch_shapes=[pltpu.VMEM((2, TOP_K * tc, d // 2), jnp.uint32), pltpu.SemaphoreType.DMA((2,))]),
        out_shape=jax.ShapeDtypeStruct((n, d), F32),
        compiler_params=_cparams(("arbitrary",), 40), name="moe_combine",
    )(dest, ys, h, wts, gain)


def even_mixer(h, gain, w_in, b_gates, pool_w, pool_scale, head_gain, w_out, j):
    d = h.shape[1]
    pool_width = N_POOL_GROUPS * pool_w.shape[-1]
    mlstm_width = head_gain.shape[0]
    main_cols = pool_width + 4 * mlstm_width
    n_gates = 2 * MLSTM_HEADS
    gate_b = jnp.pad(b_gates, (0, LANES - n_gates)).reshape(1, LANES)
    z, gates = norm_matmul_gates_t(h, gain, jnp.swapaxes(w_in, 1, 2), main_cols, n_gates, gate_b, j)
    y_p = pool_mixer(z, pool_w, pool_scale)
    y_m = mlstm_mixer(z, gates, head_gain, pool_width)
    assert pool_width == mlstm_width and pool_width + mlstm_width == d
    return matmul_residual([y_p, y_m], w_out, h, layer=j)


def odd_mixer(h, gain, w_in, conv_w, w_out, j):
    z = norm_matmul(h, gain, w_in, w_in.shape[2], layer=j)
    return conv_matmul_residual(z, conv_w, w_out, h, j)


def cross_attn(h, mem, gain, mem_gain, wq, wk, wv, wo, layer):
    d = h.shape[1]
    k = norm_matmul(mem, mem_gain, wk, d, layer=layer)
    v = norm_matmul(mem, mem_gain, wv, d, layer=layer)
    return matmul_residual([q_attention(h, gain, wq, k, v, layer)], wo, h, layer=layer)


def kernel(x, mem, mix_norm, xattn_norm, mem_norm, ffn_norm, final_norm, ev_w_in, ev_b_gates, ev_pool_w, ev_pool_scale, ev_head_norm, ev_w_out, od_w_in, od_conv_w, od_w_out, xa_wq, xa_wk, xa_wv, xa_wo, rt_group_w, rt_group_b, rt_expert_w, rt_expert_b, ex_w_gate, ex_w_up, ex_w_down):
    depth = mix_norm.shape[0]
    h = x[0]
    m = mem[0]
    for layer in range(depth):
        j = layer // 2
        if layer % 2 == 0:
            h = even_mixer(h, mix_norm[layer], ev_w_in, ev_b_gates[j], ev_pool_w[j], ev_pool_scale[j],
                           ev_head_norm[j], ev_w_out, j)
        else:
            h = odd_mixer(h, mix_norm[layer], od_w_in, od_conv_w[j], od_w_out, j)
        h = cross_attn(h, m, xattn_norm[layer], mem_norm[layer], xa_wq, xa_wk, xa_wv, xa_wo, layer)
        h = hier_moe(h, ffn_norm[layer], rt_group_w[layer], rt_group_b[layer], rt_expert_w[layer],
                     rt_expert_b[layer], ex_w_gate, ex_w_up, ex_w_down, layer,
                     final_gain=final_norm if layer == depth - 1 else None)
    return h[None]
```

```python
import functools

import jax
import jax.numpy as jnp
from jax import lax
from jax.experimental import pallas as pl
from jax.experimental.pallas import tpu as pltpu

F32 = jnp.float32
BF = jnp.bfloat16
EPS = 1e-6

POOL_WINDOWS = (2, 4, 8, 16)
N_POOL_GROUPS = 4
MLSTM_HEADS = 4
FORGET_LANE0 = MLSTM_HEADS
XATTN_HEADS = 4
N_GROUPS = 4
EXPERTS_PER_GROUP = 8
N_EXPERTS = N_GROUPS * EXPERTS_PER_GROUP
TOP_K = 2
CONV_WIDTH = 3

LANES = 128
SUBLANES = 8
DMA_QUEUE_ROWS = 0
DMA_QUEUE_BULK = 1

ROW_TILE = 1024
COL_TILE = 1024
OUT_ROW_TILE = 512
MOE_COL_CHUNK = 256
WEIGHT_DMA_PARTS = 4
MLSTM_CHUNK = 256
POOL_ROWS = 256
POOL_HALO = 128
CONV_ROWS = 512
CONV_COLS = 512
CONV_HALO = 16
ROUTER_ROWS = 512
GATHER_AHEAD = 2
COMBINE_ROWS = 256


def _moe_block_rows(assignments):
    mean_rows = assignments // N_EXPERTS
    return -(-(mean_rows * 9 // 16) // 16) * 16


def _cparams(semantics, vmem_mib):
    return pltpu.CompilerParams(dimension_semantics=semantics, vmem_limit_bytes=vmem_mib * 1024 * 1024)


def _sigmoid(x):
    return 1.0 / (1.0 + jnp.exp(-x))


def _log_sigmoid(x):
    return jnp.minimum(x, 0.0) - jnp.log(1.0 + jnp.exp(-jnp.abs(x)))


def _rms(x, g):
    ms = jnp.mean(x * x, axis=-1, keepdims=True)
    return x * lax.rsqrt(ms + EPS) * g


def _pack_halves(x):
    half = x.shape[1] // 2
    lo = pltpu.bitcast(x[:, :half].astype(BF).astype(F32), jnp.uint32)
    hi = pltpu.bitcast(x[:, half:].astype(BF).astype(F32), jnp.uint32)
    return hi | lax.shift_right_logical(lo, jnp.uint32(16))


def _unpack_halves(w):
    lo = pltpu.bitcast(lax.shift_left(w, jnp.uint32(16)), F32)
    hi = pltpu.bitcast(w & jnp.uint32(0xFFFF0000), F32)
    return lo, hi


def _norm_mm_kernel(x_ref, g_ref, w_ref, o_ref, xn_ref):
    @pl.when(pl.program_id(1) == 0)
    def _():
        xn_ref[...] = _rms(x_ref[...], g_ref[...]).astype(BF)

    o_ref[...] = jnp.dot(xn_ref[...], w_ref[...].astype(BF), preferred_element_type=F32).astype(o_ref.dtype)


_NT = (((1,), (1,)), ((), ()))


def _norm_mm_gates_t_kernel(x_ref, g_ref, wt_ref, wgt_ref, bg_ref, o_ref, gates_ref, xn_ref):
    @pl.when(pl.program_id(1) == 0)
    def _():
        xn = _rms(x_ref[...], g_ref[...]).astype(BF)
        xn_ref[...] = xn
        wg = wgt_ref[...]
        wg = jnp.concatenate([wg, jnp.zeros((LANES - wg.shape[0], wg.shape[1]), F32)], axis=0).astype(BF)
        gates_ref[...] = lax.dot_general(xn, wg, _NT, preferred_element_type=F32) + bg_ref[...]

    o_ref[...] = lax.dot_general(xn_ref[...], wt_ref[...].astype(BF), _NT,
                                 preferred_element_type=F32).astype(o_ref.dtype)


def norm_matmul_gates_t(x, gain, wt, n_cols, n_gates, gate_b, layer):
    n, k = x.shape
    tm = min(ROW_TILE, n)
    tn = COL_TILE
    return pl.pallas_call(
        _norm_mm_gates_t_kernel, grid=(n // tm, n_cols // tn),
        in_specs=[pl.BlockSpec((tm, k), lambda i, j: (i, 0)), pl.BlockSpec((1, k), lambda i, j: (0, 0)),
                  pl.BlockSpec((None, tn, k), lambda i, j: (layer, j, 0)),
                  pl.BlockSpec((None, n_gates, k), lambda i, j: (layer, n_cols // n_gates, 0)),
                  pl.BlockSpec((1, LANES), lambda i, j: (0, 0))],
        out_specs=[pl.BlockSpec((tm, tn), lambda i, j: (i, j)), pl.BlockSpec((tm, LANES), lambda i, j: (i, 0))],
        out_shape=[jax.ShapeDtypeStruct((n, n_cols), BF), jax.ShapeDtypeStruct((n, LANES), F32)],
        scratch_shapes=[pltpu.VMEM((tm, k), BF)],
        compiler_params=_cparams(("parallel", "arbitrary"), 56), name="norm_matmul_gates",
    )(x, gain.reshape(1, k), wt, wt, gate_b)


def _stacked(w, layer):
    return (w[None], 0) if layer is None else (w, layer)


def norm_matmul(x, gain, w, n_cols, layer=None):
    n, k = x.shape
    w, li = _stacked(w, layer)
    tm = min(ROW_TILE, n)
    tn = COL_TILE
    return pl.pallas_call(
        _norm_mm_kernel, grid=(n // tm, n_cols // tn),
        in_specs=[pl.BlockSpec((tm, k), lambda i, j: (i, 0)), pl.BlockSpec((1, k), lambda i, j: (0, 0)),
                  pl.BlockSpec((None, k, tn), lambda i, j: (li, 0, j))],
        out_specs=pl.BlockSpec((tm, tn), lambda i, j: (i, j)),
        out_shape=jax.ShapeDtypeStruct((n, n_cols), BF), scratch_shapes=[pltpu.VMEM((tm, k), BF)],
        compiler_params=_cparams(("parallel", "arbitrary"), 56), name="norm_matmul",
    )(x, gain.reshape(1, k), w)


def _q_attention_kernel(x_ref, g_ref, w_ref, k_ref, v_ref, o_ref, q_ref, *, hd):
    xn = _rms(x_ref[...], g_ref[...]).astype(BF)
    q_ref[...] = jnp.dot(xn, w_ref[...].astype(BF), preferred_element_type=F32).astype(BF)
    scale = hd ** -0.5
    for h in range(XATTN_HEADS):
        hs = slice(h * hd, (h + 1) * hd)
        s = lax.dot_general(q_ref[:, hs], k_ref[:, hs], _NT, preferred_element_type=F32) * scale
        e = jnp.exp(s - jnp.max(s, axis=-1, keepdims=True))
        p = (e / jnp.sum(e, axis=-1, keepdims=True)).astype(BF)
        o_ref[:, hs] = jnp.dot(p, v_ref[:, hs], preferred_element_type=F32).astype(o_ref.dtype)


def q_attention(x, gain, wq, k, v, layer):
    n, kd = x.shape
    d = wq.shape[2]
    m = k.shape[0]
    tm = min(OUT_ROW_TILE, n)
    return pl.pallas_call(
        functools.partial(_q_attention_kernel, hd=d // XATTN_HEADS), grid=(n // tm,),
        in_specs=[pl.BlockSpec((tm, kd), lambda i: (i, 0)), pl.BlockSpec((1, kd), lambda i: (0, 0)),
                  pl.BlockSpec((None, kd, d), lambda i: (layer, 0, 0), pipeline_mode=pl.Buffered(1)),
                  pl.BlockSpec((m, d), lambda i: (0, 0)), pl.BlockSpec((m, d), lambda i: (0, 0))],
        out_specs=pl.BlockSpec((tm, d), lambda i: (i, 0)), out_shape=jax.ShapeDtypeStruct((n, d), BF),
        scratch_shapes=[pltpu.VMEM((tm, d), BF)],
        compiler_params=_cparams(("parallel",), 52), name="q_attention",
    )(x, gain.reshape(1, kd), wq, k, v)


def _mm_res_kernel(*refs, nparts):
    xs = refs[:nparts]
    w_ref, res_ref, o_ref = refs[nparts:]
    acc = res_ref[...]
    k0 = 0
    for x_ref in xs:
        kp = x_ref.shape[1]
        acc = acc + jnp.dot(x_ref[...], w_ref[k0:k0 + kp, :].astype(BF), preferred_element_type=F32)
        k0 += kp
    o_ref[...] = acc


def matmul_residual(xs, w, res, layer=None):
    n, d = res.shape
    w, li = _stacked(w, layer)
    k = w.shape[1]
    tm = min(OUT_ROW_TILE, n)
    in_specs = [pl.BlockSpec((tm, x.shape[1]), lambda i: (i, 0)) for x in xs]
    in_specs += [pl.BlockSpec((None, k, d), lambda i: (li, 0, 0), pipeline_mode=pl.Buffered(1)),
                 pl.BlockSpec((tm, d), lambda i: (i, 0))]
    return pl.pallas_call(
        functools.partial(_mm_res_kernel, nparts=len(xs)), grid=(n // tm,), in_specs=in_specs,
        out_specs=pl.BlockSpec((tm, d), lambda i: (i, 0)), out_shape=jax.ShapeDtypeStruct((n, d), F32),
        compiler_params=_cparams(("parallel",), 52), name="matmul_residual",
    )(*xs, w, res)


def _pool_kernel(cur_ref, prev_ref, w_ref, sc_ref, o_ref, *, tp, gdim):
    i = pl.program_id(0)
    dist = lax.broadcasted_iota(jnp.int32, (tp, tp), 0) - lax.broadcasted_iota(jnp.int32, (tp, tp), 1)
    distp = (lax.broadcasted_iota(jnp.int32, (tp, POOL_HALO), 0) + POOL_HALO
             - lax.broadcasted_iota(jnp.int32, (tp, POOL_HALO), 1))
    pos = i * tp + lax.broadcasted_iota(jnp.int32, (tp, 1), 0)
    for j, win in enumerate(POOL_WINDOWS):
        gs = slice(j * gdim, (j + 1) * gdim)
        cur = cur_ref[:, gs]
        band = jnp.where(dist >= 0, jnp.where(dist < win, 1.0, 0.0), 0.0).astype(BF)
        bandp = jnp.where(distp < jnp.where(i > 0, win, 0), 1.0, 0.0).astype(BF)
        s = (jnp.dot(band, cur, preferred_element_type=F32)
             + jnp.dot(bandp, prev_ref[:, gs], preferred_element_type=F32))
        cnt = jnp.minimum(pos + 1, win).astype(F32)
        d = s / cnt - cur.astype(F32)
        y = jnp.dot(d.astype(BF), w_ref[j].astype(BF), preferred_element_type=F32) * sc_ref[:, gs]
        o_ref[:, gs] = y.astype(o_ref.dtype)


def pool_mixer(z, pool_w, pool_scale):
    n = z.shape[0]
    gdim = pool_w.shape[-1]
    width = N_POOL_GROUPS * gdim
    tp = min(POOL_ROWS, n)
    halo_blocks = tp // POOL_HALO
    return pl.pallas_call(
        functools.partial(_pool_kernel, tp=tp, gdim=gdim), grid=(n // tp,),
        in_specs=[
            pl.BlockSpec((tp, width), lambda i: (i, 0)),
            pl.BlockSpec((POOL_HALO, width), lambda i: (jnp.maximum(i * halo_blocks - 1, 0), 0)),
            pl.BlockSpec((N_POOL_GROUPS, gdim, gdim), lambda i: (0, 0, 0)),
            pl.BlockSpec((1, width), lambda i: (0, 0)),
        ],
        out_specs=pl.BlockSpec((tp, width), lambda i: (i, 0)),
        out_shape=jax.ShapeDtypeStruct((n, width), BF),
        compiler_params=_cparams(("parallel",), 32), name="pool_mixer",
    )(z, z, pool_w, pool_scale.reshape(1, width))


def _mlstm_kernel(q_ref, k_ref, v_ref, o_ref, g_ref, gain_ref, y_ref, ct_ref, m_ref, *, chunk, dh):
    c = pl.program_id(0)

    @pl.when(c == 0)
    def _():
        ct_ref[...] = jnp.zeros_like(ct_ref)
        m_ref[...] = jnp.zeros_like(m_ref)

    g = g_ref[...]
    lf = _log_sigmoid(g)
    row = lax.broadcasted_iota(jnp.int32, (chunk, chunk), 0)
    col = lax.broadcasted_iota(jnp.int32, (chunk, chunk), 1)
    causal = col <= row
    ltri = jnp.where(causal, 1.0, 0.0).astype(BF)
    hi = lf.astype(BF)
    r1 = lf - hi.astype(F32)
    mid = r1.astype(BF)
    lo = (r1 - mid.astype(F32)).astype(BF)
    bcum = (jnp.dot(ltri, hi, preferred_element_type=F32) + jnp.dot(ltri, mid, preferred_element_type=F32)
            + jnp.dot(ltri, lo, preferred_element_type=F32))
    g_t = g.T
    b_t = bcum.T
    ones_col = jnp.where(lax.broadcasted_iota(jnp.int32, (chunk, LANES), 1) == 0, 1.0, 0.0).astype(BF)

    for h in range(MLSTM_HEADS):
        hs = slice(h * dh, (h + 1) * dh)
        fl = FORGET_LANE0 + h
        bc = bcum[:, fl:fl + 1]
        br = b_t[fl:fl + 1, :]
        ir = g_t[h:h + 1, :]
        b_last = bcum[chunk - 1:chunk, fl:fl + 1]
        m_prev = m_ref[h][:, 0:1]

        dmat = jnp.where(causal, bc + (ir - br), -jnp.inf)
        inter = bc + m_prev
        m_t = jnp.maximum(jnp.max(dmat, axis=1, keepdims=True), inter)
        w_inter = jnp.exp(inter - m_t)
        p = jnp.exp(dmat - m_t)

        qh = q_ref[:, hs] * (dh ** -0.5)
        kh = k_ref[:, hs]
        v_aug = jnp.concatenate([v_ref[:, hs], ones_col], axis=1)
        s = lax.dot_general(qh, kh, (((1,), (1,)), ((), ())), preferred_element_type=F32)
        sc = (s * p).astype(BF)
        ct = ct_ref[h]
        num_aug = (w_inter * jnp.dot(qh, ct.astype(BF), preferred_element_type=F32)
                   + jnp.dot(sc, v_aug, preferred_element_type=F32))
        num = num_aug[:, :dh]
        den = num_aug[:, dh:dh + 1]
        hout = num / jnp.maximum(jnp.abs(den), jnp.exp(-m_t))

        yn = _rms(hout, gain_ref[:, hs])
        y_ref[:, hs] = (_sigmoid(o_ref[:, hs].astype(F32)) * yn).astype(y_ref.dtype)

        d_end = b_last - br + ir
        m_new = jnp.maximum(b_last + m_prev, jnp.max(d_end, axis=1, keepdims=True))
        a_prev = jnp.exp(b_last + m_prev - m_new)
        a_s = jnp.exp(d_end - m_new)
        k_t = (kh.astype(F32).T * a_s).astype(BF)
        ct_ref[h] = a_prev * ct + jnp.dot(k_t, v_aug, preferred_element_type=F32)
        m_ref[h] = jnp.broadcast_to(m_new, (1, LANES))


def mlstm_mixer(z, gates, head_gain, col0):
    n = z.shape[0]
    width = head_gain.shape[0]
    dh = width // MLSTM_HEADS
    chunk = min(MLSTM_CHUNK, n)
    base = col0 // width
    qkvo = [pl.BlockSpec((chunk, width), lambda c, p=p: (c, base + p)) for p in range(4)]
    return pl.pallas_call(
        functools.partial(_mlstm_kernel, chunk=chunk, dh=dh), grid=(n // chunk,),
        in_specs=qkvo + [pl.BlockSpec((chunk, LANES), lambda c: (c, 0)), pl.BlockSpec((1, width), lambda c: (0, 0))],
        out_specs=pl.BlockSpec((chunk, width), lambda c: (c, 0)),
        out_shape=jax.ShapeDtypeStruct((n, width), BF),
        scratch_shapes=[pltpu.VMEM((MLSTM_HEADS, dh, dh + LANES), F32), pltpu.VMEM((MLSTM_HEADS, 1, LANES), F32)],
        compiler_params=_cparams(("arbitrary",), 32), name="mlstm_mixer",
    )(z, z, z, z, gates, head_gain.reshape(1, width))


def _gated_conv(b, c, u, cp, up, w, first_block):
    zc = c.astype(F32) * u.astype(F32)
    zp = jnp.where(first_block, 0.0, cp.astype(F32) * up.astype(F32))
    row = lax.broadcasted_iota(jnp.int32, zc.shape, 0)
    acc = w[CONV_WIDTH - 1:CONV_WIDTH, :] * zc
    for back in range(1, CONV_WIDTH):
        shifted = pltpu.roll(zc, back, 0)
        for r in range(back):
            shifted = jnp.where(row == r, zp[CONV_HALO - back + r:CONV_HALO - back + r + 1, :], shifted)
        acc = acc + w[CONV_WIDTH - 1 - back:CONV_WIDTH - back, :] * shifted
    return b.astype(F32) * acc


def _conv_mm_res_kernel(b_ref, c_ref, u_ref, cp_ref, up_ref, cw_ref, w_ref, res_ref, o_ref, xs_ref):
    first_block = pl.program_id(0) == 0
    for c0 in range(0, xs_ref.shape[1], CONV_COLS):
        cs = slice(c0, c0 + CONV_COLS)
        xs_ref[:, cs] = _gated_conv(b_ref[:, cs], c_ref[:, cs], u_ref[:, cs], cp_ref[:, cs], up_ref[:, cs],
                                    cw_ref[:, cs], first_block).astype(BF)
    o_ref[...] = res_ref[...] + jnp.dot(xs_ref[...], w_ref[...].astype(BF), preferred_element_type=F32)


def conv_matmul_residual(z, conv_w, w, res, layer):
    n, d = res.shape
    tm = min(CONV_ROWS, n)
    halo_blocks = tm // CONV_HALO
    cur = lambda part: pl.BlockSpec((tm, d), lambda i: (i, part))
    prev = lambda part: pl.BlockSpec((CONV_HALO, d), lambda i: (jnp.maximum(i * halo_blocks - 1, 0), part))
    return pl.pallas_call(
        _conv_mm_res_kernel, grid=(n // tm,),
        in_specs=[cur(0), cur(1), cur(2), prev(1), prev(2), pl.BlockSpec((CONV_WIDTH, d), lambda i: (0, 0)),
                  pl.BlockSpec((None, d, d), lambda i: (layer, 0, 0), pipeline_mode=pl.Buffered(1)),
                  pl.BlockSpec((tm, d), lambda i: (i, 0))],
        out_specs=pl.BlockSpec((tm, d), lambda i: (i, 0)), out_shape=jax.ShapeDtypeStruct((n, d), F32),
        scratch_shapes=[pltpu.VMEM((tm, d), BF)],
        compiler_params=_cparams(("parallel",), 52), name="conv_matmul_residual",
    )(z, z, z, z, z, conv_w, w, res)


def _router_kernel(x_ref, g_ref, wr_ref, br_ref, xn_ref, ids_ref, wts_ref, cnt_ref, carry_ref, w2_ref, *, tr):
    @pl.when(pl.program_id(0) == 0)
    def _():
        carry_ref[...] = jnp.zeros_like(carry_ref)

    @pl.when(pl.program_id(0) == 0)
    def _():
        w = wr_ref[...]
        w_hi = w.astype(BF)
        w2_ref[:, :LANES] = w_hi
        w2_ref[:, LANES:] = (w - w_hi.astype(F32)).astype(BF)

    xn = _rms(x_ref[...], g_ref[...])
    xn_ref[...] = _pack_halves(xn)
    x_hi = xn.astype(BF)
    x_lo = (xn - x_hi.astype(F32)).astype(BF)
    p_hi = jnp.dot(x_hi, w2_ref[...], preferred_element_type=F32)
    p_lo = jnp.dot(x_lo, w2_ref[...], preferred_element_type=F32)
    logits = (p_hi[:, :LANES] + p_hi[:, LANES:]) + (p_lo[:, :LANES] + p_lo[:, LANES:]) + br_ref[...]
    lane = lax.broadcasted_iota(jnp.int32, (tr, LANES), 1).astype(F32)
    neg = -jnp.inf

    def first_argmax(vals):
        top = jnp.max(vals, axis=-1, keepdims=True)
        return top, jnp.min(jnp.where(vals == top, lane, float(LANES)), axis=-1, keepdims=True)

    gl = jnp.where(lane < N_GROUPS, logits, neg)
    gmax, grp = first_argmax(gl)
    g_prob = 1.0 / jnp.sum(jnp.exp(gl - gmax), axis=-1, keepdims=True)
    lo = N_GROUPS + EXPERTS_PER_GROUP * grp
    el = jnp.where(lane >= lo, jnp.where(lane < lo + EXPERTS_PER_GROUP, logits, neg), neg)
    v1, l1 = first_argmax(el)
    v2, l2 = first_argmax(jnp.where(lane == l1, neg, el))
    e2 = jnp.exp(v2 - v1)
    w1 = g_prob / (1.0 + e2)
    w2 = g_prob * e2 / (1.0 + e2)
    hot1 = lane == l1
    hot2 = lane == l2
    hot = jnp.where(hot1, 1.0, jnp.where(hot2, 1.0, 0.0))
    earlier = (lax.broadcasted_iota(jnp.int32, (tr, tr), 1) < lax.broadcasted_iota(jnp.int32, (tr, tr), 0))
    before = jnp.dot(jnp.where(earlier, 1.0, 0.0).astype(BF), hot.astype(BF), preferred_element_type=F32)
    before = before + carry_ref[0:1, :]
    r1 = jnp.sum(jnp.where(hot1, before, 0.0), axis=-1, keepdims=True)
    r2 = jnp.sum(jnp.where(hot2, before, 0.0), axis=-1, keepdims=True)
    carry_ref[0:1, :] = carry_ref[0:1, :] + jnp.sum(hot, axis=0, keepdims=True)
    ids = jnp.where(lane == 0, l1 - N_GROUPS, jnp.where(lane == 1, l2 - N_GROUPS,
                    jnp.where(lane == 2, r1, jnp.where(lane == 3, r2, 0.0))))
    ids_ref[...] = ids.astype(jnp.int32)
    wts_ref[...] = jnp.where(lane == 0, w1, jnp.where(lane == 1, w2, 0.0))
    cnt_ref[...] = carry_ref[...]


def route(h, gain, w_router, b_router):
    n, d = h.shape
    tr = min(ROUTER_ROWS, n)
    return pl.pallas_call(
        functools.partial(_router_kernel, tr=tr), grid=(n // tr,),
        in_specs=[pl.BlockSpec((tr, d), lambda i: (i, 0)), pl.BlockSpec((1, d), lambda i: (0, 0)),
                  pl.BlockSpec((d, LANES), lambda i: (0, 0)), pl.BlockSpec((1, LANES), lambda i: (0, 0))],
        out_specs=[pl.BlockSpec((tr, d // 2), lambda i: (i, 0)),
                   pl.BlockSpec((tr, LANES), lambda i: (i, 0)), pl.BlockSpec((tr, LANES), lambda i: (i, 0)),
                   pl.BlockSpec((SUBLANES, LANES), lambda i: (0, 0))],
        out_shape=[jax.ShapeDtypeStruct((n, d // 2), jnp.uint32), jax.ShapeDtypeStruct((n, LANES), jnp.int32),
                   jax.ShapeDtypeStruct((n, LANES), F32), jax.ShapeDtypeStruct((SUBLANES, LANES), F32)],
        scratch_shapes=[pltpu.VMEM((SUBLANES, LANES), F32), pltpu.VMEM((d, 2 * LANES), BF)],
        compiler_params=_cparams(("arbitrary",), 40), name="moe_router",
    )(h, gain.reshape(1, d), w_router, b_router)


def _dest_kernel(ids_ref, start_ref, o_ref, *, tr):
    ids = ids_ref[...].astype(F32)
    lane = lax.broadcasted_iota(jnp.int32, (tr, LANES), 1).astype(F32)
    start = start_ref[...]
    rows = []
    for k in range(TOP_K):
        first = jnp.sum(jnp.where(lane == ids[:, k:k + 1], start, 0.0), axis=-1, keepdims=True)
        rows.append(first + ids[:, TOP_K + k:TOP_K + k + 1])
    packed = jnp.where(lane == 0, rows[0], jnp.where(lane == 1, rows[1], 0.0))
    o_ref[...] = packed.T[0:SUBLANES, :].astype(jnp.int32)


def assignment_rows(ids, start_rows):
    n = ids.shape[0]
    tr = min(ROUTER_ROWS, n)
    out = pl.pallas_call(
        functools.partial(_dest_kernel, tr=tr), grid=(n // tr,),
        in_specs=[pl.BlockSpec((tr, LANES), lambda i: (i, 0)), pl.BlockSpec((1, LANES), lambda i: (0, 0))],
        out_specs=pl.BlockSpec((SUBLANES, tr), lambda i: (0, i)),
        out_shape=jax.ShapeDtypeStruct((SUBLANES, n), jnp.int32),
        compiler_params=_cparams(("parallel",), 32), name="moe_assignment_rows",
    )(ids, start_rows)
    return out[:TOP_K]


def _row_copy(src_hbm, row, dst, dst_row, sem):
    return pltpu.make_async_copy(src_hbm.at[pl.ds(row, 1), :], dst.at[pl.ds(dst_row, 1), :], sem)


def _rows_wait(src_hbm, dst, sem):
    pltpu.make_async_copy(src_hbm.at[pl.ds(0, dst.shape[0]), :], dst, sem).wait()


def _expert_weight_copies(w_hbm, layer, expert, stage, wset, sem):
    rows = stage.shape[1] // WEIGHT_DMA_PARTS
    return [pltpu.make_async_copy(w_hbm.at[layer, expert, pl.ds(p * rows, rows), :],
                                  stage.at[wset, pl.ds(p * rows, rows), :], sem.at[wset])
            for p in range(WEIGHT_DMA_PARTS)]


def _stream_expert_weights(b, nact, blk_ref, kin_ref, set_ref, nxt_ref, copies):
    cur = set_ref[b]

    @pl.when(b == 0)
    def _():
        for cp in copies(blk_ref[0], cur):
            cp.start(priority=DMA_QUEUE_BULK)

    @pl.when(jnp.logical_and(b < nact, kin_ref[b] == 0))
    def _():
        for cp in copies(blk_ref[b], cur):
            cp.wait()

        @pl.when(nxt_ref[b] != blk_ref[b])
        def _():
            for cp in copies(nxt_ref[b], 1 - cur):
                cp.start(priority=DMA_QUEUE_BULK)

    return cur


def _moe_up_kernel(blk_ref, kin_ref, set_ref, nxt_ref, nact_ref, rtok_ref, xn_hbm, wg_hbm, wu_hbm, hid_ref,
                   buf, stage_g, stage_u, xb, sem, wsem, *, tb, layer):
    b = pl.program_id(0)
    nact = nact_ref[0]
    nslots = GATHER_AHEAD + 1
    slot = b % nslots
    f = hid_ref.shape[1]
    up_chunks = f // MOE_COL_CHUNK
    rows_per_chunk = tb // up_chunks

    @pl.when(b == 0)
    def _():
        for ahead in range(GATHER_AHEAD):
            def body(r, carry, ahead=ahead):
                _row_copy(xn_hbm, rtok_ref[ahead * tb + r], buf.at[ahead], r,
                          sem.at[ahead]).start(priority=DMA_QUEUE_ROWS)
                return carry

            lax.fori_loop(0, tb, body, 0, unroll=8)

    def copies(e, wset):
        return (_expert_weight_copies(wg_hbm, layer, e, stage_g, wset, wsem.at[0])
                + _expert_weight_copies(wu_hbm, layer, e, stage_u, wset, wsem.at[1]))

    cur = _stream_expert_weights(b, nact, blk_ref, kin_ref, set_ref, nxt_ref, copies)

    @pl.when(b < nact)
    def _():
        _rows_wait(xn_hbm, buf.at[slot], sem.at[slot])
        x_lo, x_hi = _unpack_halves(buf[slot])
        half = x_lo.shape[1]
        xb[:, :half] = x_lo.astype(BF)
        xb[:, half:] = x_hi.astype(BF)
        ahead_slot = (b + GATHER_AHEAD) % nslots
        for c in range(up_chunks):
            for r in range(c * rows_per_chunk, (c + 1) * rows_per_chunk):
                _row_copy(xn_hbm, rtok_ref[(b + GATHER_AHEAD) * tb + r], buf.at[ahead_slot], r,
                          sem.at[ahead_slot]).start(priority=DMA_QUEUE_ROWS)
            cs = slice(c * MOE_COL_CHUNK, (c + 1) * MOE_COL_CHUNK)
            gate = jnp.dot(xb[...], stage_g[cur, :, cs].astype(BF), preferred_element_type=F32)
            up = jnp.dot(xb[...], stage_u[cur, :, cs].astype(BF), preferred_element_type=F32)
            hid_ref[:, cs] = (gate * _sigmoid(gate) * up).astype(hid_ref.dtype)

    @pl.when(b >= nact)
    def _():
        hid_ref[...] = jnp.zeros_like(hid_ref)

    @pl.when(jnp.logical_and(b >= nact, b < nact + GATHER_AHEAD))
    def _():
        _rows_wait(xn_hbm, buf.at[slot], sem.at[slot])


def _moe_down_kernel(blk_ref, kin_ref, set_ref, nxt_ref, nact_ref, hid_ref, wd_hbm, ys_ref, stage_d, wsem, *, layer):
    b = pl.program_id(0)
    nact = nact_ref[0]
    half = ys_ref.shape[1]

    def copies(e, wset):
        return _expert_weight_copies(wd_hbm, layer, e, stage_d, wset, wsem)

    cur = _stream_expert_weights(b, nact, blk_ref, kin_ref, set_ref, nxt_ref, copies)

    @pl.when(b < nact)
    def _():
        for c in range(half // MOE_COL_CHUNK):
            cs = slice(c * MOE_COL_CHUNK, (c + 1) * MOE_COL_CHUNK)
            hs = slice(half + c * MOE_COL_CHUNK, half + (c + 1) * MOE_COL_CHUNK)
            lo = jnp.dot(hid_ref[...], stage_d[cur, :, cs].astype(BF), preferred_element_type=F32)
            hi = jnp.dot(hid_ref[...], stage_d[cur, :, hs].astype(BF), preferred_element_type=F32)
            ys_ref[:, cs] = _pack_halves(jnp.concatenate([lo, hi], axis=1))

    @pl.when(b >= nact)
    def _():
        ys_ref[...] = jnp.zeros_like(ys_ref)


def _moe_expert_kernel(blk_ref, kin_ref, last_ref, set_ref, nxt_ref, nact_ref, rtok_ref,
                       xn_hbm, wg_hbm, wu_hbm, wd_hbm, ys_ref,
                       buf, stage_g, stage_u, stage_d, xb, hid, sem, wsem, dsem, *, tb, layer):
    b = pl.program_id(0)
    nact = nact_ref[0]
    nslots = GATHER_AHEAD + 1
    slot = b % nslots
    f = hid.shape[1]
    half = ys_ref.shape[1]
    up_chunks = f // MOE_COL_CHUNK
    rows_per_chunk = tb // up_chunks
    active = b < nact
    first_of_expert = jnp.logical_and(active, kin_ref[b] == 0)
    has_next = nxt_ref[b] != blk_ref[b]

    def up_copies(e, wset):
        return (_expert_weight_copies(wg_hbm, layer, e, stage_g, wset, wsem.at[0])
                + _expert_weight_copies(wu_hbm, layer, e, stage_u, wset, wsem.at[1]))

    def down_copies(e):
        rows = stage_d.shape[0] // WEIGHT_DMA_PARTS
        return [pltpu.make_async_copy(wd_hbm.at[layer, e, pl.ds(p * rows, rows), :],
                                      stage_d.at[pl.ds(p * rows, rows), :], dsem.at[0])
                for p in range(WEIGHT_DMA_PARTS)]

    @pl.when(b == 0)
    def _():
        for ahead in range(GATHER_AHEAD):
            def body(r, carry, ahead=ahead):
                _row_copy(xn_hbm, rtok_ref[ahead * tb + r], buf.at[ahead], r,
                          sem.at[ahead]).start(priority=DMA_QUEUE_ROWS)
                return carry

            lax.fori_loop(0, tb, body, 0, unroll=8)
        for cp in down_copies(blk_ref[0]):
            cp.start(priority=DMA_QUEUE_BULK)

    cur = _stream_expert_weights(b, nact, blk_ref, kin_ref, set_ref, nxt_ref, up_copies)

    @pl.when(active)
    def _():
        _rows_wait(xn_hbm, buf.at[slot], sem.at[slot])
        x_lo, x_hi = _unpack_halves(buf[slot])
        xb[:, :x_lo.shape[1]] = x_lo.astype(BF)
        xb[:, x_lo.shape[1]:] = x_hi.astype(BF)
        ahead_slot = (b + GATHER_AHEAD) % nslots
        for c in range(up_chunks):
            for r in range(c * rows_per_chunk, (c + 1) * rows_per_chunk):
                _row_copy(xn_hbm, rtok_ref[(b + GATHER_AHEAD) * tb + r], buf.at[ahead_slot], r,
                          sem.at[ahead_slot]).start(priority=DMA_QUEUE_ROWS)
            cs = slice(c * MOE_COL_CHUNK, (c + 1) * MOE_COL_CHUNK)
            gate = jnp.dot(xb[...], stage_g[cur, :, cs].astype(BF), preferred_element_type=F32)
            up = jnp.dot(xb[...], stage_u[cur, :, cs].astype(BF), preferred_element_type=F32)
            hid[:, cs] = (gate * _sigmoid(gate) * up).astype(hid.dtype)

    @pl.when(first_of_expert)
    def _():
        for cp in down_copies(blk_ref[b]):
            cp.wait()

    @pl.when(active)
    def _():
        for c in range(half // MOE_COL_CHUNK):
            cs = slice(c * MOE_COL_CHUNK, (c + 1) * MOE_COL_CHUNK)
            hs = slice(half + c * MOE_COL_CHUNK, half + (c + 1) * MOE_COL_CHUNK)
            lo = jnp.dot(hid[...], stage_d[:, cs].astype(BF), preferred_element_type=F32)
            hi = jnp.dot(hid[...], stage_d[:, hs].astype(BF), preferred_element_type=F32)
            ys_ref[:, cs] = _pack_halves(jnp.concatenate([lo, hi], axis=1))

    @pl.when(jnp.logical_and(jnp.logical_and(active, last_ref[b] == 1), has_next))
    def _():
        for cp in down_copies(nxt_ref[b]):
            cp.start(priority=DMA_QUEUE_BULK)

    @pl.when(b >= nact)
    def _():
        ys_ref[...] = jnp.zeros_like(ys_ref)

    @pl.when(jnp.logical_and(b >= nact, b < nact + GATHER_AHEAD))
    def _():
        _rows_wait(xn_hbm, buf.at[slot], sem.at[slot])


def _combine_kernel(dest_ref, ys_hbm, h_ref, w_ref, gain_ref, o_ref, buf, sem, *, tc, n, final_norm):
    i = pl.program_id(0)

    def row_copy(blk, slot, r, k):
        return _row_copy(ys_hbm, dest_ref[k * n + blk * tc + r], buf.at[slot], k * tc + r, sem.at[slot])

    @pl.when(i == 0)
    def _():
        def body(r, carry):
            for k in range(TOP_K):
                row_copy(0, 0, r, k).start(priority=k)
            return carry

        lax.fori_loop(0, tc, body, 0, unroll=8)

    @pl.when(i + 1 < pl.num_programs(0))
    def _():
        for r in range(tc):
            for k in range(TOP_K):
                row_copy(i + 1, (i + 1) % 2, r, k).start(priority=k)

    slot = i % 2
    _rows_wait(ys_hbm, buf.at[slot], sem.at[slot])
    a_lo, a_hi = _unpack_halves(buf[slot, 0:tc, :])
    b_lo, b_hi = _unpack_halves(buf[slot, tc:TOP_K * tc, :])
    w0 = w_ref[:, 0:1]
    w1 = w_ref[:, 1:2]
    out = h_ref[...] + jnp.concatenate([w0 * a_lo + w1 * b_lo, w0 * a_hi + w1 * b_hi], axis=1)
    if final_norm:
        out = _rms(out, gain_ref[...])
    o_ref[...] = out


def hier_moe(h, ffn_gain, wg_r, bg_r, we_r, be_r, w_gate, w_up, w_down, layer, final_gain=None):
    n, d = h.shape
    f = w_gate.shape[-1]
    a = n * TOP_K
    tb = _moe_block_rows(a)
    nb = a // tb + N_EXPERTS + GATHER_AHEAD

    pad = LANES - N_GROUPS - N_EXPERTS
    w_router = jnp.concatenate([wg_r, we_r, jnp.zeros((d, pad), F32)], axis=1)
    b_router = jnp.concatenate([bg_r, be_r, jnp.zeros((pad,), F32)]).reshape(1, LANES)
    xn, ids, wts, cnt = route(h, ffn_gain, w_router, b_router)

    counts = cnt[0, N_GROUPS:N_GROUPS + N_EXPERTS].astype(jnp.int32)
    nblk = (counts + tb - 1) // tb
    bend = jnp.cumsum(nblk)
    nact = bend[-1]
    start_rows = jnp.pad(((bend - nblk) * tb).astype(F32), (0, LANES - N_EXPERTS)).reshape(1, LANES)
    dest = assignment_rows(ids, start_rows).reshape(a)
    blk = jnp.minimum(jnp.arange(nb, dtype=jnp.int32), jnp.maximum(nact - 1, 0))
    owner = lambda bi: jnp.minimum(jnp.sum((bend[None, :] <= bi[:, None]).astype(jnp.int32), axis=1), N_EXPERTS - 1)
    blk_e = owner(blk)
    k_in_e = blk - (bend - nblk)[blk_e]
    following = lambda e: jnp.where(bend[e] < nact, owner(bend[e]), e)
    nxt_e = following(blk_e)
    wset = (jnp.cumsum((k_in_e == 0).astype(jnp.int32)) - 1) % 2
    tok = jnp.tile(jnp.arange(n, dtype=jnp.int32), TOP_K)
    row_tok = (jnp.arange(nb * tb, dtype=jnp.int32) % n).at[dest].set(tok)
    nact1 = nact.reshape(1).astype(jnp.int32)

    any_space = pl.BlockSpec(memory_space=pl.ANY)
    is_last = (k_in_e == nblk[blk_e] - 1).astype(jnp.int32)
    ys = pl.pallas_call(
        functools.partial(_moe_expert_kernel, tb=tb, layer=layer),
        grid_spec=pltpu.PrefetchScalarGridSpec(
            num_scalar_prefetch=7, grid=(nb,),
            in_specs=[any_space, any_space, any_space, any_space],
            out_specs=pl.BlockSpec((tb, d // 2), lambda b, *_: (b, 0)),
            scratch_shapes=[pltpu.VMEM((GATHER_AHEAD + 1, tb, d // 2), jnp.uint32),
                            pltpu.VMEM((2, d, f), F32), pltpu.VMEM((2, d, f), F32), pltpu.VMEM((f, d), F32),
                            pltpu.VMEM((tb, d), BF), pltpu.VMEM((tb, f), BF),
                            pltpu.SemaphoreType.DMA((GATHER_AHEAD + 1,)), pltpu.SemaphoreType.DMA((2, 2)),
                            pltpu.SemaphoreType.DMA((1,))]),
        out_shape=jax.ShapeDtypeStruct((nb * tb, d // 2), jnp.uint32),
        compiler_params=_cparams(("arbitrary",), 56), name="moe_experts",
    )(blk_e, k_in_e, is_last, wset, nxt_e, nact1, row_tok, xn, w_gate, w_up, w_down)

    tc = min(COMBINE_ROWS, n)
    gain = (final_gain if final_gain is not None else ffn_gain).reshape(1, d)
    return pl.pallas_call(
        functools.partial(_combine_kernel, tc=tc, n=n, final_norm=final_gain is not None),
        grid_spec=pltpu.PrefetchScalarGridSpec(
            num_scalar_prefetch=1, grid=(n // tc,),
            in_specs=[pl.BlockSpec(memory_space=pl.ANY), pl.BlockSpec((tc, d), lambda i, *_: (i, 0)),
                      pl.BlockSpec((tc, LANES), lambda i, *_: (i, 0)), pl.BlockSpec((1, d), lambda i, *_: (0, 0))],
            out_specs=pl.BlockSpec((tc, d), lambda i, *_: (i, 0)),
            scratch_shapes=[pltpu.VMEM((2, TOP_K * tc, d // 2), jnp.uint32), pltpu.SemaphoreType.DMA((2,))]),
        out_shape=jax.ShapeDtypeStruct((n, d), F32),
        compiler_params=_cparams(("arbitrary",), 40), name="moe_combine",
    )(dest, ys, h, wts, gain)


def even_mixer(h, gain, w_in, b_gates, pool_w, pool_scale, head_gain, w_out, j):
    d = h.shape[1]
    pool_width = N_POOL_GROUPS * pool_w.shape[-1]
    mlstm_width = head_gain.shape[0]
    main_cols = pool_width + 4 * mlstm_width
    n_gates = 2 * MLSTM_HEADS
    gate_b = jnp.pad(b_gates, (0, LANES - n_gates)).reshape(1, LANES)
    z, gates = norm_matmul_gates_t(h, gain, jnp.swapaxes(w_in, 1, 2), main_cols, n_gates, gate_b, j)
    y_p = pool_mixer(z, pool_w, pool_scale)
    y_m = mlstm_mixer(z, gates, head_gain, pool_width)
    assert pool_width == mlstm_width and pool_width + mlstm_width == d
    return matmul_residual([y_p, y_m], w_out, h, layer=j)


def odd_mixer(h, gain, w_in, conv_w, w_out, j):
    z = norm_matmul(h, gain, w_in, w_in.shape[2], layer=j)
    return conv_matmul_residual(z, conv_w, w_out, h, j)


def cross_attn(h, mem, gain, mem_gain, wq, wk, wv, wo, layer):
    d = h.shape[1]
    k = norm_matmul(mem, mem_gain, wk, d, layer=layer)
    v = norm_matmul(mem, mem_gain, wv, d, layer=layer)
    return matmul_residual([q_attention(h, gain, wq, k, v, layer)], wo, h, layer=layer)


def kernel(x, mem, mix_norm, xattn_norm, mem_norm, ffn_norm, final_norm, ev_w_in, ev_b_gates, ev_pool_w, ev_pool_scale, ev_head_norm, ev_w_out, od_w_in, od_conv_w, od_w_out, xa_wq, xa_wk, xa_wv, xa_wo, rt_group_w, rt_group_b, rt_expert_w, rt_expert_b, ex_w_gate, ex_w_up, ex_w_down):
    depth = mix_norm.shape[0]
    h = x[0]
    m = mem[0]
    for layer in range(depth):
        j = layer // 2
        if layer % 2 == 0:
            h = even_mixer(h, mix_norm[layer], ev_w_in, ev_b_gates[j], ev_pool_w[j], ev_pool_scale[j],
                           ev_head_norm[j], ev_w_out, j)
        else:
            h = odd_mixer(h, mix_norm[layer], od_w_in, od_conv_w[j], od_w_out, j)
        h = cross_attn(h, m, xattn_norm[layer], mem_norm[layer], xa_wq, xa_wk, xa_wv, xa_wo, layer)
        h = hier_moe(h, ffn_norm[layer], rt_group_w[layer], rt_group_b[layer], rt_expert_w[layer],
                     rt_expert_b[layer], ex_w_gate, ex_w_up, ex_w_down, layer,
                     final_gain=final_norm if layer == depth - 1 else None)
    return h[None]
```

```python
import functools

import jax
import jax.numpy as jnp
from jax import lax
from jax.experimental import pallas as pl
from jax.experimental.pallas import tpu as pltpu

F32 = jnp.float32
BF = jnp.bfloat16
EPS = 1e-6

POOL_WINDOWS = (2, 4, 8, 16)
N_POOL_GROUPS = 4
MLSTM_HEADS = 4
FORGET_LANE0 = MLSTM_HEADS
XATTN_HEADS = 4
N_GROUPS = 4
EXPERTS_PER_GROUP = 8
N_EXPERTS = N_GROUPS * EXPERTS_PER_GROUP
TOP_K = 2
CONV_WIDTH = 3

LANES = 128
SUBLANES = 8
DMA_QUEUE_ROWS = 0
DMA_QUEUE_BULK = 1

ROW_TILE = 1024
COL_TILE = 1024
OUT_ROW_TILE = 512
MOE_COL_CHUNK = 256
WEIGHT_DMA_PARTS = 4
MLSTM_CHUNK = 256
POOL_ROWS = 256
POOL_HALO = 128
CONV_ROWS = 512
CONV_COLS = 512
CONV_HALO = 16
ROUTER_ROWS = 512
GATHER_AHEAD = 2
COMBINE_ROWS = 256


def _moe_block_rows(assignments):
    mean_rows = assignments // N_EXPERTS
    return -(-(mean_rows * 9 // 16) // 16) * 16


def _cparams(semantics, vmem_mib):
    return pltpu.CompilerParams(dimension_semantics=semantics, vmem_limit_bytes=vmem_mib * 1024 * 1024)


def _sigmoid(x):
    return 1.0 / (1.0 + jnp.exp(-x))


def _log_sigmoid(x):
    return jnp.minimum(x, 0.0) - jnp.log(1.0 + jnp.exp(-jnp.abs(x)))


def _rms(x, g):
    ms = jnp.mean(x * x, axis=-1, keepdims=True)
    return x * lax.rsqrt(ms + EPS) * g


def _pack_halves(x):
    half = x.shape[1] // 2
    lo = pltpu.bitcast(x[:, :half].astype(BF).astype(F32), jnp.uint32)
    hi = pltpu.bitcast(x[:, half:].astype(BF).astype(F32), jnp.uint32)
    return hi | lax.shift_right_logical(lo, jnp.uint32(16))


def _unpack_halves(w):
    lo = pltpu.bitcast(lax.shift_left(w, jnp.uint32(16)), F32)
    hi = pltpu.bitcast(w & jnp.uint32(0xFFFF0000), F32)
    return lo, hi


def _norm_mm_kernel(x_ref, g_ref, w_ref, o_ref, xn_ref):
    @pl.when(pl.program_id(1) == 0)
    def _():
        xn_ref[...] = _rms(x_ref[...], g_ref[...]).astype(BF)

    o_ref[...] = jnp.dot(xn_ref[...], w_ref[...].astype(BF), preferred_element_type=F32).astype(o_ref.dtype)


_NT = (((1,), (1,)), ((), ()))


def _norm_mm_gates_t_kernel(x_ref, g_ref, wt_ref, wgt_ref, bg_ref, o_ref, gates_ref, xn_ref):
    @pl.when(pl.program_id(1) == 0)
    def _():
        xn = _rms(x_ref[...], g_ref[...]).astype(BF)
        xn_ref[...] = xn
        wg = wgt_ref[...]
        wg = jnp.concatenate([wg, jnp.zeros((LANES - wg.shape[0], wg.shape[1]), F32)], axis=0).astype(BF)
        gates_ref[...] = lax.dot_general(xn, wg, _NT, preferred_element_type=F32) + bg_ref[...]

    o_ref[...] = lax.dot_general(xn_ref[...], wt_ref[...].astype(BF), _NT,
                                 preferred_element_type=F32).astype(o_ref.dtype)


def norm_matmul_gates_t(x, gain, wt, n_cols, n_gates, gate_b, layer):
    n, k = x.shape
    tm = min(ROW_TILE, n)
    tn = COL_TILE
    return pl.pallas_call(
        _norm_mm_gates_t_kernel, grid=(n // tm, n_cols // tn),
        in_specs=[pl.BlockSpec((tm, k), lambda i, j: (i, 0)), pl.BlockSpec((1, k), lambda i, j: (0, 0)),
                  pl.BlockSpec((None, tn, k), lambda i, j: (layer, j, 0)),
                  pl.BlockSpec((None, n_gates, k), lambda i, j: (layer, n_cols // n_gates, 0)),
                  pl.BlockSpec((1, LANES), lambda i, j: (0, 0))],
        out_specs=[pl.BlockSpec((tm, tn), lambda i, j: (i, j)), pl.BlockSpec((tm, LANES), lambda i, j: (i, 0))],
        out_shape=[jax.ShapeDtypeStruct((n, n_cols), BF), jax.ShapeDtypeStruct((n, LANES), F32)],
        scratch_shapes=[pltpu.VMEM((tm, k), BF)],
        compiler_params=_cparams(("parallel", "arbitrary"), 56), name="norm_matmul_gates",
    )(x, gain.reshape(1, k), wt, wt, gate_b)


def _stacked(w, layer):
    return (w[None], 0) if layer is None else (w, layer)


def norm_matmul(x, gain, w, n_cols, layer=None):
    n, k = x.shape
    w, li = _stacked(w, layer)
    tm = min(ROW_TILE, n)
    tn = COL_TILE
    return pl.pallas_call(
        _norm_mm_kernel, grid=(n // tm, n_cols // tn),
        in_specs=[pl.BlockSpec((tm, k), lambda i, j: (i, 0)), pl.BlockSpec((1, k), lambda i, j: (0, 0)),
                  pl.BlockSpec((None, k, tn), lambda i, j: (li, 0, j))],
        out_specs=pl.BlockSpec((tm, tn), lambda i, j: (i, j)),
        out_shape=jax.ShapeDtypeStruct((n, n_cols), BF), scratch_shapes=[pltpu.VMEM((tm, k), BF)],
        compiler_params=_cparams(("parallel", "arbitrary"), 56), name="norm_matmul",
    )(x, gain.reshape(1, k), w)


def _q_attention_kernel(x_ref, g_ref, w_ref, k_ref, v_ref, o_ref, q_ref, *, hd):
    xn = _rms(x_ref[...], g_ref[...]).astype(BF)
    q_ref[...] = jnp.dot(xn, w_ref[...].astype(BF), preferred_element_type=F32).astype(BF)
    scale = hd ** -0.5
    for h in range(XATTN_HEADS):
        hs = slice(h * hd, (h + 1) * hd)
        s = lax.dot_general(q_ref[:, hs], k_ref[:, hs], _NT, preferred_element_type=F32) * scale
        e = jnp.exp(s - jnp.max(s, axis=-1, keepdims=True))
        p = (e / jnp.sum(e, axis=-1, keepdims=True)).astype(BF)
        o_ref[:, hs] = jnp.dot(p, v_ref[:, hs], preferred_element_type=F32).astype(o_ref.dtype)


def q_attention(x, gain, wq, k, v, layer):
    n, kd = x.shape
    d = wq.shape[2]
    m = k.shape[0]
    tm = min(OUT_ROW_TILE, n)
    return pl.pallas_call(
        functools.partial(_q_attention_kernel, hd=d // XATTN_HEADS), grid=(n // tm,),
        in_specs=[pl.BlockSpec((tm, kd), lambda i: (i, 0)), pl.BlockSpec((1, kd), lambda i: (0, 0)),
                  pl.BlockSpec((None, kd, d), lambda i: (layer, 0, 0), pipeline_mode=pl.Buffered(1)),
                  pl.BlockSpec((m, d), lambda i: (0, 0)), pl.BlockSpec((m, d), lambda i: (0, 0))],
        out_specs=pl.BlockSpec((tm, d), lambda i: (i, 0)), out_shape=jax.ShapeDtypeStruct((n, d), BF),
        scratch_shapes=[pltpu.VMEM((tm, d), BF)],
        compiler_params=_cparams(("parallel",), 52), name="q_attention",
    )(x, gain.reshape(1, kd), wq, k, v)


def _mm_res_kernel(*refs, nparts):
    xs = refs[:nparts]
    w_ref, res_ref, o_ref = refs[nparts:]
    acc = res_ref[...]
    k0 = 0
    for x_ref in xs:
        kp = x_ref.shape[1]
        acc = acc + jnp.dot(x_ref[...], w_ref[k0:k0 + kp, :].astype(BF), preferred_element_type=F32)
        k0 += kp
    o_ref[...] = acc


def matmul_residual(xs, w, res, layer=None):
    n, d = res.shape
    w, li = _stacked(w, layer)
    k = w.shape[1]
    tm = min(OUT_ROW_TILE, n)
    in_specs = [pl.BlockSpec((tm, x.shape[1]), lambda i: (i, 0)) for x in xs]
    in_specs += [pl.BlockSpec((None, k, d), lambda i: (li, 0, 0), pipeline_mode=pl.Buffered(1)),
                 pl.BlockSpec((tm, d), lambda i: (i, 0))]
    return pl.pallas_call(
        functools.partial(_mm_res_kernel, nparts=len(xs)), grid=(n // tm,), in_specs=in_specs,
        out_specs=pl.BlockSpec((tm, d), lambda i: (i, 0)), out_shape=jax.ShapeDtypeStruct((n, d), F32),
        compiler_params=_cparams(("parallel",), 52), name="matmul_residual",
    )(*xs, w, res)


def _pool_kernel(cur_ref, prev_ref, w_ref, sc_ref, o_ref, *, tp, gdim):
    i = pl.program_id(0)
    dist = lax.broadcasted_iota(jnp.int32, (tp, tp), 0) - lax.broadcasted_iota(jnp.int32, (tp, tp), 1)
    distp = (lax.broadcasted_iota(jnp.int32, (tp, POOL_HALO), 0) + POOL_HALO
             - lax.broadcasted_iota(jnp.int32, (tp, POOL_HALO), 1))
    pos = i * tp + lax.broadcasted_iota(jnp.int32, (tp, 1), 0)
    for j, win in enumerate(POOL_WINDOWS):
        gs = slice(j * gdim, (j + 1) * gdim)
        cur = cur_ref[:, gs]
        band = jnp.where(dist >= 0, jnp.where(dist < win, 1.0, 0.0), 0.0).astype(BF)
        bandp = jnp.where(distp < jnp.where(i > 0, win, 0), 1.0, 0.0).astype(BF)
        s = (jnp.dot(band, cur, preferred_element_type=F32)
             + jnp.dot(bandp, prev_ref[:, gs], preferred_element_type=F32))
        cnt = jnp.minimum(pos + 1, win).astype(F32)
        d = s / cnt - cur.astype(F32)
        y = jnp.dot(d.astype(BF), w_ref[j].astype(BF), preferred_element_type=F32) * sc_ref[:, gs]
        o_ref[:, gs] = y.astype(o_ref.dtype)


def pool_mixer(z, pool_w, pool_scale):
    n = z.shape[0]
    gdim = pool_w.shape[-1]
    width = N_POOL_GROUPS * gdim
    tp = min(POOL_ROWS, n)
    halo_blocks = tp // POOL_HALO
    return pl.pallas_call(
        functools.partial(_pool_kernel, tp=tp, gdim=gdim), grid=(n // tp,),
        in_specs=[
            pl.BlockSpec((tp, width), lambda i: (i, 0)),
            pl.BlockSpec((POOL_HALO, width), lambda i: (jnp.maximum(i * halo_blocks - 1, 0), 0)),
            pl.BlockSpec((N_POOL_GROUPS, gdim, gdim), lambda i: (0, 0, 0)),
            pl.BlockSpec((1, width), lambda i: (0, 0)),
        ],
        out_specs=pl.BlockSpec((tp, width), lambda i: (i, 0)),
        out_shape=jax.ShapeDtypeStruct((n, width), BF),
        compiler_params=_cparams(("parallel",), 32), name="pool_mixer",
    )(z, z, pool_w, pool_scale.reshape(1, width))


def _mlstm_kernel(q_ref, k_ref, v_ref, o_ref, g_ref, gain_ref, y_ref, ct_ref, m_ref, *, chunk, dh):
    c = pl.program_id(0)

    @pl.when(c == 0)
    def _():
        ct_ref[...] = jnp.zeros_like(ct_ref)
        m_ref[...] = jnp.zeros_like(m_ref)

    g = g_ref[...]
    lf = _log_sigmoid(g)
    row = lax.broadcasted_iota(jnp.int32, (chunk, chunk), 0)
    col = lax.broadcasted_iota(jnp.int32, (chunk, chunk), 1)
    causal = col <= row
    ltri = jnp.where(causal, 1.0, 0.0).astype(BF)
    hi = lf.astype(BF)
    r1 = lf - hi.astype(F32)
    mid = r1.astype(BF)
    lo = (r1 - mid.astype(F32)).astype(BF)
    bcum = (jnp.dot(ltri, hi, preferred_element_type=F32) + jnp.dot(ltri, mid, preferred_element_type=F32)
            + jnp.dot(ltri, lo, preferred_element_type=F32))
    g_t = g.T
    b_t = bcum.T
    ones_col = jnp.where(lax.broadcasted_iota(jnp.int32, (chunk, LANES), 1) == 0, 1.0, 0.0).astype(BF)

    for h in range(MLSTM_HEADS):
        hs = slice(h * dh, (h + 1) * dh)
        fl = FORGET_LANE0 + h
        bc = bcum[:, fl:fl + 1]
        br = b_t[fl:fl + 1, :]
        ir = g_t[h:h + 1, :]
        b_last = bcum[chunk - 1:chunk, fl:fl + 1]
        m_prev = m_ref[h][:, 0:1]

        dmat = jnp.where(causal, bc + (ir - br), -jnp.inf)
        inter = bc + m_prev
        m_t = jnp.maximum(jnp.max(dmat, axis=1, keepdims=True), inter)
        w_inter = jnp.exp(inter - m_t)
        p = jnp.exp(dmat - m_t)

        qh = q_ref[:, hs] * (dh ** -0.5)
        kh = k_ref[:, hs]
        v_aug = jnp.concatenate([v_ref[:, hs], ones_col], axis=1)
        s = lax.dot_general(qh, kh, (((1,), (1,)), ((), ())), preferred_element_type=F32)
        sc = (s * p).astype(BF)
        ct = ct_ref[h]
        num_aug = (w_inter * jnp.dot(qh, ct.astype(BF), preferred_element_type=F32)
                   + jnp.dot(sc, v_aug, preferred_element_type=F32))
        num = num_aug[:, :dh]
        den = num_aug[:, dh:dh + 1]
        hout = num / jnp.maximum(jnp.abs(den), jnp.exp(-m_t))

        yn = _rms(hout, gain_ref[:, hs])
        y_ref[:, hs] = (_sigmoid(o_ref[:, hs].astype(F32)) * yn).astype(y_ref.dtype)

        d_end = b_last - br + ir
        m_new = jnp.maximum(b_last + m_prev, jnp.max(d_end, axis=1, keepdims=True))
        a_prev = jnp.exp(b_last + m_prev - m_new)
        a_s = jnp.exp(d_end - m_new)
        k_t = (kh.astype(F32).T * a_s).astype(BF)
        ct_ref[h] = a_prev * ct + jnp.dot(k_t, v_aug, preferred_element_type=F32)
        m_ref[h] = jnp.broadcast_to(m_new, (1, LANES))


def mlstm_mixer(z, gates, head_gain, col0):
    n = z.shape[0]
    width = head_gain.shape[0]
    dh = width // MLSTM_HEADS
    chunk = min(MLSTM_CHUNK, n)
    base = col0 // width
    qkvo = [pl.BlockSpec((chunk, width), lambda c, p=p: (c, base + p)) for p in range(4)]
    return pl.pallas_call(
        functools.partial(_mlstm_kernel, chunk=chunk, dh=dh), grid=(n // chunk,),
        in_specs=qkvo + [pl.BlockSpec((chunk, LANES), lambda c: (c, 0)), pl.BlockSpec((1, width), lambda c: (0, 0))],
        out_specs=pl.BlockSpec((chunk, width), lambda c: (c, 0)),
        out_shape=jax.ShapeDtypeStruct((n, width), BF),
        scratch_shapes=[pltpu.VMEM((MLSTM_HEADS, dh, dh + LANES), F32), pltpu.VMEM((MLSTM_HEADS, 1, LANES), F32)],
        compiler_params=_cparams(("arbitrary",), 32), name="mlstm_mixer",
    )(z, z, z, z, gates, head_gain.reshape(1, width))


def _gated_conv(b, c, u, cp, up, w, first_block):
    zc = c.astype(F32) * u.astype(F32)
    zp = jnp.where(first_block, 0.0, cp.astype(F32) * up.astype(F32))
    row = lax.broadcasted_iota(jnp.int32, zc.shape, 0)
    acc = w[CONV_WIDTH - 1:CONV_WIDTH, :] * zc
    for back in range(1, CONV_WIDTH):
        shifted = pltpu.roll(zc, back, 0)
        for r in range(back):
            shifted = jnp.where(row == r, zp[CONV_HALO - back + r:CONV_HALO - back + r + 1, :], shifted)
        acc = acc + w[CONV_WIDTH - 1 - back:CONV_WIDTH - back, :] * shifted
    return b.astype(F32) * acc


def _conv_mm_res_kernel(b_ref, c_ref, u_ref, cp_ref, up_ref, cw_ref, w_ref, res_ref, o_ref, xs_ref):
    first_block = pl.program_id(0) == 0
    for c0 in range(0, xs_ref.shape[1], CONV_COLS):
        cs = slice(c0, c0 + CONV_COLS)
        xs_ref[:, cs] = _gated_conv(b_ref[:, cs], c_ref[:, cs], u_ref[:, cs], cp_ref[:, cs], up_ref[:, cs],
                                    cw_ref[:, cs], first_block).astype(BF)
    o_ref[...] = res_ref[...] + jnp.dot(xs_ref[...], w_ref[...].astype(BF), preferred_element_type=F32)


def conv_matmul_residual(z, conv_w, w, res, layer):
    n, d = res.shape
    tm = min(CONV_ROWS, n)
    halo_blocks = tm // CONV_HALO
    cur = lambda part: pl.BlockSpec((tm, d), lambda i: (i, part))
    prev = lambda part: pl.BlockSpec((CONV_HALO, d), lambda i: (jnp.maximum(i * halo_blocks - 1, 0), part))
    return pl.pallas_call(
        _conv_mm_res_kernel, grid=(n // tm,),
        in_specs=[cur(0), cur(1), cur(2), prev(1), prev(2), pl.BlockSpec((CONV_WIDTH, d), lambda i: (0, 0)),
                  pl.BlockSpec((None, d, d), lambda i: (layer, 0, 0), pipeline_mode=pl.Buffered(1)),
                  pl.BlockSpec((tm, d), lambda i: (i, 0))],
        out_specs=pl.BlockSpec((tm, d), lambda i: (i, 0)), out_shape=jax.ShapeDtypeStruct((n, d), F32),
        scratch_shapes=[pltpu.VMEM((tm, d), BF)],
        compiler_params=_cparams(("parallel",), 52), name="conv_matmul_residual",
    )(z, z, z, z, z, conv_w, w, res)


def _router_kernel(x_ref, g_ref, wr_ref, br_ref, xn_ref, ids_ref, wts_ref, cnt_ref, carry_ref, w2_ref, *, tr):
    @pl.when(pl.program_id(0) == 0)
    def _():
        carry_ref[...] = jnp.zeros_like(carry_ref)

    @pl.when(pl.program_id(0) == 0)
    def _():
        w = wr_ref[...]
        w_hi = w.astype(BF)
        w2_ref[:, :LANES] = w_hi
        w2_ref[:, LANES:] = (w - w_hi.astype(F32)).astype(BF)

    xn = _rms(x_ref[...], g_ref[...])
    xn_ref[...] = _pack_halves(xn)
    x_hi = xn.astype(BF)
    x_lo = (xn - x_hi.astype(F32)).astype(BF)
    p_hi = jnp.dot(x_hi, w2_ref[...], preferred_element_type=F32)
    p_lo = jnp.dot(x_lo, w2_ref[...], preferred_element_type=F32)
    logits = (p_hi[:, :LANES] + p_hi[:, LANES:]) + (p_lo[:, :LANES] + p_lo[:, LANES:]) + br_ref[...]
    lane = lax.broadcasted_iota(jnp.int32, (tr, LANES), 1).astype(F32)
    neg = -jnp.inf

    def first_argmax(vals):
        top = jnp.max(vals, axis=-1, keepdims=True)
        return top, jnp.min(jnp.where(vals == top, lane, float(LANES)), axis=-1, keepdims=True)

    gl = jnp.where(lane < N_GROUPS, logits, neg)
    gmax, grp = first_argmax(gl)
    g_prob = 1.0 / jnp.sum(jnp.exp(gl - gmax), axis=-1, keepdims=True)
    lo = N_GROUPS + EXPERTS_PER_GROUP * grp
    el = jnp.where(lane >= lo, jnp.where(lane < lo + EXPERTS_PER_GROUP, logits, neg), neg)
    v1, l1 = first_argmax(el)
    v2, l2 = first_argmax(jnp.where(lane == l1, neg, el))
    e2 = jnp.exp(v2 - v1)
    w1 = g_prob / (1.0 + e2)
    w2 = g_prob * e2 / (1.0 + e2)
    hot1 = lane == l1
    hot2 = lane == l2
    hot = jnp.where(hot1, 1.0, jnp.where(hot2, 1.0, 0.0))
    earlier = (lax.broadcasted_iota(jnp.int32, (tr, tr), 1) < lax.broadcasted_iota(jnp.int32, (tr, tr), 0))
    before = jnp.dot(jnp.where(earlier, 1.0, 0.0).astype(BF), hot.astype(BF), preferred_element_type=F32)
    before = before + carry_ref[0:1, :]
    r1 = jnp.sum(jnp.where(hot1, before, 0.0), axis=-1, keepdims=True)
    r2 = jnp.sum(jnp.where(hot2, before, 0.0), axis=-1, keepdims=True)
    carry_ref[0:1, :] = carry_ref[0:1, :] + jnp.sum(hot, axis=0, keepdims=True)
    ids = jnp.where(lane == 0, l1 - N_GROUPS, jnp.where(lane == 1, l2 - N_GROUPS,
                    jnp.where(lane == 2, r1, jnp.where(lane == 3, r2, 0.0))))
    ids_ref[...] = ids.astype(jnp.int32)
    wts_ref[...] = jnp.where(lane == 0, w1, jnp.where(lane == 1, w2, 0.0))
    cnt_ref[...] = carry_ref[...]


def route(h, gain, w_router, b_router):
    n, d = h.shape
    tr = min(ROUTER_ROWS, n)
    return pl.pallas_call(
        functools.partial(_router_kernel, tr=tr), grid=(n // tr,),
        in_specs=[pl.BlockSpec((tr, d), lambda i: (i, 0)), pl.BlockSpec((1, d), lambda i: (0, 0)),
                  pl.BlockSpec((d, LANES), lambda i: (0, 0)), pl.BlockSpec((1, LANES), lambda i: (0, 0))],
        out_specs=[pl.BlockSpec((tr, d // 2), lambda i: (i, 0)),
                   pl.BlockSpec((tr, LANES), lambda i: (i, 0)), pl.BlockSpec((tr, LANES), lambda i: (i, 0)),
                   pl.BlockSpec((SUBLANES, LANES), lambda i: (0, 0))],
        out_shape=[jax.ShapeDtypeStruct((n, d // 2), jnp.uint32), jax.ShapeDtypeStruct((n, LANES), jnp.int32),
                   jax.ShapeDtypeStruct((n, LANES), F32), jax.ShapeDtypeStruct((SUBLANES, LANES), F32)],
        scratch_shapes=[pltpu.VMEM((SUBLANES, LANES), F32), pltpu.VMEM((d, 2 * LANES), BF)],
        compiler_params=_cparams(("arbitrary",), 40), name="moe_router",
    )(h, gain.reshape(1, d), w_router, b_router)


def _dest_kernel(ids_ref, start_ref, o_ref, *, tr):
    ids = ids_ref[...].astype(F32)
    lane = lax.broadcasted_iota(jnp.int32, (tr, LANES), 1).astype(F32)
    start = start_ref[...]
    rows = []
    for k in range(TOP_K):
        first = jnp.sum(jnp.where(lane == ids[:, k:k + 1], start, 0.0), axis=-1, keepdims=True)
        rows.append(first + ids[:, TOP_K + k:TOP_K + k + 1])
    packed = jnp.where(lane == 0, rows[0], jnp.where(lane == 1, rows[1], 0.0))
    o_ref[...] = packed.T[0:SUBLANES, :].astype(jnp.int32)


def assignment_rows(ids, start_rows):
    n = ids.shape[0]
    tr = min(ROUTER_ROWS, n)
    out = pl.pallas_call(
        functools.partial(_dest_kernel, tr=tr), grid=(n // tr,),
        in_specs=[pl.BlockSpec((tr, LANES), lambda i: (i, 0)), pl.BlockSpec((1, LANES), lambda i: (0, 0))],
        out_specs=pl.BlockSpec((SUBLANES, tr), lambda i: (0, i)),
        out_shape=jax.ShapeDtypeStruct((SUBLANES, n), jnp.int32),
        compiler_params=_cparams(("parallel",), 32), name="moe_assignment_rows",
    )(ids, start_rows)
    return out[:TOP_K]


def _row_copy(src_hbm, row, dst, dst_row, sem):
    return pltpu.make_async_copy(src_hbm.at[pl.ds(row, 1), :], dst.at[pl.ds(dst_row, 1), :], sem)


def _rows_wait(src_hbm, dst, sem):
    pltpu.make_async_copy(src_hbm.at[pl.ds(0, dst.shape[0]), :], dst, sem).wait()


def _expert_weight_copies(w_hbm, layer, expert, stage, wset, sem):
    rows = stage.shape[1] // WEIGHT_DMA_PARTS
    return [pltpu.make_async_copy(w_hbm.at[layer, expert, pl.ds(p * rows, rows), :],
                                  stage.at[wset, pl.ds(p * rows, rows), :], sem.at[wset])
            for p in range(WEIGHT_DMA_PARTS)]


def _stream_expert_weights(b, nact, blk_ref, kin_ref, set_ref, nxt_ref, copies):
    cur = set_ref[b]

    @pl.when(b == 0)
    def _():
        for cp in copies(blk_ref[0], cur):
            cp.start(priority=DMA_QUEUE_BULK)

    @pl.when(jnp.logical_and(b < nact, kin_ref[b] == 0))
    def _():
        for cp in copies(blk_ref[b], cur):
            cp.wait()

        @pl.when(nxt_ref[b] != blk_ref[b])
        def _():
            for cp in copies(nxt_ref[b], 1 - cur):
                cp.start(priority=DMA_QUEUE_BULK)

    return cur


def _moe_expert_kernel(blk_ref, kin_ref, last_ref, set_ref, nxt_ref, nact_ref, rtok_ref,
                       xn_hbm, wg_hbm, wu_hbm, wd_hbm, ys_ref,
                       buf, stage_g, stage_u, stage_d, xb, hid, sem, wsem, dsem, *, tb, layer):
    b = pl.program_id(0)
    nact = nact_ref[0]
    nslots = GATHER_AHEAD + 1
    slot = b % nslots
    f = hid.shape[1]
    half = ys_ref.shape[1]
    up_chunks = f // MOE_COL_CHUNK
    rows_per_chunk = tb // up_chunks
    active = b < nact
    first_of_expert = jnp.logical_and(active, kin_ref[b] == 0)
    has_next = nxt_ref[b] != blk_ref[b]

    def up_copies(e, wset):
        return (_expert_weight_copies(wg_hbm, layer, e, stage_g, wset, wsem.at[0])
                + _expert_weight_copies(wu_hbm, layer, e, stage_u, wset, wsem.at[1]))

    def down_copies(e):
        rows = stage_d.shape[0] // WEIGHT_DMA_PARTS
        return [pltpu.make_async_copy(wd_hbm.at[layer, e, pl.ds(p * rows, rows), :],
                                      stage_d.at[pl.ds(p * rows, rows), :], dsem.at[0])
                for p in range(WEIGHT_DMA_PARTS)]

    @pl.when(b == 0)
    def _():
        for ahead in range(GATHER_AHEAD):
            def body(r, carry, ahead=ahead):
                _row_copy(xn_hbm, rtok_ref[ahead * tb + r], buf.at[ahead], r,
                          sem.at[ahead]).start(priority=DMA_QUEUE_ROWS)
                return carry

            lax.fori_loop(0, tb, body, 0, unroll=8)
        for cp in down_copies(blk_ref[0]):
            cp.start(priority=DMA_QUEUE_BULK)

    cur = _stream_expert_weights(b, nact, blk_ref, kin_ref, set_ref, nxt_ref, up_copies)

    @pl.when(active)
    def _():
        _rows_wait(xn_hbm, buf.at[slot], sem.at[slot])
        x_lo, x_hi = _unpack_halves(buf[slot])
        xb[:, :x_lo.shape[1]] = x_lo.astype(BF)
        xb[:, x_lo.shape[1]:] = x_hi.astype(BF)
        ahead_slot = (b + GATHER_AHEAD) % nslots
        for c in range(up_chunks):
            for r in range(c * rows_per_chunk, (c + 1) * rows_per_chunk):
                _row_copy(xn_hbm, rtok_ref[(b + GATHER_AHEAD) * tb + r], buf.at[ahead_slot], r,
                          sem.at[ahead_slot]).start(priority=DMA_QUEUE_ROWS)
            cs = slice(c * MOE_COL_CHUNK, (c + 1) * MOE_COL_CHUNK)
            gate = jnp.dot(xb[...], stage_g[cur, :, cs].astype(BF), preferred_element_type=F32)
            up = jnp.dot(xb[...], stage_u[cur, :, cs].astype(BF), preferred_element_type=F32)
            hid[:, cs] = (gate * _sigmoid(gate) * up).astype(hid.dtype)

    @pl.when(first_of_expert)
    def _():
        for cp in down_copies(blk_ref[b]):
            cp.wait()

    @pl.when(active)
    def _():
        for c in range(half // MOE_COL_CHUNK):
            cs = slice(c * MOE_COL_CHUNK, (c + 1) * MOE_COL_CHUNK)
            hs = slice(half + c * MOE_COL_CHUNK, half + (c + 1) * MOE_COL_CHUNK)
            lo = jnp.dot(hid[...], stage_d[:, cs].astype(BF), preferred_element_type=F32)
            hi = jnp.dot(hid[...], stage_d[:, hs].astype(BF), preferred_element_type=F32)
            ys_ref[:, cs] = _pack_halves(jnp.concatenate([lo, hi], axis=1))

    @pl.when(jnp.logical_and(jnp.logical_and(active, last_ref[b] == 1), has_next))
    def _():
        for cp in down_copies(nxt_ref[b]):
            cp.start(priority=DMA_QUEUE_BULK)

    @pl.when(b >= nact)
    def _():
        ys_ref[...] = jnp.zeros_like(ys_ref)

    @pl.when(jnp.logical_and(b >= nact, b < nact + GATHER_AHEAD))
    def _():
        _rows_wait(xn_hbm, buf.at[slot], sem.at[slot])


def _combine_kernel(dest_ref, ys_hbm, h_ref, w_ref, gain_ref, o_ref, buf, sem, *, tc, n, final_norm):
    i = pl.program_id(0)

    def row_copy(blk, slot, r, k):
        return _row_copy(ys_hbm, dest_ref[k * n + blk * tc + r], buf.at[slot], k * tc + r, sem.at[slot])

    @pl.when(i == 0)
    def _():
        def body(r, carry):
            for k in range(TOP_K):
                row_copy(0, 0, r, k).start(priority=k)
            return carry

        lax.fori_loop(0, tc, body, 0, unroll=8)

    @pl.when(i + 1 < pl.num_programs(0))
    def _():
        for r in range(tc):
            for k in range(TOP_K):
                row_copy(i + 1, (i + 1) % 2, r, k).start(priority=k)

    slot = i % 2
    _rows_wait(ys_hbm, buf.at[slot], sem.at[slot])
    a_lo, a_hi = _unpack_halves(buf[slot, 0:tc, :])
    b_lo, b_hi = _unpack_halves(buf[slot, tc:TOP_K * tc, :])
    w0 = w_ref[:, 0:1]
    w1 = w_ref[:, 1:2]
    out = h_ref[...] + jnp.concatenate([w0 * a_lo + w1 * b_lo, w0 * a_hi + w1 * b_hi], axis=1)
    if final_norm:
        out = _rms(out, gain_ref[...])
    o_ref[...] = out


def hier_moe(h, ffn_gain, wg_r, bg_r, we_r, be_r, w_gate, w_up, w_down, layer, final_gain=None):
    n, d = h.shape
    f = w_gate.shape[-1]
    a = n * TOP_K
    tb = _moe_block_rows(a)
    nb = a // tb + N_EXPERTS + GATHER_AHEAD

    pad = LANES - N_GROUPS - N_EXPERTS
    w_router = jnp.concatenate([wg_r, we_r, jnp.zeros((d, pad), F32)], axis=1)
    b_router = jnp.concatenate([bg_r, be_r, jnp.zeros((pad,), F32)]).reshape(1, LANES)
    xn, ids, wts, cnt = route(h, ffn_gain, w_router, b_router)

    counts = cnt[0, N_GROUPS:N_GROUPS + N_EXPERTS].astype(jnp.int32)
    nblk = (counts + tb - 1) // tb
    bend = jnp.cumsum(nblk)
    nact = bend[-1]
    start_rows = jnp.pad(((bend - nblk) * tb).astype(F32), (0, LANES - N_EXPERTS)).reshape(1, LANES)
    dest = assignment_rows(ids, start_rows).reshape(a)
    blk = jnp.minimum(jnp.arange(nb, dtype=jnp.int32), jnp.maximum(nact - 1, 0))
    owner = lambda bi: jnp.minimum(jnp.sum((bend[None, :] <= bi[:, None]).astype(jnp.int32), axis=1), N_EXPERTS - 1)
    blk_e = owner(blk)
    k_in_e = blk - (bend - nblk)[blk_e]
    following = lambda e: jnp.where(bend[e] < nact, owner(bend[e]), e)
    nxt_e = following(blk_e)
    wset = (jnp.cumsum((k_in_e == 0).astype(jnp.int32)) - 1) % 2
    tok = jnp.tile(jnp.arange(n, dtype=jnp.int32), TOP_K)
    row_tok = (jnp.arange(nb * tb, dtype=jnp.int32) % n).at[dest].set(tok)
    nact1 = nact.reshape(1).astype(jnp.int32)

    any_space = pl.BlockSpec(memory_space=pl.ANY)
    is_last = (k_in_e == nblk[blk_e] - 1).astype(jnp.int32)
    ys = pl.pallas_call(
        functools.partial(_moe_expert_kernel, tb=tb, layer=layer),
        grid_spec=pltpu.PrefetchScalarGridSpec(
            num_scalar_prefetch=7, grid=(nb,),
            in_specs=[any_space, any_space, any_space, any_space],
            out_specs=pl.BlockSpec((tb, d // 2), lambda b, *_: (b, 0)),
            scratch_shapes=[pltpu.VMEM((GATHER_AHEAD + 1, tb, d // 2), jnp.uint32),
                            pltpu.VMEM((2, d, f), F32), pltpu.VMEM((2, d, f), F32), pltpu.VMEM((f, d), F32),
                            pltpu.VMEM((tb, d), BF), pltpu.VMEM((tb, f), BF),
                            pltpu.SemaphoreType.DMA((GATHER_AHEAD + 1,)), pltpu.SemaphoreType.DMA((2, 2)),
                            pltpu.SemaphoreType.DMA((1,))]),
        out_shape=jax.ShapeDtypeStruct((nb * tb, d // 2), jnp.uint32),
        compiler_params=_cparams(("arbitrary",), 56), name="moe_experts",
    )(blk_e, k_in_e, is_last, wset, nxt_e, nact1, row_tok, xn, w_gate, w_up, w_down)

    tc = min(COMBINE_ROWS, n)
    gain = (final_gain if final_gain is not None else ffn_gain).reshape(1, d)
    return pl.pallas_call(
        functools.partial(_combine_kernel, tc=tc, n=n, final_norm=final_gain is not None),
        grid_spec=pltpu.PrefetchScalarGridSpec(
            num_scalar_prefetch=1, grid=(n // tc,),
            in_specs=[pl.BlockSpec(memory_space=pl.ANY), pl.BlockSpec((tc, d), lambda i, *_: (i, 0)),
                      pl.BlockSpec((tc, LANES), lambda i, *_: (i, 0)), pl.BlockSpec((1, d), lambda i, *_: (0, 0))],
            out_specs=pl.BlockSpec((tc, d), lambda i, *_: (i, 0)),
            scratch_shapes=[pltpu.VMEM((2, TOP_K * tc, d // 2), jnp.uint32), pltpu.SemaphoreType.DMA((2,))]),
        out_shape=jax.ShapeDtypeStruct((n, d), F32),
        compiler_params=_cparams(("arbitrary",), 40), name="moe_combine",
    )(dest, ys, h, wts, gain)


def even_mixer(h, gain, w_in, b_gates, pool_w, pool_scale, head_gain, w_out, j):
    d = h.shape[1]
    pool_width = N_POOL_GROUPS * pool_w.shape[-1]
    mlstm_width = head_gain.shape[0]
    main_cols = pool_width + 4 * mlstm_width
    n_gates = 2 * MLSTM_HEADS
    gate_b = jnp.pad(b_gates, (0, LANES - n_gates)).reshape(1, LANES)
    z, gates = norm_matmul_gates_t(h, gain, jnp.swapaxes(w_in, 1, 2), main_cols, n_gates, gate_b, j)
    y_p = pool_mixer(z, pool_w, pool_scale)
    y_m = mlstm_mixer(z, gates, head_gain, pool_width)
    assert pool_width == mlstm_width and pool_width + mlstm_width == d
    return matmul_residual([y_p, y_m], w_out, h, layer=j)


def odd_mixer(h, gain, w_in, conv_w, w_out, j):
    z = norm_matmul(h, gain, w_in, w_in.shape[2], layer=j)
    return conv_matmul_residual(z, conv_w, w_out, h, j)


def cross_attn(h, mem, gain, mem_gain, wq, wk, wv, wo, layer):
    d = h.shape[1]
    k = norm_matmul(mem, mem_gain, wk, d, layer=layer)
    v = norm_matmul(mem, mem_gain, wv, d, layer=layer)
    return matmul_residual([q_attention(h, gain, wq, k, v, layer)], wo, h, layer=layer)


def kernel(x, mem, mix_norm, xattn_norm, mem_norm, ffn_norm, final_norm, ev_w_in, ev_b_gates, ev_pool_w, ev_pool_scale, ev_head_norm, ev_w_out, od_w_in, od_conv_w, od_w_out, xa_wq, xa_wk, xa_wv, xa_wo, rt_group_w, rt_group_b, rt_expert_w, rt_expert_b, ex_w_gate, ex_w_up, ex_w_down):
    depth = mix_norm.shape[0]
    h = x[0]
    m = mem[0]
    for layer in range(depth):
        j = layer // 2
        if layer % 2 == 0:
            h = even_mixer(h, mix_norm[layer], ev_w_in, ev_b_gates[j], ev_pool_w[j], ev_pool_scale[j],
                           ev_head_norm[j], ev_w_out, j)
        else:
            h = odd_mixer(h, mix_norm[layer], od_w_in, od_conv_w[j], od_w_out, j)
        h = cross_attn(h, m, xattn_norm[layer], mem_norm[layer], xa_wq, xa_wk, xa_wv, xa_wo, layer)
        h = hier_moe(h, ffn_norm[layer], rt_group_w[layer], rt_group_b[layer], rt_expert_w[layer],
                     rt_expert_b[layer], ex_w_gate, ex_w_up, ex_w_down, layer,
                     final_gain=final_norm if layer == depth - 1 else None)
    return h[None]
```

```python
import functools

import jax
import jax.numpy as jnp
from jax import lax
from jax.experimental import pallas as pl
from jax.experimental.pallas import tpu as pltpu

F32 = jnp.float32
BF = jnp.bfloat16
EPS = 1e-6

POOL_WINDOWS = (2, 4, 8, 16)
N_POOL_GROUPS = 4
MLSTM_HEADS = 4
FORGET_LANE0 = MLSTM_HEADS
XATTN_HEADS = 4
N_GROUPS = 4
EXPERTS_PER_GROUP = 8
N_EXPERTS = N_GROUPS * EXPERTS_PER_GROUP
TOP_K = 2
CONV_WIDTH = 3

LANES = 128
SUBLANES = 8
DMA_QUEUE_ROWS = 0
DMA_QUEUE_BULK = 1

ROW_TILE = 1024
COL_TILE = 1024
OUT_ROW_TILE = 512
MOE_COL_CHUNK = 256
WEIGHT_DMA_PARTS = 4
MLSTM_CHUNK = 256
POOL_ROWS = 256
POOL_HALO = 128
CONV_ROWS = 512
CONV_COLS = 512
CONV_HALO = 16
ROUTER_ROWS = 512
GATHER_AHEAD = 2
COMBINE_ROWS = 256


def _moe_block_rows(assignments):
    mean_rows = assignments // N_EXPERTS
    return -(-(mean_rows * 9 // 16) // 16) * 16


def _cparams(semantics, vmem_mib):
    return pltpu.CompilerParams(dimension_semantics=semantics, vmem_limit_bytes=vmem_mib * 1024 * 1024)


def _sigmoid(x):
    return 1.0 / (1.0 + jnp.exp(-x))


def _log_sigmoid(x):
    return jnp.minimum(x, 0.0) - jnp.log(1.0 + jnp.exp(-jnp.abs(x)))


def _rms(x, g):
    ms = jnp.mean(x * x, axis=-1, keepdims=True)
    return x * lax.rsqrt(ms + EPS) * g


def _pack_halves(x):
    half = x.shape[1] // 2
    lo = pltpu.bitcast(x[:, :half].astype(BF).astype(F32), jnp.uint32)
    hi = pltpu.bitcast(x[:, half:].astype(BF).astype(F32), jnp.uint32)
    return hi | lax.shift_right_logical(lo, jnp.uint32(16))


def _unpack_halves(w):
    lo = pltpu.bitcast(lax.shift_left(w, jnp.uint32(16)), F32)
    hi = pltpu.bitcast(w & jnp.uint32(0xFFFF0000), F32)
    return lo, hi


def _norm_mm_kernel(x_ref, g_ref, w_ref, o_ref, xn_ref):
    @pl.when(pl.program_id(1) == 0)
    def _():
        xn_ref[...] = _rms(x_ref[...], g_ref[...]).astype(BF)

    o_ref[...] = jnp.dot(xn_ref[...], w_ref[...].astype(BF), preferred_element_type=F32).astype(o_ref.dtype)


_NT = (((1,), (1,)), ((), ()))


def _norm_mm_gates_t_kernel(x_ref, g_ref, wt_ref, wgt_ref, bg_ref, o_ref, gates_ref, xn_ref):
    @pl.when(pl.program_id(1) == 0)
    def _():
        xn = _rms(x_ref[...], g_ref[...]).astype(BF)
        xn_ref[...] = xn
        wg = wgt_ref[...]
        wg = jnp.concatenate([wg, jnp.zeros((LANES - wg.shape[0], wg.shape[1]), F32)], axis=0).astype(BF)
        gates_ref[...] = lax.dot_general(xn, wg, _NT, preferred_element_type=F32) + bg_ref[...]

    o_ref[...] = lax.dot_general(xn_ref[...], wt_ref[...].astype(BF), _NT,
                                 preferred_element_type=F32).astype(o_ref.dtype)


def norm_matmul_gates_t(x, gain, wt, n_cols, n_gates, gate_b, layer):
    n, k = x.shape
    tm = min(ROW_TILE, n)
    tn = COL_TILE
    return pl.pallas_call(
        _norm_mm_gates_t_kernel, grid=(n // tm, n_cols // tn),
        in_specs=[pl.BlockSpec((tm, k), lambda i, j: (i, 0)), pl.BlockSpec((1, k), lambda i, j: (0, 0)),
                  pl.BlockSpec((None, tn, k), lambda i, j: (layer, j, 0)),
                  pl.BlockSpec((None, n_gates, k), lambda i, j: (layer, n_cols // n_gates, 0)),
                  pl.BlockSpec((1, LANES), lambda i, j: (0, 0))],
        out_specs=[pl.BlockSpec((tm, tn), lambda i, j: (i, j)), pl.BlockSpec((tm, LANES), lambda i, j: (i, 0))],
        out_shape=[jax.ShapeDtypeStruct((n, n_cols), BF), jax.ShapeDtypeStruct((n, LANES), F32)],
        scratch_shapes=[pltpu.VMEM((tm, k), BF)],
        compiler_params=_cparams(("parallel", "arbitrary"), 56), name="norm_matmul_gates",
    )(x, gain.reshape(1, k), wt, wt, gate_b)


def _stacked(w, layer):
    return (w[None], 0) if layer is None else (w, layer)


def norm_matmul(x, gain, w, n_cols, layer=None):
    n, k = x.shape
    w, li = _stacked(w, layer)
    tm = min(ROW_TILE, n)
    tn = COL_TILE
    return pl.pallas_call(
        _norm_mm_kernel, grid=(n // tm, n_cols // tn),
        in_specs=[pl.BlockSpec((tm, k), lambda i, j: (i, 0)), pl.BlockSpec((1, k), lambda i, j: (0, 0)),
                  pl.BlockSpec((None, k, tn), lambda i, j: (li, 0, j))],
        out_specs=pl.BlockSpec((tm, tn), lambda i, j: (i, j)),
        out_shape=jax.ShapeDtypeStruct((n, n_cols), BF), scratch_shapes=[pltpu.VMEM((tm, k), BF)],
        compiler_params=_cparams(("parallel", "arbitrary"), 56), name="norm_matmul",
    )(x, gain.reshape(1, k), w)


def _q_attention_kernel(x_ref, g_ref, w_ref, k_ref, v_ref, o_ref, q_ref, *, hd):
    xn = _rms(x_ref[...], g_ref[...]).astype(BF)
    q_ref[...] = jnp.dot(xn, w_ref[...].astype(BF), preferred_element_type=F32).astype(BF)
    scale = hd ** -0.5
    for h in range(XATTN_HEADS):
        hs = slice(h * hd, (h + 1) * hd)
        s = lax.dot_general(q_ref[:, hs], k_ref[:, hs], _NT, preferred_element_type=F32) * scale
        e = jnp.exp(s - jnp.max(s, axis=-1, keepdims=True))
        p = (e / jnp.sum(e, axis=-1, keepdims=True)).astype(BF)
        o_ref[:, hs] = jnp.dot(p, v_ref[:, hs], preferred_element_type=F32).astype(o_ref.dtype)


def q_attention(x, gain, wq, k, v, layer):
    n, kd = x.shape
    d = wq.shape[2]
    m = k.shape[0]
    tm = min(OUT_ROW_TILE, n)
    return pl.pallas_call(
        functools.partial(_q_attention_kernel, hd=d // XATTN_HEADS), grid=(n // tm,),
        in_specs=[pl.BlockSpec((tm, kd), lambda i: (i, 0)), pl.BlockSpec((1, kd), lambda i: (0, 0)),
                  pl.BlockSpec((None, kd, d), lambda i: (layer, 0, 0), pipeline_mode=pl.Buffered(1)),
                  pl.BlockSpec((m, d), lambda i: (0, 0)), pl.BlockSpec((m, d), lambda i: (0, 0))],
        out_specs=pl.BlockSpec((tm, d), lambda i: (i, 0)), out_shape=jax.ShapeDtypeStruct((n, d), BF),
        scratch_shapes=[pltpu.VMEM((tm, d), BF)],
        compiler_params=_cparams(("parallel",), 52), name="q_attention",
    )(x, gain.reshape(1, kd), wq, k, v)


def _mm_res_kernel(*refs, nparts):
    xs = refs[:nparts]
    w_ref, res_ref, o_ref = refs[nparts:]
    acc = res_ref[...]
    k0 = 0
    for x_ref in xs:
        kp = x_ref.shape[1]
        acc = acc + jnp.dot(x_ref[...], w_ref[k0:k0 + kp, :].astype(BF), preferred_element_type=F32)
        k0 += kp
    o_ref[...] = acc


def matmul_residual(xs, w, res, layer=None):
    n, d = res.shape
    w, li = _stacked(w, layer)
    k = w.shape[1]
    tm = min(OUT_ROW_TILE, n)
    in_specs = [pl.BlockSpec((tm, x.shape[1]), lambda i: (i, 0)) for x in xs]
    in_specs += [pl.BlockSpec((None, k, d), lambda i: (li, 0, 0), pipeline_mode=pl.Buffered(1)),
                 pl.BlockSpec((tm, d), lambda i: (i, 0))]
    return pl.pallas_call(
        functools.partial(_mm_res_kernel, nparts=len(xs)), grid=(n // tm,), in_specs=in_specs,
        out_specs=pl.BlockSpec((tm, d), lambda i: (i, 0)), out_shape=jax.ShapeDtypeStruct((n, d), F32),
        compiler_params=_cparams(("parallel",), 52), name="matmul_residual",
    )(*xs, w, res)


def _pool_kernel(cur_ref, prev_ref, w_ref, sc_ref, o_ref, *, tp, gdim):
    i = pl.program_id(0)
    dist = lax.broadcasted_iota(jnp.int32, (tp, tp), 0) - lax.broadcasted_iota(jnp.int32, (tp, tp), 1)
    distp = (lax.broadcasted_iota(jnp.int32, (tp, POOL_HALO), 0) + POOL_HALO
             - lax.broadcasted_iota(jnp.int32, (tp, POOL_HALO), 1))
    pos = i * tp + lax.broadcasted_iota(jnp.int32, (tp, 1), 0)
    for j, win in enumerate(POOL_WINDOWS):
        gs = slice(j * gdim, (j + 1) * gdim)
        cur = cur_ref[:, gs]
        band = jnp.where(dist >= 0, jnp.where(dist < win, 1.0, 0.0), 0.0).astype(BF)
        bandp = jnp.where(distp < jnp.where(i > 0, win, 0), 1.0, 0.0).astype(BF)
        s = (jnp.dot(band, cur, preferred_element_type=F32)
             + jnp.dot(bandp, prev_ref[:, gs], preferred_element_type=F32))
        cnt = jnp.minimum(pos + 1, win).astype(F32)
        d = s / cnt - cur.astype(F32)
        y = jnp.dot(d.astype(BF), w_ref[j].astype(BF), preferred_element_type=F32) * sc_ref[:, gs]
        o_ref[:, gs] = y.astype(o_ref.dtype)


def pool_mixer(z, pool_w, pool_scale):
    n = z.shape[0]
    gdim = pool_w.shape[-1]
    width = N_POOL_GROUPS * gdim
    tp = min(POOL_ROWS, n)
    halo_blocks = tp // POOL_HALO
    return pl.pallas_call(
        functools.partial(_pool_kernel, tp=tp, gdim=gdim), grid=(n // tp,),
        in_specs=[
            pl.BlockSpec((tp, width), lambda i: (i, 0)),
            pl.BlockSpec((POOL_HALO, width), lambda i: (jnp.maximum(i * halo_blocks - 1, 0), 0)),
            pl.BlockSpec((N_POOL_GROUPS, gdim, gdim), lambda i: (0, 0, 0)),
            pl.BlockSpec((1, width), lambda i: (0, 0)),
        ],
        out_specs=pl.BlockSpec((tp, width), lambda i: (i, 0)),
        out_shape=jax.ShapeDtypeStruct((n, width), BF),
        compiler_params=_cparams(("parallel",), 32), name="pool_mixer",
    )(z, z, pool_w, pool_scale.reshape(1, width))


def _mlstm_kernel(q_ref, k_ref, v_ref, o_ref, g_ref, gain_ref, y_ref, ct_ref, m_ref, *, chunk, dh):
    c = pl.program_id(0)

    @pl.when(c == 0)
    def _():
        ct_ref[...] = jnp.zeros_like(ct_ref)
        m_ref[...] = jnp.zeros_like(m_ref)

    g = g_ref[...]
    lf = _log_sigmoid(g)
    row = lax.broadcasted_iota(jnp.int32, (chunk, chunk), 0)
    col = lax.broadcasted_iota(jnp.int32, (chunk, chunk), 1)
    causal = col <= row
    ltri = jnp.where(causal, 1.0, 0.0).astype(BF)
    hi = lf.astype(BF)
    r1 = lf - hi.astype(F32)
    mid = r1.astype(BF)
    lo = (r1 - mid.astype(F32)).astype(BF)
    bcum = (jnp.dot(ltri, hi, preferred_element_type=F32) + jnp.dot(ltri, mid, preferred_element_type=F32)
            + jnp.dot(ltri, lo, preferred_element_type=F32))
    g_t = g.T
    b_t = bcum.T
    ones_col = jnp.where(lax.broadcasted_iota(jnp.int32, (chunk, LANES), 1) == 0, 1.0, 0.0).astype(BF)

    for h in range(MLSTM_HEADS):
        hs = slice(h * dh, (h + 1) * dh)
        fl = FORGET_LANE0 + h
        bc = bcum[:, fl:fl + 1]
        br = b_t[fl:fl + 1, :]
        ir = g_t[h:h + 1, :]
        b_last = bcum[chunk - 1:chunk, fl:fl + 1]
        m_prev = m_ref[h][:, 0:1]

        dmat = jnp.where(causal, bc + (ir - br), -jnp.inf)
        inter = bc + m_prev
        m_t = jnp.maximum(jnp.max(dmat, axis=1, keepdims=True), inter)
        w_inter = jnp.exp(inter - m_t)
        p = jnp.exp(dmat - m_t)

        qh = q_ref[:, hs] * (dh ** -0.5)
        kh = k_ref[:, hs]
        v_aug = jnp.concatenate([v_ref[:, hs], ones_col], axis=1)
        s = lax.dot_general(qh, kh, (((1,), (1,)), ((), ())), preferred_element_type=F32)
        sc = (s * p).astype(BF)
        ct = ct_ref[h]
        num_aug = (w_inter * jnp.dot(qh, ct.astype(BF), preferred_element_type=F32)
                   + jnp.dot(sc, v_aug, preferred_element_type=F32))
        num = num_aug[:, :dh]
        den = num_aug[:, dh:dh + 1]
        hout = num / jnp.maximum(jnp.abs(den), jnp.exp(-m_t))

        yn = _rms(hout, gain_ref[:, hs])
        y_ref[:, hs] = (_sigmoid(o_ref[:, hs].astype(F32)) * yn).astype(y_ref.dtype)

        d_end = b_last - br + ir
        m_new = jnp.maximum(b_last + m_prev, jnp.max(d_end, axis=1, keepdims=True))
        a_prev = jnp.exp(b_last + m_prev - m_new)
        a_s = jnp.exp(d_end - m_new)
        k_t = (kh.astype(F32).T * a_s).astype(BF)
        ct_ref[h] = a_prev * ct + jnp.dot(k_t, v_aug, preferred_element_type=F32)
        m_ref[h] = jnp.broadcast_to(m_new, (1, LANES))


def mlstm_mixer(z, gates, head_gain, col0):
    n = z.shape[0]
    width = head_gain.shape[0]
    dh = width // MLSTM_HEADS
    chunk = min(MLSTM_CHUNK, n)
    base = col0 // width
    qkvo = [pl.BlockSpec((chunk, width), lambda c, p=p: (c, base + p)) for p in range(4)]
    return pl.pallas_call(
        functools.partial(_mlstm_kernel, chunk=chunk, dh=dh), grid=(n // chunk,),
        in_specs=qkvo + [pl.BlockSpec((chunk, LANES), lambda c: (c, 0)), pl.BlockSpec((1, width), lambda c: (0, 0))],
        out_specs=pl.BlockSpec((chunk, width), lambda c: (c, 0)),
        out_shape=jax.ShapeDtypeStruct((n, width), BF),
        scratch_shapes=[pltpu.VMEM((MLSTM_HEADS, dh, dh + LANES), F32), pltpu.VMEM((MLSTM_HEADS, 1, LANES), F32)],
        compiler_params=_cparams(("arbitrary",), 32), name="mlstm_mixer",
    )(z, z, z, z, gates, head_gain.reshape(1, width))


def _gated_conv(b, c, u, cp, up, w, first_block):
    zc = c.astype(F32) * u.astype(F32)
    zp = jnp.where(first_block, 0.0, cp.astype(F32) * up.astype(F32))
    row = lax.broadcasted_iota(jnp.int32, zc.shape, 0)
    acc = w[CONV_WIDTH - 1:CONV_WIDTH, :] * zc
    for back in range(1, CONV_WIDTH):
        shifted = pltpu.roll(zc, back, 0)
        for r in range(back):
            shifted = jnp.where(row == r, zp[CONV_HALO - back + r:CONV_HALO - back + r + 1, :], shifted)
        acc = acc + w[CONV_WIDTH - 1 - back:CONV_WIDTH - back, :] * shifted
    return b.astype(F32) * acc


def _conv_mm_res_kernel(b_ref, c_ref, u_ref, cp_ref, up_ref, cw_ref, w_ref, res_ref, o_ref, xs_ref):
    first_block = pl.program_id(0) == 0
    for c0 in range(0, xs_ref.shape[1], CONV_COLS):
        cs = slice(c0, c0 + CONV_COLS)
        xs_ref[:, cs] = _gated_conv(b_ref[:, cs], c_ref[:, cs], u_ref[:, cs], cp_ref[:, cs], up_ref[:, cs],
                                    cw_ref[:, cs], first_block).astype(BF)
    o_ref[...] = res_ref[...] + jnp.dot(xs_ref[...], w_ref[...].astype(BF), preferred_element_type=F32)


def conv_matmul_residual(z, conv_w, w, res, layer):
    n, d = res.shape
    tm = min(CONV_ROWS, n)
    halo_blocks = tm // CONV_HALO
    cur = lambda part: pl.BlockSpec((tm, d), lambda i: (i, part))
    prev = lambda part: pl.BlockSpec((CONV_HALO, d), lambda i: (jnp.maximum(i * halo_blocks - 1, 0), part))
    return pl.pallas_call(
        _conv_mm_res_kernel, grid=(n // tm,),
        in_specs=[cur(0), cur(1), cur(2), prev(1), prev(2), pl.BlockSpec((CONV_WIDTH, d), lambda i: (0, 0)),
                  pl.BlockSpec((None, d, d), lambda i: (layer, 0, 0), pipeline_mode=pl.Buffered(1)),
                  pl.BlockSpec((tm, d), lambda i: (i, 0))],
        out_specs=pl.BlockSpec((tm, d), lambda i: (i, 0)), out_shape=jax.ShapeDtypeStruct((n, d), F32),
        scratch_shapes=[pltpu.VMEM((tm, d), BF)],
        compiler_params=_cparams(("parallel",), 52), name="conv_matmul_residual",
    )(z, z, z, z, z, conv_w, w, res)


def _router_kernel(x_ref, g_ref, wr_ref, br_ref, xn_ref, ids_ref, wts_ref, cnt_ref, carry_ref, w2_ref, *, tr):
    @pl.when(pl.program_id(0) == 0)
    def _():
        carry_ref[...] = jnp.zeros_like(carry_ref)

    @pl.when(pl.program_id(0) == 0)
    def _():
        w = wr_ref[...]
        w_hi = w.astype(BF)
        w2_ref[:, :LANES] = w_hi
        w2_ref[:, LANES:] = (w - w_hi.astype(F32)).astype(BF)

    xn = _rms(x_ref[...], g_ref[...])
    xn_ref[...] = _pack_halves(xn)
    x_hi = xn.astype(BF)
    x_lo = (xn - x_hi.astype(F32)).astype(BF)
    p_hi = jnp.dot(x_hi, w2_ref[...], preferred_element_type=F32)
    p_lo = jnp.dot(x_lo, w2_ref[...], preferred_element_type=F32)
    logits = (p_hi[:, :LANES] + p_hi[:, LANES:]) + (p_lo[:, :LANES] + p_lo[:, LANES:]) + br_ref[...]
    lane = lax.broadcasted_iota(jnp.int32, (tr, LANES), 1).astype(F32)
    neg = -jnp.inf

    def first_argmax(vals):
        top = jnp.max(vals, axis=-1, keepdims=True)
        return top, jnp.min(jnp.where(vals == top, lane, float(LANES)), axis=-1, keepdims=True)

    gl = jnp.where(lane < N_GROUPS, logits, neg)
    gmax, grp = first_argmax(gl)
    g_prob = 1.0 / jnp.sum(jnp.exp(gl - gmax), axis=-1, keepdims=True)
    lo = N_GROUPS + EXPERTS_PER_GROUP * grp
    el = jnp.where(lane >= lo, jnp.where(lane < lo + EXPERTS_PER_GROUP, logits, neg), neg)
    v1, l1 = first_argmax(el)
    v2, l2 = first_argmax(jnp.where(lane == l1, neg, el))
    e2 = jnp.exp(v2 - v1)
    w1 = g_prob / (1.0 + e2)
    w2 = g_prob * e2 / (1.0 + e2)
    hot1 = lane == l1
    hot2 = lane == l2
    hot = jnp.where(hot1, 1.0, jnp.where(hot2, 1.0, 0.0))
    earlier = (lax.broadcasted_iota(jnp.int32, (tr, tr), 1) < lax.broadcasted_iota(jnp.int32, (tr, tr), 0))
    before = jnp.dot(jnp.where(earlier, 1.0, 0.0).astype(BF), hot.astype(BF), preferred_element_type=F32)
    before = before + carry_ref[0:1, :]
    r1 = jnp.sum(jnp.where(hot1, before, 0.0), axis=-1, keepdims=True)
    r2 = jnp.sum(jnp.where(hot2, before, 0.0), axis=-1, keepdims=True)
    carry_ref[0:1, :] = carry_ref[0:1, :] + jnp.sum(hot, axis=0, keepdims=True)
    ids = jnp.where(lane == 0, l1 - N_GROUPS, jnp.where(lane == 1, l2 - N_GROUPS,
                    jnp.where(lane == 2, r1, jnp.where(lane == 3, r2, 0.0))))
    ids_ref[...] = ids.astype(jnp.int32)
    wts_ref[...] = jnp.where(lane == 0, w1, jnp.where(lane == 1, w2, 0.0))
    cnt_ref[...] = carry_ref[...]


def route(h, gain, w_router, b_router):
    n, d = h.shape
    tr = min(ROUTER_ROWS, n)
    return pl.pallas_call(
        functools.partial(_router_kernel, tr=tr), grid=(n // tr,),
        in_specs=[pl.BlockSpec((tr, d), lambda i: (i, 0)), pl.BlockSpec((1, d), lambda i: (0, 0)),
                  pl.BlockSpec((d, LANES), lambda i: (0, 0)), pl.BlockSpec((1, LANES), lambda i: (0, 0))],
        out_specs=[pl.BlockSpec((tr, d // 2), lambda i: (i, 0)),
                   pl.BlockSpec((tr, LANES), lambda i: (i, 0)), pl.BlockSpec((tr, LANES), lambda i: (i, 0)),
                   pl.BlockSpec((SUBLANES, LANES), lambda i: (0, 0))],
        out_shape=[jax.ShapeDtypeStruct((n, d // 2), jnp.uint32), jax.ShapeDtypeStruct((n, LANES), jnp.int32),
                   jax.ShapeDtypeStruct((n, LANES), F32), jax.ShapeDtypeStruct((SUBLANES, LANES), F32)],
        scratch_shapes=[pltpu.VMEM((SUBLANES, LANES), F32), pltpu.VMEM((d, 2 * LANES), BF)],
        compiler_params=_cparams(("arbitrary",), 40), name="moe_router",
    )(h, gain.reshape(1, d), w_router, b_router)


def _dest_kernel(ids_ref, start_ref, o_ref, *, tr):
    ids = ids_ref[...].astype(F32)
    lane = lax.broadcasted_iota(jnp.int32, (tr, LANES), 1).astype(F32)
    start = start_ref[...]
    rows = []
    for k in range(TOP_K):
        first = jnp.sum(jnp.where(lane == ids[:, k:k + 1], start, 0.0), axis=-1, keepdims=True)
        rows.append(first + ids[:, TOP_K + k:TOP_K + k + 1])
    packed = jnp.where(lane == 0, rows[0], jnp.where(lane == 1, rows[1], 0.0))
    o_ref[...] = packed.T[0:SUBLANES, :].astype(jnp.int32)


def assignment_rows(ids, start_rows):
    n = ids.shape[0]
    tr = min(ROUTER_ROWS, n)
    out = pl.pallas_call(
        functools.partial(_dest_kernel, tr=tr), grid=(n // tr,),
        in_specs=[pl.BlockSpec((tr, LANES), lambda i: (i, 0)), pl.BlockSpec((1, LANES), lambda i: (0, 0))],
        out_specs=pl.BlockSpec((SUBLANES, tr), lambda i: (0, i)),
        out_shape=jax.ShapeDtypeStruct((SUBLANES, n), jnp.int32),
        compiler_params=_cparams(("parallel",), 32), name="moe_assignment_rows",
    )(ids, start_rows)
    return out[:TOP_K]


def _row_copy(src_hbm, row, dst, dst_row, sem):
    return pltpu.make_async_copy(src_hbm.at[pl.ds(row, 1), :], dst.at[pl.ds(dst_row, 1), :], sem)


def _rows_wait(src_hbm, dst, sem):
    pltpu.make_async_copy(src_hbm.at[pl.ds(0, dst.shape[0]), :], dst, sem).wait()


def _expert_weight_copies(w_hbm, layer, expert, stage, wset, sem):
    rows = stage.shape[1] // WEIGHT_DMA_PARTS
    return [pltpu.make_async_copy(w_hbm.at[layer, expert, pl.ds(p * rows, rows), :],
                                  stage.at[wset, pl.ds(p * rows, rows), :], sem.at[wset])
            for p in range(WEIGHT_DMA_PARTS)]


def _stream_expert_weights(b, nact, blk_ref, kin_ref, set_ref, nxt_ref, copies):
    cur = set_ref[b]

    @pl.when(b == 0)
    def _():
        for cp in copies(blk_ref[0], cur):
            cp.start(priority=DMA_QUEUE_BULK)

    @pl.when(jnp.logical_and(b < nact, kin_ref[b] == 0))
    def _():
        for cp in copies(blk_ref[b], cur):
            cp.wait()

        @pl.when(nxt_ref[b] != blk_ref[b])
        def _():
            for cp in copies(nxt_ref[b], 1 - cur):
                cp.start(priority=DMA_QUEUE_BULK)

    return cur


def _moe_expert_kernel(blk_ref, kin_ref, last_ref, set_ref, nxt_ref, nact_ref, rtok_ref,
                       xn_hbm, wg_hbm, wu_hbm, wd_hbm, ys_ref,
                       buf, stage_g, stage_u, stage_d, xb, hid, sem, wsem, dsem, *, tb, layer):
    b = pl.program_id(0)
    nact = nact_ref[0]
    nslots = GATHER_AHEAD + 1
    slot = b % nslots
    f = hid.shape[1]
    half = ys_ref.shape[1]
    up_chunks = f // MOE_COL_CHUNK
    rows_per_chunk = tb // up_chunks
    active = b < nact
    first_of_expert = jnp.logical_and(active, kin_ref[b] == 0)
    has_next = nxt_ref[b] != blk_ref[b]

    def up_copies(e, wset):
        return (_expert_weight_copies(wg_hbm, layer, e, stage_g, wset, wsem.at[0])
                + _expert_weight_copies(wu_hbm, layer, e, stage_u, wset, wsem.at[1]))

    def down_copies(e):
        rows = stage_d.shape[0] // WEIGHT_DMA_PARTS
        return [pltpu.make_async_copy(wd_hbm.at[layer, e, pl.ds(p * rows, rows), :],
                                      stage_d.at[pl.ds(p * rows, rows), :], dsem.at[0])
                for p in range(WEIGHT_DMA_PARTS)]

    @pl.when(b == 0)
    def _():
        for ahead in range(GATHER_AHEAD):
            def body(r, carry, ahead=ahead):
                _row_copy(xn_hbm, rtok_ref[ahead * tb + r], buf.at[ahead], r,
                          sem.at[ahead]).start(priority=DMA_QUEUE_ROWS)
                return carry

            lax.fori_loop(0, tb, body, 0, unroll=8)
        for cp in down_copies(blk_ref[0]):
            cp.start(priority=DMA_QUEUE_BULK)

    cur = _stream_expert_weights(b, nact, blk_ref, kin_ref, set_ref, nxt_ref, up_copies)

    @pl.when(active)
    def _():
        _rows_wait(xn_hbm, buf.at[slot], sem.at[slot])
        x_lo, x_hi = _unpack_halves(buf[slot])
        xb[:, :x_lo.shape[1]] = x_lo.astype(BF)
        xb[:, x_lo.shape[1]:] = x_hi.astype(BF)
        ahead_slot = (b + GATHER_AHEAD) % nslots
        for c in range(up_chunks):
            for r in range(c * rows_per_chunk, (c + 1) * rows_per_chunk):
                _row_copy(xn_hbm, rtok_ref[(b + GATHER_AHEAD) * tb + r], buf.at[ahead_slot], r,
                          sem.at[ahead_slot]).start(priority=DMA_QUEUE_ROWS)
            cs = slice(c * MOE_COL_CHUNK, (c + 1) * MOE_COL_CHUNK)
            gate = jnp.dot(xb[...], stage_g[cur, :, cs].astype(BF), preferred_element_type=F32)
            up = jnp.dot(xb[...], stage_u[cur, :, cs].astype(BF), preferred_element_type=F32)
            hid[:, cs] = (gate * _sigmoid(gate) * up).astype(hid.dtype)

    @pl.when(first_of_expert)
    def _():
        for cp in down_copies(blk_ref[b]):
            cp.wait()

    @pl.when(active)
    def _():
        for c in range(half // MOE_COL_CHUNK):
            cs = slice(c * MOE_COL_CHUNK, (c + 1) * MOE_COL_CHUNK)
            hs = slice(half + c * MOE_COL_CHUNK, half + (c + 1) * MOE_COL_CHUNK)
            lo = jnp.dot(hid[...], stage_d[:, cs].astype(BF), preferred_element_type=F32)
            hi = jnp.dot(hid[...], stage_d[:, hs].astype(BF), preferred_element_type=F32)
            ys_ref[:, cs] = _pack_halves(jnp.concatenate([lo, hi], axis=1))

    @pl.when(jnp.logical_and(jnp.logical_and(active, last_ref[b] == 1), has_next))
    def _():
        for cp in down_copies(nxt_ref[b]):
            cp.start(priority=DMA_QUEUE_BULK)

    @pl.when(b >= nact)
    def _():
        ys_ref[...] = jnp.zeros_like(ys_ref)

    @pl.when(jnp.logical_and(b >= nact, b < nact + GATHER_AHEAD))
    def _():
        _rows_wait(xn_hbm, buf.at[slot], sem.at[slot])


def _combine_kernel(dest_ref, ys_hbm, h_ref, w_ref, gain_ref, o_ref, buf, sem, *, tc, n, final_norm):
    i = pl.program_id(0)

    def row_copy(blk, slot, r, k):
        return _row_copy(ys_hbm, dest_ref[k * n + blk * tc + r], buf.at[slot], k * tc + r, sem.at[slot])

    @pl.when(i == 0)
    def _():
        def body(r, carry):
            for k in range(TOP_K):
                row_copy(0, 0, r, k).start(priority=k)
            return carry

        lax.fori_loop(0, tc, body, 0, unroll=8)

    @pl.when(i + 1 < pl.num_programs(0))
    def _():
        for r in range(tc):
            for k in range(TOP_K):
                row_copy(i + 1, (i + 1) % 2, r, k).start(priority=k)

    slot = i % 2
    _rows_wait(ys_hbm, buf.at[slot], sem.at[slot])
    a_lo, a_hi = _unpack_halves(buf[slot, 0:tc, :])
    b_lo, b_hi = _unpack_halves(buf[slot, tc:TOP_K * tc, :])
    w0 = w_ref[:, 0:1]
    w1 = w_ref[:, 1:2]
    out = h_ref[...] + jnp.concatenate([w0 * a_lo + w1 * b_lo, w0 * a_hi + w1 * b_hi], axis=1)
    if final_norm:
        out = _rms(out, gain_ref[...])
    o_ref[...] = out


def hier_moe(h, ffn_gain, wg_r, bg_r, we_r, be_r, w_gate, w_up, w_down, layer, final_gain=None):
    n, d = h.shape
    f = w_gate.shape[-1]
    a = n * TOP_K
    tb = _moe_block_rows(a)
    nb = a // tb + N_EXPERTS + GATHER_AHEAD

    pad = LANES - N_GROUPS - N_EXPERTS
    w_router = jnp.concatenate([wg_r, we_r, jnp.zeros((d, pad), F32)], axis=1)
    b_router = jnp.concatenate([bg_r, be_r, jnp.zeros((pad,), F32)]).reshape(1, LANES)
    xn, ids, wts, cnt = route(h, ffn_gain, w_router, b_router)

    counts = cnt[0, N_GROUPS:N_GROUPS + N_EXPERTS].astype(jnp.int32)
    nblk = (counts + tb - 1) // tb
    bend = jnp.cumsum(nblk)
    nact = bend[-1]
    start_rows = jnp.pad(((bend - nblk) * tb).astype(F32), (0, LANES - N_EXPERTS)).reshape(1, LANES)
    dest = assignment_rows(ids, start_rows).reshape(a)
    blk = jnp.minimum(jnp.arange(nb, dtype=jnp.int32), jnp.maximum(nact - 1, 0))
    owner = lambda bi: jnp.minimum(jnp.sum((bend[None, :] <= bi[:, None]).astype(jnp.int32), axis=1), N_EXPERTS - 1)
    blk_e = owner(blk)
    k_in_e = blk - (bend - nblk)[blk_e]
    following = lambda e: jnp.where(bend[e] < nact, owner(bend[e]), e)
    nxt_e = following(blk_e)
    wset = (jnp.cumsum((k_in_e == 0).astype(jnp.int32)) - 1) % 2
    tok = jnp.tile(jnp.arange(n, dtype=jnp.int32), TOP_K)
    row_tok = (jnp.arange(nb * tb, dtype=jnp.int32) % n).at[dest].set(
        tok, unique_indices=True, mode='promise_in_bounds')
    nact1 = nact.reshape(1).astype(jnp.int32)

    any_space = pl.BlockSpec(memory_space=pl.ANY)
    is_last = (k_in_e == nblk[blk_e] - 1).astype(jnp.int32)
    ys = pl.pallas_call(
        functools.partial(_moe_expert_kernel, tb=tb, layer=layer),
        grid_spec=pltpu.PrefetchScalarGridSpec(
            num_scalar_prefetch=7, grid=(nb,),
            in_specs=[any_space, any_space, any_space, any_space],
            out_specs=pl.BlockSpec((tb, d // 2), lambda b, *_: (b, 0)),
            scratch_shapes=[pltpu.VMEM((GATHER_AHEAD + 1, tb, d // 2), jnp.uint32),
                            pltpu.VMEM((2, d, f), F32), pltpu.VMEM((2, d, f), F32), pltpu.VMEM((f, d), F32),
                            pltpu.VMEM((tb, d), BF), pltpu.VMEM((tb, f), BF),
                            pltpu.SemaphoreType.DMA((GATHER_AHEAD + 1,)), pltpu.SemaphoreType.DMA((2, 2)),
                            pltpu.SemaphoreType.DMA((1,))]),
        out_shape=jax.ShapeDtypeStruct((nb * tb, d // 2), jnp.uint32),
        compiler_params=_cparams(("arbitrary",), 56), name="moe_experts",
    )(blk_e, k_in_e, is_last, wset, nxt_e, nact1, row_tok, xn, w_gate, w_up, w_down)

    tc = min(COMBINE_ROWS, n)
    gain = (final_gain if final_gain is not None else ffn_gain).reshape(1, d)
    return pl.pallas_call(
        functools.partial(_combine_kernel, tc=tc, n=n, final_norm=final_gain is not None),
        grid_spec=pltpu.PrefetchScalarGridSpec(
            num_scalar_prefetch=1, grid=(n // tc,),
            in_specs=[pl.BlockSpec(memory_space=pl.ANY), pl.BlockSpec((tc, d), lambda i, *_: (i, 0)),
                      pl.BlockSpec((tc, LANES), lambda i, *_: (i, 0)), pl.BlockSpec((1, d), lambda i, *_: (0, 0))],
            out_specs=pl.BlockSpec((tc, d), lambda i, *_: (i, 0)),
            scratch_shapes=[pltpu.VMEM((2, TOP_K * tc, d // 2), jnp.uint32), pltpu.SemaphoreType.DMA((2,))]),
        out_shape=jax.ShapeDtypeStruct((n, d), F32),
        compiler_params=_cparams(("arbitrary",), 40), name="moe_combine",
    )(dest, ys, h, wts, gain)


def even_mixer(h, gain, w_in, b_gates, pool_w, pool_scale, head_gain, w_out, j):
    d = h.shape[1]
    pool_width = N_POOL_GROUPS * pool_w.shape[-1]
    mlstm_width = head_gain.shape[0]
    main_cols = pool_width + 4 * mlstm_width
    n_gates = 2 * MLSTM_HEADS
    gate_b = jnp.pad(b_gates, (0, LANES - n_gates)).reshape(1, LANES)
    z, gates = norm_matmul_gates_t(h, gain, jnp.swapaxes(w_in, 1, 2), main_cols, n_gates, gate_b, j)
    y_p = pool_mixer(z, pool_w, pool_scale)
    y_m = mlstm_mixer(z, gates, head_gain, pool_width)
    assert pool_width == mlstm_width and pool_width + mlstm_width == d
    return matmul_residual([y_p, y_m], w_out, h, layer=j)


def odd_mixer(h, gain, w_in, conv_w, w_out, j):
    z = norm_matmul(h, gain, w_in, w_in.shape[2], layer=j)
    return conv_matmul_residual(z, conv_w, w_out, h, j)


def cross_attn(h, mem, gain, mem_gain, wq, wk, wv, wo, layer):
    d = h.shape[1]
    k = norm_matmul(mem, mem_gain, wk, d, layer=layer)
    v = norm_matmul(mem, mem_gain, wv, d, layer=layer)
    return matmul_residual([q_attention(h, gain, wq, k, v, layer)], wo, h, layer=layer)


def kernel(x, mem, mix_norm, xattn_norm, mem_norm, ffn_norm, final_norm, ev_w_in, ev_b_gates, ev_pool_w, ev_pool_scale, ev_head_norm, ev_w_out, od_w_in, od_conv_w, od_w_out, xa_wq, xa_wk, xa_wv, xa_wo, rt_group_w, rt_group_b, rt_expert_w, rt_expert_b, ex_w_gate, ex_w_up, ex_w_down):
    depth = mix_norm.shape[0]
    h = x[0]
    m = mem[0]
    for layer in range(depth):
        j = layer // 2
        if layer % 2 == 0:
            h = even_mixer(h, mix_norm[layer], ev_w_in, ev_b_gates[j], ev_pool_w[j], ev_pool_scale[j],
                           ev_head_norm[j], ev_w_out, j)
        else:
            h = odd_mixer(h, mix_norm[layer], od_w_in, od_conv_w[j], od_w_out, j)
        h = cross_attn(h, m, xattn_norm[layer], mem_norm[layer], xa_wq, xa_wk, xa_wv, xa_wo, layer)
        h = hier_moe(h, ffn_norm[layer], rt_group_w[layer], rt_group_b[layer], rt_expert_w[layer],
                     rt_expert_b[layer], ex_w_gate, ex_w_up, ex_w_down, layer,
                     final_gain=final_norm if layer == depth - 1 else None)
    return h[None]
```

```python
import functools

import jax
import jax.numpy as jnp
from jax import lax
from jax.experimental import pallas as pl
from jax.experimental.pallas import tpu as pltpu

F32 = jnp.float32
BF = jnp.bfloat16
EPS = 1e-6

POOL_WINDOWS = (2, 4, 8, 16)
N_POOL_GROUPS = 4
MLSTM_HEADS = 4
FORGET_LANE0 = MLSTM_HEADS
XATTN_HEADS = 4
N_GROUPS = 4
EXPERTS_PER_GROUP = 8
N_EXPERTS = N_GROUPS * EXPERTS_PER_GROUP
TOP_K = 2
CONV_WIDTH = 3

LANES = 128
SUBLANES = 8
DMA_QUEUE_ROWS = 0
DMA_QUEUE_BULK = 1

ROW_TILE = 1024
COL_TILE = 1024
OUT_ROW_TILE = 512
MOE_COL_CHUNK = 256
WEIGHT_DMA_PARTS = 4
MLSTM_CHUNK = 256
POOL_ROWS = 256
POOL_HALO = 128
CONV_ROWS = 512
CONV_COLS = 512
CONV_HALO = 16
ROUTER_ROWS = 512
GATHER_AHEAD = 2
COMBINE_ROWS = 512


def _moe_block_rows(assignments):
    mean_rows = assignments // N_EXPERTS
    return -(-(mean_rows * 9 // 16) // 16) * 16


def _cparams(semantics, vmem_mib):
    return pltpu.CompilerParams(dimension_semantics=semantics, vmem_limit_bytes=vmem_mib * 1024 * 1024)


def _sigmoid(x):
    return 1.0 / (1.0 + jnp.exp(-x))


def _log_sigmoid(x):
    return jnp.minimum(x, 0.0) - jnp.log(1.0 + jnp.exp(-jnp.abs(x)))


def _rms(x, g):
    ms = jnp.mean(x * x, axis=-1, keepdims=True)
    return x * lax.rsqrt(ms + EPS) * g


def _pack_halves(x):
    half = x.shape[1] // 2
    lo = pltpu.bitcast(x[:, :half].astype(BF).astype(F32), jnp.uint32)
    hi = pltpu.bitcast(x[:, half:].astype(BF).astype(F32), jnp.uint32)
    return hi | lax.shift_right_logical(lo, jnp.uint32(16))


def _unpack_halves(w):
    lo = pltpu.bitcast(lax.shift_left(w, jnp.uint32(16)), F32)
    hi = pltpu.bitcast(w & jnp.uint32(0xFFFF0000), F32)
    return lo, hi


def _norm_mm_kernel(x_ref, g_ref, w_ref, o_ref, xn_ref):
    @pl.when(pl.program_id(1) == 0)
    def _():
        xn_ref[...] = _rms(x_ref[...], g_ref[...]).astype(BF)

    o_ref[...] = jnp.dot(xn_ref[...], w_ref[...].astype(BF), preferred_element_type=F32).astype(o_ref.dtype)


_NT = (((1,), (1,)), ((), ()))


def _norm_mm_gates_t_kernel(x_ref, g_ref, wt_ref, wgt_ref, bg_ref, o_ref, gates_ref, xn_ref):
    @pl.when(pl.program_id(1) == 0)
    def _():
        xn = _rms(x_ref[...], g_ref[...]).astype(BF)
        xn_ref[...] = xn
        wg = wgt_ref[...]
        wg = jnp.concatenate([wg, jnp.zeros((LANES - wg.shape[0], wg.shape[1]), F32)], axis=0).astype(BF)
        gates_ref[...] = lax.dot_general(xn, wg, _NT, preferred_element_type=F32) + bg_ref[...]

    o_ref[...] = lax.dot_general(xn_ref[...], wt_ref[...].astype(BF), _NT,
                                 preferred_element_type=F32).astype(o_ref.dtype)


def norm_matmul_gates_t(x, gain, wt, n_cols, n_gates, gate_b, layer):
    n, k = x.shape
    tm = min(ROW_TILE, n)
    tn = COL_TILE
    return pl.pallas_call(
        _norm_mm_gates_t_kernel, grid=(n // tm, n_cols // tn),
        in_specs=[pl.BlockSpec((tm, k), lambda i, j: (i, 0)), pl.BlockSpec((1, k), lambda i, j: (0, 0)),
                  pl.BlockSpec((None, tn, k), lambda i, j: (layer, j, 0)),
                  pl.BlockSpec((None, n_gates, k), lambda i, j: (layer, n_cols // n_gates, 0)),
                  pl.BlockSpec((1, LANES), lambda i, j: (0, 0))],
        out_specs=[pl.BlockSpec((tm, tn), lambda i, j: (i, j)), pl.BlockSpec((tm, LANES), lambda i, j: (i, 0))],
        out_shape=[jax.ShapeDtypeStruct((n, n_cols), BF), jax.ShapeDtypeStruct((n, LANES), F32)],
        scratch_shapes=[pltpu.VMEM((tm, k), BF)],
        compiler_params=_cparams(("parallel", "arbitrary"), 56), name="norm_matmul_gates",
    )(x, gain.reshape(1, k), wt, wt, gate_b)


def _stacked(w, layer):
    return (w[None], 0) if layer is None else (w, layer)


def norm_matmul(x, gain, w, n_cols, layer=None):
    n, k = x.shape
    w, li = _stacked(w, layer)
    tm = min(ROW_TILE, n)
    tn = COL_TILE
    return pl.pallas_call(
        _norm_mm_kernel, grid=(n // tm, n_cols // tn),
        in_specs=[pl.BlockSpec((tm, k), lambda i, j: (i, 0)), pl.BlockSpec((1, k), lambda i, j: (0, 0)),
                  pl.BlockSpec((None, k, tn), lambda i, j: (li, 0, j))],
        out_specs=pl.BlockSpec((tm, tn), lambda i, j: (i, j)),
        out_shape=jax.ShapeDtypeStruct((n, n_cols), BF), scratch_shapes=[pltpu.VMEM((tm, k), BF)],
        compiler_params=_cparams(("parallel", "arbitrary"), 56), name="norm_matmul",
    )(x, gain.reshape(1, k), w)


def _q_attention_kernel(x_ref, g_ref, w_ref, k_ref, v_ref, o_ref, q_ref, *, hd):
    xn = _rms(x_ref[...], g_ref[...]).astype(BF)
    q_ref[...] = jnp.dot(xn, w_ref[...].astype(BF), preferred_element_type=F32).astype(BF)
    scale = hd ** -0.5
    for h in range(XATTN_HEADS):
        hs = slice(h * hd, (h + 1) * hd)
        s = lax.dot_general(q_ref[:, hs], k_ref[:, hs], _NT, preferred_element_type=F32) * scale
        e = jnp.exp(s - jnp.max(s, axis=-1, keepdims=True))
        p = (e / jnp.sum(e, axis=-1, keepdims=True)).astype(BF)
        o_ref[:, hs] = jnp.dot(p, v_ref[:, hs], preferred_element_type=F32).astype(o_ref.dtype)


def q_attention(x, gain, wq, k, v, layer):
    n, kd = x.shape
    d = wq.shape[2]
    m = k.shape[0]
    tm = min(OUT_ROW_TILE, n)
    return pl.pallas_call(
        functools.partial(_q_attention_kernel, hd=d // XATTN_HEADS), grid=(n // tm,),
        in_specs=[pl.BlockSpec((tm, kd), lambda i: (i, 0)), pl.BlockSpec((1, kd), lambda i: (0, 0)),
                  pl.BlockSpec((None, kd, d), lambda i: (layer, 0, 0), pipeline_mode=pl.Buffered(1)),
                  pl.BlockSpec((m, d), lambda i: (0, 0)), pl.BlockSpec((m, d), lambda i: (0, 0))],
        out_specs=pl.BlockSpec((tm, d), lambda i: (i, 0)), out_shape=jax.ShapeDtypeStruct((n, d), BF),
        scratch_shapes=[pltpu.VMEM((tm, d), BF)],
        compiler_params=_cparams(("parallel",), 52), name="q_attention",
    )(x, gain.reshape(1, kd), wq, k, v)


def _mm_res_kernel(*refs, nparts):
    xs = refs[:nparts]
    w_ref, res_ref, o_ref = refs[nparts:]
    acc = res_ref[...]
    k0 = 0
    for x_ref in xs:
        kp = x_ref.shape[1]
        acc = acc + jnp.dot(x_ref[...], w_ref[k0:k0 + kp, :].astype(BF), preferred_element_type=F32)
        k0 += kp
    o_ref[...] = acc


def matmul_residual(xs, w, res, layer=None):
    n, d = res.shape
    w, li = _stacked(w, layer)
    k = w.shape[1]
    tm = min(OUT_ROW_TILE, n)
    in_specs = [pl.BlockSpec((tm, x.shape[1]), lambda i: (i, 0)) for x in xs]
    in_specs += [pl.BlockSpec((None, k, d), lambda i: (li, 0, 0), pipeline_mode=pl.Buffered(1)),
                 pl.BlockSpec((tm, d), lambda i: (i, 0))]
    return pl.pallas_call(
        functools.partial(_mm_res_kernel, nparts=len(xs)), grid=(n // tm,), in_specs=in_specs,
        out_specs=pl.BlockSpec((tm, d), lambda i: (i, 0)), out_shape=jax.ShapeDtypeStruct((n, d), F32),
        compiler_params=_cparams(("parallel",), 52), name="matmul_residual",
    )(*xs, w, res)


def _pool_kernel(cur_ref, prev_ref, w_ref, sc_ref, o_ref, *, tp, gdim):
    i = pl.program_id(0)
    dist = lax.broadcasted_iota(jnp.int32, (tp, tp), 0) - lax.broadcasted_iota(jnp.int32, (tp, tp), 1)
    distp = (lax.broadcasted_iota(jnp.int32, (tp, POOL_HALO), 0) + POOL_HALO
             - lax.broadcasted_iota(jnp.int32, (tp, POOL_HALO), 1))
    pos = i * tp + lax.broadcasted_iota(jnp.int32, (tp, 1), 0)
    for j, win in enumerate(POOL_WINDOWS):
        gs = slice(j * gdim, (j + 1) * gdim)
        cur = cur_ref[:, gs]
        band = jnp.where(dist >= 0, jnp.where(dist < win, 1.0, 0.0), 0.0).astype(BF)
        bandp = jnp.where(distp < jnp.where(i > 0, win, 0), 1.0, 0.0).astype(BF)
        s = (jnp.dot(band, cur, preferred_element_type=F32)
             + jnp.dot(bandp, prev_ref[:, gs], preferred_element_type=F32))
        cnt = jnp.minimum(pos + 1, win).astype(F32)
        d = s / cnt - cur.astype(F32)
        y = jnp.dot(d.astype(BF), w_ref[j].astype(BF), preferred_element_type=F32) * sc_ref[:, gs]
        o_ref[:, gs] = y.astype(o_ref.dtype)


def pool_mixer(z, pool_w, pool_scale):
    n = z.shape[0]
    gdim = pool_w.shape[-1]
    width = N_POOL_GROUPS * gdim
    tp = min(POOL_ROWS, n)
    halo_blocks = tp // POOL_HALO
    return pl.pallas_call(
        functools.partial(_pool_kernel, tp=tp, gdim=gdim), grid=(n // tp,),
        in_specs=[
            pl.BlockSpec((tp, width), lambda i: (i, 0)),
            pl.BlockSpec((POOL_HALO, width), lambda i: (jnp.maximum(i * halo_blocks - 1, 0), 0)),
            pl.BlockSpec((N_POOL_GROUPS, gdim, gdim), lambda i: (0, 0, 0)),
            pl.BlockSpec((1, width), lambda i: (0, 0)),
        ],
        out_specs=pl.BlockSpec((tp, width), lambda i: (i, 0)),
        out_shape=jax.ShapeDtypeStruct((n, width), BF),
        compiler_params=_cparams(("parallel",), 32), name="pool_mixer",
    )(z, z, pool_w, pool_scale.reshape(1, width))


def _mlstm_kernel(q_ref, k_ref, v_ref, o_ref, g_ref, gain_ref, y_ref, ct_ref, m_ref, *, chunk, dh):
    c = pl.program_id(0)

    @pl.when(c == 0)
    def _():
        ct_ref[...] = jnp.zeros_like(ct_ref)
        m_ref[...] = jnp.zeros_like(m_ref)

    g = g_ref[...]
    lf = _log_sigmoid(g)
    row = lax.broadcasted_iota(jnp.int32, (chunk, chunk), 0)
    col = lax.broadcasted_iota(jnp.int32, (chunk, chunk), 1)
    causal = col <= row
    ltri = jnp.where(causal, 1.0, 0.0).astype(BF)
    hi = lf.astype(BF)
    r1 = lf - hi.astype(F32)
    mid = r1.astype(BF)
    lo = (r1 - mid.astype(F32)).astype(BF)
    bcum = (jnp.dot(ltri, hi, preferred_element_type=F32) + jnp.dot(ltri, mid, preferred_element_type=F32)
            + jnp.dot(ltri, lo, preferred_element_type=F32))
    g_t = g.T
    b_t = bcum.T
    ones_col = jnp.where(lax.broadcasted_iota(jnp.int32, (chunk, LANES), 1) == 0, 1.0, 0.0).astype(BF)

    for h in range(MLSTM_HEADS):
        hs = slice(h * dh, (h + 1) * dh)
        fl = FORGET_LANE0 + h
        bc = bcum[:, fl:fl + 1]
        br = b_t[fl:fl + 1, :]
        ir = g_t[h:h + 1, :]
        b_last = bcum[chunk - 1:chunk, fl:fl + 1]
        m_prev = m_ref[h][:, 0:1]

        dmat = jnp.where(causal, bc + (ir - br), -jnp.inf)
        inter = bc + m_prev
        m_t = jnp.maximum(jnp.max(dmat, axis=1, keepdims=True), inter)
        w_inter = jnp.exp(inter - m_t)
        p = jnp.exp(dmat - m_t)

        qh = q_ref[:, hs] * (dh ** -0.5)
        kh = k_ref[:, hs]
        v_aug = jnp.concatenate([v_ref[:, hs], ones_col], axis=1)
        s = lax.dot_general(qh, kh, (((1,), (1,)), ((), ())), preferred_element_type=F32)
        sc = (s * p).astype(BF)
        ct = ct_ref[h]
        num_aug = (w_inter * jnp.dot(qh, ct.astype(BF), preferred_element_type=F32)
                   + jnp.dot(sc, v_aug, preferred_element_type=F32))
        num = num_aug[:, :dh]
        den = num_aug[:, dh:dh + 1]
        hout = num / jnp.maximum(jnp.abs(den), jnp.exp(-m_t))

        yn = _rms(hout, gain_ref[:, hs])
        y_ref[:, hs] = (_sigmoid(o_ref[:, hs].astype(F32)) * yn).astype(y_ref.dtype)

        d_end = b_last - br + ir
        m_new = jnp.maximum(b_last + m_prev, jnp.max(d_end, axis=1, keepdims=True))
        a_prev = jnp.exp(b_last + m_prev - m_new)
        a_s = jnp.exp(d_end - m_new)
        k_t = (kh.astype(F32).T * a_s).astype(BF)
        ct_ref[h] = a_prev * ct + jnp.dot(k_t, v_aug, preferred_element_type=F32)
        m_ref[h] = jnp.broadcast_to(m_new, (1, LANES))


def mlstm_mixer(z, gates, head_gain, col0):
    n = z.shape[0]
    width = head_gain.shape[0]
    dh = width // MLSTM_HEADS
    chunk = min(MLSTM_CHUNK, n)
    base = col0 // width
    qkvo = [pl.BlockSpec((chunk, width), lambda c, p=p: (c, base + p)) for p in range(4)]
    return pl.pallas_call(
        functools.partial(_mlstm_kernel, chunk=chunk, dh=dh), grid=(n // chunk,),
        in_specs=qkvo + [pl.BlockSpec((chunk, LANES), lambda c: (c, 0)), pl.BlockSpec((1, width), lambda c: (0, 0))],
        out_specs=pl.BlockSpec((chunk, width), lambda c: (c, 0)),
        out_shape=jax.ShapeDtypeStruct((n, width), BF),
        scratch_shapes=[pltpu.VMEM((MLSTM_HEADS, dh, dh + LANES), F32), pltpu.VMEM((MLSTM_HEADS, 1, LANES), F32)],
        compiler_params=_cparams(("arbitrary",), 32), name="mlstm_mixer",
    )(z, z, z, z, gates, head_gain.reshape(1, width))


def _gated_conv(b, c, u, cp, up, w, first_block):
    zc = c.astype(F32) * u.astype(F32)
    zp = jnp.where(first_block, 0.0, cp.astype(F32) * up.astype(F32))
    row = lax.broadcasted_iota(jnp.int32, zc.shape, 0)
    acc = w[CONV_WIDTH - 1:CONV_WIDTH, :] * zc
    for back in range(1, CONV_WIDTH):
        shifted = pltpu.roll(zc, back, 0)
        for r in range(back):
            shifted = jnp.where(row == r, zp[CONV_HALO - back + r:CONV_HALO - back + r + 1, :], shifted)
        acc = acc + w[CONV_WIDTH - 1 - back:CONV_WIDTH - back, :] * shifted
    return b.astype(F32) * acc


def _conv_mm_res_kernel(b_ref, c_ref, u_ref, cp_ref, up_ref, cw_ref, w_ref, res_ref, o_ref, xs_ref):
    first_block = pl.program_id(0) == 0
    for c0 in range(0, xs_ref.shape[1], CONV_COLS):
        cs = slice(c0, c0 + CONV_COLS)
        xs_ref[:, cs] = _gated_conv(b_ref[:, cs], c_ref[:, cs], u_ref[:, cs], cp_ref[:, cs], up_ref[:, cs],
                                    cw_ref[:, cs], first_block).astype(BF)
    o_ref[...] = res_ref[...] + jnp.dot(xs_ref[...], w_ref[...].astype(BF), preferred_element_type=F32)


def conv_matmul_residual(z, conv_w, w, res, layer):
    n, d = res.shape
    tm = min(CONV_ROWS, n)
    halo_blocks = tm // CONV_HALO
    cur = lambda part: pl.BlockSpec((tm, d), lambda i: (i, part))
    prev = lambda part: pl.BlockSpec((CONV_HALO, d), lambda i: (jnp.maximum(i * halo_blocks - 1, 0), part))
    return pl.pallas_call(
        _conv_mm_res_kernel, grid=(n // tm,),
        in_specs=[cur(0), cur(1), cur(2), prev(1), prev(2), pl.BlockSpec((CONV_WIDTH, d), lambda i: (0, 0)),
                  pl.BlockSpec((None, d, d), lambda i: (layer, 0, 0), pipeline_mode=pl.Buffered(1)),
                  pl.BlockSpec((tm, d), lambda i: (i, 0))],
        out_specs=pl.BlockSpec((tm, d), lambda i: (i, 0)), out_shape=jax.ShapeDtypeStruct((n, d), F32),
        scratch_shapes=[pltpu.VMEM((tm, d), BF)],
        compiler_params=_cparams(("parallel",), 52), name="conv_matmul_residual",
    )(z, z, z, z, z, conv_w, w, res)


def _router_kernel(x_ref, g_ref, wr_ref, br_ref, xn_ref, ids_ref, wts_ref, cnt_ref, carry_ref, w2_ref, *, tr):
    @pl.when(pl.program_id(0) == 0)
    def _():
        carry_ref[...] = jnp.zeros_like(carry_ref)

    @pl.when(pl.program_id(0) == 0)
    def _():
        w = wr_ref[...]
        w_hi = w.astype(BF)
        w2_ref[:, :LANES] = w_hi
        w2_ref[:, LANES:] = (w - w_hi.astype(F32)).astype(BF)

    xn = _rms(x_ref[...], g_ref[...])
    xn_ref[...] = _pack_halves(xn)
    x_hi = xn.astype(BF)
    x_lo = (xn - x_hi.astype(F32)).astype(BF)
    p_hi = jnp.dot(x_hi, w2_ref[...], preferred_element_type=F32)
    p_lo = jnp.dot(x_lo, w2_ref[...], preferred_element_type=F32)
    logits = (p_hi[:, :LANES] + p_hi[:, LANES:]) + (p_lo[:, :LANES] + p_lo[:, LANES:]) + br_ref[...]
    lane = lax.broadcasted_iota(jnp.int32, (tr, LANES), 1).astype(F32)
    neg = -jnp.inf

    def first_argmax(vals):
        top = jnp.max(vals, axis=-1, keepdims=True)
        return top, jnp.min(jnp.where(vals == top, lane, float(LANES)), axis=-1, keepdims=True)

    gl = jnp.where(lane < N_GROUPS, logits, neg)
    gmax, grp = first_argmax(gl)
    g_prob = 1.0 / jnp.sum(jnp.exp(gl - gmax), axis=-1, keepdims=True)
    lo = N_GROUPS + EXPERTS_PER_GROUP * grp
    el = jnp.where(lane >= lo, jnp.where(lane < lo + EXPERTS_PER_GROUP, logits, neg), neg)
    v1, l1 = first_argmax(el)
    v2, l2 = first_argmax(jnp.where(lane == l1, neg, el))
    e2 = jnp.exp(v2 - v1)
    w1 = g_prob / (1.0 + e2)
    w2 = g_prob * e2 / (1.0 + e2)
    hot1 = lane == l1
    hot2 = lane == l2
    hot = jnp.where(hot1, 1.0, jnp.where(hot2, 1.0, 0.0))
    earlier = (lax.broadcasted_iota(jnp.int32, (tr, tr), 1) < lax.broadcasted_iota(jnp.int32, (tr, tr), 0))
    before = jnp.dot(jnp.where(earlier, 1.0, 0.0).astype(BF), hot.astype(BF), preferred_element_type=F32)
    before = before + carry_ref[0:1, :]
    r1 = jnp.sum(jnp.where(hot1, before, 0.0), axis=-1, keepdims=True)
    r2 = jnp.sum(jnp.where(hot2, before, 0.0), axis=-1, keepdims=True)
    carry_ref[0:1, :] = carry_ref[0:1, :] + jnp.sum(hot, axis=0, keepdims=True)
    ids = jnp.where(lane == 0, l1 - N_GROUPS, jnp.where(lane == 1, l2 - N_GROUPS,
                    jnp.where(lane == 2, r1, jnp.where(lane == 3, r2, 0.0))))
    ids_ref[...] = ids.astype(jnp.int32)
    wts_ref[...] = jnp.where(lane == 0, w1, jnp.where(lane == 1, w2, 0.0))
    cnt_ref[...] = carry_ref[...]


def route(h, gain, w_router, b_router):
    n, d = h.shape
    tr = min(ROUTER_ROWS, n)
    return pl.pallas_call(
        functools.partial(_router_kernel, tr=tr), grid=(n // tr,),
        in_specs=[pl.BlockSpec((tr, d), lambda i: (i, 0)), pl.BlockSpec((1, d), lambda i: (0, 0)),
                  pl.BlockSpec((d, LANES), lambda i: (0, 0)), pl.BlockSpec((1, LANES), lambda i: (0, 0))],
        out_specs=[pl.BlockSpec((tr, d // 2), lambda i: (i, 0)),
                   pl.BlockSpec((tr, LANES), lambda i: (i, 0)), pl.BlockSpec((tr, LANES), lambda i: (i, 0)),
                   pl.BlockSpec((SUBLANES, LANES), lambda i: (0, 0))],
        out_shape=[jax.ShapeDtypeStruct((n, d // 2), jnp.uint32), jax.ShapeDtypeStruct((n, LANES), jnp.int32),
                   jax.ShapeDtypeStruct((n, LANES), F32), jax.ShapeDtypeStruct((SUBLANES, LANES), F32)],
        scratch_shapes=[pltpu.VMEM((SUBLANES, LANES), F32), pltpu.VMEM((d, 2 * LANES), BF)],
        compiler_params=_cparams(("arbitrary",), 40), name="moe_router",
    )(h, gain.reshape(1, d), w_router, b_router)


def _dest_kernel(ids_ref, start_ref, o_ref, *, tr):
    ids = ids_ref[...].astype(F32)
    lane = lax.broadcasted_iota(jnp.int32, (tr, LANES), 1).astype(F32)
    start = start_ref[...]
    rows = []
    for k in range(TOP_K):
        first = jnp.sum(jnp.where(lane == ids[:, k:k + 1], start, 0.0), axis=-1, keepdims=True)
        rows.append(first + ids[:, TOP_K + k:TOP_K + k + 1])
    packed = jnp.where(lane == 0, rows[0], jnp.where(lane == 1, rows[1], 0.0))
    o_ref[...] = packed.T[0:SUBLANES, :].astype(jnp.int32)


def assignment_rows(ids, start_rows):
    n = ids.shape[0]
    tr = min(ROUTER_ROWS, n)
    out = pl.pallas_call(
        functools.partial(_dest_kernel, tr=tr), grid=(n // tr,),
        in_specs=[pl.BlockSpec((tr, LANES), lambda i: (i, 0)), pl.BlockSpec((1, LANES), lambda i: (0, 0))],
        out_specs=pl.BlockSpec((SUBLANES, tr), lambda i: (0, i)),
        out_shape=jax.ShapeDtypeStruct((SUBLANES, n), jnp.int32),
        compiler_params=_cparams(("parallel",), 32), name="moe_assignment_rows",
    )(ids, start_rows)
    return out[:TOP_K]


def _row_copy(src_hbm, row, dst, dst_row, sem):
    return pltpu.make_async_copy(src_hbm.at[pl.ds(row, 1), :], dst.at[pl.ds(dst_row, 1), :], sem)


def _rows_wait(src_hbm, dst, sem):
    pltpu.make_async_copy(src_hbm.at[pl.ds(0, dst.shape[0]), :], dst, sem).wait()


def _expert_weight_copies(w_hbm, layer, expert, stage, wset, sem):
    rows = stage.shape[1] // WEIGHT_DMA_PARTS
    return [pltpu.make_async_copy(w_hbm.at[layer, expert, pl.ds(p * rows, rows), :],
                                  stage.at[wset, pl.ds(p * rows, rows), :], sem.at[wset])
            for p in range(WEIGHT_DMA_PARTS)]


def _stream_expert_weights(b, nact, blk_ref, kin_ref, set_ref, nxt_ref, copies):
    cur = set_ref[b]

    @pl.when(b == 0)
    def _():
        for cp in copies(blk_ref[0], cur):
            cp.start(priority=DMA_QUEUE_BULK)

    @pl.when(jnp.logical_and(b < nact, kin_ref[b] == 0))
    def _():
        for cp in copies(blk_ref[b], cur):
            cp.wait()

        @pl.when(nxt_ref[b] != blk_ref[b])
        def _():
            for cp in copies(nxt_ref[b], 1 - cur):
                cp.start(priority=DMA_QUEUE_BULK)

    return cur


def _moe_expert_kernel(blk_ref, kin_ref, last_ref, set_ref, nxt_ref, nact_ref, rtok_ref,
                       xn_hbm, wg_hbm, wu_hbm, wd_hbm, ys_ref,
                       buf, stage_g, stage_u, stage_d, xb, hid, sem, wsem, dsem, *, tb, layer):
    b = pl.program_id(0)
    nact = nact_ref[0]
    nslots = GATHER_AHEAD + 1
    slot = b % nslots
    f = hid.shape[1]
    half = ys_ref.shape[1]
    up_chunks = f // MOE_COL_CHUNK
    rows_per_chunk = tb // up_chunks
    active = b < nact
    first_of_expert = jnp.logical_and(active, kin_ref[b] == 0)
    has_next = nxt_ref[b] != blk_ref[b]

    def up_copies(e, wset):
        return (_expert_weight_copies(wg_hbm, layer, e, stage_g, wset, wsem.at[0])
                + _expert_weight_copies(wu_hbm, layer, e, stage_u, wset, wsem.at[1]))

    def down_copies(e):
        rows = stage_d.shape[0] // WEIGHT_DMA_PARTS
        return [pltpu.make_async_copy(wd_hbm.at[layer, e, pl.ds(p * rows, rows), :],
                                      stage_d.at[pl.ds(p * rows, rows), :], dsem.at[0])
                for p in range(WEIGHT_DMA_PARTS)]

    @pl.when(b == 0)
    def _():
        for ahead in range(GATHER_AHEAD):
            def body(r, carry, ahead=ahead):
                _row_copy(xn_hbm, rtok_ref[ahead * tb + r], buf.at[ahead], r,
                          sem.at[ahead]).start(priority=DMA_QUEUE_ROWS)
                return carry

            lax.fori_loop(0, tb, body, 0, unroll=8)
        for cp in down_copies(blk_ref[0]):
            cp.start(priority=DMA_QUEUE_BULK)

    cur = _stream_expert_weights(b, nact, blk_ref, kin_ref, set_ref, nxt_ref, up_copies)

    @pl.when(active)
    def _():
        _rows_wait(xn_hbm, buf.at[slot], sem.at[slot])
        x_lo, x_hi = _unpack_halves(buf[slot])
        xb[:, :x_lo.shape[1]] = x_lo.astype(BF)
        xb[:, x_lo.shape[1]:] = x_hi.astype(BF)
        ahead_slot = (b + GATHER_AHEAD) % nslots
        for c in range(up_chunks):
            for r in range(c * rows_per_chunk, (c + 1) * rows_per_chunk):
                _row_copy(xn_hbm, rtok_ref[(b + GATHER_AHEAD) * tb + r], buf.at[ahead_slot], r,
                          sem.at[ahead_slot]).start(priority=DMA_QUEUE_ROWS)
            cs = slice(c * MOE_COL_CHUNK, (c + 1) * MOE_COL_CHUNK)
            gate = jnp.dot(xb[...], stage_g[cur, :, cs].astype(BF), preferred_element_type=F32)
            up = jnp.dot(xb[...], stage_u[cur, :, cs].astype(BF), preferred_element_type=F32)
            hid[:, cs] = (gate * _sigmoid(gate) * up).astype(hid.dtype)

    @pl.when(first_of_expert)
    def _():
        for cp in down_copies(blk_ref[b]):
            cp.wait()

    @pl.when(active)
    def _():
        for c in range(half // MOE_COL_CHUNK):
            cs = slice(c * MOE_COL_CHUNK, (c + 1) * MOE_COL_CHUNK)
            hs = slice(half + c * MOE_COL_CHUNK, half + (c + 1) * MOE_COL_CHUNK)
            lo = jnp.dot(hid[...], stage_d[:, cs].astype(BF), preferred_element_type=F32)
            hi = jnp.dot(hid[...], stage_d[:, hs].astype(BF), preferred_element_type=F32)
            ys_ref[:, cs] = _pack_halves(jnp.concatenate([lo, hi], axis=1))

    @pl.when(jnp.logical_and(jnp.logical_and(active, last_ref[b] == 1), has_next))
    def _():
        for cp in down_copies(nxt_ref[b]):
            cp.start(priority=DMA_QUEUE_BULK)

    @pl.when(b >= nact)
    def _():
        ys_ref[...] = jnp.zeros_like(ys_ref)

    @pl.when(jnp.logical_and(b >= nact, b < nact + GATHER_AHEAD))
    def _():
        _rows_wait(xn_hbm, buf.at[slot], sem.at[slot])


def _combine_kernel(dest_ref, ys_hbm, h_ref, w_ref, gain_ref, o_ref, buf, sem, *, tc, n, final_norm):
    i = pl.program_id(0)

    def row_copy(blk, slot, r, k):
        return _row_copy(ys_hbm, dest_ref[k * n + blk * tc + r], buf.at[slot], k * tc + r, sem.at[slot])

    @pl.when(i == 0)
    def _():
        def body(r, carry):
            for k in range(TOP_K):
                row_copy(0, 0, r, k).start(priority=k)
            return carry

        lax.fori_loop(0, tc, body, 0, unroll=8)

    @pl.when(i + 1 < pl.num_programs(0))
    def _():
        for r in range(tc):
            for k in range(TOP_K):
                row_copy(i + 1, (i + 1) % 2, r, k).start(priority=k)

    slot = i % 2
    _rows_wait(ys_hbm, buf.at[slot], sem.at[slot])
    a_lo, a_hi = _unpack_halves(buf[slot, 0:tc, :])
    b_lo, b_hi = _unpack_halves(buf[slot, tc:TOP_K * tc, :])
    w0 = w_ref[:, 0:1]
    w1 = w_ref[:, 1:2]
    out = h_ref[...] + jnp.concatenate([w0 * a_lo + w1 * b_lo, w0 * a_hi + w1 * b_hi], axis=1)
    if final_norm:
        out = _rms(out, gain_ref[...])
    o_ref[...] = out


def hier_moe(h, ffn_gain, wg_r, bg_r, we_r, be_r, w_gate, w_up, w_down, layer, final_gain=None):
    n, d = h.shape
    f = w_gate.shape[-1]
    a = n * TOP_K
    tb = _moe_block_rows(a)
    nb = a // tb + N_EXPERTS + GATHER_AHEAD

    pad = LANES - N_GROUPS - N_EXPERTS
    w_router = jnp.concatenate([wg_r, we_r, jnp.zeros((d, pad), F32)], axis=1)
    b_router = jnp.concatenate([bg_r, be_r, jnp.zeros((pad,), F32)]).reshape(1, LANES)
    xn, ids, wts, cnt = route(h, ffn_gain, w_router, b_router)

    counts = cnt[0, N_GROUPS:N_GROUPS + N_EXPERTS].astype(jnp.int32)
    nblk = (counts + tb - 1) // tb
    bend = jnp.cumsum(nblk)
    nact = bend[-1]
    start_rows = jnp.pad(((bend - nblk) * tb).astype(F32), (0, LANES - N_EXPERTS)).reshape(1, LANES)
    dest = assignment_rows(ids, start_rows).reshape(a)
    blk = jnp.minimum(jnp.arange(nb, dtype=jnp.int32), jnp.maximum(nact - 1, 0))
    owner = lambda bi: jnp.minimum(jnp.sum((bend[None, :] <= bi[:, None]).astype(jnp.int32), axis=1), N_EXPERTS - 1)
    blk_e = owner(blk)
    k_in_e = blk - (bend - nblk)[blk_e]
    following = lambda e: jnp.where(bend[e] < nact, owner(bend[e]), e)
    nxt_e = following(blk_e)
    wset = (jnp.cumsum((k_in_e == 0).astype(jnp.int32)) - 1) % 2
    tok = jnp.tile(jnp.arange(n, dtype=jnp.int32), TOP_K)
    row_tok = (jnp.arange(nb * tb, dtype=jnp.int32) % n).at[dest].set(tok)
    nact1 = nact.reshape(1).astype(jnp.int32)

    any_space = pl.BlockSpec(memory_space=pl.ANY)
    is_last = (k_in_e == nblk[blk_e] - 1).astype(jnp.int32)
    ys = pl.pallas_call(
        functools.partial(_moe_expert_kernel, tb=tb, layer=layer),
        grid_spec=pltpu.PrefetchScalarGridSpec(
            num_scalar_prefetch=7, grid=(nb,),
            in_specs=[any_space, any_space, any_space, any_space],
            out_specs=pl.BlockSpec((tb, d // 2), lambda b, *_: (b, 0)),
            scratch_shapes=[pltpu.VMEM((GATHER_AHEAD + 1, tb, d // 2), jnp.uint32),
                            pltpu.VMEM((2, d, f), F32), pltpu.VMEM((2, d, f), F32), pltpu.VMEM((f, d), F32),
                            pltpu.VMEM((tb, d), BF), pltpu.VMEM((tb, f), BF),
                            pltpu.SemaphoreType.DMA((GATHER_AHEAD + 1,)), pltpu.SemaphoreType.DMA((2, 2)),
                            pltpu.SemaphoreType.DMA((1,))]),
        out_shape=jax.ShapeDtypeStruct((nb * tb, d // 2), jnp.uint32),
        compiler_params=_cparams(("arbitrary",), 56), name="moe_experts",
    )(blk_e, k_in_e, is_last, wset, nxt_e, nact1, row_tok, xn, w_gate, w_up, w_down)

    tc = min(COMBINE_ROWS, n)
    gain = (final_gain if final_gain is not None else ffn_gain).reshape(1, d)
    return pl.pallas_call(
        functools.partial(_combine_kernel, tc=tc, n=n, final_norm=final_gain is not None),
        grid_spec=pltpu.PrefetchScalarGridSpec(
            num_scalar_prefetch=1, grid=(n // tc,),
            in_specs=[pl.BlockSpec(memory_space=pl.ANY), pl.BlockSpec((tc, d), lambda i, *_: (i, 0)),
                      pl.BlockSpec((tc, LANES), lambda i, *_: (i, 0)), pl.BlockSpec((1, d), lambda i, *_: (0, 0))],
            out_specs=pl.BlockSpec((tc, d), lambda i, *_: (i, 0)),
            scratch_shapes=[pltpu.VMEM((2, TOP_K * tc, d // 2), jnp.uint32), pltpu.SemaphoreType.DMA((2,))]),
        out_shape=jax.ShapeDtypeStruct((n, d), F32),
        compiler_params=_cparams(("arbitrary",), 40), name="moe_combine",
    )(dest, ys, h, wts, gain)


def even_mixer(h, gain, w_in, b_gates, pool_w, pool_scale, head_gain, w_out, j):
    d = h.shape[1]
    pool_width = N_POOL_GROUPS * pool_w.shape[-1]
    mlstm_width = head_gain.shape[0]
    main_cols = pool_width + 4 * mlstm_width
    n_gates = 2 * MLSTM_HEADS
    gate_b = jnp.pad(b_gates, (0, LANES - n_gates)).reshape(1, LANES)
    z, gates = norm_matmul_gates_t(h, gain, jnp.swapaxes(w_in, 1, 2), main_cols, n_gates, gate_b, j)
    y_p = pool_mixer(z, pool_w, pool_scale)
    y_m = mlstm_mixer(z, gates, head_gain, pool_width)
    assert pool_width == mlstm_width and pool_width + mlstm_width == d
    return matmul_residual([y_p, y_m], w_out, h, layer=j)


def odd_mixer(h, gain, w_in, conv_w, w_out, j):
    z = norm_matmul(h, gain, w_in, w_in.shape[2], layer=j)
    return conv_matmul_residual(z, conv_w, w_out, h, j)


def cross_attn(h, mem, gain, mem_gain, wq, wk, wv, wo, layer):
    d = h.shape[1]
    k = norm_matmul(mem, mem_gain, wk, d, layer=layer)
    v = norm_matmul(mem, mem_gain, wv, d, layer=layer)
    return matmul_residual([q_attention(h, gain, wq, k, v, layer)], wo, h, layer=layer)


def kernel(x, mem, mix_norm, xattn_norm, mem_norm, ffn_norm, final_norm, ev_w_in, ev_b_gates, ev_pool_w, ev_pool_scale, ev_head_norm, ev_w_out, od_w_in, od_conv_w, od_w_out, xa_wq, xa_wk, xa_wv, xa_wo, rt_group_w, rt_group_b, rt_expert_w, rt_expert_b, ex_w_gate, ex_w_up, ex_w_down):
    depth = mix_norm.shape[0]
    h = x[0]
    m = mem[0]
    for layer in range(depth):
        j = layer // 2
        if layer % 2 == 0:
            h = even_mixer(h, mix_norm[layer], ev_w_in, ev_b_gates[j], ev_pool_w[j], ev_pool_scale[j],
                           ev_head_norm[j], ev_w_out, j)
        else:
            h = odd_mixer(h, mix_norm[layer], od_w_in, od_conv_w[j], od_w_out, j)
        h = cross_attn(h, m, xattn_norm[layer], mem_norm[layer], xa_wq, xa_wk, xa_wv, xa_wo, layer)
        h = hier_moe(h, ffn_norm[layer], rt_group_w[layer], rt_group_b[layer], rt_expert_w[layer],
                     rt_expert_b[layer], ex_w_gate, ex_w_up, ex_w_down, layer,
                     final_gain=final_norm if layer == depth - 1 else None)
    return h[None]
```

```python
import functools

import jax
import jax.numpy as jnp
from jax import lax
from jax.experimental import pallas as pl
from jax.experimental.pallas import tpu as pltpu

F32 = jnp.float32
BF = jnp.bfloat16
EPS = 1e-6

POOL_WINDOWS = (2, 4, 8, 16)
N_POOL_GROUPS = 4
MLSTM_HEADS = 4
FORGET_LANE0 = MLSTM_HEADS
XATTN_HEADS = 4
N_GROUPS = 4
EXPERTS_PER_GROUP = 8
N_EXPERTS = N_GROUPS * EXPERTS_PER_GROUP
TOP_K = 2
CONV_WIDTH = 3

LANES = 128
SUBLANES = 8
DMA_QUEUE_ROWS = 0
DMA_QUEUE_BULK = 1

ROW_TILE = 1024
COL_TILE = 1024
OUT_ROW_TILE = 512
MOE_COL_CHUNK = 256
WEIGHT_DMA_PARTS = 4
MLSTM_CHUNK = 256
POOL_ROWS = 256
POOL_HALO = 128
CONV_ROWS = 512
CONV_COLS = 512
CONV_HALO = 16
ROUTER_ROWS = 512
GATHER_AHEAD = 3
COMBINE_ROWS = 256


def _moe_block_rows(assignments):
    mean_rows = assignments // N_EXPERTS
    return -(-(mean_rows * 9 // 16) // 16) * 16


def _cparams(semantics, vmem_mib):
    return pltpu.CompilerParams(dimension_semantics=semantics, vmem_limit_bytes=vmem_mib * 1024 * 1024)


def _sigmoid(x):
    return 1.0 / (1.0 + jnp.exp(-x))


def _log_sigmoid(x):
    return jnp.minimum(x, 0.0) - jnp.log(1.0 + jnp.exp(-jnp.abs(x)))


def _rms(x, g):
    ms = jnp.mean(x * x, axis=-1, keepdims=True)
    return x * lax.rsqrt(ms + EPS) * g


def _pack_halves(x):
    half = x.shape[1] // 2
    lo = pltpu.bitcast(x[:, :half].astype(BF).astype(F32), jnp.uint32)
    hi = pltpu.bitcast(x[:, half:].astype(BF).astype(F32), jnp.uint32)
    return hi | lax.shift_right_logical(lo, jnp.uint32(16))


def _unpack_halves(w):
    lo = pltpu.bitcast(lax.shift_left(w, jnp.uint32(16)), F32)
    hi = pltpu.bitcast(w & jnp.uint32(0xFFFF0000), F32)
    return lo, hi


def _norm_mm_kernel(x_ref, g_ref, w_ref, o_ref, xn_ref):
    @pl.when(pl.program_id(1) == 0)
    def _():
        xn_ref[...] = _rms(x_ref[...], g_ref[...]).astype(BF)

    o_ref[...] = jnp.dot(xn_ref[...], w_ref[...].astype(BF), preferred_element_type=F32).astype(o_ref.dtype)


_NT = (((1,), (1,)), ((), ()))


def _norm_mm_gates_t_kernel(x_ref, g_ref, wt_ref, wgt_ref, bg_ref, o_ref, gates_ref, xn_ref):
    @pl.when(pl.program_id(1) == 0)
    def _():
        xn = _rms(x_ref[...], g_ref[...]).astype(BF)
        xn_ref[...] = xn
        wg = wgt_ref[...]
        wg = jnp.concatenate([wg, jnp.zeros((LANES - wg.shape[0], wg.shape[1]), F32)], axis=0).astype(BF)
        gates_ref[...] = lax.dot_general(xn, wg, _NT, preferred_element_type=F32) + bg_ref[...]

    o_ref[...] = lax.dot_general(xn_ref[...], wt_ref[...].astype(BF), _NT,
                                 preferred_element_type=F32).astype(o_ref.dtype)


def norm_matmul_gates_t(x, gain, wt, n_cols, n_gates, gate_b, layer):
    n, k = x.shape
    tm = min(ROW_TILE, n)
    tn = COL_TILE
    return pl.pallas_call(
        _norm_mm_gates_t_kernel, grid=(n // tm, n_cols // tn),
        in_specs=[pl.BlockSpec((tm, k), lambda i, j: (i, 0)), pl.BlockSpec((1, k), lambda i, j: (0, 0)),
                  pl.BlockSpec((None, tn, k), lambda i, j: (layer, j, 0)),
                  pl.BlockSpec((None, n_gates, k), lambda i, j: (layer, n_cols // n_gates, 0)),
                  pl.BlockSpec((1, LANES), lambda i, j: (0, 0))],
        out_specs=[pl.BlockSpec((tm, tn), lambda i, j: (i, j)), pl.BlockSpec((tm, LANES), lambda i, j: (i, 0))],
        out_shape=[jax.ShapeDtypeStruct((n, n_cols), BF), jax.ShapeDtypeStruct((n, LANES), F32)],
        scratch_shapes=[pltpu.VMEM((tm, k), BF)],
        compiler_params=_cparams(("parallel", "arbitrary"), 56), name="norm_matmul_gates",
    )(x, gain.reshape(1, k), wt, wt, gate_b)


def _stacked(w, layer):
    return (w[None], 0) if layer is None else (w, layer)


def norm_matmul(x, gain, w, n_cols, layer=None):
    n, k = x.shape
    w, li = _stacked(w, layer)
    tm = min(ROW_TILE, n)
    tn = COL_TILE
    return pl.pallas_call(
        _norm_mm_kernel, grid=(n // tm, n_cols // tn),
        in_specs=[pl.BlockSpec((tm, k), lambda i, j: (i, 0)), pl.BlockSpec((1, k), lambda i, j: (0, 0)),
                  pl.BlockSpec((None, k, tn), lambda i, j: (li, 0, j))],
        out_specs=pl.BlockSpec((tm, tn), lambda i, j: (i, j)),
        out_shape=jax.ShapeDtypeStruct((n, n_cols), BF), scratch_shapes=[pltpu.VMEM((tm, k), BF)],
        compiler_params=_cparams(("parallel", "arbitrary"), 56), name="norm_matmul",
    )(x, gain.reshape(1, k), w)


def _q_attention_kernel(x_ref, g_ref, w_ref, k_ref, v_ref, o_ref, q_ref, *, hd):
    xn = _rms(x_ref[...], g_ref[...]).astype(BF)
    q_ref[...] = jnp.dot(xn, w_ref[...].astype(BF), preferred_element_type=F32).astype(BF)
    scale = hd ** -0.5
    for h in range(XATTN_HEADS):
        hs = slice(h * hd, (h + 1) * hd)
        s = lax.dot_general(q_ref[:, hs], k_ref[:, hs], _NT, preferred_element_type=F32) * scale
        e = jnp.exp(s - jnp.max(s, axis=-1, keepdims=True))
        p = (e / jnp.sum(e, axis=-1, keepdims=True)).astype(BF)
        o_ref[:, hs] = jnp.dot(p, v_ref[:, hs], preferred_element_type=F32).astype(o_ref.dtype)


def q_attention(x, gain, wq, k, v, layer):
    n, kd = x.shape
    d = wq.shape[2]
    m = k.shape[0]
    tm = min(OUT_ROW_TILE, n)
    return pl.pallas_call(
        functools.partial(_q_attention_kernel, hd=d // XATTN_HEADS), grid=(n // tm,),
        in_specs=[pl.BlockSpec((tm, kd), lambda i: (i, 0)), pl.BlockSpec((1, kd), lambda i: (0, 0)),
                  pl.BlockSpec((None, kd, d), lambda i: (layer, 0, 0), pipeline_mode=pl.Buffered(1)),
                  pl.BlockSpec((m, d), lambda i: (0, 0)), pl.BlockSpec((m, d), lambda i: (0, 0))],
        out_specs=pl.BlockSpec((tm, d), lambda i: (i, 0)), out_shape=jax.ShapeDtypeStruct((n, d), BF),
        scratch_shapes=[pltpu.VMEM((tm, d), BF)],
        compiler_params=_cparams(("parallel",), 52), name="q_attention",
    )(x, gain.reshape(1, kd), wq, k, v)


def _mm_res_kernel(*refs, nparts):
    xs = refs[:nparts]
    w_ref, res_ref, o_ref = refs[nparts:]
    acc = res_ref[...]
    k0 = 0
    for x_ref in xs:
        kp = x_ref.shape[1]
        acc = acc + jnp.dot(x_ref[...], w_ref[k0:k0 + kp, :].astype(BF), preferred_element_type=F32)
        k0 += kp
    o_ref[...] = acc


def matmul_residual(xs, w, res, layer=None):
    n, d = res.shape
    w, li = _stacked(w, layer)
    k = w.shape[1]
    tm = min(OUT_ROW_TILE, n)
    in_specs = [pl.BlockSpec((tm, x.shape[1]), lambda i: (i, 0)) for x in xs]
    in_specs += [pl.BlockSpec((None, k, d), lambda i: (li, 0, 0), pipeline_mode=pl.Buffered(1)),
                 pl.BlockSpec((tm, d), lambda i: (i, 0))]
    return pl.pallas_call(
        functools.partial(_mm_res_kernel, nparts=len(xs)), grid=(n // tm,), in_specs=in_specs,
        out_specs=pl.BlockSpec((tm, d), lambda i: (i, 0)), out_shape=jax.ShapeDtypeStruct((n, d), F32),
        compiler_params=_cparams(("parallel",), 52), name="matmul_residual",
    )(*xs, w, res)


def _pool_kernel(cur_ref, prev_ref, w_ref, sc_ref, o_ref, *, tp, gdim):
    i = pl.program_id(0)
    dist = lax.broadcasted_iota(jnp.int32, (tp, tp), 0) - lax.broadcasted_iota(jnp.int32, (tp, tp), 1)
    distp = (lax.broadcasted_iota(jnp.int32, (tp, POOL_HALO), 0) + POOL_HALO
             - lax.broadcasted_iota(jnp.int32, (tp, POOL_HALO), 1))
    pos = i * tp + lax.broadcasted_iota(jnp.int32, (tp, 1), 0)
    for j, win in enumerate(POOL_WINDOWS):
        gs = slice(j * gdim, (j + 1) * gdim)
        cur = cur_ref[:, gs]
        band = jnp.where(dist >= 0, jnp.where(dist < win, 1.0, 0.0), 0.0).astype(BF)
        bandp = jnp.where(distp < jnp.where(i > 0, win, 0), 1.0, 0.0).astype(BF)
        s = (jnp.dot(band, cur, preferred_element_type=F32)
             + jnp.dot(bandp, prev_ref[:, gs], preferred_element_type=F32))
        cnt = jnp.minimum(pos + 1, win).astype(F32)
        d = s / cnt - cur.astype(F32)
        y = jnp.dot(d.astype(BF), w_ref[j].astype(BF), preferred_element_type=F32) * sc_ref[:, gs]
        o_ref[:, gs] = y.astype(o_ref.dtype)


def pool_mixer(z, pool_w, pool_scale):
    n = z.shape[0]
    gdim = pool_w.shape[-1]
    width = N_POOL_GROUPS * gdim
    tp = min(POOL_ROWS, n)
    halo_blocks = tp // POOL_HALO
    return pl.pallas_call(
        functools.partial(_pool_kernel, tp=tp, gdim=gdim), grid=(n // tp,),
        in_specs=[
            pl.BlockSpec((tp, width), lambda i: (i, 0)),
            pl.BlockSpec((POOL_HALO, width), lambda i: (jnp.maximum(i * halo_blocks - 1, 0), 0)),
            pl.BlockSpec((N_POOL_GROUPS, gdim, gdim), lambda i: (0, 0, 0)),
            pl.BlockSpec((1, width), lambda i: (0, 0)),
        ],
        out_specs=pl.BlockSpec((tp, width), lambda i: (i, 0)),
        out_shape=jax.ShapeDtypeStruct((n, width), BF),
        compiler_params=_cparams(("parallel",), 32), name="pool_mixer",
    )(z, z, pool_w, pool_scale.reshape(1, width))


def _mlstm_kernel(q_ref, k_ref, v_ref, o_ref, g_ref, gain_ref, y_ref, ct_ref, m_ref, *, chunk, dh):
    c = pl.program_id(0)

    @pl.when(c == 0)
    def _():
        ct_ref[...] = jnp.zeros_like(ct_ref)
        m_ref[...] = jnp.zeros_like(m_ref)

    g = g_ref[...]
    lf = _log_sigmoid(g)
    row = lax.broadcasted_iota(jnp.int32, (chunk, chunk), 0)
    col = lax.broadcasted_iota(jnp.int32, (chunk, chunk), 1)
    causal = col <= row
    ltri = jnp.where(causal, 1.0, 0.0).astype(BF)
    hi = lf.astype(BF)
    r1 = lf - hi.astype(F32)
    mid = r1.astype(BF)
    lo = (r1 - mid.astype(F32)).astype(BF)
    bcum = (jnp.dot(ltri, hi, preferred_element_type=F32) + jnp.dot(ltri, mid, preferred_element_type=F32)
            + jnp.dot(ltri, lo, preferred_element_type=F32))
    g_t = g.T
    b_t = bcum.T
    ones_col = jnp.where(lax.broadcasted_iota(jnp.int32, (chunk, LANES), 1) == 0, 1.0, 0.0).astype(BF)

    for h in range(MLSTM_HEADS):
        hs = slice(h * dh, (h + 1) * dh)
        fl = FORGET_LANE0 + h
        bc = bcum[:, fl:fl + 1]
        br = b_t[fl:fl + 1, :]
        ir = g_t[h:h + 1, :]
        b_last = bcum[chunk - 1:chunk, fl:fl + 1]
        m_prev = m_ref[h][:, 0:1]

        dmat = jnp.where(causal, bc + (ir - br), -jnp.inf)
        inter = bc + m_prev
        m_t = jnp.maximum(jnp.max(dmat, axis=1, keepdims=True), inter)
        w_inter = jnp.exp(inter - m_t)
        p = jnp.exp(dmat - m_t)

        qh = q_ref[:, hs] * (dh ** -0.5)
        kh = k_ref[:, hs]
        v_aug = jnp.concatenate([v_ref[:, hs], ones_col], axis=1)
        s = lax.dot_general(qh, kh, (((1,), (1,)), ((), ())), preferred_element_type=F32)
        sc = (s * p).astype(BF)
        ct = ct_ref[h]
        num_aug = (w_inter * jnp.dot(qh, ct.astype(BF), preferred_element_type=F32)
                   + jnp.dot(sc, v_aug, preferred_element_type=F32))
        num = num_aug[:, :dh]
        den = num_aug[:, dh:dh + 1]
        hout = num / jnp.maximum(jnp.abs(den), jnp.exp(-m_t))

        yn = _rms(hout, gain_ref[:, hs])
        y_ref[:, hs] = (_sigmoid(o_ref[:, hs].astype(F32)) * yn).astype(y_ref.dtype)

        d_end = b_last - br + ir
        m_new = jnp.maximum(b_last + m_prev, jnp.max(d_end, axis=1, keepdims=True))
        a_prev = jnp.exp(b_last + m_prev - m_new)
        a_s = jnp.exp(d_end - m_new)
        k_t = (kh.astype(F32).T * a_s).astype(BF)
        ct_ref[h] = a_prev * ct + jnp.dot(k_t, v_aug, preferred_element_type=F32)
        m_ref[h] = jnp.broadcast_to(m_new, (1, LANES))


def mlstm_mixer(z, gates, head_gain, col0):
    n = z.shape[0]
    width = head_gain.shape[0]
    dh = width // MLSTM_HEADS
    chunk = min(MLSTM_CHUNK, n)
    base = col0 // width
    qkvo = [pl.BlockSpec((chunk, width), lambda c, p=p: (c, base + p)) for p in range(4)]
    return pl.pallas_call(
        functools.partial(_mlstm_kernel, chunk=chunk, dh=dh), grid=(n // chunk,),
        in_specs=qkvo + [pl.BlockSpec((chunk, LANES), lambda c: (c, 0)), pl.BlockSpec((1, width), lambda c: (0, 0))],
        out_specs=pl.BlockSpec((chunk, width), lambda c: (c, 0)),
        out_shape=jax.ShapeDtypeStruct((n, width), BF),
        scratch_shapes=[pltpu.VMEM((MLSTM_HEADS, dh, dh + LANES), F32), pltpu.VMEM((MLSTM_HEADS, 1, LANES), F32)],
        compiler_params=_cparams(("arbitrary",), 32), name="mlstm_mixer",
    )(z, z, z, z, gates, head_gain.reshape(1, width))


def _gated_conv(b, c, u, cp, up, w, first_block):
    zc = c.astype(F32) * u.astype(F32)
    zp = jnp.where(first_block, 0.0, cp.astype(F32) * up.astype(F32))
    row = lax.broadcasted_iota(jnp.int32, zc.shape, 0)
    acc = w[CONV_WIDTH - 1:CONV_WIDTH, :] * zc
    for back in range(1, CONV_WIDTH):
        shifted = pltpu.roll(zc, back, 0)
        for r in range(back):
            shifted = jnp.where(row == r, zp[CONV_HALO - back + r:CONV_HALO - back + r + 1, :], shifted)
        acc = acc + w[CONV_WIDTH - 1 - back:CONV_WIDTH - back, :] * shifted
    return b.astype(F32) * acc


def _conv_mm_res_kernel(b_ref, c_ref, u_ref, cp_ref, up_ref, cw_ref, w_ref, res_ref, o_ref, xs_ref):
    first_block = pl.program_id(0) == 0
    for c0 in range(0, xs_ref.shape[1], CONV_COLS):
        cs = slice(c0, c0 + CONV_COLS)
        xs_ref[:, cs] = _gated_conv(b_ref[:, cs], c_ref[:, cs], u_ref[:, cs], cp_ref[:, cs], up_ref[:, cs],
                                    cw_ref[:, cs], first_block).astype(BF)
    o_ref[...] = res_ref[...] + jnp.dot(xs_ref[...], w_ref[...].astype(BF), preferred_element_type=F32)


def conv_matmul_residual(z, conv_w, w, res, layer):
    n, d = res.shape
    tm = min(CONV_ROWS, n)
    halo_blocks = tm // CONV_HALO
    cur = lambda part: pl.BlockSpec((tm, d), lambda i: (i, part))
    prev = lambda part: pl.BlockSpec((CONV_HALO, d), lambda i: (jnp.maximum(i * halo_blocks - 1, 0), part))
    return pl.pallas_call(
        _conv_mm_res_kernel, grid=(n // tm,),
        in_specs=[cur(0), cur(1), cur(2), prev(1), prev(2), pl.BlockSpec((CONV_WIDTH, d), lambda i: (0, 0)),
                  pl.BlockSpec((None, d, d), lambda i: (layer, 0, 0), pipeline_mode=pl.Buffered(1)),
                  pl.BlockSpec((tm, d), lambda i: (i, 0))],
        out_specs=pl.BlockSpec((tm, d), lambda i: (i, 0)), out_shape=jax.ShapeDtypeStruct((n, d), F32),
        scratch_shapes=[pltpu.VMEM((tm, d), BF)],
        compiler_params=_cparams(("parallel",), 52), name="conv_matmul_residual",
    )(z, z, z, z, z, conv_w, w, res)


def _router_kernel(x_ref, g_ref, wr_ref, br_ref, xn_ref, ids_ref, wts_ref, cnt_ref, carry_ref, w2_ref, *, tr):
    @pl.when(pl.program_id(0) == 0)
    def _():
        carry_ref[...] = jnp.zeros_like(carry_ref)

    @pl.when(pl.program_id(0) == 0)
    def _():
        w = wr_ref[...]
        w_hi = w.astype(BF)
        w2_ref[:, :LANES] = w_hi
        w2_ref[:, LANES:] = (w - w_hi.astype(F32)).astype(BF)

    xn = _rms(x_ref[...], g_ref[...])
    xn_ref[...] = _pack_halves(xn)
    x_hi = xn.astype(BF)
    x_lo = (xn - x_hi.astype(F32)).astype(BF)
    p_hi = jnp.dot(x_hi, w2_ref[...], preferred_element_type=F32)
    p_lo = jnp.dot(x_lo, w2_ref[...], preferred_element_type=F32)
    logits = (p_hi[:, :LANES] + p_hi[:, LANES:]) + (p_lo[:, :LANES] + p_lo[:, LANES:]) + br_ref[...]
    lane = lax.broadcasted_iota(jnp.int32, (tr, LANES), 1).astype(F32)
    neg = -jnp.inf

    def first_argmax(vals):
        top = jnp.max(vals, axis=-1, keepdims=True)
        return top, jnp.min(jnp.where(vals == top, lane, float(LANES)), axis=-1, keepdims=True)

    gl = jnp.where(lane < N_GROUPS, logits, neg)
    gmax, grp = first_argmax(gl)
    g_prob = 1.0 / jnp.sum(jnp.exp(gl - gmax), axis=-1, keepdims=True)
    lo = N_GROUPS + EXPERTS_PER_GROUP * grp
    el = jnp.where(lane >= lo, jnp.where(lane < lo + EXPERTS_PER_GROUP, logits, neg), neg)
    v1, l1 = first_argmax(el)
    v2, l2 = first_argmax(jnp.where(lane == l1, neg, el))
    e2 = jnp.exp(v2 - v1)
    w1 = g_prob / (1.0 + e2)
    w2 = g_prob * e2 / (1.0 + e2)
    hot1 = lane == l1
    hot2 = lane == l2
    hot = jnp.where(hot1, 1.0, jnp.where(hot2, 1.0, 0.0))
    earlier = (lax.broadcasted_iota(jnp.int32, (tr, tr), 1) < lax.broadcasted_iota(jnp.int32, (tr, tr), 0))
    before = jnp.dot(jnp.where(earlier, 1.0, 0.0).astype(BF), hot.astype(BF), preferred_element_type=F32)
    before = before + carry_ref[0:1, :]
    r1 = jnp.sum(jnp.where(hot1, before, 0.0), axis=-1, keepdims=True)
    r2 = jnp.sum(jnp.where(hot2, before, 0.0), axis=-1, keepdims=True)
    carry_ref[0:1, :] = carry_ref[0:1, :] + jnp.sum(hot, axis=0, keepdims=True)
    ids = jnp.where(lane == 0, l1 - N_GROUPS, jnp.where(lane == 1, l2 - N_GROUPS,
                    jnp.where(lane == 2, r1, jnp.where(lane == 3, r2, 0.0))))
    ids_ref[...] = ids.astype(jnp.int32)
    wts_ref[...] = jnp.where(lane == 0, w1, jnp.where(lane == 1, w2, 0.0))
    cnt_ref[...] = carry_ref[...]


def route(h, gain, w_router, b_router):
    n, d = h.shape
    tr = min(ROUTER_ROWS, n)
    return pl.pallas_call(
        functools.partial(_router_kernel, tr=tr), grid=(n // tr,),
        in_specs=[pl.BlockSpec((tr, d), lambda i: (i, 0)), pl.BlockSpec((1, d), lambda i: (0, 0)),
                  pl.BlockSpec((d, LANES), lambda i: (0, 0)), pl.BlockSpec((1, LANES), lambda i: (0, 0))],
        out_specs=[pl.BlockSpec((tr, d // 2), lambda i: (i, 0)),
                   pl.BlockSpec((tr, LANES), lambda i: (i, 0)), pl.BlockSpec((tr, LANES), lambda i: (i, 0)),
                   pl.BlockSpec((SUBLANES, LANES), lambda i: (0, 0))],
        out_shape=[jax.ShapeDtypeStruct((n, d // 2), jnp.uint32), jax.ShapeDtypeStruct((n, LANES), jnp.int32),
                   jax.ShapeDtypeStruct((n, LANES), F32), jax.ShapeDtypeStruct((SUBLANES, LANES), F32)],
        scratch_shapes=[pltpu.VMEM((SUBLANES, LANES), F32), pltpu.VMEM((d, 2 * LANES), BF)],
        compiler_params=_cparams(("arbitrary",), 40), name="moe_router",
    )(h, gain.reshape(1, d), w_router, b_router)


def _dest_kernel(ids_ref, start_ref, o_ref, *, tr):
    ids = ids_ref[...].astype(F32)
    lane = lax.broadcasted_iota(jnp.int32, (tr, LANES), 1).astype(F32)
    start = start_ref[...]
    rows = []
    for k in range(TOP_K):
        first = jnp.sum(jnp.where(lane == ids[:, k:k + 1], start, 0.0), axis=-1, keepdims=True)
        rows.append(first + ids[:, TOP_K + k:TOP_K + k + 1])
    packed = jnp.where(lane == 0, rows[0], jnp.where(lane == 1, rows[1], 0.0))
    o_ref[...] = packed.T[0:SUBLANES, :].astype(jnp.int32)


def assignment_rows(ids, start_rows):
    n = ids.shape[0]
    tr = min(ROUTER_ROWS, n)
    out = pl.pallas_call(
        functools.partial(_dest_kernel, tr=tr), grid=(n // tr,),
        in_specs=[pl.BlockSpec((tr, LANES), lambda i: (i, 0)), pl.BlockSpec((1, LANES), lambda i: (0, 0))],
        out_specs=pl.BlockSpec((SUBLANES, tr), lambda i: (0, i)),
        out_shape=jax.ShapeDtypeStruct((SUBLANES, n), jnp.int32),
        compiler_params=_cparams(("parallel",), 32), name="moe_assignment_rows",
    )(ids, start_rows)
    return out[:TOP_K]


def _row_copy(src_hbm, row, dst, dst_row, sem):
    return pltpu.make_async_copy(src_hbm.at[pl.ds(row, 1), :], dst.at[pl.ds(dst_row, 1), :], sem)


def _rows_wait(src_hbm, dst, sem):
    pltpu.make_async_copy(src_hbm.at[pl.ds(0, dst.shape[0]), :], dst, sem).wait()


def _expert_weight_copies(w_hbm, layer, expert, stage, wset, sem):
    rows = stage.shape[1] // WEIGHT_DMA_PARTS
    return [pltpu.make_async_copy(w_hbm.at[layer, expert, pl.ds(p * rows, rows), :],
                                  stage.at[wset, pl.ds(p * rows, rows), :], sem.at[wset])
            for p in range(WEIGHT_DMA_PARTS)]


def _stream_expert_weights(b, nact, blk_ref, kin_ref, set_ref, nxt_ref, copies):
    cur = set_ref[b]

    @pl.when(b == 0)
    def _():
        for cp in copies(blk_ref[0], cur):
            cp.start(priority=DMA_QUEUE_BULK)

    @pl.when(jnp.logical_and(b < nact, kin_ref[b] == 0))
    def _():
        for cp in copies(blk_ref[b], cur):
            cp.wait()

        @pl.when(nxt_ref[b] != blk_ref[b])
        def _():
            for cp in copies(nxt_ref[b], 1 - cur):
                cp.start(priority=DMA_QUEUE_BULK)

    return cur


def _moe_expert_kernel(blk_ref, kin_ref, last_ref, set_ref, nxt_ref, nact_ref, rtok_ref,
                       xn_hbm, wg_hbm, wu_hbm, wd_hbm, ys_ref,
                       buf, stage_g, stage_u, stage_d, xb, hid, sem, wsem, dsem, *, tb, layer):
    b = pl.program_id(0)
    nact = nact_ref[0]
    nslots = GATHER_AHEAD + 1
    slot = b % nslots
    f = hid.shape[1]
    half = ys_ref.shape[1]
    up_chunks = f // MOE_COL_CHUNK
    rows_per_chunk = tb // up_chunks
    active = b < nact
    first_of_expert = jnp.logical_and(active, kin_ref[b] == 0)
    has_next = nxt_ref[b] != blk_ref[b]

    def up_copies(e, wset):
        return (_expert_weight_copies(wg_hbm, layer, e, stage_g, wset, wsem.at[0])
                + _expert_weight_copies(wu_hbm, layer, e, stage_u, wset, wsem.at[1]))

    def down_copies(e):
        rows = stage_d.shape[0] // WEIGHT_DMA_PARTS
        return [pltpu.make_async_copy(wd_hbm.at[layer, e, pl.ds(p * rows, rows), :],
                                      stage_d.at[pl.ds(p * rows, rows), :], dsem.at[0])
                for p in range(WEIGHT_DMA_PARTS)]

    @pl.when(b == 0)
    def _():
        for ahead in range(GATHER_AHEAD):
            def body(r, carry, ahead=ahead):
                _row_copy(xn_hbm, rtok_ref[ahead * tb + r], buf.at[ahead], r,
                          sem.at[ahead]).start(priority=DMA_QUEUE_ROWS)
                return carry

            lax.fori_loop(0, tb, body, 0, unroll=8)
        for cp in down_copies(blk_ref[0]):
            cp.start(priority=DMA_QUEUE_BULK)

    cur = _stream_expert_weights(b, nact, blk_ref, kin_ref, set_ref, nxt_ref, up_copies)

    @pl.when(active)
    def _():
        _rows_wait(xn_hbm, buf.at[slot], sem.at[slot])
        x_lo, x_hi = _unpack_halves(buf[slot])
        xb[:, :x_lo.shape[1]] = x_lo.astype(BF)
        xb[:, x_lo.shape[1]:] = x_hi.astype(BF)
        ahead_slot = (b + GATHER_AHEAD) % nslots
        for c in range(up_chunks):
            for r in range(c * rows_per_chunk, (c + 1) * rows_per_chunk):
                _row_copy(xn_hbm, rtok_ref[(b + GATHER_AHEAD) * tb + r], buf.at[ahead_slot], r,
                          sem.at[ahead_slot]).start(priority=DMA_QUEUE_ROWS)
            cs = slice(c * MOE_COL_CHUNK, (c + 1) * MOE_COL_CHUNK)
            gate = jnp.dot(xb[...], stage_g[cur, :, cs].astype(BF), preferred_element_type=F32)
            up = jnp.dot(xb[...], stage_u[cur, :, cs].astype(BF), preferred_element_type=F32)
            hid[:, cs] = (gate * _sigmoid(gate) * up).astype(hid.dtype)

    @pl.when(first_of_expert)
    def _():
        for cp in down_copies(blk_ref[b]):
            cp.wait()

    @pl.when(active)
    def _():
        for c in range(half // MOE_COL_CHUNK):
            cs = slice(c * MOE_COL_CHUNK, (c + 1) * MOE_COL_CHUNK)
            hs = slice(half + c * MOE_COL_CHUNK, half + (c + 1) * MOE_COL_CHUNK)
            lo = jnp.dot(hid[...], stage_d[:, cs].astype(BF), preferred_element_type=F32)
            hi = jnp.dot(hid[...], stage_d[:, hs].astype(BF), preferred_element_type=F32)
            ys_ref[:, cs] = _pack_halves(jnp.concatenate([lo, hi], axis=1))

    @pl.when(jnp.logical_and(jnp.logical_and(active, last_ref[b] == 1), has_next))
    def _():
        for cp in down_copies(nxt_ref[b]):
            cp.start(priority=DMA_QUEUE_BULK)

    @pl.when(b >= nact)
    def _():
        ys_ref[...] = jnp.zeros_like(ys_ref)

    @pl.when(jnp.logical_and(b >= nact, b < nact + GATHER_AHEAD))
    def _():
        _rows_wait(xn_hbm, buf.at[slot], sem.at[slot])


def _combine_kernel(dest_ref, ys_hbm, h_ref, w_ref, gain_ref, o_ref, buf, sem, *, tc, n, final_norm):
    i = pl.program_id(0)

    def row_copy(blk, slot, r, k):
        return _row_copy(ys_hbm, dest_ref[k * n + blk * tc + r], buf.at[slot], k * tc + r, sem.at[slot])

    @pl.when(i == 0)
    def _():
        def body(r, carry):
            for k in range(TOP_K):
                row_copy(0, 0, r, k).start(priority=k)
            return carry

        lax.fori_loop(0, tc, body, 0, unroll=8)

    @pl.when(i + 1 < pl.num_programs(0))
    def _():
        for r in range(tc):
            for k in range(TOP_K):
                row_copy(i + 1, (i + 1) % 2, r, k).start(priority=k)

    slot = i % 2
    _rows_wait(ys_hbm, buf.at[slot], sem.at[slot])
    a_lo, a_hi = _unpack_halves(buf[slot, 0:tc, :])
    b_lo, b_hi = _unpack_halves(buf[slot, tc:TOP_K * tc, :])
    w0 = w_ref[:, 0:1]
    w1 = w_ref[:, 1:2]
    out = h_ref[...] + jnp.concatenate([w0 * a_lo + w1 * b_lo, w0 * a_hi + w1 * b_hi], axis=1)
    if final_norm:
        out = _rms(out, gain_ref[...])
    o_ref[...] = out


def hier_moe(h, ffn_gain, wg_r, bg_r, we_r, be_r, w_gate, w_up, w_down, layer, final_gain=None):
    n, d = h.shape
    f = w_gate.shape[-1]
    a = n * TOP_K
    tb = _moe_block_rows(a)
    nb = a // tb + N_EXPERTS + GATHER_AHEAD

    pad = LANES - N_GROUPS - N_EXPERTS
    w_router = jnp.concatenate([wg_r, we_r, jnp.zeros((d, pad), F32)], axis=1)
    b_router = jnp.concatenate([bg_r, be_r, jnp.zeros((pad,), F32)]).reshape(1, LANES)
    xn, ids, wts, cnt = route(h, ffn_gain, w_router, b_router)

    counts = cnt[0, N_GROUPS:N_GROUPS + N_EXPERTS].astype(jnp.int32)
    nblk = (counts + tb - 1) // tb
    bend = jnp.cumsum(nblk)
    nact = bend[-1]
    start_rows = jnp.pad(((bend - nblk) * tb).astype(F32), (0, LANES - N_EXPERTS)).reshape(1, LANES)
    dest = assignment_rows(ids, start_rows).reshape(a)
    blk = jnp.minimum(jnp.arange(nb, dtype=jnp.int32), jnp.maximum(nact - 1, 0))
    owner = lambda bi: jnp.minimum(jnp.sum((bend[None, :] <= bi[:, None]).astype(jnp.int32), axis=1), N_EXPERTS - 1)
    blk_e = owner(blk)
    k_in_e = blk - (bend - nblk)[blk_e]
    following = lambda e: jnp.where(bend[e] < nact, owner(bend[e]), e)
    nxt_e = following(blk_e)
    wset = (jnp.cumsum((k_in_e == 0).astype(jnp.int32)) - 1) % 2
    tok = jnp.tile(jnp.arange(n, dtype=jnp.int32), TOP_K)
    row_tok = (jnp.arange(nb * tb, dtype=jnp.int32) % n).at[dest].set(tok)
    nact1 = nact.reshape(1).astype(jnp.int32)

    any_space = pl.BlockSpec(memory_space=pl.ANY)
    is_last = (k_in_e == nblk[blk_e] - 1).astype(jnp.int32)
    ys = pl.pallas_call(
        functools.partial(_moe_expert_kernel, tb=tb, layer=layer),
        grid_spec=pltpu.PrefetchScalarGridSpec(
            num_scalar_prefetch=7, grid=(nb,),
            in_specs=[any_space, any_space, any_space, any_space],
            out_specs=pl.BlockSpec((tb, d // 2), lambda b, *_: (b, 0)),
            scratch_shapes=[pltpu.VMEM((GATHER_AHEAD + 1, tb, d // 2), jnp.uint32),
                            pltpu.VMEM((2, d, f), F32), pltpu.VMEM((2, d, f), F32), pltpu.VMEM((f, d), F32),
                            pltpu.VMEM((tb, d), BF), pltpu.VMEM((tb, f), BF),
                            pltpu.SemaphoreType.DMA((GATHER_AHEAD + 1,)), pltpu.SemaphoreType.DMA((2, 2)),
                            pltpu.SemaphoreType.DMA((1,))]),
        out_shape=jax.ShapeDtypeStruct((nb * tb, d // 2), jnp.uint32),
        compiler_params=_cparams(("arbitrary",), 56), name="moe_experts",
    )(blk_e, k_in_e, is_last, wset, nxt_e, nact1, row_tok, xn, w_gate, w_up, w_down)

    tc = min(COMBINE_ROWS, n)
    gain = (final_gain if final_gain is not None else ffn_gain).reshape(1, d)
    return pl.pallas_call(
        functools.partial(_combine_kernel, tc=tc, n=n, final_norm=final_gain is not None),
        grid_spec=pltpu.PrefetchScalarGridSpec(
            num_scalar_prefetch=1, grid=(n // tc,),
            in_specs=[pl.BlockSpec(memory_space=pl.ANY), pl.BlockSpec((tc, d), lambda i, *_: (i, 0)),
                      pl.BlockSpec((tc, LANES), lambda i, *_: (i, 0)), pl.BlockSpec((1, d), lambda i, *_: (0, 0))],
            out_specs=pl.BlockSpec((tc, d), lambda i, *_: (i, 0)),
            scratch_shapes=[pltpu.VMEM((2, TOP_K * tc, d // 2), jnp.uint32), pltpu.SemaphoreType.DMA((2,))]),
        out_shape=jax.ShapeDtypeStruct((n, d), F32),
        compiler_params=_cparams(("arbitrary",), 40), name="moe_combine",
    )(dest, ys, h, wts, gain)


def even_mixer(h, gain, w_in, b_gates, pool_w, pool_scale, head_gain, w_out, j):
    d = h.shape[1]
    pool_width = N_POOL_GROUPS * pool_w.shape[-1]
    mlstm_width = head_gain.shape[0]
    main_cols = pool_width + 4 * mlstm_width
    n_gates = 2 * MLSTM_HEADS
    gate_b = jnp.pad(b_gates, (0, LANES - n_gates)).reshape(1, LANES)
    z, gates = norm_matmul_gates_t(h, gain, jnp.swapaxes(w_in, 1, 2), main_cols, n_gates, gate_b, j)
    y_p = pool_mixer(z, pool_w, pool_scale)
    y_m = mlstm_mixer(z, gates, head_gain, pool_width)
    assert pool_width == mlstm_width and pool_width + mlstm_width == d
    return matmul_residual([y_p, y_m], w_out, h, layer=j)


def odd_mixer(h, gain, w_in, conv_w, w_out, j):
    z = norm_matmul(h, gain, w_in, w_in.shape[2], layer=j)
    return conv_matmul_residual(z, conv_w, w_out, h, j)


def cross_attn(h, mem, gain, mem_gain, wq, wk, wv, wo, layer):
    d = h.shape[1]
    k = norm_matmul(mem, mem_gain, wk, d, layer=layer)
    v = norm_matmul(mem, mem_gain, wv, d, layer=layer)
    return matmul_residual([q_attention(h, gain, wq, k, v, layer)], wo, h, layer=layer)


def kernel(x, mem, mix_norm, xattn_norm, mem_norm, ffn_norm, final_norm, ev_w_in, ev_b_gates, ev_pool_w, ev_pool_scale, ev_head_norm, ev_w_out, od_w_in, od_conv_w, od_w_out, xa_wq, xa_wk, xa_wv, xa_wo, rt_group_w, rt_group_b, rt_expert_w, rt_expert_b, ex_w_gate, ex_w_up, ex_w_down):
    depth = mix_norm.shape[0]
    h = x[0]
    m = mem[0]
    for layer in range(depth):
        j = layer // 2
        if layer % 2 == 0:
            h = even_mixer(h, mix_norm[layer], ev_w_in, ev_b_gates[j], ev_pool_w[j], ev_pool_scale[j],
                           ev_head_norm[j], ev_w_out, j)
        else:
            h = odd_mixer(h, mix_norm[layer], od_w_in, od_conv_w[j], od_w_out, j)
        h = cross_attn(h, m, xattn_norm[layer], mem_norm[layer], xa_wq, xa_wk, xa_wv, xa_wo, layer)
        h = hier_moe(h, ffn_norm[layer], rt_group_w[layer], rt_group_b[layer], rt_expert_w[layer],
                     rt_expert_b[layer], ex_w_gate, ex_w_up, ex_w_down, layer,
                     final_gain=final_norm if layer == depth - 1 else None)
    return h[None]
```

```python
import functools

import jax
import jax.numpy as jnp
from jax import lax
from jax.experimental import pallas as pl
from jax.experimental.pallas import tpu as pltpu

F32 = jnp.float32
BF = jnp.bfloat16
EPS = 1e-6

POOL_WINDOWS = (2, 4, 8, 16)
N_POOL_GROUPS = 4
MLSTM_HEADS = 4
FORGET_LANE0 = MLSTM_HEADS
XATTN_HEADS = 4
N_GROUPS = 4
EXPERTS_PER_GROUP = 8
N_EXPERTS = N_GROUPS * EXPERTS_PER_GROUP
TOP_K = 2
CONV_WIDTH = 3

LANES = 128
SUBLANES = 8
DMA_QUEUE_ROWS = 0
DMA_QUEUE_BULK = 1

ROW_TILE = 1024
COL_TILE = 1024
OUT_ROW_TILE = 512
ATTN_ROW_TILE = 1024
MOE_COL_CHUNK = 256
WEIGHT_DMA_PARTS = 4
MLSTM_CHUNK = 256
POOL_ROWS = 256
POOL_HALO = 128
CONV_ROWS = 512
CONV_COLS = 512
CONV_HALO = 16
ROUTER_ROWS = 512
GATHER_AHEAD = 2
COMBINE_ROWS = 256


def _moe_block_rows(assignments):
    mean_rows = assignments // N_EXPERTS
    return -(-(mean_rows * 9 // 16) // 16) * 16


def _cparams(semantics, vmem_mib):
    return pltpu.CompilerParams(dimension_semantics=semantics, vmem_limit_bytes=vmem_mib * 1024 * 1024)


def _sigmoid(x):
    return 1.0 / (1.0 + jnp.exp(-x))


def _log_sigmoid(x):
    return jnp.minimum(x, 0.0) - jnp.log(1.0 + jnp.exp(-jnp.abs(x)))


def _rms(x, g):
    ms = jnp.mean(x * x, axis=-1, keepdims=True)
    return x * lax.rsqrt(ms + EPS) * g


def _pack_halves(x):
    half = x.shape[1] // 2
    lo = pltpu.bitcast(x[:, :half].astype(BF).astype(F32), jnp.uint32)
    hi = pltpu.bitcast(x[:, half:].astype(BF).astype(F32), jnp.uint32)
    return hi | lax.shift_right_logical(lo, jnp.uint32(16))


def _unpack_halves(w):
    lo = pltpu.bitcast(lax.shift_left(w, jnp.uint32(16)), F32)
    hi = pltpu.bitcast(w & jnp.uint32(0xFFFF0000), F32)
    return lo, hi


def _norm_mm_kernel(x_ref, g_ref, w_ref, o_ref, xn_ref):
    @pl.when(pl.program_id(1) == 0)
    def _():
        xn_ref[...] = _rms(x_ref[...], g_ref[...]).astype(BF)

    o_ref[...] = jnp.dot(xn_ref[...], w_ref[...].astype(BF), preferred_element_type=F32).astype(o_ref.dtype)


_NT = (((1,), (1,)), ((), ()))


def _norm_mm_gates_t_kernel(x_ref, g_ref, wt_ref, wgt_ref, bg_ref, o_ref, gates_ref, xn_ref):
    @pl.when(pl.program_id(1) == 0)
    def _():
        xn = _rms(x_ref[...], g_ref[...]).astype(BF)
        xn_ref[...] = xn
        wg = wgt_ref[...]
        wg = jnp.concatenate([wg, jnp.zeros((LANES - wg.shape[0], wg.shape[1]), F32)], axis=0).astype(BF)
        gates_ref[...] = lax.dot_general(xn, wg, _NT, preferred_element_type=F32) + bg_ref[...]

    o_ref[...] = lax.dot_general(xn_ref[...], wt_ref[...].astype(BF), _NT,
                                 preferred_element_type=F32).astype(o_ref.dtype)


def norm_matmul_gates_t(x, gain, wt, n_cols, n_gates, gate_b, layer):
    n, k = x.shape
    tm = min(ROW_TILE, n)
    tn = COL_TILE
    return pl.pallas_call(
        _norm_mm_gates_t_kernel, grid=(n // tm, n_cols // tn),
        in_specs=[pl.BlockSpec((tm, k), lambda i, j: (i, 0)), pl.BlockSpec((1, k), lambda i, j: (0, 0)),
                  pl.BlockSpec((None, tn, k), lambda i, j: (layer, j, 0)),
                  pl.BlockSpec((None, n_gates, k), lambda i, j: (layer, n_cols // n_gates, 0)),
                  pl.BlockSpec((1, LANES), lambda i, j: (0, 0))],
        out_specs=[pl.BlockSpec((tm, tn), lambda i, j: (i, j)), pl.BlockSpec((tm, LANES), lambda i, j: (i, 0))],
        out_shape=[jax.ShapeDtypeStruct((n, n_cols), BF), jax.ShapeDtypeStruct((n, LANES), F32)],
        scratch_shapes=[pltpu.VMEM((tm, k), BF)],
        compiler_params=_cparams(("parallel", "arbitrary"), 56), name="norm_matmul_gates",
    )(x, gain.reshape(1, k), wt, wt, gate_b)


def _stacked(w, layer):
    return (w[None], 0) if layer is None else (w, layer)


def norm_matmul(x, gain, w, n_cols, layer=None):
    n, k = x.shape
    w, li = _stacked(w, layer)
    tm = min(ROW_TILE, n)
    tn = COL_TILE
    return pl.pallas_call(
        _norm_mm_kernel, grid=(n // tm, n_cols // tn),
        in_specs=[pl.BlockSpec((tm, k), lambda i, j: (i, 0)), pl.BlockSpec((1, k), lambda i, j: (0, 0)),
                  pl.BlockSpec((None, k, tn), lambda i, j: (li, 0, j))],
        out_specs=pl.BlockSpec((tm, tn), lambda i, j: (i, j)),
        out_shape=jax.ShapeDtypeStruct((n, n_cols), BF), scratch_shapes=[pltpu.VMEM((tm, k), BF)],
        compiler_params=_cparams(("parallel", "arbitrary"), 56), name="norm_matmul",
    )(x, gain.reshape(1, k), w)


def _q_attention_kernel(x_ref, g_ref, w_ref, k_ref, v_ref, o_ref, q_ref, *, hd):
    xn = _rms(x_ref[...], g_ref[...]).astype(BF)
    q_ref[...] = jnp.dot(xn, w_ref[...].astype(BF), preferred_element_type=F32).astype(BF)
    scale = hd ** -0.5
    for h in range(XATTN_HEADS):
        hs = slice(h * hd, (h + 1) * hd)
        s = lax.dot_general(q_ref[:, hs], k_ref[:, hs], _NT, preferred_element_type=F32) * scale
        e = jnp.exp(s - jnp.max(s, axis=-1, keepdims=True))
        p = (e / jnp.sum(e, axis=-1, keepdims=True)).astype(BF)
        o_ref[:, hs] = jnp.dot(p, v_ref[:, hs], preferred_element_type=F32).astype(o_ref.dtype)


def q_attention(x, gain, wq, k, v, layer):
    n, kd = x.shape
    d = wq.shape[2]
    m = k.shape[0]
    tm = min(ATTN_ROW_TILE, n)
    return pl.pallas_call(
        functools.partial(_q_attention_kernel, hd=d // XATTN_HEADS), grid=(n // tm,),
        in_specs=[pl.BlockSpec((tm, kd), lambda i: (i, 0)), pl.BlockSpec((1, kd), lambda i: (0, 0)),
                  pl.BlockSpec((None, kd, d), lambda i: (layer, 0, 0), pipeline_mode=pl.Buffered(1)),
                  pl.BlockSpec((m, d), lambda i: (0, 0)), pl.BlockSpec((m, d), lambda i: (0, 0))],
        out_specs=pl.BlockSpec((tm, d), lambda i: (i, 0)), out_shape=jax.ShapeDtypeStruct((n, d), BF),
        scratch_shapes=[pltpu.VMEM((tm, d), BF)],
        compiler_params=_cparams(("parallel",), 58), name="q_attention",
    )(x, gain.reshape(1, kd), wq, k, v)


def _mm_res_kernel(*refs, nparts):
    xs = refs[:nparts]
    w_ref, res_ref, o_ref = refs[nparts:]
    acc = res_ref[...]
    k0 = 0
    for x_ref in xs:
        kp = x_ref.shape[1]
        acc = acc + jnp.dot(x_ref[...], w_ref[k0:k0 + kp, :].astype(BF), preferred_element_type=F32)
        k0 += kp
    o_ref[...] = acc


def matmul_residual(xs, w, res, layer=None):
    n, d = res.shape
    w, li = _stacked(w, layer)
    k = w.shape[1]
    tm = min(OUT_ROW_TILE, n)
    in_specs = [pl.BlockSpec((tm, x.shape[1]), lambda i: (i, 0)) for x in xs]
    in_specs += [pl.BlockSpec((None, k, d), lambda i: (li, 0, 0), pipeline_mode=pl.Buffered(1)),
                 pl.BlockSpec((tm, d), lambda i: (i, 0))]
    return pl.pallas_call(
        functools.partial(_mm_res_kernel, nparts=len(xs)), grid=(n // tm,), in_specs=in_specs,
        out_specs=pl.BlockSpec((tm, d), lambda i: (i, 0)), out_shape=jax.ShapeDtypeStruct((n, d), F32),
        compiler_params=_cparams(("parallel",), 52), name="matmul_residual",
    )(*xs, w, res)


def _pool_kernel(cur_ref, prev_ref, w_ref, sc_ref, o_ref, *, tp, gdim):
    i = pl.program_id(0)
    dist = lax.broadcasted_iota(jnp.int32, (tp, tp), 0) - lax.broadcasted_iota(jnp.int32, (tp, tp), 1)
    distp = (lax.broadcasted_iota(jnp.int32, (tp, POOL_HALO), 0) + POOL_HALO
             - lax.broadcasted_iota(jnp.int32, (tp, POOL_HALO), 1))
    pos = i * tp + lax.broadcasted_iota(jnp.int32, (tp, 1), 0)
    for j, win in enumerate(POOL_WINDOWS):
        gs = slice(j * gdim, (j + 1) * gdim)
        cur = cur_ref[:, gs]
        band = jnp.where(dist >= 0, jnp.where(dist < win, 1.0, 0.0), 0.0).astype(BF)
        bandp = jnp.where(distp < jnp.where(i > 0, win, 0), 1.0, 0.0).astype(BF)
        s = (jnp.dot(band, cur, preferred_element_type=F32)
             + jnp.dot(bandp, prev_ref[:, gs], preferred_element_type=F32))
        cnt = jnp.minimum(pos + 1, win).astype(F32)
        d = s / cnt - cur.astype(F32)
        y = jnp.dot(d.astype(BF), w_ref[j].astype(BF), preferred_element_type=F32) * sc_ref[:, gs]
        o_ref[:, gs] = y.astype(o_ref.dtype)


def pool_mixer(z, pool_w, pool_scale):
    n = z.shape[0]
    gdim = pool_w.shape[-1]
    width = N_POOL_GROUPS * gdim
    tp = min(POOL_ROWS, n)
    halo_blocks = tp // POOL_HALO
    return pl.pallas_call(
        functools.partial(_pool_kernel, tp=tp, gdim=gdim), grid=(n // tp,),
        in_specs=[
            pl.BlockSpec((tp, width), lambda i: (i, 0)),
            pl.BlockSpec((POOL_HALO, width), lambda i: (jnp.maximum(i * halo_blocks - 1, 0), 0)),
            pl.BlockSpec((N_POOL_GROUPS, gdim, gdim), lambda i: (0, 0, 0)),
            pl.BlockSpec((1, width), lambda i: (0, 0)),
        ],
        out_specs=pl.BlockSpec((tp, width), lambda i: (i, 0)),
        out_shape=jax.ShapeDtypeStruct((n, width), BF),
        compiler_params=_cparams(("parallel",), 32), name="pool_mixer",
    )(z, z, pool_w, pool_scale.reshape(1, width))


def _mlstm_kernel(q_ref, k_ref, v_ref, o_ref, g_ref, gain_ref, y_ref, ct_ref, m_ref, *, chunk, dh):
    c = pl.program_id(0)

    @pl.when(c == 0)
    def _():
        ct_ref[...] = jnp.zeros_like(ct_ref)
        m_ref[...] = jnp.zeros_like(m_ref)

    g = g_ref[...]
    lf = _log_sigmoid(g)
    row = lax.broadcasted_iota(jnp.int32, (chunk, chunk), 0)
    col = lax.broadcasted_iota(jnp.int32, (chunk, chunk), 1)
    causal = col <= row
    ltri = jnp.where(causal, 1.0, 0.0).astype(BF)
    hi = lf.astype(BF)
    r1 = lf - hi.astype(F32)
    mid = r1.astype(BF)
    lo = (r1 - mid.astype(F32)).astype(BF)
    bcum = (jnp.dot(ltri, hi, preferred_element_type=F32) + jnp.dot(ltri, mid, preferred_element_type=F32)
            + jnp.dot(ltri, lo, preferred_element_type=F32))
    g_t = g.T
    b_t = bcum.T
    ones_col = jnp.where(lax.broadcasted_iota(jnp.int32, (chunk, LANES), 1) == 0, 1.0, 0.0).astype(BF)

    for h in range(MLSTM_HEADS):
        hs = slice(h * dh, (h + 1) * dh)
        fl = FORGET_LANE0 + h
        bc = bcum[:, fl:fl + 1]
        br = b_t[fl:fl + 1, :]
        ir = g_t[h:h + 1, :]
        b_last = bcum[chunk - 1:chunk, fl:fl + 1]
        m_prev = m_ref[h][:, 0:1]

        dmat = jnp.where(causal, bc + (ir - br), -jnp.inf)
        inter = bc + m_prev
        m_t = jnp.maximum(jnp.max(dmat, axis=1, keepdims=True), inter)
        w_inter = jnp.exp(inter - m_t)
        p = jnp.exp(dmat - m_t)

        qh = q_ref[:, hs] * (dh ** -0.5)
        kh = k_ref[:, hs]
        v_aug = jnp.concatenate([v_ref[:, hs], ones_col], axis=1)
        s = lax.dot_general(qh, kh, (((1,), (1,)), ((), ())), preferred_element_type=F32)
        sc = (s * p).astype(BF)
        ct = ct_ref[h]
        num_aug = (w_inter * jnp.dot(qh, ct.astype(BF), preferred_element_type=F32)
                   + jnp.dot(sc, v_aug, preferred_element_type=F32))
        num = num_aug[:, :dh]
        den = num_aug[:, dh:dh + 1]
        hout = num / jnp.maximum(jnp.abs(den), jnp.exp(-m_t))

        yn = _rms(hout, gain_ref[:, hs])
        y_ref[:, hs] = (_sigmoid(o_ref[:, hs].astype(F32)) * yn).astype(y_ref.dtype)

        d_end = b_last - br + ir
        m_new = jnp.maximum(b_last + m_prev, jnp.max(d_end, axis=1, keepdims=True))
        a_prev = jnp.exp(b_last + m_prev - m_new)
        a_s = jnp.exp(d_end - m_new)
        k_t = (kh.astype(F32).T * a_s).astype(BF)
        ct_ref[h] = a_prev * ct + jnp.dot(k_t, v_aug, preferred_element_type=F32)
        m_ref[h] = jnp.broadcast_to(m_new, (1, LANES))


def mlstm_mixer(z, gates, head_gain, col0):
    n = z.shape[0]
    width = head_gain.shape[0]
    dh = width // MLSTM_HEADS
    chunk = min(MLSTM_CHUNK, n)
    base = col0 // width
    qkvo = [pl.BlockSpec((chunk, width), lambda c, p=p: (c, base + p)) for p in range(4)]
    return pl.pallas_call(
        functools.partial(_mlstm_kernel, chunk=chunk, dh=dh), grid=(n // chunk,),
        in_specs=qkvo + [pl.BlockSpec((chunk, LANES), lambda c: (c, 0)), pl.BlockSpec((1, width), lambda c: (0, 0))],
        out_specs=pl.BlockSpec((chunk, width), lambda c: (c, 0)),
        out_shape=jax.ShapeDtypeStruct((n, width), BF),
        scratch_shapes=[pltpu.VMEM((MLSTM_HEADS, dh, dh + LANES), F32), pltpu.VMEM((MLSTM_HEADS, 1, LANES), F32)],
        compiler_params=_cparams(("arbitrary",), 32), name="mlstm_mixer",
    )(z, z, z, z, gates, head_gain.reshape(1, width))


def _gated_conv(b, c, u, cp, up, w, first_block):
    zc = c.astype(F32) * u.astype(F32)
    zp = jnp.where(first_block, 0.0, cp.astype(F32) * up.astype(F32))
    row = lax.broadcasted_iota(jnp.int32, zc.shape, 0)
    acc = w[CONV_WIDTH - 1:CONV_WIDTH, :] * zc
    for back in range(1, CONV_WIDTH):
        shifted = pltpu.roll(zc, back, 0)
        for r in range(back):
            shifted = jnp.where(row == r, zp[CONV_HALO - back + r:CONV_HALO - back + r + 1, :], shifted)
        acc = acc + w[CONV_WIDTH - 1 - back:CONV_WIDTH - back, :] * shifted
    return b.astype(F32) * acc


def _conv_mm_res_kernel(b_ref, c_ref, u_ref, cp_ref, up_ref, cw_ref, w_ref, res_ref, o_ref, xs_ref):
    first_block = pl.program_id(0) == 0
    for c0 in range(0, xs_ref.shape[1], CONV_COLS):
        cs = slice(c0, c0 + CONV_COLS)
        xs_ref[:, cs] = _gated_conv(b_ref[:, cs], c_ref[:, cs], u_ref[:, cs], cp_ref[:, cs], up_ref[:, cs],
                                    cw_ref[:, cs], first_block).astype(BF)
    o_ref[...] = res_ref[...] + jnp.dot(xs_ref[...], w_ref[...].astype(BF), preferred_element_type=F32)


def conv_matmul_residual(z, conv_w, w, res, layer):
    n, d = res.shape
    tm = min(CONV_ROWS, n)
    halo_blocks = tm // CONV_HALO
    cur = lambda part: pl.BlockSpec((tm, d), lambda i: (i, part))
    prev = lambda part: pl.BlockSpec((CONV_HALO, d), lambda i: (jnp.maximum(i * halo_blocks - 1, 0), part))
    return pl.pallas_call(
        _conv_mm_res_kernel, grid=(n // tm,),
        in_specs=[cur(0), cur(1), cur(2), prev(1), prev(2), pl.BlockSpec((CONV_WIDTH, d), lambda i: (0, 0)),
                  pl.BlockSpec((None, d, d), lambda i: (layer, 0, 0), pipeline_mode=pl.Buffered(1)),
                  pl.BlockSpec((tm, d), lambda i: (i, 0))],
        out_specs=pl.BlockSpec((tm, d), lambda i: (i, 0)), out_shape=jax.ShapeDtypeStruct((n, d), F32),
        scratch_shapes=[pltpu.VMEM((tm, d), BF)],
        compiler_params=_cparams(("parallel",), 52), name="conv_matmul_residual",
    )(z, z, z, z, z, conv_w, w, res)


def _router_kernel(x_ref, g_ref, wr_ref, br_ref, xn_ref, ids_ref, wts_ref, cnt_ref, carry_ref, w2_ref, *, tr):
    @pl.when(pl.program_id(0) == 0)
    def _():
        carry_ref[...] = jnp.zeros_like(carry_ref)

    @pl.when(pl.program_id(0) == 0)
    def _():
        w = wr_ref[...]
        w_hi = w.astype(BF)
        w2_ref[:, :LANES] = w_hi
        w2_ref[:, LANES:] = (w - w_hi.astype(F32)).astype(BF)

    xn = _rms(x_ref[...], g_ref[...])
    xn_ref[...] = _pack_halves(xn)
    x_hi = xn.astype(BF)
    x_lo = (xn - x_hi.astype(F32)).astype(BF)
    p_hi = jnp.dot(x_hi, w2_ref[...], preferred_element_type=F32)
    p_lo = jnp.dot(x_lo, w2_ref[...], preferred_element_type=F32)
    logits = (p_hi[:, :LANES] + p_hi[:, LANES:]) + (p_lo[:, :LANES] + p_lo[:, LANES:]) + br_ref[...]
    lane = lax.broadcasted_iota(jnp.int32, (tr, LANES), 1).astype(F32)
    neg = -jnp.inf

    def first_argmax(vals):
        top = jnp.max(vals, axis=-1, keepdims=True)
        return top, jnp.min(jnp.where(vals == top, lane, float(LANES)), axis=-1, keepdims=True)

    gl = jnp.where(lane < N_GROUPS, logits, neg)
    gmax, grp = first_argmax(gl)
    g_prob = 1.0 / jnp.sum(jnp.exp(gl - gmax), axis=-1, keepdims=True)
    lo = N_GROUPS + EXPERTS_PER_GROUP * grp
    el = jnp.where(lane >= lo, jnp.where(lane < lo + EXPERTS_PER_GROUP, logits, neg), neg)
    v1, l1 = first_argmax(el)
    v2, l2 = first_argmax(jnp.where(lane == l1, neg, el))
    e2 = jnp.exp(v2 - v1)
    w1 = g_prob / (1.0 + e2)
    w2 = g_prob * e2 / (1.0 + e2)
    hot1 = lane == l1
    hot2 = lane == l2
    hot = jnp.where(hot1, 1.0, jnp.where(hot2, 1.0, 0.0))
    earlier = (lax.broadcasted_iota(jnp.int32, (tr, tr), 1) < lax.broadcasted_iota(jnp.int32, (tr, tr), 0))
    before = jnp.dot(jnp.where(earlier, 1.0, 0.0).astype(BF), hot.astype(BF), preferred_element_type=F32)
    before = before + carry_ref[0:1, :]
    r1 = jnp.sum(jnp.where(hot1, before, 0.0), axis=-1, keepdims=True)
    r2 = jnp.sum(jnp.where(hot2, before, 0.0), axis=-1, keepdims=True)
    carry_ref[0:1, :] = carry_ref[0:1, :] + jnp.sum(hot, axis=0, keepdims=True)
    ids = jnp.where(lane == 0, l1 - N_GROUPS, jnp.where(lane == 1, l2 - N_GROUPS,
                    jnp.where(lane == 2, r1, jnp.where(lane == 3, r2, 0.0))))
    ids_ref[...] = ids.astype(jnp.int32)
    wts_ref[...] = jnp.where(lane == 0, w1, jnp.where(lane == 1, w2, 0.0))
    cnt_ref[...] = carry_ref[...]


def route(h, gain, w_router, b_router):
    n, d = h.shape
    tr = min(ROUTER_ROWS, n)
    return pl.pallas_call(
        functools.partial(_router_kernel, tr=tr), grid=(n // tr,),
        in_specs=[pl.BlockSpec((tr, d), lambda i: (i, 0)), pl.BlockSpec((1, d), lambda i: (0, 0)),
                  pl.BlockSpec((d, LANES), lambda i: (0, 0)), pl.BlockSpec((1, LANES), lambda i: (0, 0))],
        out_specs=[pl.BlockSpec((tr, d // 2), lambda i: (i, 0)),
                   pl.BlockSpec((tr, LANES), lambda i: (i, 0)), pl.BlockSpec((tr, LANES), lambda i: (i, 0)),
                   pl.BlockSpec((SUBLANES, LANES), lambda i: (0, 0))],
        out_shape=[jax.ShapeDtypeStruct((n, d // 2), jnp.uint32), jax.ShapeDtypeStruct((n, LANES), jnp.int32),
                   jax.ShapeDtypeStruct((n, LANES), F32), jax.ShapeDtypeStruct((SUBLANES, LANES), F32)],
        scratch_shapes=[pltpu.VMEM((SUBLANES, LANES), F32), pltpu.VMEM((d, 2 * LANES), BF)],
        compiler_params=_cparams(("arbitrary",), 40), name="moe_router",
    )(h, gain.reshape(1, d), w_router, b_router)


def _dest_kernel(ids_ref, start_ref, o_ref, *, tr):
    ids = ids_ref[...].astype(F32)
    lane = lax.broadcasted_iota(jnp.int32, (tr, LANES), 1).astype(F32)
    start = start_ref[...]
    rows = []
    for k in range(TOP_K):
        first = jnp.sum(jnp.where(lane == ids[:, k:k + 1], start, 0.0), axis=-1, keepdims=True)
        rows.append(first + ids[:, TOP_K + k:TOP_K + k + 1])
    packed = jnp.where(lane == 0, rows[0], jnp.where(lane == 1, rows[1], 0.0))
    o_ref[...] = packed.T[0:SUBLANES, :].astype(jnp.int32)


def assignment_rows(ids, start_rows):
    n = ids.shape[0]
    tr = min(ROUTER_ROWS, n)
    out = pl.pallas_call(
        functools.partial(_dest_kernel, tr=tr), grid=(n // tr,),
        in_specs=[pl.BlockSpec((tr, LANES), lambda i: (i, 0)), pl.BlockSpec((1, LANES), lambda i: (0, 0))],
        out_specs=pl.BlockSpec((SUBLANES, tr), lambda i: (0, i)),
        out_shape=jax.ShapeDtypeStruct((SUBLANES, n), jnp.int32),
        compiler_params=_cparams(("parallel",), 32), name="moe_assignment_rows",
    )(ids, start_rows)
    return out[:TOP_K]


def _row_copy(src_hbm, row, dst, dst_row, sem):
    return pltpu.make_async_copy(src_hbm.at[pl.ds(row, 1), :], dst.at[pl.ds(dst_row, 1), :], sem)


def _rows_wait(src_hbm, dst, sem):
    pltpu.make_async_copy(src_hbm.at[pl.ds(0, dst.shape[0]), :], dst, sem).wait()


def _expert_weight_copies(w_hbm, layer, expert, stage, wset, sem):
    rows = stage.shape[1] // WEIGHT_DMA_PARTS
    return [pltpu.make_async_copy(w_hbm.at[layer, expert, pl.ds(p * rows, rows), :],
                                  stage.at[wset, pl.ds(p * rows, rows), :], sem.at[wset])
            for p in range(WEIGHT_DMA_PARTS)]


def _stream_expert_weights(b, nact, blk_ref, kin_ref, set_ref, nxt_ref, copies):
    cur = set_ref[b]

    @pl.when(b == 0)
    def _():
        for cp in copies(blk_ref[0], cur):
            cp.start(priority=DMA_QUEUE_BULK)

    @pl.when(jnp.logical_and(b < nact, kin_ref[b] == 0))
    def _():
        for cp in copies(blk_ref[b], cur):
            cp.wait()

        @pl.when(nxt_ref[b] != blk_ref[b])
        def _():
            for cp in copies(nxt_ref[b], 1 - cur):
                cp.start(priority=DMA_QUEUE_BULK)

    return cur


def _moe_expert_kernel(blk_ref, kin_ref, last_ref, set_ref, nxt_ref, nact_ref, rtok_ref,
                       xn_hbm, wg_hbm, wu_hbm, wd_hbm, ys_ref,
                       buf, stage_g, stage_u, stage_d, xb, hid, sem, wsem, dsem, *, tb, layer):
    b = pl.program_id(0)
    nact = nact_ref[0]
    nslots = GATHER_AHEAD + 1
    slot = b % nslots
    f = hid.shape[1]
    half = ys_ref.shape[1]
    up_chunks = f // MOE_COL_CHUNK
    rows_per_chunk = tb // up_chunks
    active = b < nact
    first_of_expert = jnp.logical_and(active, kin_ref[b] == 0)
    has_next = nxt_ref[b] != blk_ref[b]

    def up_copies(e, wset):
        return (_expert_weight_copies(wg_hbm, layer, e, stage_g, wset, wsem.at[0])
                + _expert_weight_copies(wu_hbm, layer, e, stage_u, wset, wsem.at[1]))

    def down_copies(e):
        rows = stage_d.shape[0] // WEIGHT_DMA_PARTS
        return [pltpu.make_async_copy(wd_hbm.at[layer, e, pl.ds(p * rows, rows), :],
                                      stage_d.at[pl.ds(p * rows, rows), :], dsem.at[0])
                for p in range(WEIGHT_DMA_PARTS)]

    @pl.when(b == 0)
    def _():
        for ahead in range(GATHER_AHEAD):
            def body(r, carry, ahead=ahead):
                _row_copy(xn_hbm, rtok_ref[ahead * tb + r], buf.at[ahead], r,
                          sem.at[ahead]).start(priority=DMA_QUEUE_ROWS)
                return carry

            lax.fori_loop(0, tb, body, 0, unroll=8)
        for cp in down_copies(blk_ref[0]):
            cp.start(priority=DMA_QUEUE_BULK)

    cur = _stream_expert_weights(b, nact, blk_ref, kin_ref, set_ref, nxt_ref, up_copies)

    @pl.when(active)
    def _():
        _rows_wait(xn_hbm, buf.at[slot], sem.at[slot])
        x_lo, x_hi = _unpack_halves(buf[slot])
        xb[:, :x_lo.shape[1]] = x_lo.astype(BF)
        xb[:, x_lo.shape[1]:] = x_hi.astype(BF)
        ahead_slot = (b + GATHER_AHEAD) % nslots
        for c in range(up_chunks):
            for r in range(c * rows_per_chunk, (c + 1) * rows_per_chunk):
                _row_copy(xn_hbm, rtok_ref[(b + GATHER_AHEAD) * tb + r], buf.at[ahead_slot], r,
                          sem.at[ahead_slot]).start(priority=DMA_QUEUE_ROWS)
            cs = slice(c * MOE_COL_CHUNK, (c + 1) * MOE_COL_CHUNK)
            gate = jnp.dot(xb[...], stage_g[cur, :, cs].astype(BF), preferred_element_type=F32)
            up = jnp.dot(xb[...], stage_u[cur, :, cs].astype(BF), preferred_element_type=F32)
            hid[:, cs] = (gate * _sigmoid(gate) * up).astype(hid.dtype)

    @pl.when(first_of_expert)
    def _():
        for cp in down_copies(blk_ref[b]):
            cp.wait()

    @pl.when(active)
    def _():
        for c in range(half // MOE_COL_CHUNK):
            cs = slice(c * MOE_COL_CHUNK, (c + 1) * MOE_COL_CHUNK)
            hs = slice(half + c * MOE_COL_CHUNK, half + (c + 1) * MOE_COL_CHUNK)
            lo = jnp.dot(hid[...], stage_d[:, cs].astype(BF), preferred_element_type=F32)
            hi = jnp.dot(hid[...], stage_d[:, hs].astype(BF), preferred_element_type=F32)
            ys_ref[:, cs] = _pack_halves(jnp.concatenate([lo, hi], axis=1))

    @pl.when(jnp.logical_and(jnp.logical_and(active, last_ref[b] == 1), has_next))
    def _():
        for cp in down_copies(nxt_ref[b]):
            cp.start(priority=DMA_QUEUE_BULK)

    @pl.when(b >= nact)
    def _():
        ys_ref[...] = jnp.zeros_like(ys_ref)

    @pl.when(jnp.logical_and(b >= nact, b < nact + GATHER_AHEAD))
    def _():
        _rows_wait(xn_hbm, buf.at[slot], sem.at[slot])


def _combine_kernel(dest_ref, ys_hbm, h_ref, w_ref, gain_ref, o_ref, buf, sem, *, tc, n, final_norm):
    i = pl.program_id(0)

    def row_copy(blk, slot, r, k):
        return _row_copy(ys_hbm, dest_ref[k * n + blk * tc + r], buf.at[slot], k * tc + r, sem.at[slot])

    @pl.when(i == 0)
    def _():
        def body(r, carry):
            for k in range(TOP_K):
                row_copy(0, 0, r, k).start(priority=k)
            return carry

        lax.fori_loop(0, tc, body, 0, unroll=8)

    @pl.when(i + 1 < pl.num_programs(0))
    def _():
        for r in range(tc):
            for k in range(TOP_K):
                row_copy(i + 1, (i + 1) % 2, r, k).start(priority=k)

    slot = i % 2
    _rows_wait(ys_hbm, buf.at[slot], sem.at[slot])
    a_lo, a_hi = _unpack_halves(buf[slot, 0:tc, :])
    b_lo, b_hi = _unpack_halves(buf[slot, tc:TOP_K * tc, :])
    w0 = w_ref[:, 0:1]
    w1 = w_ref[:, 1:2]
    out = h_ref[...] + jnp.concatenate([w0 * a_lo + w1 * b_lo, w0 * a_hi + w1 * b_hi], axis=1)
    if final_norm:
        out = _rms(out, gain_ref[...])
    o_ref[...] = out


def hier_moe(h, ffn_gain, wg_r, bg_r, we_r, be_r, w_gate, w_up, w_down, layer, final_gain=None):
    n, d = h.shape
    f = w_gate.shape[-1]
    a = n * TOP_K
    tb = _moe_block_rows(a)
    nb = a // tb + N_EXPERTS + GATHER_AHEAD

    pad = LANES - N_GROUPS - N_EXPERTS
    w_router = jnp.concatenate([wg_r, we_r, jnp.zeros((d, pad), F32)], axis=1)
    b_router = jnp.concatenate([bg_r, be_r, jnp.zeros((pad,), F32)]).reshape(1, LANES)
    xn, ids, wts, cnt = route(h, ffn_gain, w_router, b_router)

    counts = cnt[0, N_GROUPS:N_GROUPS + N_EXPERTS].astype(jnp.int32)
    nblk = (counts + tb - 1) // tb
    bend = jnp.cumsum(nblk)
    nact = bend[-1]
    start_rows = jnp.pad(((bend - nblk) * tb).astype(F32), (0, LANES - N_EXPERTS)).reshape(1, LANES)
    dest = assignment_rows(ids, start_rows).reshape(a)
    blk = jnp.minimum(jnp.arange(nb, dtype=jnp.int32), jnp.maximum(nact - 1, 0))
    owner = lambda bi: jnp.minimum(jnp.sum((bend[None, :] <= bi[:, None]).astype(jnp.int32), axis=1), N_EXPERTS - 1)
    blk_e = owner(blk)
    k_in_e = blk - (bend - nblk)[blk_e]
    following = lambda e: jnp.where(bend[e] < nact, owner(bend[e]), e)
    nxt_e = following(blk_e)
    wset = (jnp.cumsum((k_in_e == 0).astype(jnp.int32)) - 1) % 2
    tok = jnp.tile(jnp.arange(n, dtype=jnp.int32), TOP_K)
    row_tok = (jnp.arange(nb * tb, dtype=jnp.int32) % n).at[dest].set(tok)
    nact1 = nact.reshape(1).astype(jnp.int32)

    any_space = pl.BlockSpec(memory_space=pl.ANY)
    is_last = (k_in_e == nblk[blk_e] - 1).astype(jnp.int32)
    ys = pl.pallas_call(
        functools.partial(_moe_expert_kernel, tb=tb, layer=layer),
        grid_spec=pltpu.PrefetchScalarGridSpec(
            num_scalar_prefetch=7, grid=(nb,),
            in_specs=[any_space, any_space, any_space, any_space],
            out_specs=pl.BlockSpec((tb, d // 2), lambda b, *_: (b, 0)),
            scratch_shapes=[pltpu.VMEM((GATHER_AHEAD + 1, tb, d // 2), jnp.uint32),
                            pltpu.VMEM((2, d, f), F32), pltpu.VMEM((2, d, f), F32), pltpu.VMEM((f, d), F32),
                            pltpu.VMEM((tb, d), BF), pltpu.VMEM((tb, f), BF),
                            pltpu.SemaphoreType.DMA((GATHER_AHEAD + 1,)), pltpu.SemaphoreType.DMA((2, 2)),
                            pltpu.SemaphoreType.DMA((1,))]),
        out_shape=jax.ShapeDtypeStruct((nb * tb, d // 2), jnp.uint32),
        compiler_params=_cparams(("arbitrary",), 56), name="moe_experts",
    )(blk_e, k_in_e, is_last, wset, nxt_e, nact1, row_tok, xn, w_gate, w_up, w_down)

    tc = min(COMBINE_ROWS, n)
    gain = (final_gain if final_gain is not None else ffn_gain).reshape(1, d)
    return pl.pallas_call(
        functools.partial(_combine_kernel, tc=tc, n=n, final_norm=final_gain is not None),
        grid_spec=pltpu.PrefetchScalarGridSpec(
            num_scalar_prefetch=1, grid=(n // tc,),
            in_specs=[pl.BlockSpec(memory_space=pl.ANY), pl.BlockSpec((tc, d), lambda i, *_: (i, 0)),
                      pl.BlockSpec((tc, LANES), lambda i, *_: (i, 0)), pl.BlockSpec((1, d), lambda i, *_: (0, 0))],
            out_specs=pl.BlockSpec((tc, d), lambda i, *_: (i, 0)),
            scratch_shapes=[pltpu.VMEM((2, TOP_K * tc, d // 2), jnp.uint32), pltpu.SemaphoreType.DMA((2,))]),
        out_shape=jax.ShapeDtypeStruct((n, d), F32),
        compiler_params=_cparams(("arbitrary",), 40), name="moe_combine",
    )(dest, ys, h, wts, gain)


def even_mixer(h, gain, w_in, b_gates, pool_w, pool_scale, head_gain, w_out, j):
    d = h.shape[1]
    pool_width = N_POOL_GROUPS * pool_w.shape[-1]
    mlstm_width = head_gain.shape[0]
    main_cols = pool_width + 4 * mlstm_width
    n_gates = 2 * MLSTM_HEADS
    gate_b = jnp.pad(b_gates, (0, LANES - n_gates)).reshape(1, LANES)
    z, gates = norm_matmul_gates_t(h, gain, jnp.swapaxes(w_in, 1, 2), main_cols, n_gates, gate_b, j)
    y_p = pool_mixer(z, pool_w, pool_scale)
    y_m = mlstm_mixer(z, gates, head_gain, pool_width)
    assert pool_width == mlstm_width and pool_width + mlstm_width == d
    return matmul_residual([y_p, y_m], w_out, h, layer=j)


def odd_mixer(h, gain, w_in, conv_w, w_out, j):
    z = norm_matmul(h, gain, w_in, w_in.shape[2], layer=j)
    return conv_matmul_residual(z, conv_w, w_out, h, j)


def cross_attn(h, mem, gain, mem_gain, wq, wk, wv, wo, layer):
    d = h.shape[1]
    k = norm_matmul(mem, mem_gain, wk, d, layer=layer)
    v = norm_matmul(mem, mem_gain, wv, d, layer=layer)
    return matmul_residual([q_attention(h, gain, wq, k, v, layer)], wo, h, layer=layer)


def kernel(x, mem, mix_norm, xattn_norm, mem_norm, ffn_norm, final_norm, ev_w_in, ev_b_gates, ev_pool_w, ev_pool_scale, ev_head_norm, ev_w_out, od_w_in, od_conv_w, od_w_out, xa_wq, xa_wk, xa_wv, xa_wo, rt_group_w, rt_group_b, rt_expert_w, rt_expert_b, ex_w_gate, ex_w_up, ex_w_down):
    depth = mix_norm.shape[0]
    h = x[0]
    m = mem[0]
    for layer in range(depth):
        j = layer // 2
        if layer % 2 == 0:
            h = even_mixer(h, mix_norm[layer], ev_w_in, ev_b_gates[j], ev_pool_w[j], ev_pool_scale[j],
                           ev_head_norm[j], ev_w_out, j)
        else:
            h = odd_mixer(h, mix_norm[layer], od_w_in, od_conv_w[j], od_w_out, j)
        h = cross_attn(h, m, xattn_norm[layer], mem_norm[layer], xa_wq, xa_wk, xa_wv, xa_wo, layer)
        h = hier_moe(h, ffn_norm[layer], rt_group_w[layer], rt_group_b[layer], rt_expert_w[layer],
                     rt_expert_b[layer], ex_w_gate, ex_w_up, ex_w_down, layer,
                     final_gain=final_norm if layer == depth - 1 else None)
    return h[None]
```
